```python
import math
import jax
import jax.numpy as jnp
from jax import lax
import numpy as np

D_MODEL = 2048
BATCH = 4
SEQ = 4096
DEPTH = 2

GRID_W = 64
CTX_LEN = 256
MIX = D_MODEL
N_MIXERS = 4
BR = MIX // N_MIXERS
S5_GSIZE = 16
S5_GROUPS = BR // S5_GSIZE
S5_STATE = 64
HY_ORDER = 2
HY_EMB = 33
HY_BANDS = (HY_EMB - 1) // 2
HY_FFN = 64
HY_SHORT = 3
HY_FAST_DECAY = 0.3
HY_SLOW_DECAY = 1.5
HY_TARGET = 1e-2
RET_HEADS = 4
RET_DK = BR // RET_HEADS
RET_DV = BR // RET_HEADS
RET_CHUNK = 128
GLA_HEADS = 4
GLA_KW = BR // 2
GLA_DK = GLA_KW // GLA_HEADS
GLA_DV = BR // GLA_HEADS
GLA_LR = 16
GLA_TAU = 16.0
GLA_CHUNK = 64
ROPE_BASE = 10000.0
EPS = 1e-6

SEGMENTS = (
    ('s5_u', BR), ('s5_g', BR),
    ('hy_x', 3 * BR), ('hy_g', BR),
    ('ret_q', RET_HEADS * RET_DK), ('ret_k', RET_HEADS * RET_DK), ('ret_v', BR), ('ret_g', BR),
    ('gla_q', GLA_KW), ('gla_k', GLA_KW), ('gla_v', BR), ('gla_lr', 2 * GLA_LR), ('gla_g', BR),
)
SEG_NAMES = tuple(name for name, _ in SEGMENTS)
IN_W = sum(size for _, size in SEGMENTS)
CTX_STATE_SEGMENTS = ('s5_u', 'ret_k', 'ret_v', 'gla_k', 'gla_v', 'gla_lr')

kernel_name = 'hybrid_s5_hyena_retnet_gla_prefix_dit'

F32 = jnp.float32


def _rmsnorm(x, w):
    xf = x.astype(F32)
    y = xf * lax.rsqrt(jnp.mean(xf * xf, axis=-1, keepdims=True) + EPS)
    return (y * w.astype(F32)).astype(x.dtype)


def _project(h, w_in, names):
    blocks, sizes, off = [], [], 0
    for name, size in SEGMENTS:
        if name in names:
            blocks.append(w_in[:, off:off + size])
            sizes.append((name, size))
        off += size
    w = w_in if len(blocks) == len(SEGMENTS) else jnp.concatenate(blocks, axis=1)
    p = h @ w
    out, off = {}, 0
    for name, size in sizes:
        out[name] = p[..., off:off + size]
        off += size
    return out


def _heads(t, n):
    b, l, _ = t.shape
    return t.astype(F32).reshape(b, l, n, -1).transpose(0, 2, 1, 3)


def _merge_norm(o, w):
    o = o * lax.rsqrt(jnp.mean(o * o, axis=-1, keepdims=True) + EPS)
    b, h, l, d = o.shape
    return o.transpose(0, 2, 1, 3).reshape(b, l, h * d) * w.astype(F32)


def _rope(x, ang):
    cos, sin = jnp.cos(ang), jnp.sin(ang)
    x1, x2 = jnp.split(x, 2, axis=-1)
    return jnp.concatenate([x1 * cos - x2 * sin, x1 * sin + x2 * cos], axis=-1)


def _latent_angles(rows):
    half = RET_DK // 4
    inv = ROPE_BASE ** (-jnp.arange(half, dtype=F32) / half)
    r = jnp.repeat(jnp.arange(rows, dtype=F32), GRID_W)
    cl = jnp.tile(jnp.arange(GRID_W, dtype=F32), rows)
    return jnp.concatenate([r[:, None] * inv, cl[:, None] * inv], axis=-1)


def _ctx_angles(n_ctx):
    n = RET_DK // 2
    inv = ROPE_BASE ** (-jnp.arange(n, dtype=F32) / n)
    return jnp.arange(n_ctx, dtype=F32)[:, None] * inv


def _s5_discretize(a_re, a_im, log_dt):
    a_re, a_im = a_re.astype(F32), a_im.astype(F32)
    dt = jnp.exp(log_dt.astype(F32))[:, None]
    mag = jnp.exp(a_re * dt)
    ab_re, ab_im = mag * jnp.cos(a_im * dt), mag * jnp.sin(a_im * dt)
    den = a_re * a_re + a_im * a_im
    nr = ab_re - 1.0
    co_re = (nr * a_re + ab_im * a_im) / den
    co_im = (ab_im * a_re - nr * a_im) / den
    return ab_re, ab_im, co_re, co_im


def _cplx_combine(e1, e2):
    ar1, ai1, br1, bi1 = e1
    ar2, ai2, br2, bi2 = e2
    return (ar2 * ar1 - ai2 * ai1, ar2 * ai1 + ai2 * ar1,
            ar2 * br1 - ai2 * bi1 + br2, ar2 * bi1 + ai2 * br1 + bi2)


def _s5_scan(u, h0, disc, b_re, b_im):
    ab_re, ab_im, co_re, co_im = disc
    r_re = jnp.einsum('blgi,gpi->lbgp', u, b_re.astype(F32))
    r_im = jnp.einsum('blgi,gpi->lbgp', u, b_im.astype(F32))
    bu_re = co_re * r_re - co_im * r_im
    bu_im = co_re * r_im + co_im * r_re
    if h0 is not None:
        h0_re, h0_im = h0
        bu_re = bu_re.at[0].add(ab_re * h0_re - ab_im * h0_im)
        bu_im = bu_im.at[0].add(ab_re * h0_im + ab_im * h0_re)
    L = u.shape[1]
    a_re = jnp.broadcast_to(ab_re, (L, 1) + ab_re.shape)
    a_im = jnp.broadcast_to(ab_im, (L, 1) + ab_im.shape)
    _, _, h_re, h_im = lax.associative_scan(_cplx_combine, (a_re, a_im, bu_re, bu_im), axis=0)
    return h_re, h_im


def _s5_readout(h, c_re, c_im):
    h_re, h_im = h
    return (jnp.einsum('lbgp,gip->blgi', h_re, c_re)
            - jnp.einsum('lbgp,gip->blgi', h_im, c_im))


def _s5_mixer(u_c, u_l, a_re, a_im, log_dt, b_re, b_im, c_re, c_im, d_skip,
              glu_w, glu_b, ctx_out):
    bsz, l_lat, _ = u_l.shape
    l_ctx = u_c.shape[1]
    uc = u_c.astype(F32).reshape(bsz, l_ctx, S5_GROUPS, S5_GSIZE)
    ul = u_l.astype(F32).reshape(bsz, l_lat, S5_GROUPS, S5_GSIZE)
    dsk = d_skip.astype(F32).reshape(S5_GROUPS, S5_GSIZE)
    y_l = ul * dsk
    y_c = uc * dsk if ctx_out else None
    for dr in range(2):
        disc = _s5_discretize(a_re[dr], a_im[dr], log_dt[dr])
        cr, ci = c_re[dr].astype(F32), c_im[dr].astype(F32)
        flip = (lambda t: t[:, ::-1]) if dr == 1 else (lambda t: t)
        h_c = _s5_scan(flip(uc), None, disc, b_re[dr], b_im[dr])
        h_l = _s5_scan(flip(ul), (h_c[0][-1], h_c[1][-1]), disc, b_re[dr], b_im[dr])
        y_l = y_l + flip(_s5_readout(h_l, cr, ci))
        if ctx_out:
            y_c = y_c + flip(_s5_readout(h_c, cr, ci))

    def glu(y):
        g = jax.nn.gelu(y.reshape(y.shape[0], y.shape[1], BR))
        return g * jax.nn.sigmoid(g @ glu_w.astype(F32) + glu_b.astype(F32))

    return (glu(y_c) if ctx_out else None), glu(y_l)


def _short_conv(t, w, b):
    ch = t.shape[-1]
    y = lax.conv_general_dilated(
        t, w[:, None, :].astype(t.dtype), window_strides=(1,),
        padding=[(HY_SHORT // 2, HY_SHORT // 2)],
        dimension_numbers=('NWC', 'WIO', 'NWC'), feature_group_count=ch)
    return y + b.astype(t.dtype)


def _hyena_filters(L, w1, b1, f1, w2, b2, f2, w3, b3, f3, w4):
    t = jnp.linspace(0.0, 1.0, L, dtype=F32)[:, None]
    wpos = 2.0 * math.pi * jnp.arange(L, dtype=F32)[:, None] / L
    f = jnp.linspace(1e-4, HY_BANDS - 1, HY_BANDS, dtype=F32)[None, :]
    z = jnp.concatenate([t, jnp.cos(f * wpos), -jnp.sin(f * wpos)], axis=-1)
    h = jnp.sin(f1 * (z @ w1 + b1))
    h = jnp.sin(f2 * (h @ w2 + b2))
    h = jnp.sin(f3 * (h @ w3 + b3))
    h = (h @ w4).reshape(L, 2, HY_ORDER, BR)
    deltas = jnp.linspace(math.log(HY_TARGET) / HY_SLOW_DECAY,
                          math.log(HY_TARGET) / HY_FAST_DECAY, BR, dtype=F32)
    h = h * jnp.exp(-t * jnp.abs(deltas))[:, None, None, :]
    fwd, bwd = h[:, 0], h[:, 1]
    k = jnp.concatenate([fwd, jnp.zeros_like(fwd[:1]), bwd[:0:-1]], axis=0)
    return k * lax.rsqrt(jnp.sum(k * k, axis=0, keepdims=True) + EPS)


def _hyena_seq(xp, conv_w, conv_b, filt, bias):
    L = xp.shape[1]
    xs = _short_conv(xp, conv_w, conv_b).astype(F32)
    v, g1, g2 = jnp.split(xs, 3, axis=-1)
    kf = jnp.fft.rfft(_hyena_filters(L, *[p.astype(F32) for p in filt]), n=2 * L, axis=0)
    z = v
    for o, g in enumerate((g1, g2)):
        zf = jnp.fft.rfft(z, n=2 * L, axis=1)
        conv = jnp.fft.irfft(zf * kf[None, :, o], n=2 * L, axis=1)[:, :L]
        z = g * (conv + bias[o].astype(F32) * z)
    return z


def _retention_dir(q, k, v, s0, log_g, with_output):
    bsz, nh, L, dk = k.shape
    dv = v.shape[-1]
    C = min(RET_CHUNK, L)
    n = L // C
    k = k.reshape(bsz, nh, n, C, dk)
    v = v.reshape(bsz, nh, n, C, dv)
    idx = jnp.arange(C, dtype=F32)
    lg = log_g[:, None]
    kv = jnp.einsum('bhncd,bhnce->bhnde',
                    k * jnp.exp((C - 1 - idx) * lg)[None, :, None, :, None], v)
    decay_c = jnp.exp(C * log_g)[None, :, None, None]
    s_init = jnp.zeros((bsz, nh, dk, dv), F32) if s0 is None else s0

    def step(s, kv_c):
        return decay_c * s + kv_c, s

    s_fin, s_prev = lax.scan(step, s_init, jnp.moveaxis(kv, 2, 0))
    if not with_output:
        return None, s_fin
    s_prev = jnp.moveaxis(s_prev, 0, 2)
    q = q.reshape(bsz, nh, n, C, dk)
    rel = idx[:, None] - idx[None, :]
    dmask = jnp.where(rel >= 0, jnp.exp(jnp.maximum(rel, 0.0)[None] * log_g[:, None, None]), 0.0)
    att = jnp.einsum('bhncd,bhnjd->bhncj', q, k) * dmask[None, :, None]
    o = jnp.einsum('bhncj,bhnje->bhnce', att, v)
    o = o + jnp.einsum('bhncd,bhnde->bhnce',
                       q * jnp.exp((idx + 1.0) * lg)[None, :, None, :, None], s_prev)
    return o.reshape(bsz, nh, L, dv), s_fin


def _retention_mixer(pc, pl, ang_c, ang_l, norm_w, ctx_out):
    log_g = jnp.log(1.0 - 2.0 ** (-5.0 - jnp.arange(RET_HEADS, dtype=F32)))

    def prep(p, ang, with_q):
        k = _rope(_heads(p['ret_k'], RET_HEADS), ang) * RET_DK ** -0.5
        v = _heads(p['ret_v'], RET_HEADS)
        q = _rope(_heads(p['ret_q'], RET_HEADS), ang) if with_q else None
        return q, k, v

    rev = lambda t: None if t is None else t[:, :, ::-1]
    qc, kc, vc = prep(pc, ang_c, ctx_out)
    ql, kl, vl = prep(pl, ang_l, True)
    oc_f, sc_f = _retention_dir(qc, kc, vc, None, log_g, ctx_out)
    oc_b, sc_b = _retention_dir(rev(qc), rev(kc), rev(vc), None, log_g, ctx_out)
    ol_f, _ = _retention_dir(ql, kl, vl, sc_f, log_g, True)
    ol_b, _ = _retention_dir(rev(ql), rev(kl), rev(vl), sc_b, log_g, True)
    y_l = _merge_norm(ol_f + rev(ol_b), norm_w)
    y_c = _merge_norm(oc_f + rev(oc_b), norm_w) if ctx_out else None
    return y_c, y_l


def _gla_dir(q, k, v, log_a, s0, with_output):
    bsz, nh, L, dk = k.shape
    dv = v.shape[-1]
    C = min(GLA_CHUNK, L)
    n = L // C
    r = lambda t: t.reshape(bsz, nh, n, C, t.shape[-1])
    k, v, log_a = r(k), r(v), r(log_a)
    b = jnp.cumsum(log_a, axis=3)
    b_last = b[:, :, :, -1:]
    kv = jnp.einsum('bhncd,bhnce->bhnde', k * jnp.exp(b_last - b), v)
    s_init = jnp.zeros((bsz, nh, dk, dv), F32) if s0 is None else s0

    def step(s, inp):
        dec, kv_c = inp
        return dec[..., None] * s + kv_c, s

    s_fin, s_prev = lax.scan(step, s_init, (jnp.moveaxis(jnp.exp(b_last[:, :, :, 0]), 2, 0),
                                            jnp.moveaxis(kv, 2, 0)))
    if not with_output:
        return None, s_fin
    s_prev = jnp.moveaxis(s_prev, 0, 2)
    q = r(q)
    ref = b[:, :, :, C // 2 - 1:C // 2]
    att = jnp.einsum('bhncd,bhnjd->bhncj', q * jnp.exp(b - ref), k * jnp.exp(ref - b))
    lower = jnp.tril(jnp.ones((C, C), dtype=bool))
    att = jnp.where(lower, att, 0.0)
    o = (jnp.einsum('bhncj,bhnje->bhnce', att, v)
         + jnp.einsum('bhncd,bhnde->bhnce', q * jnp.exp(b), s_prev))
    return o.reshape(bsz, nh, L, dv), s_fin


def _gla_mixer(pc, pl, gate_w, gate_b, norm_w, ctx_out):
    def prep(p, with_q):
        k = _heads(p['gla_k'], GLA_HEADS)
        v = _heads(p['gla_v'], GLA_HEADS)
        q = _heads(p['gla_q'], GLA_HEADS) * GLA_DK ** -0.5 if with_q else None
        lr = p['gla_lr'].astype(F32)
        la = [_heads(jax.nn.log_sigmoid(lr[..., d * GLA_LR:(d + 1) * GLA_LR] @ gate_w[d].astype(F32)
                                        + gate_b[d].astype(F32)) / GLA_TAU, GLA_HEADS)
              for d in range(2)]
        return q, k, v, la

    rev = lambda t: None if t is None else t[:, :, ::-1]
    qc, kc, vc, lac = prep(pc, ctx_out)
    ql, kl, vl, lal = prep(pl, True)
    oc_f, sc_f = _gla_dir(qc, kc, vc, lac[0], None, ctx_out)
    oc_b, sc_b = _gla_dir(rev(qc), rev(kc), rev(vc), rev(lac[1]), None, ctx_out)
    ol_f, _ = _gla_dir(ql, kl, vl, lal[0], sc_f, True)
    ol_b, _ = _gla_dir(rev(ql), rev(kl), rev(vl), rev(lal[1]), sc_b, True)
    y_l = _merge_norm(ol_f + rev(ol_b), norm_w)
    y_c = _merge_norm(oc_f + rev(oc_b), norm_w) if ctx_out else None
    return y_c, y_l


def _merge_branches(p, ys, w_out):
    gated = [y * jax.nn.silu(p[g].astype(F32))
             for y, g in zip(ys, ('s5_g', 'hy_g', 'ret_g', 'gla_g'))]
    return jnp.concatenate(gated, axis=-1).astype(w_out.dtype) @ w_out


def setup_inputs(seed: int = 0) -> dict:
    key = jax.random.key(seed)
    ks = iter(jax.random.split(key, 40))
    nrm = lambda shape, scale: jax.random.normal(next(ks), shape, jnp.float32) * scale
    G, P, S = S5_GROUPS, S5_STATE, S5_GSIZE
    n_idx = jnp.arange(P, dtype=jnp.float32)
    return {
        'x': nrm((BATCH, SEQ, D_MODEL), 1.0),
        'c': nrm((BATCH, D_MODEL), 1.0),
        'ctx': nrm((BATCH, CTX_LEN, D_MODEL), 1.0),
        'c_ctx': nrm((D_MODEL,), 1.0),
        'norm_w': 1.0 + nrm((DEPTH, D_MODEL), 0.02),
        'ada_w': nrm((DEPTH, D_MODEL, 3 * D_MODEL), D_MODEL ** -0.5),
        'ada_b': nrm((DEPTH, 3 * D_MODEL), 0.02),
        'w_in': nrm((DEPTH, D_MODEL, IN_W), D_MODEL ** -0.5),
        'w_out': nrm((DEPTH, MIX, D_MODEL), MIX ** -0.5),
        's5_a_re': -0.5 + nrm((DEPTH, 2, G, P), 0.01),
        's5_a_im': math.pi * n_idx + nrm((DEPTH, 2, G, P), 0.01),
        's5_log_dt': jax.random.uniform(next(ks), (DEPTH, 2, G), jnp.float32,
                                        math.log(1e-3), math.log(1e-1)),
        's5_b_re': nrm((DEPTH, 2, G, P, S), (2 * S) ** -0.5),
        's5_b_im': nrm((DEPTH, 2, G, P, S), (2 * S) ** -0.5),
        's5_c_re': nrm((DEPTH, 2, G, S, P), 0.5),
        's5_c_im': nrm((DEPTH, 2, G, S, P), 0.5),
        's5_d': nrm((DEPTH, BR), 1.0),
        's5_glu_w': nrm((DEPTH, BR, BR), BR ** -0.5),
        's5_glu_b': nrm((DEPTH, BR), 0.02),
        'hy_conv_w': nrm((DEPTH, HY_SHORT, 3 * BR), HY_SHORT ** -0.5),
        'hy_conv_b': nrm((DEPTH, 3 * BR), 0.02),
        'hy_w1': nrm((DEPTH, HY_EMB, HY_FFN), HY_EMB ** -0.5),
        'hy_b1': nrm((DEPTH, HY_FFN), 0.1),
        'hy_f1': 1.0 + nrm((DEPTH, HY_FFN), 0.02),
        'hy_w2': nrm((DEPTH, HY_FFN, HY_FFN), HY_FFN ** -0.5),
        'hy_b2': nrm((DEPTH, HY_FFN), 0.1),
        'hy_f2': 1.0 + nrm((DEPTH, HY_FFN), 0.02),
        'hy_w3': nrm((DEPTH, HY_FFN, HY_FFN), HY_FFN ** -0.5),
        'hy_b3': nrm((DEPTH, HY_FFN), 0.1),
        'hy_f3': 1.0 + nrm((DEPTH, HY_FFN), 0.02),
        'hy_w4': nrm((DEPTH, HY_FFN, 2 * HY_ORDER * BR), HY_FFN ** -0.5),
        'hy_bias': nrm((DEPTH, HY_ORDER, BR), 0.5),
        'ret_norm_w': 1.0 + nrm((DEPTH, BR), 0.02),
        'gla_gate_w': nrm((DEPTH, 2, GLA_LR, GLA_KW), GLA_LR ** -0.5),
        'gla_gate_b': nrm((DEPTH, 2, GLA_KW), 0.1),
        'gla_norm_w': 1.0 + nrm((DEPTH, BR), 0.02),
        'final_norm_w': 1.0 + nrm((D_MODEL,), 0.02),
    }


def reference(x, c, ctx, c_ctx, norm_w, ada_w, ada_b, w_in, w_out,
              s5_a_re, s5_a_im, s5_log_dt, s5_b_re, s5_b_im, s5_c_re, s5_c_im,
              s5_d, s5_glu_w, s5_glu_b,
              hy_conv_w, hy_conv_b, hy_w1, hy_b1, hy_f1, hy_w2, hy_b2, hy_f2,
              hy_w3, hy_b3, hy_f3, hy_w4, hy_bias,
              ret_norm_w, gla_gate_w, gla_gate_b, gla_norm_w, final_norm_w):
    rows = x.shape[1] // GRID_W
    ang_l = _latent_angles(rows)
    ang_c = _ctx_angles(ctx.shape[1])
    x_l, x_c = x, ctx
    for i in range(DEPTH):
        ctx_out = i < DEPTH - 1
        mod_l = jax.nn.silu(c) @ ada_w[i] + ada_b[i]
        mod_c = jax.nn.silu(c_ctx) @ ada_w[i] + ada_b[i]
        sh_l, sc_l, gt_l = jnp.split(mod_l[:, None, :], 3, axis=-1)
        sh_c, sc_c, gt_c = jnp.split(mod_c, 3, axis=-1)
        h_l = _rmsnorm(x_l, norm_w[i]) * (1.0 + sc_l) + sh_l
        h_c = _rmsnorm(x_c, norm_w[i]) * (1.0 + sc_c) + sh_c
        pl = _project(h_l, w_in[i], SEG_NAMES)
        pc = _project(h_c, w_in[i], SEG_NAMES if ctx_out else CTX_STATE_SEGMENTS)

        s5_c, s5_l = _s5_mixer(pc['s5_u'], pl['s5_u'], s5_a_re[i], s5_a_im[i], s5_log_dt[i],
                               s5_b_re[i], s5_b_im[i], s5_c_re[i], s5_c_im[i], s5_d[i],
                               s5_glu_w[i], s5_glu_b[i], ctx_out)
        hy_filt = (hy_w1[i], hy_b1[i], hy_f1[i], hy_w2[i], hy_b2[i], hy_f2[i],
                   hy_w3[i], hy_b3[i], hy_f3[i], hy_w4[i])
        hy_l = _hyena_seq(pl['hy_x'], hy_conv_w[i], hy_conv_b[i], hy_filt, hy_bias[i])
        ret_c, ret_l = _retention_mixer(pc, pl, ang_c, ang_l, ret_norm_w[i], ctx_out)
        gla_c, gla_l = _gla_mixer(pc, pl, gla_gate_w[i], gla_gate_b[i], gla_norm_w[i], ctx_out)

        x_l = x_l + gt_l * _merge_branches(pl, (s5_l, hy_l, ret_l, gla_l), w_out[i])
        if ctx_out:
            hy_c = _hyena_seq(pc['hy_x'], hy_conv_w[i], hy_conv_b[i], hy_filt, hy_bias[i])
            x_c = x_c + gt_c * _merge_branches(pc, (s5_c, hy_c, ret_c, gla_c), w_out[i])
    return _rmsnorm(x_l, final_norm_w)
```

```python
import functools
import math

import numpy as np
import jax
import jax.numpy as jnp
from jax import lax
from jax.experimental import pallas as pl
from jax.experimental.pallas import tpu as pltpu

F32 = jnp.float32
BF16 = jnp.bfloat16

EPS = 1e-6
GRID_W = 64
BR = 512
S5_GSIZE = 16
S5_GROUPS = BR // S5_GSIZE
S5_STATE = 64
S5_T = 16
HY_ORDER = 2
HY_EMB = 33
HY_BANDS = (HY_EMB - 1) // 2
HY_FAST_DECAY = 0.3
HY_SLOW_DECAY = 1.5
HY_TARGET = 1e-2
RET_HEADS = 4
RET_DK = 128
GLA_HEADS = 4
GLA_DK = 64
GLA_DV = 128
GLA_LR = 16
GLA_TAU = 16.0
ROPE_BASE = 10000.0
ATT_CHUNK = 128

LANE = 128
VMEM_LIMIT = 56 * 1024 * 1024

COL_S5_U = 0
COL_RET_K = 4
COL_RET_V = 8
COL_GLA_V = 12
COL_GATES = 16
COL_HY_X = 32
COL_RET_Q = 44
COL_GLA_Q = 48
COL_GLA_K = 50
COL_GLA_LR = 52
NP_UNITS = 54
NP = NP_UNITS * LANE
_SEG_DST = (
    (BR, COL_S5_U), (BR, COL_GATES),
    (3 * BR, COL_HY_X), (BR, COL_GATES + 4),
    (BR, COL_RET_Q), (BR, COL_RET_K), (BR, COL_RET_V), (BR, COL_GATES + 8),
    (BR // 2, COL_GLA_Q), (BR // 2, COL_GLA_K), (BR, COL_GLA_V), (2 * GLA_LR, COL_GLA_LR),
    (BR, COL_GATES + 12),
)


def _cparams(sem):
    return pltpu.CompilerParams(dimension_semantics=sem, vmem_limit_bytes=VMEM_LIMIT)


def _silu(x):
    return x * jax.nn.sigmoid(x)


def _permute_w_in(w):
    d = w.shape[0]
    out = jnp.zeros((d, NP), BF16)
    off = 0
    for size, unit in _SEG_DST:
        out = lax.dynamic_update_slice(out, w[:, off:off + size].astype(BF16), (0, unit * LANE))
        off += size
    return out


def _ada_kernel(c_ref, w_ref, b_ref, o_ref):
    a = _silu(c_ref[...]).astype(BF16)
    o_ref[0] = jnp.dot(a, w_ref[0].astype(BF16), preferred_element_type=F32) + b_ref[0]


def _ada(cc, ada_w, ada_b, tn=768):
    depth, d, n = ada_w.shape
    return pl.pallas_call(
        _ada_kernel,
        grid=(depth, n // tn),
        in_specs=[pl.BlockSpec((8, d), lambda i, j: (0, 0)),
                  pl.BlockSpec((1, d, tn), lambda i, j: (i, 0, j)),
                  pl.BlockSpec((1, 1, tn), lambda i, j: (i, 0, j))],
        out_specs=pl.BlockSpec((1, 8, tn), lambda i, j: (i, 0, j)),
        out_shape=jax.ShapeDtypeStruct((depth, 8, n), F32),
        compiler_params=_cparams(("parallel", "parallel")),
        name="ada_mod",
    )(cc, ada_w, ada_b.reshape(depth, 1, n))


def _inproj_kernel(x_ref, nw_ref, sc_ref, sh_ref, w_ref, o_ref, h_ref):
    @pl.when(pl.program_id(1) == 0)
    def _():
        x = x_ref[...]
        y = x * lax.rsqrt(jnp.mean(x * x, axis=-1, keepdims=True) + EPS) * nw_ref[...]
        h_ref[...] = (y * (1.0 + sc_ref[0]) + sh_ref[0]).astype(BF16)

    o_ref[...] = jnp.dot(h_ref[...], w_ref[...], preferred_element_type=F32)


def _inproj(x2, nw, sc, sh, w, rows_per_mod, tm=512, tn=768):
    m, d = x2.shape
    tm = min(tm, rows_per_mod)
    per = rows_per_mod // tm
    return pl.pallas_call(
        _inproj_kernel,
        grid=(m // tm, NP // tn),
        in_specs=[pl.BlockSpec((tm, d), lambda i, j: (i, 0)),
                  pl.BlockSpec((1, d), lambda i, j: (0, 0)),
                  pl.BlockSpec((1, 1, d), lambda i, j: (i // per, 0, 0)),
                  pl.BlockSpec((1, 1, d), lambda i, j: (i // per, 0, 0)),
                  pl.BlockSpec((d, tn), lambda i, j: (0, j))],
        out_specs=pl.BlockSpec((tm, tn), lambda i, j: (i, j)),
        out_shape=jax.ShapeDtypeStruct((m, NP), F32),
        scratch_shapes=[pltpu.VMEM((tm, d), BF16)],
        compiler_params=_cparams(("parallel", "arbitrary")),
        name="inproj",
    )(x2, nw.reshape(1, d), sc, sh, w)


def _outproj_kernel(x_ref, y0_ref, y1_ref, y2_ref, y3_ref, g_ref, w_ref, gt_ref, fw_ref, o_ref, *, final):
    acc = None
    for k, y_ref in enumerate((y0_ref, y1_ref, y2_ref, y3_ref)):
        gated = (y_ref[...] * _silu(g_ref[:, k * BR:(k + 1) * BR])).astype(BF16)
        part = jnp.dot(gated, w_ref[k * BR:(k + 1) * BR, :], preferred_element_type=F32)
        acc = part if acc is None else acc + part
    x = x_ref[...] + gt_ref[0] * acc
    if final:
        x = x * lax.rsqrt(jnp.mean(x * x, axis=-1, keepdims=True) + EPS) * fw_ref[...]
    o_ref[...] = x


def _outproj(x2, ys, p, w_out, gt, fw, rows_per_mod, final, tm=256):
    m, d = x2.shape
    tm = min(tm, rows_per_mod)
    per = rows_per_mod // tm
    gates_blk = COL_GATES * LANE // (4 * BR)
    yspec = pl.BlockSpec((tm, BR), lambda i: (i, 0))
    return pl.pallas_call(
        functools.partial(_outproj_kernel, final=final),
        grid=(m // tm,),
        in_specs=[pl.BlockSpec((tm, d), lambda i: (i, 0)), yspec, yspec, yspec, yspec,
                  pl.BlockSpec((tm, 4 * BR), lambda i: (i, gates_blk)),
                  pl.BlockSpec((4 * BR, d), lambda i: (0, 0)),
                  pl.BlockSpec((1, 1, d), lambda i: (i // per, 0, 0)),
                  pl.BlockSpec((1, d), lambda i: (0, 0))],
        out_specs=pl.BlockSpec((tm, d), lambda i: (i, 0)),
        out_shape=jax.ShapeDtypeStruct((m, d), F32),
        compiler_params=_cparams(("parallel",)),
        name="outproj",
    )(x2, *ys, p, w_out, gt, fw.reshape(1, d))


def _s5_tables(a_re, a_im, log_dt, b_re, b_im, c_re, c_im, nch):
    hp = lax.Precision.HIGHEST
    t_len, g_n, p_n, s_n = S5_T, S5_GROUPS, S5_STATE, S5_GSIZE
    a_re, a_im = a_re.astype(F32), a_im.astype(F32)
    dt = jnp.exp(log_dt.astype(F32))[:, :, None]
    lam_re, lam_im = a_re * dt, a_im * dt

    def power(tau):
        tau = tau.astype(F32)[:, None, None, None]
        mag = jnp.exp(lam_re[None] * tau)
        return mag * jnp.cos(lam_im[None] * tau), mag * jnp.sin(lam_im[None] * tau)

    ab_re, ab_im = power(jnp.ones((1,)))
    ab_re, ab_im = ab_re[0], ab_im[0]
    den = a_re * a_re + a_im * a_im
    nr = ab_re - 1.0
    co_re = (nr * a_re + ab_im * a_im) / den
    co_im = (ab_im * a_re - nr * a_im) / den
    b_re, b_im = b_re.astype(F32), b_im.astype(F32)
    bco_re = co_re[..., None] * b_re - co_im[..., None] * b_im
    bco_im = co_re[..., None] * b_im + co_im[..., None] * b_re
    c_re, c_im = c_re.astype(F32), c_im.astype(F32)

    pr, pi = power(jnp.arange(t_len + 1))
    ca_re = c_re[None] * pr[:, :, :, None, :] - c_im[None] * pi[:, :, :, None, :]
    ca_im = c_re[None] * pi[:, :, :, None, :] + c_im[None] * pr[:, :, :, None, :]
    kk = (jnp.einsum('tdgip,dgpj->tdgij', ca_re[:t_len], bco_re, precision=hp)
          - jnp.einsum('tdgip,dgpj->tdgij', ca_im[:t_len], bco_im, precision=hp))
    kf, kb = kk[:, 0], kk[:, 1]
    kfull = jnp.concatenate([kb[:0:-1], (kf[0] + kb[0])[None], kf[1:]], axis=0)
    tt = jnp.arange(t_len)
    idx = tt[None, :] - tt[:, None] + (t_len - 1)
    mm = kfull[idx]
    mmat = mm.transpose(2, 0, 4, 1, 3).reshape(g_n, t_len * s_n, t_len * s_n)

    wexp = jnp.stack([pr[t_len - 1 - tt, 0], pr[tt, 1]], 0), jnp.stack([pi[t_len - 1 - tt, 0], pi[tt, 1]], 0)
    w_re = wexp[0][..., None] * bco_re[:, None] - wexp[1][..., None] * bco_im[:, None]
    w_im = wexp[0][..., None] * bco_im[:, None] + wexp[1][..., None] * bco_re[:, None]
    w4 = jnp.stack([w_re[0], w_im[0], w_re[1], w_im[1]], 0)
    w4 = w4.transpose(0, 2, 1, 4, 3).reshape(4, g_n // 2, 2, t_len * s_n, p_n)
    eye2 = jnp.eye(2, dtype=F32)
    wpair = jnp.einsum('kqlrp,lm->qlrkmp', w4, eye2).reshape(g_n // 2, 2 * t_len * s_n, 4 * 2 * p_n)

    vexp_f, vexp_b = tt + 1, t_len - tt
    v4 = jnp.stack([ca_re[vexp_f, 0], -ca_im[vexp_f, 0], ca_re[vexp_b, 1], -ca_im[vexp_b, 1]], 0)
    v4 = v4.transpose(0, 2, 4, 1, 3).reshape(4, g_n // 2, 2, p_n, t_len * s_n)
    vpair = jnp.einsum('kqlpr,lm->qklpmr', v4, eye2).reshape(g_n // 2, 4 * 2 * p_n, 2 * t_len * s_n)

    qr, qi = power(t_len * jnp.arange(nch + 1))
    pw = jnp.stack([qr, qi], 0)
    pw = pw.reshape(2, nch + 1, 2, g_n // 2, 2 * p_n).transpose(3, 2, 0, 1, 4)
    return mmat.astype(BF16), wpair.astype(BF16), vpair.astype(BF16), pw


def _cmul(ar, ai, br, bi):
    return ar * br - ai * bi, ar * bi + ai * br


def _s5_kernel(x_ref, m_ref, w_ref, v_ref, pw_ref, h0_ref, y_ref, fin_ref, s_ref, h_ref, *, nch):
    ln = 2 * S5_STATE
    x = x_ref[0]
    s_ref[...] = jnp.dot(x, w_ref[0], preferred_element_type=F32)
    a_rf, a_if = pw_ref[0, 0, 0, 1:2, :], pw_ref[0, 0, 1, 1:2, :]
    a_rb, a_ib = pw_ref[0, 1, 0, 1:2, :], pw_ref[0, 1, 1, 1:2, :]

    def scan_step(s, carry):
        hrf, hif, hrb, hib = carry
        rf = pl.multiple_of(s * 8, 8)
        rb = pl.multiple_of((nch - 1 - s) * 8, 8)
        h_ref[pl.ds(rf, 8), 0:ln] = hrf
        h_ref[pl.ds(rf, 8), ln:2 * ln] = hif
        h_ref[pl.ds(rb, 8), 2 * ln:3 * ln] = hrb
        h_ref[pl.ds(rb, 8), 3 * ln:4 * ln] = hib
        pr, pi = _cmul(a_rf, a_if, hrf, hif)
        qr, qi = _cmul(a_rb, a_ib, hrb, hib)
        return (pr + s_ref[pl.ds(rf, 8), 0:ln], pi + s_ref[pl.ds(rf, 8), ln:2 * ln],
                qr + s_ref[pl.ds(rb, 8), 2 * ln:3 * ln], qi + s_ref[pl.ds(rb, 8), 3 * ln:4 * ln])

    init = (h0_ref[0, 0, 0], h0_ref[0, 0, 1], h0_ref[0, 1, 0], h0_ref[0, 1, 1])
    hrf, hif, hrb, hib = lax.fori_loop(0, nch, scan_step, init)

    odd = (lax.broadcasted_iota(jnp.int32, (8, ln), 0) % 2) == 1
    crf = jnp.where(odd, pltpu.roll(hrf, 1, axis=0), 0.0)
    cif = jnp.where(odd, pltpu.roll(hif, 1, axis=0), 0.0)
    crb = jnp.where(odd, 0.0, pltpu.roll(hrb, 7, axis=0))
    cib = jnp.where(odd, 0.0, pltpu.roll(hib, 7, axis=0))

    def fix_step(s, _):
        rf = pl.multiple_of(s * 8, 8)
        rb = pl.multiple_of((nch - 1 - s) * 8, 8)
        pr, pi = _cmul(pw_ref[0, 0, 0, pl.ds(s, 1), :], pw_ref[0, 0, 1, pl.ds(s, 1), :], crf, cif)
        qr, qi = _cmul(pw_ref[0, 1, 0, pl.ds(s, 1), :], pw_ref[0, 1, 1, pl.ds(s, 1), :], crb, cib)
        h_ref[pl.ds(rf, 8), 0:ln] += pr
        h_ref[pl.ds(rf, 8), ln:2 * ln] += pi
        h_ref[pl.ds(rb, 8), 2 * ln:3 * ln] += qr
        h_ref[pl.ds(rb, 8), 3 * ln:4 * ln] += qi
        return 0

    lax.fori_loop(0, nch, fix_step, 0)
    pr, pi = _cmul(pw_ref[0, 0, 0, nch:nch + 1, :], pw_ref[0, 0, 1, nch:nch + 1, :], crf, cif)
    qr, qi = _cmul(pw_ref[0, 1, 0, nch:nch + 1, :], pw_ref[0, 1, 1, nch:nch + 1, :], crb, cib)
    fin_ref[0, 0, 0] = hrf + pr
    fin_ref[0, 0, 1] = hif + pi
    fin_ref[0, 1, 0] = hrb + qr
    fin_ref[0, 1, 1] = hib + qi

    half = S5_T * S5_GSIZE
    y_intra = jnp.concatenate(
        [jnp.dot(x[:, :half], m_ref[0], preferred_element_type=F32),
         jnp.dot(x[:, half:], m_ref[1], preferred_element_type=F32)], axis=1)
    y_ref[0] = y_intra + jnp.dot(h_ref[...].astype(BF16), v_ref[0], preferred_element_type=F32)


def _s5_core(u, tables, h0):
    mmat, wpair, vpair, pw = tables
    bsz, l, _ = u.shape
    assert bsz * 2 == 8
    gp_n = S5_GROUPS // 2
    nch = l // (2 * S5_T)
    rows = nch * 8
    wdt = 2 * S5_T * S5_GSIZE
    x = u.reshape(bsz, 2, nch, S5_T, gp_n, 2, S5_GSIZE).transpose(4, 2, 0, 1, 5, 3, 6)
    x = x.reshape(gp_n, rows, wdt).astype(BF16)
    y, fin = pl.pallas_call(
        functools.partial(_s5_kernel, nch=nch),
        grid=(gp_n,),
        in_specs=[pl.BlockSpec((1, rows, wdt), lambda g: (g, 0, 0)),
                  pl.BlockSpec((2, wdt // 2, wdt // 2), lambda g: (g, 0, 0)),
                  pl.BlockSpec((1, wdt, wdt), lambda g: (g, 0, 0)),
                  pl.BlockSpec((1, wdt, wdt), lambda g: (g, 0, 0)),
                  pl.BlockSpec((1, 2, 2, nch + 1, 2 * S5_STATE), lambda g: (g, 0, 0, 0, 0)),
                  pl.BlockSpec((1, 2, 2, 8, 2 * S5_STATE), lambda g: (g, 0, 0, 0, 0))],
        out_specs=[pl.BlockSpec((1, rows, wdt), lambda g: (g, 0, 0)),
                   pl.BlockSpec((1, 2, 2, 8, 2 * S5_STATE), lambda g: (g, 0, 0, 0, 0))],
        out_shape=[jax.ShapeDtypeStruct((gp_n, rows, wdt), F32),
                   jax.ShapeDtypeStruct((gp_n, 2, 2, 8, 2 * S5_STATE), F32)],
        scratch_shapes=[pltpu.VMEM((rows, wdt), F32), pltpu.VMEM((rows, wdt), F32)],
        compiler_params=_cparams(("parallel",)),
        name="s5_core",
    )(x, mmat, wpair, vpair, pw, h0)
    y = y.reshape(gp_n, nch, bsz, 2, 2, S5_T, S5_GSIZE).transpose(2, 3, 1, 5, 0, 4, 6)
    return y.reshape(bsz, l, BR), fin


def _s5_next_h0(fin):
    g, d, r, _, n = fin.shape
    sw = fin.reshape(g, d, r, 4, 2, n)[:, :, :, :, ::-1, :]
    keep = jnp.array([[1.0, 0.0], [0.0, 1.0]], F32)[None, :, None, None, :, None]
    return (sw * keep).reshape(fin.shape)


def _s5_glu_kernel(y_ref, u_ref, d_ref, w_ref, b_ref, o_ref):
    y = y_ref[...] + u_ref[...] * d_ref[...]
    g = jax.nn.gelu(y)
    z = jnp.dot(g.astype(BF16), w_ref[...], preferred_element_type=F32) + b_ref[...]
    o_ref[...] = g * jax.nn.sigmoid(z)


def _s5_glu(y2, p, d_skip, glu_w, glu_b, tm=512):
    m = y2.shape[0]
    tm = min(tm, m)
    return pl.pallas_call(
        _s5_glu_kernel,
        grid=(m // tm,),
        in_specs=[pl.BlockSpec((tm, BR), lambda i: (i, 0)),
                  pl.BlockSpec((tm, BR), lambda i: (i, COL_S5_U * LANE // BR)),
                  pl.BlockSpec((1, BR), lambda i: (0, 0)),
                  pl.BlockSpec((BR, BR), lambda i: (0, 0)),
                  pl.BlockSpec((1, BR), lambda i: (0, 0))],
        out_specs=pl.BlockSpec((tm, BR), lambda i: (i, 0)),
        out_shape=jax.ShapeDtypeStruct((m, BR), F32),
        compiler_params=_cparams(("parallel",)),
        name="s5_glu",
    )(y2, p, d_skip.reshape(1, BR), glu_w.astype(BF16), glu_b.reshape(1, BR))


def _fft_tables(n_total, n2_len):
    n1_len = n_total // n2_len
    n1h = n1_len // 2
    k1 = jnp.arange(n1_len, dtype=jnp.int32)
    n2 = jnp.arange(n2_len, dtype=jnp.int32)
    m = (k1[None, :, None] * k1[None, None, :] * n2_len + n2[:, None, None] * k1[None, :, None]) % n_total
    ang = m.astype(F32) * (2.0 * math.pi / n_total)
    c, s = jnp.cos(ang), jnp.sin(ang)
    ch, sh = c[:, :, :n1h], s[:, :, :n1h]
    gf = jnp.concatenate([jnp.concatenate([ch, sh], 2), jnp.concatenate([-sh, ch], 2)], 1)
    gk = jnp.concatenate([c, -s], 1)
    cht, sht = ch.transpose(0, 2, 1), sh.transpose(0, 2, 1)
    gi = jnp.concatenate([jnp.concatenate([cht, -sht], 2), jnp.concatenate([sht, cht], 2)], 1)
    a2 = ((n2[:, None] * n2[None, :]) % n2_len).astype(F32) * (2.0 * math.pi / n2_len)
    c2, s2 = jnp.cos(a2), jnp.sin(a2)
    f2 = jnp.concatenate([jnp.concatenate([c2, s2], 1), jnp.concatenate([-s2, c2], 1)], 0)
    f2c = jnp.concatenate([jnp.concatenate([c2, -s2], 1), jnp.concatenate([s2, c2], 1)], 0)
    return gf.astype(BF16), gk.astype(BF16), gi.astype(BF16), f2.astype(BF16), f2c.astype(BF16)


def _hy_shortconv_kernel(x_ref, w_ref, b_ref, o_ref):
    x = x_ref[0]
    l = x.shape[0]
    row = lax.broadcasted_iota(jnp.int32, x.shape, 0)
    prev = jnp.where(row == 0, 0.0, pltpu.roll(x, 1, axis=0))
    nxt = jnp.where(row == l - 1, 0.0, pltpu.roll(x, l - 1, axis=0))
    o_ref[0] = w_ref[0:1, :] * prev + w_ref[1:2, :] * x + w_ref[2:3, :] * nxt + b_ref[...]


def _hy_shortconv(p3, conv_w, conv_b):
    bsz, l, _ = p3.shape
    units = 3 * BR // LANE
    return pl.pallas_call(
        _hy_shortconv_kernel,
        grid=(bsz, units),
        in_specs=[pl.BlockSpec((1, l, LANE), lambda b, u: (b, 0, COL_HY_X + u)),
                  pl.BlockSpec((3, LANE), lambda b, u: (0, u)),
                  pl.BlockSpec((1, LANE), lambda b, u: (0, u))],
        out_specs=pl.BlockSpec((1, l, LANE), lambda b, u: (b, 0, u)),
        out_shape=jax.ShapeDtypeStruct((bsz, l, 3 * BR), F32),
        compiler_params=_cparams(("parallel", "parallel")),
        name="hy_shortconv",
    )(p3, conv_w, conv_b.reshape(1, 3 * BR))


def _fft_stage2(mid_ref, f2_ref, n1_len, n2_len, emit):
    def body(kp, _):
        def rhs(k1):
            return jnp.concatenate([mid_ref[pl.ds(k1, n2_len, stride=2 * n1_len), :],
                                    mid_ref[pl.ds(n1_len + k1, n2_len, stride=2 * n1_len), :]], axis=0)
        r = jnp.concatenate([rhs(2 * kp), rhs(2 * kp + 1)], axis=1).astype(BF16)
        xs = jnp.dot(f2_ref[...], r, preferred_element_type=F32)
        emit(2 * kp, xs[:, :LANE])
        emit(2 * kp + 1, xs[:, LANE:])
        return 0
    lax.fori_loop(0, n1_len // 2, body, 0)


def _hy_kf_kernel(k_ref, gk_ref, f2_ref, o_ref, mid_ref, *, n1_len, n2_len):
    def stage1(n2, _):
        rhs = k_ref[pl.ds(n2, n1_len, stride=n2_len), :].astype(BF16)
        r0 = pl.multiple_of(n2 * 2 * n1_len, 2 * n1_len)
        mid_ref[pl.ds(r0, 2 * n1_len), :] = jnp.dot(gk_ref[n2], rhs, preferred_element_type=F32)
        return 0
    lax.fori_loop(0, n2_len, stage1, 0)
    scale = 1.0 / (n1_len * n2_len)

    def emit(k1, xs):
        o_ref[0, k1] = (xs * scale).astype(BF16)
    _fft_stage2(mid_ref, f2_ref, n1_len, n2_len, emit)


def _hy_kf(k2d, gk, f2, n2_len):
    n_total, cols = k2d.shape
    n1_len = n_total // n2_len
    tiles = cols // LANE
    const = lambda *_: (0, 0, 0)
    return pl.pallas_call(
        functools.partial(_hy_kf_kernel, n1_len=n1_len, n2_len=n2_len),
        grid=(tiles,),
        in_specs=[pl.BlockSpec((n_total, LANE), lambda t: (0, t)),
                  pl.BlockSpec(gk.shape, const),
                  pl.BlockSpec(f2.shape, lambda t: (0, 0))],
        out_specs=pl.BlockSpec((1, n1_len, 2 * n2_len, LANE), lambda t: (t, 0, 0, 0)),
        out_shape=jax.ShapeDtypeStruct((tiles, n1_len, 2 * n2_len, LANE), BF16),
        scratch_shapes=[pltpu.VMEM((2 * n_total, LANE), F32)],
        compiler_params=_cparams(("parallel",)),
        name="hy_kf",
    )(k2d, gk, f2)


def _hy_conv_kernel(z_ref, g_ref, kf_ref, bias_ref, gf_ref, gi_ref, f2_ref, f2c_ref, o_ref,
                    mid_ref, spec_ref, *, n1_len, n2_len, order):
    n1h = n1_len // 2

    def stage1(n2, _):
        rhs = jnp.concatenate([z_ref[0, pl.ds(n2, n1h, stride=n2_len), :],
                               z_ref[1, pl.ds(n2, n1h, stride=n2_len), :]], axis=0).astype(BF16)
        r0 = pl.multiple_of(n2 * 2 * n1_len, 2 * n1_len)
        mid_ref[pl.ds(r0, 2 * n1_len), :] = jnp.dot(gf_ref[n2], rhs, preferred_element_type=F32)
        return 0
    lax.fori_loop(0, n2_len, stage1, 0)

    def emit(k1, xs):
        kf = kf_ref[0, k1].astype(F32)
        xr, xi = xs[:n2_len], xs[n2_len:]
        kr, ki = kf[:n2_len], kf[n2_len:]
        spec_ref[k1] = jnp.concatenate([xr * kr - xi * ki, xr * ki + xi * kr], axis=0).astype(BF16)
    _fft_stage2(mid_ref, f2_ref, n1_len, n2_len, emit)

    def stage2i(kp, _):
        r = jnp.concatenate([spec_ref[2 * kp], spec_ref[2 * kp + 1]], axis=1)
        b = jnp.dot(f2c_ref[...], r, preferred_element_type=F32)
        r0 = pl.multiple_of(kp * 4 * n2_len, 4 * n2_len)
        mid_ref[pl.ds(r0, 2 * n2_len), :] = b[:, :LANE]
        mid_ref[pl.ds(r0 + 2 * n2_len, 2 * n2_len), :] = b[:, LANE:]
        return 0
    lax.fori_loop(0, n1_len // 2, stage2i, 0)

    def stage1i(n2, _):
        rhs = jnp.concatenate([mid_ref[pl.ds(n2, n1_len, stride=2 * n2_len), :],
                               mid_ref[pl.ds(n2_len + n2, n1_len, stride=2 * n2_len), :]], axis=0).astype(BF16)
        o = jnp.dot(gi_ref[n2], rhs, preferred_element_type=F32)
        o_ref[0, pl.ds(n2, n1h, stride=n2_len), :] = o[:n1h]
        o_ref[1, pl.ds(n2, n1h, stride=n2_len), :] = o[n1h:]
        return 0
    lax.fori_loop(0, n2_len, stage1i, 0)

    bias = bias_ref[order:order + 1, :]
    for j in range(2):
        o_ref[j] = g_ref[j] * (o_ref[j] + bias * z_ref[j])


def _hy_conv(z, zcol, g, gcol, kf, bias, tabs, n2_len, order):
    gf, _, gi, f2, f2c = tabs
    bsz, l, _ = z.shape
    n1_len = 2 * l // n2_len
    tiles = BR // LANE
    const2 = lambda *_: (0, 0)
    const3 = lambda *_: (0, 0, 0)
    one = pl.Buffered(1)
    return pl.pallas_call(
        functools.partial(_hy_conv_kernel, n1_len=n1_len, n2_len=n2_len, order=order),
        grid=(tiles, bsz // 2),
        in_specs=[pl.BlockSpec((2, l, LANE), lambda t, b: (b, 0, zcol + t)),
                  pl.BlockSpec((2, l, LANE), lambda t, b: (b, 0, gcol + t)),
                  pl.BlockSpec((1, n1_len, 2 * n2_len, LANE), lambda t, b: (order * tiles + t, 0, 0, 0),
                               pipeline_mode=one),
                  pl.BlockSpec((HY_ORDER, LANE), lambda t, b: (0, t)),
                  pl.BlockSpec(gf.shape, const3, pipeline_mode=one),
                  pl.BlockSpec(gi.shape, const3, pipeline_mode=one),
                  pl.BlockSpec(f2.shape, const2, pipeline_mode=one),
                  pl.BlockSpec(f2c.shape, const2, pipeline_mode=one)],
        out_specs=pl.BlockSpec((2, l, LANE), lambda t, b: (b, 0, t)),
        out_shape=jax.ShapeDtypeStruct((bsz, l, BR), F32),
        scratch_shapes=[pltpu.VMEM((4 * l, LANE), F32),
                        pltpu.VMEM((n1_len, 2 * n2_len, LANE), BF16)],
        compiler_params=_cparams(("parallel", "arbitrary")),
        name="hy_conv%d" % order,
    )(z, g, kf, bias, gf, gi, f2, f2c)


def _hyena_filters(l, w1, b1, f1, w2, b2, f2, w3, b3, f3, w4):
    hp = lax.Precision.HIGHEST
    t = jnp.linspace(0.0, 1.0, l, dtype=F32)[:, None]
    wpos = 2.0 * math.pi * jnp.arange(l, dtype=F32)[:, None] / l
    f = jnp.linspace(1e-4, HY_BANDS - 1, HY_BANDS, dtype=F32)[None, :]
    z = jnp.concatenate([t, jnp.cos(f * wpos), -jnp.sin(f * wpos)], axis=-1)
    h = jnp.sin(f1 * (jnp.dot(z, w1, precision=hp) + b1))
    h = jnp.sin(f2 * (jnp.dot(h, w2, precision=hp) + b2))
    h = jnp.sin(f3 * (jnp.dot(h, w3, precision=hp) + b3))
    h = jnp.dot(h, w4, precision=hp).reshape(l, 2, HY_ORDER, BR)
    deltas = jnp.linspace(math.log(HY_TARGET) / HY_SLOW_DECAY,
                          math.log(HY_TARGET) / HY_FAST_DECAY, BR, dtype=F32)
    h = h * jnp.exp(-t * jnp.abs(deltas))[:, None, None, :]
    fwd, bwd = h[:, 0], h[:, 1]
    k = jnp.concatenate([fwd, jnp.zeros_like(fwd[:1]), bwd[:0:-1]], axis=0)
    k = k * lax.rsqrt(jnp.sum(k * k, axis=0, keepdims=True) + EPS)
    return k.reshape(2 * l, HY_ORDER * BR)


def _hyena(p3, conv_w, conv_b, filt, bias, n2_len):
    bsz, l, _ = p3.shape
    tabs = _fft_tables(2 * l, n2_len)
    xs = _hy_shortconv(p3, conv_w, conv_b)
    k2d = _hyena_filters(l, *[w.astype(F32) for w in filt])
    kf = _hy_kf(k2d, tabs[1], tabs[3], n2_len)
    tiles = BR // LANE
    z1 = _hy_conv(xs, 0, xs, tiles, kf, bias, tabs, n2_len, 0)
    return _hy_conv(z1, 0, xs, 2 * tiles, kf, bias, tabs, n2_len, 1)


def _nt(a, b):
    return lax.dot_general(a, b, (((1,), (1,)), ((), ())), preferred_element_type=F32)


def _tn(a, b):
    return lax.dot_general(a, b, (((0,), (0,)), ((), ())), preferred_element_type=F32)


def _ret_kernel(q_ref, k_ref, v_ref, cos_ref, sin_ref, lg_ref, nw_ref, s0_ref, o_ref, sfin_ref, *, nchunks):
    c = ATT_CHUNK
    lg = lg_ref[0]
    jrow = lax.broadcasted_iota(jnp.int32, (c, c), 0)
    icol = lax.broadcasted_iota(jnp.int32, (c, c), 1)
    jf = jrow.astype(F32)
    dsym = jnp.exp(jnp.abs(jrow - icol).astype(F32) * lg) * jnp.where(jrow == icol, 2.0, 1.0)
    dq_f = jnp.exp((jf + 1.0) * lg)
    dq_b = jnp.exp((c - jf) * lg)
    dk_f = jnp.exp((c - 1.0 - jf) * lg)
    dk_b = jnp.exp(jf * lg)
    dchunk = jnp.exp(c * lg)

    def rope(x, r0):
        return x * cos_ref[pl.ds(r0, c), :] + pltpu.roll(x, RET_DK // 2, axis=1) * sin_ref[pl.ds(r0, c), :]

    def fwd_step(n, s):
        r0 = pl.multiple_of(n * c, c)
        q = rope(q_ref[0, pl.ds(r0, c), :], r0)
        k = rope(k_ref[0, pl.ds(r0, c), :], r0) * (RET_DK ** -0.5)
        v = v_ref[0, pl.ds(r0, c), :].astype(BF16)
        att = _nt(q.astype(BF16), k.astype(BF16)) * dsym
        o = jnp.dot(att.astype(BF16), v, preferred_element_type=F32)
        o = o + jnp.dot((q * dq_f).astype(BF16), s.astype(BF16), preferred_element_type=F32)
        o_ref[0, pl.ds(r0, c), :] = o
        return dchunk * s + _tn((k * dk_f).astype(BF16), v)

    s_f = lax.fori_loop(0, nchunks, fwd_step, s0_ref[0, 0, 0])
    sfin_ref[0, 0, 0] = s_f

    def bwd_step(i, s):
        r0 = pl.multiple_of((nchunks - 1 - i) * c, c)
        q = rope(q_ref[0, pl.ds(r0, c), :], r0)
        k = rope(k_ref[0, pl.ds(r0, c), :], r0) * (RET_DK ** -0.5)
        v = v_ref[0, pl.ds(r0, c), :].astype(BF16)
        o = o_ref[0, pl.ds(r0, c), :] + jnp.dot((q * dq_b).astype(BF16), s.astype(BF16),
                                                preferred_element_type=F32)
        o = o * lax.rsqrt(jnp.mean(o * o, axis=-1, keepdims=True) + EPS)
        o_ref[0, pl.ds(r0, c), :] = o * nw_ref[...]
        return dchunk * s + _tn((k * dk_b).astype(BF16), v)

    sfin_ref[0, 0, 1] = lax.fori_loop(0, nchunks, bwd_step, s0_ref[0, 0, 1])


def _retention(p3, cos2, sin2, norm_w, s0):
    bsz, l, _ = p3.shape
    lgs = jnp.log(1.0 - 2.0 ** (-5.0 - jnp.arange(RET_HEADS, dtype=F32)))
    lgs = jnp.broadcast_to(lgs[:, None, None], (RET_HEADS, 1, LANE))
    tab = pl.BlockSpec((l, LANE), lambda b, h: (0, 0))
    st = pl.BlockSpec((1, 1, 2, RET_DK, LANE), lambda b, h: (b, h, 0, 0, 0))
    return pl.pallas_call(
        functools.partial(_ret_kernel, nchunks=l // ATT_CHUNK),
        grid=(bsz, RET_HEADS),
        in_specs=[pl.BlockSpec((1, l, LANE), lambda b, h: (b, 0, COL_RET_Q + h)),
                  pl.BlockSpec((1, l, LANE), lambda b, h: (b, 0, COL_RET_K + h)),
                  pl.BlockSpec((1, l, LANE), lambda b, h: (b, 0, COL_RET_V + h)),
                  tab, tab,
                  pl.BlockSpec((1, 1, LANE), lambda b, h: (h, 0, 0)),
                  pl.BlockSpec((1, LANE), lambda b, h: (0, h)),
                  st],
        out_specs=[pl.BlockSpec((1, l, LANE), lambda b, h: (b, 0, h)), st],
        out_shape=[jax.ShapeDtypeStruct((bsz, l, BR), F32),
                   jax.ShapeDtypeStruct((bsz, RET_HEADS, 2, RET_DK, LANE), F32)],
        compiler_params=_cparams(("parallel", "parallel")),
        name="retention",
    )(p3, p3, p3, cos2, sin2, lgs, norm_w.reshape(1, BR), s0)


def _rope_tables(ang):
    cos, sin = jnp.cos(ang), jnp.sin(ang)
    return jnp.concatenate([cos, cos], -1), jnp.concatenate([-sin, sin], -1)


def _latent_angles(rows):
    half = RET_DK // 4
    inv = ROPE_BASE ** (-jnp.arange(half, dtype=F32) / half)
    r = jnp.repeat(jnp.arange(rows, dtype=F32), GRID_W)
    cl = jnp.tile(jnp.arange(GRID_W, dtype=F32), rows)
    return jnp.concatenate([r[:, None] * inv, cl[:, None] * inv], axis=-1)


def _ctx_angles(n_ctx):
    n = RET_DK // 2
    inv = ROPE_BASE ** (-jnp.arange(n, dtype=F32) / n)
    return jnp.arange(n_ctx, dtype=F32)[:, None] * inv


def _gla_kernel(q_ref, k_ref, v_ref, lr_ref, gw_ref, gb_ref, nw_ref, s0_ref, o_ref, sfin_ref, *, nchunks):
    c = ATT_CHUNK
    jrow = lax.broadcasted_iota(jnp.int32, (c, c), 0)
    icol = lax.broadcasted_iota(jnp.int32, (c, c), 1)
    tri_f = jnp.where(icol <= jrow, 1.0, 0.0).astype(BF16)
    tri_b = jnp.where(icol >= jrow, 1.0, 0.0).astype(BF16)
    lane = lax.broadcasted_iota(jnp.int32, (1, LANE), 1)
    head0 = lane < GLA_DK

    def decays(r0, d):
        x = jnp.dot(lr_ref[0, pl.ds(r0, c), :].astype(BF16), gw_ref[d], preferred_element_type=F32) + gb_ref[d]
        la = (jnp.minimum(x, 0.0) - jnp.log(1.0 + jnp.exp(-jnp.abs(x)))) * (1.0 / GLA_TAU)
        hi = la.astype(BF16)
        lo = (la - hi.astype(F32)).astype(BF16)
        cs = jnp.dot(tri_f if d == 0 else tri_b, jnp.concatenate([hi, lo], axis=1), preferred_element_type=F32)
        return cs[:, :LANE] + cs[:, LANE:]

    def fwd_step(n, st):
        r0 = pl.multiple_of(n * c, c)
        q = q_ref[0, pl.ds(r0, c), :] * (GLA_DK ** -0.5)
        k = k_ref[0, pl.ds(r0, c), :]
        bf = decays(r0, 0)
        bb = decays(r0, 1)
        ref_f = bf[c // 2 - 1:c // 2, :]
        ref_b = bb[c // 2:c // 2 + 1, :]
        qf = q * jnp.exp(bf - ref_f)
        kf = (k * jnp.exp(ref_f - bf)).astype(BF16)
        qb = q * jnp.exp(bb - ref_b)
        kb = (k * jnp.exp(ref_b - bb)).astype(BF16)
        qs = q * jnp.exp(bf)
        b_last = bf[c - 1:c, :]
        kdec = (k * jnp.exp(b_last - bf)).astype(BF16)
        kvs = []
        for h in range(2):
            hm = head0 if h == 0 else jnp.logical_not(head0)
            v = v_ref[0, pl.ds(r0, c), h * GLA_DV:(h + 1) * GLA_DV].astype(BF16)
            att = (jnp.where(icol <= jrow, _nt(jnp.where(hm, qf, 0.0).astype(BF16), kf), 0.0)
                   + jnp.where(icol >= jrow, _nt(jnp.where(hm, qb, 0.0).astype(BF16), kb), 0.0))
            o = jnp.dot(att.astype(BF16), v, preferred_element_type=F32)
            o = o + _nt(jnp.where(hm, qs, 0.0).astype(BF16), st[h].astype(BF16))
            o_ref[0, pl.ds(r0, c), h * GLA_DV:(h + 1) * GLA_DV] = o
            kvs.append(_tn(v, kdec))
        dec = jnp.exp(b_last)
        return tuple(jnp.where(head0 if h == 0 else jnp.logical_not(head0), dec * st[h] + kvs[h], 0.0)
                     for h in range(2))

    s_f = lax.fori_loop(0, nchunks, fwd_step, (s0_ref[0, 0, 0, 0], s0_ref[0, 0, 0, 1]))
    sfin_ref[0, 0, 0, 0] = s_f[0]
    sfin_ref[0, 0, 0, 1] = s_f[1]

    def bwd_step(i, st):
        r0 = pl.multiple_of((nchunks - 1 - i) * c, c)
        q = q_ref[0, pl.ds(r0, c), :] * (GLA_DK ** -0.5)
        k = k_ref[0, pl.ds(r0, c), :]
        bb = decays(r0, 1)
        qs = q * jnp.exp(bb)
        b_last = bb[0:1, :]
        kdec = (k * jnp.exp(b_last - bb)).astype(BF16)
        kvs = []
        for h in range(2):
            hm = head0 if h == 0 else jnp.logical_not(head0)
            v = v_ref[0, pl.ds(r0, c), h * GLA_DV:(h + 1) * GLA_DV].astype(BF16)
            o = o_ref[0, pl.ds(r0, c), h * GLA_DV:(h + 1) * GLA_DV]
            o = o + _nt(jnp.where(hm, qs, 0.0).astype(BF16), st[h].astype(BF16))
            o = o * lax.rsqrt(jnp.mean(o * o, axis=-1, keepdims=True) + EPS)
            o_ref[0, pl.ds(r0, c), h * GLA_DV:(h + 1) * GLA_DV] = o * nw_ref[:, h * GLA_DV:(h + 1) * GLA_DV]
            kvs.append(_tn(v, kdec))
        dec = jnp.exp(b_last)
        return tuple(jnp.where(head0 if h == 0 else jnp.logical_not(head0), dec * st[h] + kvs[h], 0.0)
                     for h in range(2))

    s_b = lax.fori_loop(0, nchunks, bwd_step, (s0_ref[0, 0, 1, 0], s0_ref[0, 0, 1, 1]))
    sfin_ref[0, 0, 1, 0] = s_b[0]
    sfin_ref[0, 0, 1, 1] = s_b[1]


def _gla(p3, gate_w, gate_b, norm_w, s0):
    bsz, l, _ = p3.shape
    pairs = GLA_HEADS // 2
    gw = jnp.zeros((2, LANE, GLA_HEADS * GLA_DK), F32)
    for d in range(2):
        gw = gw.at[d, d * GLA_LR:(d + 1) * GLA_LR, :].set(gate_w[d].astype(F32))
    st = pl.BlockSpec((1, 1, 2, 2, GLA_DV, LANE), lambda b, h: (b, h, 0, 0, 0, 0))
    return pl.pallas_call(
        functools.partial(_gla_kernel, nchunks=l // ATT_CHUNK),
        grid=(bsz, pairs),
        in_specs=[pl.BlockSpec((1, l, LANE), lambda b, h: (b, 0, COL_GLA_Q + h)),
                  pl.BlockSpec((1, l, LANE), lambda b, h: (b, 0, COL_GLA_K + h)),
                  pl.BlockSpec((1, l, 2 * GLA_DV), lambda b, h: (b, 0, COL_GLA_V // 2 + h)),
                  pl.BlockSpec((1, l, LANE), lambda b, h: (b, 0, COL_GLA_LR)),
                  pl.BlockSpec((2, LANE, LANE), lambda b, h: (0, 0, h)),
                  pl.BlockSpec((2, 1, LANE), lambda b, h: (0, 0, h)),
                  pl.BlockSpec((1, 2 * GLA_DV), lambda b, h: (0, h)),
                  st],
        out_specs=[pl.BlockSpec((1, l, 2 * GLA_DV), lambda b, h: (b, 0, h)), st],
        out_shape=[jax.ShapeDtypeStruct((bsz, l, BR), F32),
                   jax.ShapeDtypeStruct((bsz, pairs, 2, 2, GLA_DV, LANE), F32)],
        compiler_params=_cparams(("parallel", "parallel")),
        name="gla",
    )(p3, p3, p3, p3, gw.astype(BF16), gate_b.astype(F32).reshape(2, 1, GLA_HEADS * GLA_DK),
      norm_w.reshape(1, BR), s0)


def _mixers(p2, bsz, prm, i, rope, s5_h0, ret_s0, gla_s0, with_hyena):
    m = p2.shape[0]
    l = m // bsz
    p3 = p2.reshape(bsz, l, NP)
    tabs = _s5_tables(prm['s5_a_re'][i], prm['s5_a_im'][i], prm['s5_log_dt'][i], prm['s5_b_re'][i],
                      prm['s5_b_im'][i], prm['s5_c_re'][i], prm['s5_c_im'][i], l // (2 * S5_T))
    y, s5_fin = _s5_core(p3[:, :, COL_S5_U * LANE:COL_S5_U * LANE + BR], tabs, s5_h0)
    s5 = _s5_glu(y.reshape(m, BR), p2, prm['s5_d'][i], prm['s5_glu_w'][i], prm['s5_glu_b'][i])
    hy = None
    if with_hyena:
        filt = tuple(prm[n][i] for n in ('hy_w1', 'hy_b1', 'hy_f1', 'hy_w2', 'hy_b2', 'hy_f2',
                                         'hy_w3', 'hy_b3', 'hy_f3', 'hy_w4'))
        hy = _hyena(p3, prm['hy_conv_w'][i], prm['hy_conv_b'][i], filt, prm['hy_bias'][i].astype(F32),
                    64 if l >= 512 else 16).reshape(m, BR)
    ret, ret_fin = _retention(p3, rope[0], rope[1], prm['ret_norm_w'][i], ret_s0)
    gla, gla_fin = _gla(p3, prm['gla_gate_w'][i], prm['gla_gate_b'][i], prm['gla_norm_w'][i], gla_s0)
    return (s5, hy, ret.reshape(m, BR), gla.reshape(m, BR)), (s5_fin, ret_fin, gla_fin)


def kernel(x, c, ctx, c_ctx, norm_w, ada_w, ada_b, w_in, w_out, s5_a_re, s5_a_im, s5_log_dt, s5_b_re, s5_b_im, s5_c_re, s5_c_im, s5_d, s5_glu_w, s5_glu_b, hy_conv_w, hy_conv_b, hy_w1, hy_b1, hy_f1, hy_w2, hy_b2, hy_f2, hy_w3, hy_b3, hy_f3, hy_w4, hy_bias, ret_norm_w, gla_gate_w, gla_gate_b, gla_norm_w, final_norm_w):
    prm = dict(s5_a_re=s5_a_re, s5_a_im=s5_a_im, s5_log_dt=s5_log_dt, s5_b_re=s5_b_re, s5_b_im=s5_b_im,
               s5_c_re=s5_c_re, s5_c_im=s5_c_im, s5_d=s5_d, s5_glu_w=s5_glu_w, s5_glu_b=s5_glu_b,
               hy_conv_w=hy_conv_w, hy_conv_b=hy_conv_b, hy_w1=hy_w1, hy_b1=hy_b1, hy_f1=hy_f1,
               hy_w2=hy_w2, hy_b2=hy_b2, hy_f2=hy_f2, hy_w3=hy_w3, hy_b3=hy_b3, hy_f3=hy_f3,
               hy_w4=hy_w4, hy_bias=hy_bias, ret_norm_w=ret_norm_w, gla_gate_w=gla_gate_w,
               gla_gate_b=gla_gate_b, gla_norm_w=gla_norm_w)
    bsz, l, d = x.shape
    lc = ctx.shape[1]
    depth = w_in.shape[0]
    cc = jnp.concatenate([c, c_ctx[None, :], jnp.zeros((8 - bsz - 1, d), F32)], axis=0)
    mod = _ada(cc, ada_w, ada_b)
    rope_l = _rope_tables(_latent_angles(l // GRID_W))
    rope_c = _rope_tables(_ctx_angles(lc))
    s5_zero = jnp.zeros((S5_GROUPS // 2, 2, 2, 8, 2 * S5_STATE), F32)
    ret_zero = jnp.zeros((bsz, RET_HEADS, 2, RET_DK, LANE), F32)
    gla_zero = jnp.zeros((bsz, GLA_HEADS // 2, 2, 2, GLA_DV, LANE), F32)

    x_l = x.reshape(bsz * l, d)
    x_c = ctx.reshape(bsz * lc, d)
    for i in range(depth):
        last = i == depth - 1
        sh, sc, gt = mod[i, :, :d], mod[i, :, d:2 * d], mod[i, :, 2 * d:]
        w = _permute_w_in(w_in[i])
        wo = w_out[i].astype(BF16)
        p_c = _inproj(x_c, norm_w[i], sc[bsz:bsz + 1, None, :], sh[bsz:bsz + 1, None, :], w, bsz * lc)
        p_l = _inproj(x_l, norm_w[i], sc[:bsz, None, :], sh[:bsz, None, :], w, l)
        ys_c, (s5_fin, ret_fin, gla_fin) = _mixers(p_c, bsz, prm, i, rope_c, s5_zero, ret_zero, gla_zero,
                                                   with_hyena=not last)
        ys_l, _ = _mixers(p_l, bsz, prm, i, rope_l, _s5_next_h0(s5_fin), ret_fin, gla_fin, with_hyena=True)
        x_l = _outproj(x_l, ys_l, p_l, wo, gt[:bsz, None, :], final_norm_w, l, final=last)
        if not last:
            x_c = _outproj(x_c, ys_c, p_c, wo, gt[bsz:bsz + 1, None, :], final_norm_w, bsz * lc, final=False)
    return x_l.reshape(bsz, l, d)
```

```python
import functools
import math

import numpy as np
import jax
import jax.numpy as jnp
from jax import lax
from jax.experimental import pallas as pl
from jax.experimental.pallas import tpu as pltpu

F32 = jnp.float32
BF16 = jnp.bfloat16

EPS = 1e-6
GRID_W = 64
BR = 512
S5_GSIZE = 16
S5_GROUPS = BR // S5_GSIZE
S5_STATE = 64
S5_T = 16
HY_ORDER = 2
HY_EMB = 33
HY_BANDS = (HY_EMB - 1) // 2
HY_FAST_DECAY = 0.3
HY_SLOW_DECAY = 1.5
HY_TARGET = 1e-2
RET_HEADS = 4
RET_DK = 128
GLA_HEADS = 4
GLA_DK = 64
GLA_DV = 128
GLA_LR = 16
GLA_TAU = 16.0
ROPE_BASE = 10000.0
ATT_CHUNK = 128
_FFT_UNROLL = 8
_ATT_UNROLL = 2

LANE = 128
VMEM_LIMIT = 56 * 1024 * 1024

COL_S5_U = 0
COL_RET_K = 4
COL_RET_V = 8
COL_GLA_V = 12
COL_GATES = 16
COL_HY_X = 32
COL_RET_Q = 44
COL_GLA_Q = 48
COL_GLA_K = 50
COL_GLA_LR = 52
NP_UNITS = 54
NP = NP_UNITS * LANE
_SEG_DST = (
    (BR, COL_S5_U), (BR, COL_GATES),
    (3 * BR, COL_HY_X), (BR, COL_GATES + 4),
    (BR, COL_RET_Q), (BR, COL_RET_K), (BR, COL_RET_V), (BR, COL_GATES + 8),
    (BR // 2, COL_GLA_Q), (BR // 2, COL_GLA_K), (BR, COL_GLA_V), (2 * GLA_LR, COL_GLA_LR),
    (BR, COL_GATES + 12),
)


def _cparams(sem):
    return pltpu.CompilerParams(dimension_semantics=sem, vmem_limit_bytes=VMEM_LIMIT)


def _silu(x):
    return x * jax.nn.sigmoid(x)


def _permute_w_in(w):
    d = w.shape[0]
    out = jnp.zeros((d, NP), BF16)
    off = 0
    for size, unit in _SEG_DST:
        out = lax.dynamic_update_slice(out, w[:, off:off + size].astype(BF16), (0, unit * LANE))
        off += size
    return out


def _ada_kernel(c_ref, w_ref, b_ref, o_ref):
    a = _silu(c_ref[...]).astype(BF16)
    o_ref[0] = jnp.dot(a, w_ref[0].astype(BF16), preferred_element_type=F32) + b_ref[0]


def _ada(cc, ada_w, ada_b, tn=768):
    depth, d, n = ada_w.shape
    return pl.pallas_call(
        _ada_kernel,
        grid=(depth, n // tn),
        in_specs=[pl.BlockSpec((8, d), lambda i, j: (0, 0)),
                  pl.BlockSpec((1, d, tn), lambda i, j: (i, 0, j)),
                  pl.BlockSpec((1, 1, tn), lambda i, j: (i, 0, j))],
        out_specs=pl.BlockSpec((1, 8, tn), lambda i, j: (i, 0, j)),
        out_shape=jax.ShapeDtypeStruct((depth, 8, n), F32),
        compiler_params=_cparams(("parallel", "parallel")),
        name="ada_mod",
    )(cc, ada_w, ada_b.reshape(depth, 1, n))


_S5_SLOT = 2 * S5_GSIZE
_S5_PAIRS = S5_GROUPS // 2
_S5_ROW = S5_T * _S5_SLOT


def _s5_lane_perm():
    perm = np.zeros((4, _S5_ROW), np.int32)
    for s in range(4):
        for tau in range(S5_T):
            for c in range(_S5_SLOT):
                gl, i = divmod(c, S5_GSIZE)
                perm[s, LANE * (tau // 4) + _S5_SLOT * ((s + tau) % 4) + c] = (gl * S5_T + tau) * S5_GSIZE + i
    return perm


def _s5_to_chunk_rows(nat_ref, xs_ref, rows16):
    slot = lax.broadcasted_iota(jnp.int32, (rows16, LANE), 1) // _S5_SLOT
    for q in range(4):
        for qd in range(4):
            rolled = []
            for t in range(4):
                piece = nat_ref[q, pl.ds(4 * qd + t, rows16, stride=S5_T), :]
                rolled.append(pltpu.roll(piece, _S5_SLOT * t, axis=1) if t else piece)
            for s in range(4):
                acc = rolled[0]
                for t in range(1, 4):
                    acc = jnp.where(slot == (s + t) % 4, rolled[t], acc)
                xs_ref[4 * q + s, :, qd * LANE:(qd + 1) * LANE] = acc.astype(xs_ref.dtype)


def _s5_from_chunk_rows(y_ref, nat_ref, rows16):
    slot = lax.broadcasted_iota(jnp.int32, (rows16, LANE), 1) // _S5_SLOT
    for q in range(4):
        for qd in range(4):
            src = [y_ref[4 * q + s, :, qd * LANE:(qd + 1) * LANE] for s in range(4)]
            for t in range(4):
                acc = src[0]
                for s in range(1, 4):
                    acc = jnp.where(slot == (s + t) % 4, src[s], acc)
                if t:
                    acc = pltpu.roll(acc, LANE - _S5_SLOT * t, axis=1)
                nat_ref[q, pl.ds(4 * qd + t, rows16, stride=S5_T), :] = acc


def _inproj_kernel(x_ref, nw_ref, sc_ref, sh_ref, w_ref, o_ref, xs_ref, h_ref, nat_ref, *, rows16):
    j = pl.program_id(1)

    @pl.when(j == 0)
    def _():
        x = x_ref[...]
        y = x * lax.rsqrt(jnp.mean(x * x, axis=-1, keepdims=True) + EPS) * nw_ref[...]
        h_ref[...] = (y * (1.0 + sc_ref[0]) + sh_ref[0]).astype(BF16)

    res = jnp.dot(h_ref[...], w_ref[...], preferred_element_type=F32)
    o_ref[...] = res

    @pl.when(j == 0)
    def _():
        for q in range(BR // LANE):
            nat_ref[q] = res[:, q * LANE:(q + 1) * LANE]
        _s5_to_chunk_rows(nat_ref, xs_ref, rows16)


def _inproj(x2, nw, sc, sh, w, rows_per_mod, tm=1024, tn=768):
    m, d = x2.shape
    tm = min(tm, rows_per_mod)
    per = rows_per_mod // tm
    rows16 = tm // S5_T
    assert COL_S5_U == 0 and tn >= BR
    return pl.pallas_call(
        functools.partial(_inproj_kernel, rows16=rows16),
        grid=(m // tm, NP // tn),
        in_specs=[pl.BlockSpec((tm, d), lambda i, j: (i, 0)),
                  pl.BlockSpec((1, d), lambda i, j: (0, 0)),
                  pl.BlockSpec((1, 1, d), lambda i, j: (i // per, 0, 0)),
                  pl.BlockSpec((1, 1, d), lambda i, j: (i // per, 0, 0)),
                  pl.BlockSpec((d, tn), lambda i, j: (0, j))],
        out_specs=[pl.BlockSpec((tm, tn), lambda i, j: (i, j)),
                   pl.BlockSpec((_S5_PAIRS, rows16, _S5_ROW), lambda i, j: (0, i, 0))],
        out_shape=[jax.ShapeDtypeStruct((m, NP), F32),
                   jax.ShapeDtypeStruct((_S5_PAIRS, m // S5_T, _S5_ROW), BF16)],
        scratch_shapes=[pltpu.VMEM((tm, d), BF16), pltpu.VMEM((BR // LANE, tm, LANE), F32)],
        compiler_params=_cparams(("parallel", "arbitrary")),
        name="inproj",
    )(x2, nw.reshape(1, d), sc, sh, w)


def _outproj_kernel(x_ref, y0_ref, y1_ref, y2_ref, y3_ref, g_ref, w_ref, gt_ref, fw_ref, o_ref, *, final):
    acc = None
    for k, y_ref in enumerate((y0_ref, y1_ref, y2_ref, y3_ref)):
        gated = (y_ref[...] * _silu(g_ref[:, k * BR:(k + 1) * BR])).astype(BF16)
        part = jnp.dot(gated, w_ref[k * BR:(k + 1) * BR, :], preferred_element_type=F32)
        acc = part if acc is None else acc + part
    x = x_ref[...] + gt_ref[0] * acc
    if final:
        x = x * lax.rsqrt(jnp.mean(x * x, axis=-1, keepdims=True) + EPS) * fw_ref[...]
    o_ref[...] = x


def _outproj(x2, ys, p, w_out, gt, fw, rows_per_mod, final, tm=256):
    m, d = x2.shape
    tm = min(tm, rows_per_mod)
    per = rows_per_mod // tm
    gates_blk = COL_GATES * LANE // (4 * BR)
    yspec = pl.BlockSpec((tm, BR), lambda i: (i, 0))
    return pl.pallas_call(
        functools.partial(_outproj_kernel, final=final),
        grid=(m // tm,),
        in_specs=[pl.BlockSpec((tm, d), lambda i: (i, 0)), yspec, yspec, yspec, yspec,
                  pl.BlockSpec((tm, 4 * BR), lambda i: (i, gates_blk)),
                  pl.BlockSpec((4 * BR, d), lambda i: (0, 0)),
                  pl.BlockSpec((1, 1, d), lambda i: (i // per, 0, 0)),
                  pl.BlockSpec((1, d), lambda i: (0, 0))],
        out_specs=pl.BlockSpec((tm, d), lambda i: (i, 0)),
        out_shape=jax.ShapeDtypeStruct((m, d), F32),
        compiler_params=_cparams(("parallel",)),
        name="outproj",
    )(x2, *ys, p, w_out, gt, fw.reshape(1, d))


def _s5_tables(a_re, a_im, log_dt, b_re, b_im, c_re, c_im, nch):
    hp = lax.Precision.HIGHEST
    t_len, g_n, p_n, s_n = S5_T, S5_GROUPS, S5_STATE, S5_GSIZE
    a_re, a_im = a_re.astype(F32), a_im.astype(F32)
    dt = jnp.exp(log_dt.astype(F32))[:, :, None]
    lam_re, lam_im = a_re * dt, a_im * dt

    def power(tau):
        tau = tau.astype(F32)[:, None, None, None]
        mag = jnp.exp(lam_re[None] * tau)
        return mag * jnp.cos(lam_im[None] * tau), mag * jnp.sin(lam_im[None] * tau)

    ab_re, ab_im = power(jnp.ones((1,)))
    ab_re, ab_im = ab_re[0], ab_im[0]
    den = a_re * a_re + a_im * a_im
    nr = ab_re - 1.0
    co_re = (nr * a_re + ab_im * a_im) / den
    co_im = (ab_im * a_re - nr * a_im) / den
    b_re, b_im = b_re.astype(F32), b_im.astype(F32)
    bco_re = co_re[..., None] * b_re - co_im[..., None] * b_im
    bco_im = co_re[..., None] * b_im + co_im[..., None] * b_re
    c_re, c_im = c_re.astype(F32), c_im.astype(F32)

    pr, pi = power(jnp.arange(t_len + 1))
    ca_re = c_re[None] * pr[:, :, :, None, :] - c_im[None] * pi[:, :, :, None, :]
    ca_im = c_re[None] * pi[:, :, :, None, :] + c_im[None] * pr[:, :, :, None, :]
    kk = (jnp.einsum('tdgip,dgpj->tdgij', ca_re[:t_len], bco_re, precision=hp)
          - jnp.einsum('tdgip,dgpj->tdgij', ca_im[:t_len], bco_im, precision=hp))
    kf, kb = kk[:, 0], kk[:, 1]
    kfull = jnp.concatenate([kb[:0:-1], (kf[0] + kb[0])[None], kf[1:]], axis=0)
    tt = jnp.arange(t_len)
    idx = tt[None, :] - tt[:, None] + (t_len - 1)
    mm = kfull[idx]
    mmat = mm.transpose(2, 0, 4, 1, 3).reshape(g_n, t_len * s_n, t_len * s_n)

    wexp = jnp.stack([pr[t_len - 1 - tt, 0], pr[tt, 1]], 0), jnp.stack([pi[t_len - 1 - tt, 0], pi[tt, 1]], 0)
    w_re = wexp[0][..., None] * bco_re[:, None] - wexp[1][..., None] * bco_im[:, None]
    w_im = wexp[0][..., None] * bco_im[:, None] + wexp[1][..., None] * bco_re[:, None]
    w4 = jnp.stack([w_re[0], w_im[0], w_re[1], w_im[1]], 0)
    w4 = w4.transpose(0, 2, 1, 4, 3).reshape(4, g_n // 2, 2, t_len * s_n, p_n)
    eye2 = jnp.eye(2, dtype=F32)
    wpair = jnp.einsum('kqlrp,lm->qlrkmp', w4, eye2).reshape(g_n // 2, 2 * t_len * s_n, 4 * 2 * p_n)

    vexp_f, vexp_b = tt + 1, t_len - tt
    v4 = jnp.stack([ca_re[vexp_f, 0], -ca_im[vexp_f, 0], ca_re[vexp_b, 1], -ca_im[vexp_b, 1]], 0)
    v4 = v4.transpose(0, 2, 4, 1, 3).reshape(4, g_n // 2, 2, p_n, t_len * s_n)
    vpair = jnp.einsum('kqlpr,lm->qklpmr', v4, eye2).reshape(g_n // 2, 4 * 2 * p_n, 2 * t_len * s_n)

    qr, qi = power(t_len * jnp.arange(nch + 1))
    pw = jnp.stack([qr, qi], 0)
    pw = pw.reshape(2, nch + 1, 2, g_n // 2, 2 * p_n).transpose(3, 2, 0, 1, 4)
    mpair = jnp.einsum('qlrc,lm->qlrmc', mmat.reshape(g_n // 2, 2, t_len * s_n, t_len * s_n), eye2)
    mpair = mpair.reshape(g_n // 2, _S5_ROW, _S5_ROW)
    perm = jnp.asarray(_s5_lane_perm()[np.arange(g_n // 2) % 4])
    mpair = jnp.take_along_axis(jnp.take_along_axis(mpair, perm[:, :, None], axis=1), perm[:, None, :], axis=2)
    wpair = jnp.take_along_axis(wpair, perm[:, :, None], axis=1)
    vpair = jnp.take_along_axis(vpair, perm[:, None, :], axis=2)
    return mpair.astype(BF16), wpair.astype(BF16), vpair.astype(BF16), pw


def _cmul(ar, ai, br, bi):
    return ar * br - ai * bi, ar * bi + ai * br


def _s5_kernel(x_ref, m_ref, w_ref, v_ref, pw_ref, h0_ref, y_ref, fin_ref, s_ref, h_ref, *, nch):
    ln = 2 * S5_STATE
    x = x_ref[0]
    s_all = jnp.dot(x, w_ref[0], preferred_element_type=F32)
    for k in range(4):
        s_ref[k] = s_all[:, k * ln:(k + 1) * ln]
    a_rf, a_if = pw_ref[0, 0, 0, 1:2, :], pw_ref[0, 0, 1, 1:2, :]
    a_rb, a_ib = pw_ref[0, 1, 0, 1:2, :], pw_ref[0, 1, 1, 1:2, :]

    def tile(c):
        return pl.ds(c, 8, stride=nch)

    def scan_step(s, carry):
        hrf, hif, hrb, hib = carry
        rf, rb = tile(s), tile(nch - 1 - s)
        h_ref[0, rf, :] = hrf
        h_ref[1, rf, :] = hif
        h_ref[2, rb, :] = hrb
        h_ref[3, rb, :] = hib
        pr, pi = _cmul(a_rf, a_if, hrf, hif)
        qr, qi = _cmul(a_rb, a_ib, hrb, hib)
        return (pr + s_ref[0, rf, :], pi + s_ref[1, rf, :], qr + s_ref[2, rb, :], qi + s_ref[3, rb, :])

    init = (h0_ref[0, 0, 0], h0_ref[0, 0, 1], h0_ref[0, 1, 0], h0_ref[0, 1, 1])
    hrf, hif, hrb, hib = lax.fori_loop(0, nch, scan_step, init, unroll=2)

    odd = (lax.broadcasted_iota(jnp.int32, (8, ln), 0) % 2) == 1
    crf = jnp.where(odd, pltpu.roll(hrf, 1, axis=0), 0.0)
    cif = jnp.where(odd, pltpu.roll(hif, 1, axis=0), 0.0)
    crb = jnp.where(odd, 0.0, pltpu.roll(hrb, 7, axis=0))
    cib = jnp.where(odd, 0.0, pltpu.roll(hib, 7, axis=0))

    def fix_step(s, _):
        rf, rb = tile(s), tile(nch - 1 - s)
        pr, pi = _cmul(pw_ref[0, 0, 0, pl.ds(s, 1), :], pw_ref[0, 0, 1, pl.ds(s, 1), :], crf, cif)
        qr, qi = _cmul(pw_ref[0, 1, 0, pl.ds(s, 1), :], pw_ref[0, 1, 1, pl.ds(s, 1), :], crb, cib)
        h_ref[0, rf, :] = h_ref[0, rf, :] + pr
        h_ref[1, rf, :] = h_ref[1, rf, :] + pi
        h_ref[2, rb, :] = h_ref[2, rb, :] + qr
        h_ref[3, rb, :] = h_ref[3, rb, :] + qi
        return 0

    lax.fori_loop(0, nch, fix_step, 0, unroll=2)
    pr, pi = _cmul(pw_ref[0, 0, 0, nch:nch + 1, :], pw_ref[0, 0, 1, nch:nch + 1, :], crf, cif)
    qr, qi = _cmul(pw_ref[0, 1, 0, nch:nch + 1, :], pw_ref[0, 1, 1, nch:nch + 1, :], crb, cib)
    fin_ref[0, 0, 0] = hrf + pr
    fin_ref[0, 0, 1] = hif + pi
    fin_ref[0, 1, 0] = hrb + qr
    fin_ref[0, 1, 1] = hib + qi

    h_all = jnp.concatenate([h_ref[k] for k in range(4)], axis=1).astype(BF16)
    y_ref[0] = (jnp.dot(x, m_ref[0], preferred_element_type=F32)
                + jnp.dot(h_all, v_ref[0], preferred_element_type=F32))


def _s5_core(xs, tables, h0):
    mpair, wpair, vpair, pw = tables
    gp_n, rows, wdt = xs.shape
    nch = rows // 8
    return pl.pallas_call(
        functools.partial(_s5_kernel, nch=nch),
        grid=(gp_n,),
        in_specs=[pl.BlockSpec((1, rows, wdt), lambda g: (g, 0, 0)),
                  pl.BlockSpec((1, wdt, wdt), lambda g: (g, 0, 0)),
                  pl.BlockSpec((1, wdt, wdt), lambda g: (g, 0, 0)),
                  pl.BlockSpec((1, wdt, wdt), lambda g: (g, 0, 0)),
                  pl.BlockSpec((1, 2, 2, nch + 1, 2 * S5_STATE), lambda g: (g, 0, 0, 0, 0)),
                  pl.BlockSpec((1, 2, 2, 8, 2 * S5_STATE), lambda g: (g, 0, 0, 0, 0))],
        out_specs=[pl.BlockSpec((1, rows, wdt), lambda g: (g, 0, 0)),
                   pl.BlockSpec((1, 2, 2, 8, 2 * S5_STATE), lambda g: (g, 0, 0, 0, 0))],
        out_shape=[jax.ShapeDtypeStruct((gp_n, rows, wdt), F32),
                   jax.ShapeDtypeStruct((gp_n, 2, 2, 8, 2 * S5_STATE), F32)],
        scratch_shapes=[pltpu.VMEM((4, rows, 2 * S5_STATE), F32), pltpu.VMEM((4, rows, 2 * S5_STATE), F32)],
        compiler_params=_cparams(("parallel",)),
        name="s5_core",
    )(xs, mpair, wpair, vpair, pw, h0)


def _s5_next_h0(fin):
    g, d, r, _, n = fin.shape
    sw = fin.reshape(g, d, r, 4, 2, n)[:, :, :, :, ::-1, :]
    keep = jnp.array([[1.0, 0.0], [0.0, 1.0]], F32)[None, :, None, None, :, None]
    return (sw * keep).reshape(fin.shape)


def _s5_glu_kernel(y_ref, u_ref, d_ref, w_ref, b_ref, o_ref, nat_ref, *, rows16):
    _s5_from_chunk_rows(y_ref, nat_ref, rows16)
    y = jnp.concatenate([nat_ref[q] for q in range(BR // LANE)], axis=1) + u_ref[...] * d_ref[...]
    g = jax.nn.gelu(y)
    z = jnp.dot(g.astype(BF16), w_ref[...], preferred_element_type=F32) + b_ref[...]
    o_ref[...] = g * jax.nn.sigmoid(z)


def _s5_glu(ys, p, d_skip, glu_w, glu_b, tm=512):
    m = p.shape[0]
    tm = min(tm, m)
    rows16 = tm // S5_T
    return pl.pallas_call(
        functools.partial(_s5_glu_kernel, rows16=rows16),
        grid=(m // tm,),
        in_specs=[pl.BlockSpec((_S5_PAIRS, rows16, _S5_ROW), lambda i: (0, i, 0)),
                  pl.BlockSpec((tm, BR), lambda i: (i, COL_S5_U * LANE // BR)),
                  pl.BlockSpec((1, BR), lambda i: (0, 0)),
                  pl.BlockSpec((BR, BR), lambda i: (0, 0)),
                  pl.BlockSpec((1, BR), lambda i: (0, 0))],
        out_specs=pl.BlockSpec((tm, BR), lambda i: (i, 0)),
        out_shape=jax.ShapeDtypeStruct((m, BR), F32),
        scratch_shapes=[pltpu.VMEM((BR // LANE, tm, LANE), F32)],
        compiler_params=_cparams(("parallel",)),
        name="s5_glu",
    )(ys, p, d_skip.reshape(1, BR), glu_w.astype(BF16), glu_b.reshape(1, BR))


@functools.lru_cache(maxsize=None)
def _fft_tables_np(n_total, n2_len):
    n1_len = n_total // n2_len
    n1h = n1_len // 2
    k1 = np.arange(n1_len, dtype=np.int64)
    n2 = np.arange(n2_len, dtype=np.int64)
    m = (k1[None, :, None] * k1[None, None, :] * n2_len + n2[:, None, None] * k1[None, :, None]) % n_total
    ang = m.astype(np.float64) * (2.0 * math.pi / n_total)
    c, s = np.cos(ang), np.sin(ang)
    ch, sh = c[:, :, :n1h], s[:, :, :n1h]
    gf = np.concatenate([np.concatenate([ch, sh], 2), np.concatenate([-sh, ch], 2)], 1)
    gk = np.concatenate([c, -s], 1)
    cht, sht = ch.transpose(0, 2, 1), sh.transpose(0, 2, 1)
    gi = np.concatenate([np.concatenate([cht, -sht], 2), np.concatenate([sht, cht], 2)], 1)
    a2 = ((n2[:, None] * n2[None, :]) % n2_len).astype(np.float64) * (2.0 * math.pi / n2_len)
    c2, s2 = np.cos(a2), np.sin(a2)
    f2 = np.concatenate([np.concatenate([c2, s2], 1), np.concatenate([-s2, c2], 1)], 0)
    f2c = np.concatenate([np.concatenate([c2, -s2], 1), np.concatenate([s2, c2], 1)], 0)
    return tuple(np.asarray(t, np.float32) for t in (gf, gk, gi, f2, f2c))


def _fft_tables(n_total, n2_len):
    return tuple(jnp.asarray(t).astype(BF16) for t in _fft_tables_np(n_total, n2_len))


def _hy_shortconv_kernel(x_ref, w_ref, b_ref, o_ref):
    x = x_ref[0]
    l = x.shape[0]
    row = lax.broadcasted_iota(jnp.int32, x.shape, 0)
    prev = jnp.where(row == 0, 0.0, pltpu.roll(x, 1, axis=0))
    nxt = jnp.where(row == l - 1, 0.0, pltpu.roll(x, l - 1, axis=0))
    o_ref[0] = w_ref[0:1, :] * prev + w_ref[1:2, :] * x + w_ref[2:3, :] * nxt + b_ref[...]


def _hy_shortconv(p3, conv_w, conv_b):
    bsz, l, _ = p3.shape
    units = 3 * BR // LANE
    return pl.pallas_call(
        _hy_shortconv_kernel,
        grid=(bsz, units),
        in_specs=[pl.BlockSpec((1, l, LANE), lambda b, u: (b, 0, COL_HY_X + u)),
                  pl.BlockSpec((3, LANE), lambda b, u: (0, u)),
                  pl.BlockSpec((1, LANE), lambda b, u: (0, u))],
        out_specs=pl.BlockSpec((1, l, LANE), lambda b, u: (b, 0, u)),
        out_shape=jax.ShapeDtypeStruct((bsz, l, 3 * BR), F32),
        compiler_params=_cparams(("parallel", "parallel")),
        name="hy_shortconv",
    )(p3, conv_w, conv_b.reshape(1, 3 * BR))


def _fft_stage2(mid_ref, f2_ref, n1_len, n2_len, emit):
    def body(kp, _):
        def rhs(k1):
            return jnp.concatenate([mid_ref[pl.ds(k1, n2_len, stride=2 * n1_len), :],
                                    mid_ref[pl.ds(n1_len + k1, n2_len, stride=2 * n1_len), :]], axis=0)
        r = jnp.concatenate([rhs(2 * kp), rhs(2 * kp + 1)], axis=1).astype(BF16)
        xs = jnp.dot(f2_ref[...], r, preferred_element_type=F32)
        emit(2 * kp, xs[:, :LANE])
        emit(2 * kp + 1, xs[:, LANE:])
        return 0
    lax.fori_loop(0, n1_len // 2, body, 0, unroll=_FFT_UNROLL)


def _hy_kf_kernel(k_ref, gk_ref, f2_ref, o_ref, mid_ref, *, n1_len, n2_len):
    def stage1(n2, _):
        rhs = k_ref[pl.ds(n2, n1_len, stride=n2_len), :].astype(BF16)
        r0 = pl.multiple_of(n2 * 2 * n1_len, 2 * n1_len)
        mid_ref[pl.ds(r0, 2 * n1_len), :] = jnp.dot(gk_ref[n2], rhs, preferred_element_type=F32)
        return 0
    lax.fori_loop(0, n2_len, stage1, 0, unroll=_FFT_UNROLL)
    scale = 1.0 / (n1_len * n2_len)

    def emit(k1, xs):
        o_ref[0, k1] = (xs * scale).astype(BF16)
    _fft_stage2(mid_ref, f2_ref, n1_len, n2_len, emit)


def _hy_kf(k2d, gk, f2, n2_len):
    n_total, cols = k2d.shape
    n1_len = n_total // n2_len
    tiles = cols // LANE
    const = lambda *_: (0, 0, 0)
    return pl.pallas_call(
        functools.partial(_hy_kf_kernel, n1_len=n1_len, n2_len=n2_len),
        grid=(tiles,),
        in_specs=[pl.BlockSpec((n_total, LANE), lambda t: (0, t)),
                  pl.BlockSpec(gk.shape, const),
                  pl.BlockSpec(f2.shape, lambda t: (0, 0))],
        out_specs=pl.BlockSpec((1, n1_len, 2 * n2_len, LANE), lambda t: (t, 0, 0, 0)),
        out_shape=jax.ShapeDtypeStruct((tiles, n1_len, 2 * n2_len, LANE), BF16),
        scratch_shapes=[pltpu.VMEM((2 * n_total, LANE), F32)],
        compiler_params=_cparams(("parallel",)),
        name="hy_kf",
    )(k2d, gk, f2)


def _hy_conv_kernel(z_ref, g_ref, kf_ref, bias_ref, gf_ref, gi_ref, f2_ref, f2c_ref, o_ref,
                    mid_ref, spec_ref, *, n1_len, n2_len, order):
    n1h = n1_len // 2

    def stage1(n2, _):
        rhs = jnp.concatenate([z_ref[0, pl.ds(n2, n1h, stride=n2_len), :],
                               z_ref[1, pl.ds(n2, n1h, stride=n2_len), :]], axis=0).astype(BF16)
        r0 = pl.multiple_of(n2 * 2 * n1_len, 2 * n1_len)
        mid_ref[pl.ds(r0, 2 * n1_len), :] = jnp.dot(gf_ref[n2], rhs, preferred_element_type=F32)
        return 0
    lax.fori_loop(0, n2_len, stage1, 0, unroll=_FFT_UNROLL)

    def emit(k1, xs):
        kf = kf_ref[0, k1].astype(F32)
        xr, xi = xs[:n2_len], xs[n2_len:]
        kr, ki = kf[:n2_len], kf[n2_len:]
        spec_ref[k1] = jnp.concatenate([xr * kr - xi * ki, xr * ki + xi * kr], axis=0).astype(BF16)
    _fft_stage2(mid_ref, f2_ref, n1_len, n2_len, emit)

    def stage2i(kp, _):
        r = jnp.concatenate([spec_ref[2 * kp], spec_ref[2 * kp + 1]], axis=1)
        b = jnp.dot(f2c_ref[...], r, preferred_element_type=F32)
        r0 = pl.multiple_of(kp * 4 * n2_len, 4 * n2_len)
        mid_ref[pl.ds(r0, 2 * n2_len), :] = b[:, :LANE]
        mid_ref[pl.ds(r0 + 2 * n2_len, 2 * n2_len), :] = b[:, LANE:]
        return 0
    lax.fori_loop(0, n1_len // 2, stage2i, 0, unroll=_FFT_UNROLL)

    def stage1i(n2, _):
        rhs = jnp.concatenate([mid_ref[pl.ds(n2, n1_len, stride=2 * n2_len), :],
                               mid_ref[pl.ds(n2_len + n2, n1_len, stride=2 * n2_len), :]], axis=0).astype(BF16)
        o = jnp.dot(gi_ref[n2], rhs, preferred_element_type=F32)
        o_ref[0, pl.ds(n2, n1h, stride=n2_len), :] = o[:n1h]
        o_ref[1, pl.ds(n2, n1h, stride=n2_len), :] = o[n1h:]
        return 0
    lax.fori_loop(0, n2_len, stage1i, 0, unroll=_FFT_UNROLL)

    bias = bias_ref[order:order + 1, :]
    for j in range(2):
        o_ref[j] = g_ref[j] * (o_ref[j] + bias * z_ref[j])


def _hy_conv(z, zcol, g, gcol, kf, bias, tabs, n2_len, order):
    gf, _, gi, f2, f2c = tabs
    bsz, l, _ = z.shape
    n1_len = 2 * l // n2_len
    tiles = BR // LANE
    const2 = lambda *_: (0, 0)
    const3 = lambda *_: (0, 0, 0)
    one = pl.Buffered(1)
    return pl.pallas_call(
        functools.partial(_hy_conv_kernel, n1_len=n1_len, n2_len=n2_len, order=order),
        grid=(tiles, bsz // 2),
        in_specs=[pl.BlockSpec((2, l, LANE), lambda t, b: (b, 0, zcol + t)),
                  pl.BlockSpec((2, l, LANE), lambda t, b: (b, 0, gcol + t)),
                  pl.BlockSpec((1, n1_len, 2 * n2_len, LANE), lambda t, b: (order * tiles + t, 0, 0, 0),
                               pipeline_mode=one),
                  pl.BlockSpec((HY_ORDER, LANE), lambda t, b: (0, t)),
                  pl.BlockSpec(gf.shape, const3, pipeline_mode=one),
                  pl.BlockSpec(gi.shape, const3, pipeline_mode=one),
                  pl.BlockSpec(f2.shape, const2, pipeline_mode=one),
                  pl.BlockSpec(f2c.shape, const2, pipeline_mode=one)],
        out_specs=pl.BlockSpec((2, l, LANE), lambda t, b: (b, 0, t)),
        out_shape=jax.ShapeDtypeStruct((bsz, l, BR), F32),
        scratch_shapes=[pltpu.VMEM((4 * l, LANE), F32),
                        pltpu.VMEM((n1_len, 2 * n2_len, LANE), BF16)],
        compiler_params=_cparams(("parallel", "arbitrary")),
        name="hy_conv%d" % order,
    )(z, g, kf, bias, gf, gi, f2, f2c)


def _hyena_filters(l, w1, b1, f1, w2, b2, f2, w3, b3, f3, w4):
    hp = lax.Precision.HIGHEST
    t = jnp.linspace(0.0, 1.0, l, dtype=F32)[:, None]
    wpos = 2.0 * math.pi * jnp.arange(l, dtype=F32)[:, None] / l
    f = jnp.linspace(1e-4, HY_BANDS - 1, HY_BANDS, dtype=F32)[None, :]
    z = jnp.concatenate([t, jnp.cos(f * wpos), -jnp.sin(f * wpos)], axis=-1)
    h = jnp.sin(f1 * (jnp.dot(z, w1, precision=hp) + b1))
    h = jnp.sin(f2 * (jnp.dot(h, w2, precision=hp) + b2))
    h = jnp.sin(f3 * (jnp.dot(h, w3, precision=hp) + b3))
    h = jnp.dot(h, w4, precision=hp).reshape(l, 2, HY_ORDER, BR)
    deltas = jnp.linspace(math.log(HY_TARGET) / HY_SLOW_DECAY,
                          math.log(HY_TARGET) / HY_FAST_DECAY, BR, dtype=F32)
    h = h * jnp.exp(-t * jnp.abs(deltas))[:, None, None, :]
    fwd, bwd = h[:, 0], h[:, 1]
    k = jnp.concatenate([fwd, jnp.zeros_like(fwd[:1]), bwd[:0:-1]], axis=0)
    k = k * lax.rsqrt(jnp.sum(k * k, axis=0, keepdims=True) + EPS)
    return k.reshape(2 * l, HY_ORDER * BR)


def _hyena(p3, conv_w, conv_b, filt, bias, n2_len):
    bsz, l, _ = p3.shape
    tabs = _fft_tables(2 * l, n2_len)
    xs = _hy_shortconv(p3, conv_w, conv_b)
    k2d = _hyena_filters(l, *[w.astype(F32) for w in filt])
    kf = _hy_kf(k2d, tabs[1], tabs[3], n2_len)
    tiles = BR // LANE
    z1 = _hy_conv(xs, 0, xs, tiles, kf, bias, tabs, n2_len, 0)
    return _hy_conv(z1, 0, xs, 2 * tiles, kf, bias, tabs, n2_len, 1)


def _nt(a, b):
    return lax.dot_general(a, b, (((1,), (1,)), ((), ())), preferred_element_type=F32)


def _tn(a, b):
    return lax.dot_general(a, b, (((0,), (0,)), ((), ())), preferred_element_type=F32)


def _ret_kernel(q_ref, k_ref, v_ref, cos_ref, sin_ref, lg_ref, nw_ref, s0_ref, o_ref, sfin_ref, *, nchunks):
    c = ATT_CHUNK
    lg = lg_ref[0]
    jrow = lax.broadcasted_iota(jnp.int32, (c, c), 0)
    icol = lax.broadcasted_iota(jnp.int32, (c, c), 1)
    jf = jrow.astype(F32)
    dsym = jnp.exp(jnp.abs(jrow - icol).astype(F32) * lg) * jnp.where(jrow == icol, 2.0, 1.0)
    dq_f = jnp.exp((jf + 1.0) * lg)
    dq_b = jnp.exp((c - jf) * lg)
    dk_f = jnp.exp((c - 1.0 - jf) * lg)
    dk_b = jnp.exp(jf * lg)
    dchunk = jnp.exp(c * lg)

    def rope(x, r0):
        return x * cos_ref[pl.ds(r0, c), :] + pltpu.roll(x, RET_DK // 2, axis=1) * sin_ref[pl.ds(r0, c), :]

    def fwd_step(n, s):
        r0 = pl.multiple_of(n * c, c)
        q = rope(q_ref[0, pl.ds(r0, c), :], r0)
        k = rope(k_ref[0, pl.ds(r0, c), :], r0) * (RET_DK ** -0.5)
        v = v_ref[0, pl.ds(r0, c), :].astype(BF16)
        att = _nt(q.astype(BF16), k.astype(BF16)) * dsym
        o = jnp.dot(att.astype(BF16), v, preferred_element_type=F32)
        o = o + jnp.dot((q * dq_f).astype(BF16), s.astype(BF16), preferred_element_type=F32)
        o_ref[0, pl.ds(r0, c), :] = o
        return dchunk * s + _tn((k * dk_f).astype(BF16), v)

    s_f = lax.fori_loop(0, nchunks, fwd_step, s0_ref[0, 0, 0], unroll=_ATT_UNROLL)
    sfin_ref[0, 0, 0] = s_f

    def bwd_step(i, s):
        r0 = pl.multiple_of((nchunks - 1 - i) * c, c)
        q = rope(q_ref[0, pl.ds(r0, c), :], r0)
        k = rope(k_ref[0, pl.ds(r0, c), :], r0) * (RET_DK ** -0.5)
        v = v_ref[0, pl.ds(r0, c), :].astype(BF16)
        o = o_ref[0, pl.ds(r0, c), :] + jnp.dot((q * dq_b).astype(BF16), s.astype(BF16),
                                                preferred_element_type=F32)
        o = o * lax.rsqrt(jnp.mean(o * o, axis=-1, keepdims=True) + EPS)
        o_ref[0, pl.ds(r0, c), :] = o * nw_ref[...]
        return dchunk * s + _tn((k * dk_b).astype(BF16), v)

    sfin_ref[0, 0, 1] = lax.fori_loop(0, nchunks, bwd_step, s0_ref[0, 0, 1], unroll=_ATT_UNROLL)


def _retention(p3, cos2, sin2, norm_w, s0):
    bsz, l, _ = p3.shape
    lgs = jnp.log(1.0 - 2.0 ** (-5.0 - jnp.arange(RET_HEADS, dtype=F32)))
    lgs = jnp.broadcast_to(lgs[:, None, None], (RET_HEADS, 1, LANE))
    tab = pl.BlockSpec((l, LANE), lambda b, h: (0, 0))
    st = pl.BlockSpec((1, 1, 2, RET_DK, LANE), lambda b, h: (b, h, 0, 0, 0))
    return pl.pallas_call(
        functools.partial(_ret_kernel, nchunks=l // ATT_CHUNK),
        grid=(bsz, RET_HEADS),
        in_specs=[pl.BlockSpec((1, l, LANE), lambda b, h: (b, 0, COL_RET_Q + h)),
                  pl.BlockSpec((1, l, LANE), lambda b, h: (b, 0, COL_RET_K + h)),
                  pl.BlockSpec((1, l, LANE), lambda b, h: (b, 0, COL_RET_V + h)),
                  tab, tab,
                  pl.BlockSpec((1, 1, LANE), lambda b, h: (h, 0, 0)),
                  pl.BlockSpec((1, LANE), lambda b, h: (0, h)),
                  st],
        out_specs=[pl.BlockSpec((1, l, LANE), lambda b, h: (b, 0, h)), st],
        out_shape=[jax.ShapeDtypeStruct((bsz, l, BR), F32),
                   jax.ShapeDtypeStruct((bsz, RET_HEADS, 2, RET_DK, LANE), F32)],
        compiler_params=_cparams(("parallel", "parallel")),
        name="retention",
    )(p3, p3, p3, cos2, sin2, lgs, norm_w.reshape(1, BR), s0)


def _rope_tables(ang):
    cos, sin = jnp.cos(ang), jnp.sin(ang)
    return jnp.concatenate([cos, cos], -1), jnp.concatenate([-sin, sin], -1)


def _latent_angles(rows):
    half = RET_DK // 4
    inv = ROPE_BASE ** (-jnp.arange(half, dtype=F32) / half)
    r = jnp.repeat(jnp.arange(rows, dtype=F32), GRID_W)
    cl = jnp.tile(jnp.arange(GRID_W, dtype=F32), rows)
    return jnp.concatenate([r[:, None] * inv, cl[:, None] * inv], axis=-1)


def _ctx_angles(n_ctx):
    n = RET_DK // 2
    inv = ROPE_BASE ** (-jnp.arange(n, dtype=F32) / n)
    return jnp.arange(n_ctx, dtype=F32)[:, None] * inv


def _gla_kernel(q_ref, k_ref, v_ref, lr_ref, gw_ref, gb_ref, nw_ref, s0_ref, o_ref, sfin_ref, *, nchunks):
    c = ATT_CHUNK
    jrow = lax.broadcasted_iota(jnp.int32, (c, c), 0)
    icol = lax.broadcasted_iota(jnp.int32, (c, c), 1)
    tri_f = jnp.where(icol <= jrow, 1.0, 0.0).astype(BF16)
    tri_b = jnp.where(icol >= jrow, 1.0, 0.0).astype(BF16)
    lane = lax.broadcasted_iota(jnp.int32, (1, LANE), 1)
    head0 = lane < GLA_DK

    def decays(r0, d):
        x = jnp.dot(lr_ref[0, pl.ds(r0, c), :].astype(BF16), gw_ref[d], preferred_element_type=F32) + gb_ref[d]
        la = (jnp.minimum(x, 0.0) - jnp.log(1.0 + jnp.exp(-jnp.abs(x)))) * (1.0 / GLA_TAU)
        hi = la.astype(BF16)
        lo = (la - hi.astype(F32)).astype(BF16)
        cs = jnp.dot(tri_f if d == 0 else tri_b, jnp.concatenate([hi, lo], axis=1), preferred_element_type=F32)
        return cs[:, :LANE] + cs[:, LANE:]

    def fwd_step(n, st):
        r0 = pl.multiple_of(n * c, c)
        q = q_ref[0, pl.ds(r0, c), :] * (GLA_DK ** -0.5)
        k = k_ref[0, pl.ds(r0, c), :]
        bf = decays(r0, 0)
        bb = decays(r0, 1)
        ref_f = bf[c // 2 - 1:c // 2, :]
        ref_b = bb[c // 2:c // 2 + 1, :]
        qf = q * jnp.exp(bf - ref_f)
        kf = (k * jnp.exp(ref_f - bf)).astype(BF16)
        qb = q * jnp.exp(bb - ref_b)
        kb = (k * jnp.exp(ref_b - bb)).astype(BF16)
        qs = q * jnp.exp(bf)
        b_last = bf[c - 1:c, :]
        kdec = (k * jnp.exp(b_last - bf)).astype(BF16)
        kvs = []
        for h in range(2):
            hm = head0 if h == 0 else jnp.logical_not(head0)
            v = v_ref[0, pl.ds(r0, c), h * GLA_DV:(h + 1) * GLA_DV].astype(BF16)
            att = (jnp.where(icol <= jrow, _nt(jnp.where(hm, qf, 0.0).astype(BF16), kf), 0.0)
                   + jnp.where(icol >= jrow, _nt(jnp.where(hm, qb, 0.0).astype(BF16), kb), 0.0))
            o = jnp.dot(att.astype(BF16), v, preferred_element_type=F32)
            o = o + _nt(jnp.where(hm, qs, 0.0).astype(BF16), st[h].astype(BF16))
            o_ref[0, pl.ds(r0, c), h * GLA_DV:(h + 1) * GLA_DV] = o
            kvs.append(_tn(v, kdec))
        dec = jnp.exp(b_last)
        return tuple(jnp.where(head0 if h == 0 else jnp.logical_not(head0), dec * st[h] + kvs[h], 0.0)
                     for h in range(2))

    s_f = lax.fori_loop(0, nchunks, fwd_step, (s0_ref[0, 0, 0, 0], s0_ref[0, 0, 0, 1]), unroll=_ATT_UNROLL)
    sfin_ref[0, 0, 0, 0] = s_f[0]
    sfin_ref[0, 0, 0, 1] = s_f[1]

    def bwd_step(i, st):
        r0 = pl.multiple_of((nchunks - 1 - i) * c, c)
        q = q_ref[0, pl.ds(r0, c), :] * (GLA_DK ** -0.5)
        k = k_ref[0, pl.ds(r0, c), :]
        bb = decays(r0, 1)
        qs = q * jnp.exp(bb)
        b_last = bb[0:1, :]
        kdec = (k * jnp.exp(b_last - bb)).astype(BF16)
        kvs = []
        for h in range(2):
            hm = head0 if h == 0 else jnp.logical_not(head0)
            v = v_ref[0, pl.ds(r0, c), h * GLA_DV:(h + 1) * GLA_DV].astype(BF16)
            o = o_ref[0, pl.ds(r0, c), h * GLA_DV:(h + 1) * GLA_DV]
            o = o + _nt(jnp.where(hm, qs, 0.0).astype(BF16), st[h].astype(BF16))
            o = o * lax.rsqrt(jnp.mean(o * o, axis=-1, keepdims=True) + EPS)
            o_ref[0, pl.ds(r0, c), h * GLA_DV:(h + 1) * GLA_DV] = o * nw_ref[:, h * GLA_DV:(h + 1) * GLA_DV]
            kvs.append(_tn(v, kdec))
        dec = jnp.exp(b_last)
        return tuple(jnp.where(head0 if h == 0 else jnp.logical_not(head0), dec * st[h] + kvs[h], 0.0)
                     for h in range(2))

    s_b = lax.fori_loop(0, nchunks, bwd_step, (s0_ref[0, 0, 1, 0], s0_ref[0, 0, 1, 1]), unroll=_ATT_UNROLL)
    sfin_ref[0, 0, 1, 0] = s_b[0]
    sfin_ref[0, 0, 1, 1] = s_b[1]


def _gla(p3, gate_w, gate_b, norm_w, s0):
    bsz, l, _ = p3.shape
    pairs = GLA_HEADS // 2
    gw = jnp.zeros((2, LANE, GLA_HEADS * GLA_DK), F32)
    for d in range(2):
        gw = gw.at[d, d * GLA_LR:(d + 1) * GLA_LR, :].set(gate_w[d].astype(F32))
    st = pl.BlockSpec((1, 1, 2, 2, GLA_DV, LANE), lambda b, h: (b, h, 0, 0, 0, 0))
    return pl.pallas_call(
        functools.partial(_gla_kernel, nchunks=l // ATT_CHUNK),
        grid=(bsz, pairs),
        in_specs=[pl.BlockSpec((1, l, LANE), lambda b, h: (b, 0, COL_GLA_Q + h)),
                  pl.BlockSpec((1, l, LANE), lambda b, h: (b, 0, COL_GLA_K + h)),
                  pl.BlockSpec((1, l, 2 * GLA_DV), lambda b, h: (b, 0, COL_GLA_V // 2 + h)),
                  pl.BlockSpec((1, l, LANE), lambda b, h: (b, 0, COL_GLA_LR)),
                  pl.BlockSpec((2, LANE, LANE), lambda b, h: (0, 0, h)),
                  pl.BlockSpec((2, 1, LANE), lambda b, h: (0, 0, h)),
                  pl.BlockSpec((1, 2 * GLA_DV), lambda b, h: (0, h)),
                  st],
        out_specs=[pl.BlockSpec((1, l, 2 * GLA_DV), lambda b, h: (b, 0, h)), st],
        out_shape=[jax.ShapeDtypeStruct((bsz, l, BR), F32),
                   jax.ShapeDtypeStruct((bsz, pairs, 2, 2, GLA_DV, LANE), F32)],
        compiler_params=_cparams(("parallel", "parallel")),
        name="gla",
    )(p3, p3, p3, p3, gw.astype(BF16), gate_b.astype(F32).reshape(2, 1, GLA_HEADS * GLA_DK),
      norm_w.reshape(1, BR), s0)


def _mixers(p2, xs, bsz, prm, i, rope, s5_h0, ret_s0, gla_s0, with_hyena):
    m = p2.shape[0]
    l = m // bsz
    p3 = p2.reshape(bsz, l, NP)
    tabs = _s5_tables(prm['s5_a_re'][i], prm['s5_a_im'][i], prm['s5_log_dt'][i], prm['s5_b_re'][i],
                      prm['s5_b_im'][i], prm['s5_c_re'][i], prm['s5_c_im'][i], l // (2 * S5_T))
    y, s5_fin = _s5_core(xs, tabs, s5_h0)
    s5 = _s5_glu(y, p2, prm['s5_d'][i], prm['s5_glu_w'][i], prm['s5_glu_b'][i])
    hy = None
    if with_hyena:
        filt = tuple(prm[n][i] for n in ('hy_w1', 'hy_b1', 'hy_f1', 'hy_w2', 'hy_b2', 'hy_f2',
                                         'hy_w3', 'hy_b3', 'hy_f3', 'hy_w4'))
        hy = _hyena(p3, prm['hy_conv_w'][i], prm['hy_conv_b'][i], filt, prm['hy_bias'][i].astype(F32),
                    64 if l >= 512 else 16).reshape(m, BR)
    ret, ret_fin = _retention(p3, rope[0], rope[1], prm['ret_norm_w'][i], ret_s0)
    gla, gla_fin = _gla(p3, prm['gla_gate_w'][i], prm['gla_gate_b'][i], prm['gla_norm_w'][i], gla_s0)
    return (s5, hy, ret.reshape(m, BR), gla.reshape(m, BR)), (s5_fin, ret_fin, gla_fin)


def kernel(x, c, ctx, c_ctx, norm_w, ada_w, ada_b, w_in, w_out, s5_a_re, s5_a_im, s5_log_dt, s5_b_re, s5_b_im, s5_c_re, s5_c_im, s5_d, s5_glu_w, s5_glu_b, hy_conv_w, hy_conv_b, hy_w1, hy_b1, hy_f1, hy_w2, hy_b2, hy_f2, hy_w3, hy_b3, hy_f3, hy_w4, hy_bias, ret_norm_w, gla_gate_w, gla_gate_b, gla_norm_w, final_norm_w):
    prm = dict(s5_a_re=s5_a_re, s5_a_im=s5_a_im, s5_log_dt=s5_log_dt, s5_b_re=s5_b_re, s5_b_im=s5_b_im,
               s5_c_re=s5_c_re, s5_c_im=s5_c_im, s5_d=s5_d, s5_glu_w=s5_glu_w, s5_glu_b=s5_glu_b,
               hy_conv_w=hy_conv_w, hy_conv_b=hy_conv_b, hy_w1=hy_w1, hy_b1=hy_b1, hy_f1=hy_f1,
               hy_w2=hy_w2, hy_b2=hy_b2, hy_f2=hy_f2, hy_w3=hy_w3, hy_b3=hy_b3, hy_f3=hy_f3,
               hy_w4=hy_w4, hy_bias=hy_bias, ret_norm_w=ret_norm_w, gla_gate_w=gla_gate_w,
               gla_gate_b=gla_gate_b, gla_norm_w=gla_norm_w)
    bsz, l, d = x.shape
    lc = ctx.shape[1]
    depth = w_in.shape[0]
    cc = jnp.concatenate([c, c_ctx[None, :], jnp.zeros((8 - bsz - 1, d), F32)], axis=0)
    mod = _ada(cc, ada_w, ada_b)
    rope_l = _rope_tables(_latent_angles(l // GRID_W))
    rope_c = _rope_tables(_ctx_angles(lc))
    s5_zero = jnp.zeros((S5_GROUPS // 2, 2, 2, 8, 2 * S5_STATE), F32)
    ret_zero = jnp.zeros((bsz, RET_HEADS, 2, RET_DK, LANE), F32)
    gla_zero = jnp.zeros((bsz, GLA_HEADS // 2, 2, 2, GLA_DV, LANE), F32)

    x_l = x.reshape(bsz * l, d)
    x_c = ctx.reshape(bsz * lc, d)
    for i in range(depth):
        last = i == depth - 1
        sh, sc, gt = mod[i, :, :d], mod[i, :, d:2 * d], mod[i, :, 2 * d:]
        w = _permute_w_in(w_in[i])
        wo = w_out[i].astype(BF16)
        p_c, xs_c = _inproj(x_c, norm_w[i], sc[bsz:bsz + 1, None, :], sh[bsz:bsz + 1, None, :], w, bsz * lc)
        p_l, xs_l = _inproj(x_l, norm_w[i], sc[:bsz, None, :], sh[:bsz, None, :], w, l)
        ys_c, (s5_fin, ret_fin, gla_fin) = _mixers(p_c, xs_c, bsz, prm, i, rope_c, s5_zero, ret_zero, gla_zero,
                                                   with_hyena=not last)
        ys_l, _ = _mixers(p_l, xs_l, bsz, prm, i, rope_l, _s5_next_h0(s5_fin), ret_fin, gla_fin,
                          with_hyena=True)
        x_l = _outproj(x_l, ys_l, p_l, wo, gt[:bsz, None, :], final_norm_w, l, final=last)
        if not last:
            x_c = _outproj(x_c, ys_c, p_c, wo, gt[bsz:bsz + 1, None, :], final_norm_w, bsz * lc, final=False)
    return x_l.reshape(bsz, l, d)
```

```python
import functools
import math

import numpy as np
import jax
import jax.numpy as jnp
from jax import lax
from jax.experimental import pallas as pl
from jax.experimental.pallas import tpu as pltpu

F32 = jnp.float32
BF16 = jnp.bfloat16

EPS = 1e-6
GRID_W = 64
BR = 512
S5_GSIZE = 16
S5_GROUPS = BR // S5_GSIZE
S5_STATE = 64
S5_T = 16
HY_ORDER = 2
HY_EMB = 33
HY_BANDS = (HY_EMB - 1) // 2
HY_FAST_DECAY = 0.3
HY_SLOW_DECAY = 1.5
HY_TARGET = 1e-2
RET_HEADS = 4
RET_DK = 128
GLA_HEADS = 4
GLA_DK = 64
GLA_DV = 128
GLA_LR = 16
GLA_TAU = 16.0
ROPE_BASE = 10000.0
ATT_CHUNK = 128
_FFT_UNROLL = 8
_ATT_UNROLL = 2

LANE = 128
VMEM_LIMIT = 56 * 1024 * 1024

COL_S5_U = 0
COL_RET_K = 4
COL_RET_V = 8
COL_GLA_V = 12
COL_GATES = 16
COL_HY_X = 32
COL_RET_Q = 44
COL_GLA_Q = 48
COL_GLA_K = 50
COL_GLA_LR = 52
NP_UNITS = 54
NP = NP_UNITS * LANE
_SEG_DST = (
    (BR, COL_S5_U), (BR, COL_GATES),
    (3 * BR, COL_HY_X), (BR, COL_GATES + 4),
    (BR, COL_RET_Q), (BR, COL_RET_K), (BR, COL_RET_V), (BR, COL_GATES + 8),
    (BR // 2, COL_GLA_Q), (BR // 2, COL_GLA_K), (BR, COL_GLA_V), (2 * GLA_LR, COL_GLA_LR),
    (BR, COL_GATES + 12),
)


def _cparams(sem):
    return pltpu.CompilerParams(dimension_semantics=sem, vmem_limit_bytes=VMEM_LIMIT)


def _silu(x):
    return x * jax.nn.sigmoid(x)


def _permute_w_in(w):
    d = w.shape[0]
    segs, off = [], 0
    for size, unit in _SEG_DST:
        segs.append((unit * LANE, off, size))
        off += size
    pieces, pos = [], 0
    for dst, src, size in sorted(segs):
        if dst > pos:
            pieces.append(jnp.zeros((d, dst - pos), BF16))
        pieces.append(w[:, src:src + size].astype(BF16))
        pos = dst + size
    if pos < NP:
        pieces.append(jnp.zeros((d, NP - pos), BF16))
    return jnp.concatenate(pieces, axis=1)


def _ada_kernel(c_ref, w_ref, b_ref, o_ref):
    a = _silu(c_ref[...]).astype(BF16)
    o_ref[0] = jnp.dot(a, w_ref[0].astype(BF16), preferred_element_type=F32) + b_ref[0]


def _ada(cc, ada_w, ada_b, tn=768):
    depth, d, n = ada_w.shape
    return pl.pallas_call(
        _ada_kernel,
        grid=(depth, n // tn),
        in_specs=[pl.BlockSpec((8, d), lambda i, j: (0, 0)),
                  pl.BlockSpec((1, d, tn), lambda i, j: (i, 0, j)),
                  pl.BlockSpec((1, 1, tn), lambda i, j: (i, 0, j))],
        out_specs=pl.BlockSpec((1, 8, tn), lambda i, j: (i, 0, j)),
        out_shape=jax.ShapeDtypeStruct((depth, 8, n), F32),
        compiler_params=_cparams(("parallel", "parallel")),
        name="ada_mod",
    )(cc, ada_w, ada_b.reshape(depth, 1, n))


_S5_SLOT = 2 * S5_GSIZE
_S5_PAIRS = S5_GROUPS // 2
_S5_ROW = S5_T * _S5_SLOT


def _s5_lane_perm():
    perm = np.zeros((4, _S5_ROW), np.int32)
    for s in range(4):
        for tau in range(S5_T):
            for c in range(_S5_SLOT):
                gl, i = divmod(c, S5_GSIZE)
                perm[s, LANE * (tau // 4) + _S5_SLOT * ((s + tau) % 4) + c] = (gl * S5_T + tau) * S5_GSIZE + i
    return perm


def _s5_to_chunk_rows(nat_ref, xs_ref, rows16):
    slot = lax.broadcasted_iota(jnp.int32, (rows16, LANE), 1) // _S5_SLOT
    for q in range(4):
        for qd in range(4):
            rolled = []
            for t in range(4):
                piece = nat_ref[q, pl.ds(4 * qd + t, rows16, stride=S5_T), :]
                rolled.append(pltpu.roll(piece, _S5_SLOT * t, axis=1) if t else piece)
            for s in range(4):
                acc = rolled[0]
                for t in range(1, 4):
                    acc = jnp.where(slot == (s + t) % 4, rolled[t], acc)
                xs_ref[4 * q + s, :, qd * LANE:(qd + 1) * LANE] = acc.astype(xs_ref.dtype)


def _s5_from_chunk_rows(y_ref, nat_ref, rows16):
    slot = lax.broadcasted_iota(jnp.int32, (rows16, LANE), 1) // _S5_SLOT
    for q in range(4):
        for qd in range(4):
            src = [y_ref[4 * q + s, :, qd * LANE:(qd + 1) * LANE] for s in range(4)]
            for t in range(4):
                acc = src[0]
                for s in range(1, 4):
                    acc = jnp.where(slot == (s + t) % 4, src[s], acc)
                if t:
                    acc = pltpu.roll(acc, LANE - _S5_SLOT * t, axis=1)
                nat_ref[q, pl.ds(4 * qd + t, rows16, stride=S5_T), :] = acc


def _inproj_kernel(x_ref, nw_ref, sc_ref, sh_ref, w_ref, o_ref, xs_ref, h_ref, nat_ref, *, rows16):
    j = pl.program_id(1)

    @pl.when(j == 0)
    def _():
        x = x_ref[...]
        y = x * lax.rsqrt(jnp.mean(x * x, axis=-1, keepdims=True) + EPS) * nw_ref[...]
        h_ref[...] = (y * (1.0 + sc_ref[0]) + sh_ref[0]).astype(BF16)

    res = jnp.dot(h_ref[...], w_ref[...], preferred_element_type=F32)
    o_ref[...] = res

    @pl.when(j == 0)
    def _():
        for q in range(BR // LANE):
            nat_ref[q] = res[:, q * LANE:(q + 1) * LANE]
        _s5_to_chunk_rows(nat_ref, xs_ref, rows16)


def _inproj(x2, nw, sc, sh, w, rows_per_mod, tm=1024, tn=768):
    m, d = x2.shape
    tm = min(tm, rows_per_mod)
    per = rows_per_mod // tm
    rows16 = tm // S5_T
    assert COL_S5_U == 0 and tn >= BR
    return pl.pallas_call(
        functools.partial(_inproj_kernel, rows16=rows16),
        grid=(m // tm, NP // tn),
        in_specs=[pl.BlockSpec((tm, d), lambda i, j: (i, 0)),
                  pl.BlockSpec((1, d), lambda i, j: (0, 0)),
                  pl.BlockSpec((1, 1, d), lambda i, j: (i // per, 0, 0)),
                  pl.BlockSpec((1, 1, d), lambda i, j: (i // per, 0, 0)),
                  pl.BlockSpec((d, tn), lambda i, j: (0, j))],
        out_specs=[pl.BlockSpec((tm, tn), lambda i, j: (i, j)),
                   pl.BlockSpec((_S5_PAIRS, rows16, _S5_ROW), lambda i, j: (0, i, 0))],
        out_shape=[jax.ShapeDtypeStruct((m, NP), F32),
                   jax.ShapeDtypeStruct((_S5_PAIRS, m // S5_T, _S5_ROW), BF16)],
        scratch_shapes=[pltpu.VMEM((tm, d), BF16), pltpu.VMEM((BR // LANE, tm, LANE), F32)],
        compiler_params=_cparams(("parallel", "arbitrary")),
        name="inproj",
    )(x2, nw.reshape(1, d), sc, sh, w)


def _outproj_kernel(x_ref, y0_ref, y1_ref, y2_ref, y3_ref, g_ref, w_ref, gt_ref, fw_ref, o_ref, *, final):
    acc = None
    for k, y_ref in enumerate((y0_ref, y1_ref, y2_ref, y3_ref)):
        gated = (y_ref[...] * _silu(g_ref[:, k * BR:(k + 1) * BR])).astype(BF16)
        part = jnp.dot(gated, w_ref[k * BR:(k + 1) * BR, :], preferred_element_type=F32)
        acc = part if acc is None else acc + part
    x = x_ref[...] + gt_ref[0] * acc
    if final:
        x = x * lax.rsqrt(jnp.mean(x * x, axis=-1, keepdims=True) + EPS) * fw_ref[...]
    o_ref[...] = x


def _outproj(x2, ys, p, w_out, gt, fw, rows_per_mod, final, tm=256):
    m, d = x2.shape
    tm = min(tm, rows_per_mod)
    per = rows_per_mod // tm
    gates_blk = COL_GATES * LANE // (4 * BR)
    yspec = pl.BlockSpec((tm, BR), lambda i: (i, 0))
    return pl.pallas_call(
        functools.partial(_outproj_kernel, final=final),
        grid=(m // tm,),
        in_specs=[pl.BlockSpec((tm, d), lambda i: (i, 0)), yspec, yspec, yspec, yspec,
                  pl.BlockSpec((tm, 4 * BR), lambda i: (i, gates_blk)),
                  pl.BlockSpec((4 * BR, d), lambda i: (0, 0)),
                  pl.BlockSpec((1, 1, d), lambda i: (i // per, 0, 0)),
                  pl.BlockSpec((1, d), lambda i: (0, 0))],
        out_specs=pl.BlockSpec((tm, d), lambda i: (i, 0)),
        out_shape=jax.ShapeDtypeStruct((m, d), F32),
        compiler_params=_cparams(("parallel",)),
        name="outproj",
    )(x2, *ys, p, w_out, gt, fw.reshape(1, d))


def _s5_tables(a_re, a_im, log_dt, b_re, b_im, c_re, c_im, nch):
    hp = lax.Precision.HIGHEST
    t_len, g_n, p_n, s_n = S5_T, S5_GROUPS, S5_STATE, S5_GSIZE
    a_re, a_im = a_re.astype(F32), a_im.astype(F32)
    dt = jnp.exp(log_dt.astype(F32))[:, :, None]
    lam_re, lam_im = a_re * dt, a_im * dt

    def power(tau):
        tau = tau.astype(F32)[:, None, None, None]
        mag = jnp.exp(lam_re[None] * tau)
        return mag * jnp.cos(lam_im[None] * tau), mag * jnp.sin(lam_im[None] * tau)

    ab_re, ab_im = power(jnp.ones((1,)))
    ab_re, ab_im = ab_re[0], ab_im[0]
    den = a_re * a_re + a_im * a_im
    nr = ab_re - 1.0
    co_re = (nr * a_re + ab_im * a_im) / den
    co_im = (ab_im * a_re - nr * a_im) / den
    b_re, b_im = b_re.astype(F32), b_im.astype(F32)
    bco_re = co_re[..., None] * b_re - co_im[..., None] * b_im
    bco_im = co_re[..., None] * b_im + co_im[..., None] * b_re
    c_re, c_im = c_re.astype(F32), c_im.astype(F32)

    pr, pi = power(jnp.arange(t_len + 1))
    ca_re = c_re[None] * pr[:, :, :, None, :] - c_im[None] * pi[:, :, :, None, :]
    ca_im = c_re[None] * pi[:, :, :, None, :] + c_im[None] * pr[:, :, :, None, :]
    ca_cat = jnp.concatenate([ca_re[:t_len], -ca_im[:t_len]], axis=-1).transpose(1, 2, 0, 3, 4)
    bco_cat = jnp.concatenate([bco_re, bco_im], axis=-2)
    kk = jnp.einsum('dgmp,dgpj->dgmj', ca_cat.reshape(2, g_n, t_len * s_n, 2 * p_n), bco_cat, precision=hp)
    kk = kk.reshape(2, g_n, t_len, s_n, s_n).transpose(2, 0, 1, 3, 4)
    kf, kb = kk[:, 0], kk[:, 1]
    kfull = jnp.concatenate([kb[:0:-1], (kf[0] + kb[0])[None], kf[1:]], axis=0)
    tt = jnp.arange(t_len)
    idx = tt[None, :] - tt[:, None] + (t_len - 1)
    mm = kfull[idx]
    mmat = mm.transpose(2, 0, 4, 1, 3).reshape(g_n, t_len * s_n, t_len * s_n)

    wexp = jnp.stack([pr[t_len - 1 - tt, 0], pr[tt, 1]], 0), jnp.stack([pi[t_len - 1 - tt, 0], pi[tt, 1]], 0)
    w_re = wexp[0][..., None] * bco_re[:, None] - wexp[1][..., None] * bco_im[:, None]
    w_im = wexp[0][..., None] * bco_im[:, None] + wexp[1][..., None] * bco_re[:, None]
    w4 = jnp.stack([w_re[0], w_im[0], w_re[1], w_im[1]], 0)
    w4 = w4.transpose(0, 2, 1, 4, 3).reshape(4, g_n // 2, 2, t_len * s_n, p_n)
    eye2 = jnp.eye(2, dtype=F32)
    wpair = jnp.einsum('kqlrp,lm->qlrkmp', w4, eye2).reshape(g_n // 2, 2 * t_len * s_n, 4 * 2 * p_n)

    vexp_f, vexp_b = tt + 1, t_len - tt
    v4 = jnp.stack([ca_re[vexp_f, 0], -ca_im[vexp_f, 0], ca_re[vexp_b, 1], -ca_im[vexp_b, 1]], 0)
    v4 = v4.transpose(0, 2, 4, 1, 3).reshape(4, g_n // 2, 2, p_n, t_len * s_n)
    vpair = jnp.einsum('kqlpr,lm->qklpmr', v4, eye2).reshape(g_n // 2, 4 * 2 * p_n, 2 * t_len * s_n)

    qr, qi = power(t_len * jnp.arange(nch + 1))
    pw = jnp.stack([qr, qi], 0)
    pw = pw.reshape(2, nch + 1, 2, g_n // 2, 2 * p_n).transpose(3, 2, 0, 1, 4)
    mpair = jnp.einsum('qlrc,lm->qlrmc', mmat.reshape(g_n // 2, 2, t_len * s_n, t_len * s_n), eye2)
    mpair = mpair.reshape(g_n // 2, _S5_ROW, _S5_ROW)
    onehot = jnp.asarray(np.eye(_S5_ROW, dtype=np.float32)[_s5_lane_perm()]).astype(BF16)

    def rows_to_lane_order(a):
        a4 = a.astype(BF16).reshape(g_n // 8, 4, _S5_ROW, a.shape[-1])
        return jnp.einsum('sno,qsoc->qsnc', onehot, a4, preferred_element_type=F32).astype(BF16).reshape(a.shape)

    def cols_to_lane_order(a):
        a4 = a.astype(BF16).reshape(g_n // 8, 4, a.shape[1], _S5_ROW)
        return jnp.einsum('qsro,sno->qsrn', a4, onehot, preferred_element_type=F32).astype(BF16).reshape(a.shape)

    return cols_to_lane_order(rows_to_lane_order(mpair)), rows_to_lane_order(wpair), cols_to_lane_order(vpair), pw


def _cmul(ar, ai, br, bi):
    return ar * br - ai * bi, ar * bi + ai * br


def _s5_kernel(x_ref, m_ref, w_ref, v_ref, pw_ref, h0_ref, y_ref, fin_ref, s_ref, h_ref, *, nch):
    ln = 2 * S5_STATE
    x = x_ref[0]
    s_all = jnp.dot(x, w_ref[0], preferred_element_type=F32)
    for k in range(4):
        s_ref[k] = s_all[:, k * ln:(k + 1) * ln]
    a_rf, a_if = pw_ref[0, 0, 0, 1:2, :], pw_ref[0, 0, 1, 1:2, :]
    a_rb, a_ib = pw_ref[0, 1, 0, 1:2, :], pw_ref[0, 1, 1, 1:2, :]

    def tile(c):
        return pl.ds(c, 8, stride=nch)

    def scan_step(s, carry):
        hrf, hif, hrb, hib = carry
        rf, rb = tile(s), tile(nch - 1 - s)
        h_ref[0, rf, :] = hrf
        h_ref[1, rf, :] = hif
        h_ref[2, rb, :] = hrb
        h_ref[3, rb, :] = hib
        pr, pi = _cmul(a_rf, a_if, hrf, hif)
        qr, qi = _cmul(a_rb, a_ib, hrb, hib)
        return (pr + s_ref[0, rf, :], pi + s_ref[1, rf, :], qr + s_ref[2, rb, :], qi + s_ref[3, rb, :])

    init = (h0_ref[0, 0, 0], h0_ref[0, 0, 1], h0_ref[0, 1, 0], h0_ref[0, 1, 1])
    hrf, hif, hrb, hib = lax.fori_loop(0, nch, scan_step, init, unroll=2)

    odd = (lax.broadcasted_iota(jnp.int32, (8, ln), 0) % 2) == 1
    crf = jnp.where(odd, pltpu.roll(hrf, 1, axis=0), 0.0)
    cif = jnp.where(odd, pltpu.roll(hif, 1, axis=0), 0.0)
    crb = jnp.where(odd, 0.0, pltpu.roll(hrb, 7, axis=0))
    cib = jnp.where(odd, 0.0, pltpu.roll(hib, 7, axis=0))

    def fix_step(s, _):
        rf, rb = tile(s), tile(nch - 1 - s)
        pr, pi = _cmul(pw_ref[0, 0, 0, pl.ds(s, 1), :], pw_ref[0, 0, 1, pl.ds(s, 1), :], crf, cif)
        qr, qi = _cmul(pw_ref[0, 1, 0, pl.ds(s, 1), :], pw_ref[0, 1, 1, pl.ds(s, 1), :], crb, cib)
        h_ref[0, rf, :] = h_ref[0, rf, :] + pr
        h_ref[1, rf, :] = h_ref[1, rf, :] + pi
        h_ref[2, rb, :] = h_ref[2, rb, :] + qr
        h_ref[3, rb, :] = h_ref[3, rb, :] + qi
        return 0

    lax.fori_loop(0, nch, fix_step, 0, unroll=2)
    pr, pi = _cmul(pw_ref[0, 0, 0, nch:nch + 1, :], pw_ref[0, 0, 1, nch:nch + 1, :], crf, cif)
    qr, qi = _cmul(pw_ref[0, 1, 0, nch:nch + 1, :], pw_ref[0, 1, 1, nch:nch + 1, :], crb, cib)
    fin_ref[0, 0, 0] = hrf + pr
    fin_ref[0, 0, 1] = hif + pi
    fin_ref[0, 1, 0] = hrb + qr
    fin_ref[0, 1, 1] = hib + qi

    h_all = jnp.concatenate([h_ref[k] for k in range(4)], axis=1).astype(BF16)
    y_ref[0] = (jnp.dot(x, m_ref[0], preferred_element_type=F32)
                + jnp.dot(h_all, v_ref[0], preferred_element_type=F32))


def _s5_core(xs, tables, h0):
    mpair, wpair, vpair, pw = tables
    gp_n, rows, wdt = xs.shape
    nch = rows // 8
    return pl.pallas_call(
        functools.partial(_s5_kernel, nch=nch),
        grid=(gp_n,),
        in_specs=[pl.BlockSpec((1, rows, wdt), lambda g: (g, 0, 0)),
                  pl.BlockSpec((1, wdt, wdt), lambda g: (g, 0, 0)),
                  pl.BlockSpec((1, wdt, wdt), lambda g: (g, 0, 0)),
                  pl.BlockSpec((1, wdt, wdt), lambda g: (g, 0, 0)),
                  pl.BlockSpec((1, 2, 2, nch + 1, 2 * S5_STATE), lambda g: (g, 0, 0, 0, 0)),
                  pl.BlockSpec((1, 2, 2, 8, 2 * S5_STATE), lambda g: (g, 0, 0, 0, 0))],
        out_specs=[pl.BlockSpec((1, rows, wdt), lambda g: (g, 0, 0)),
                   pl.BlockSpec((1, 2, 2, 8, 2 * S5_STATE), lambda g: (g, 0, 0, 0, 0))],
        out_shape=[jax.ShapeDtypeStruct((gp_n, rows, wdt), F32),
                   jax.ShapeDtypeStruct((gp_n, 2, 2, 8, 2 * S5_STATE), F32)],
        scratch_shapes=[pltpu.VMEM((4, rows, 2 * S5_STATE), F32), pltpu.VMEM((4, rows, 2 * S5_STATE), F32)],
        compiler_params=_cparams(("parallel",)),
        name="s5_core",
    )(xs, mpair, wpair, vpair, pw, h0)


def _s5_next_h0(fin):
    g, d, r, _, n = fin.shape
    sw = fin.reshape(g, d, r, 4, 2, n)[:, :, :, :, ::-1, :]
    keep = jnp.array([[1.0, 0.0], [0.0, 1.0]], F32)[None, :, None, None, :, None]
    return (sw * keep).reshape(fin.shape)


def _s5_glu_kernel(y_ref, u_ref, d_ref, w_ref, b_ref, o_ref, nat_ref, *, rows16):
    _s5_from_chunk_rows(y_ref, nat_ref, rows16)
    y = jnp.concatenate([nat_ref[q] for q in range(BR // LANE)], axis=1) + u_ref[...] * d_ref[...]
    g = jax.nn.gelu(y)
    z = jnp.dot(g.astype(BF16), w_ref[...], preferred_element_type=F32) + b_ref[...]
    o_ref[...] = g * jax.nn.sigmoid(z)


def _s5_glu(ys, p, d_skip, glu_w, glu_b, tm=512):
    m = p.shape[0]
    tm = min(tm, m)
    rows16 = tm // S5_T
    return pl.pallas_call(
        functools.partial(_s5_glu_kernel, rows16=rows16),
        grid=(m // tm,),
        in_specs=[pl.BlockSpec((_S5_PAIRS, rows16, _S5_ROW), lambda i: (0, i, 0)),
                  pl.BlockSpec((tm, BR), lambda i: (i, COL_S5_U * LANE // BR)),
                  pl.BlockSpec((1, BR), lambda i: (0, 0)),
                  pl.BlockSpec((BR, BR), lambda i: (0, 0)),
                  pl.BlockSpec((1, BR), lambda i: (0, 0))],
        out_specs=pl.BlockSpec((tm, BR), lambda i: (i, 0)),
        out_shape=jax.ShapeDtypeStruct((m, BR), F32),
        scratch_shapes=[pltpu.VMEM((BR // LANE, tm, LANE), F32)],
        compiler_params=_cparams(("parallel",)),
        name="s5_glu",
    )(ys, p, d_skip.reshape(1, BR), glu_w.astype(BF16), glu_b.reshape(1, BR))


@functools.lru_cache(maxsize=None)
def _fft_tables_np(n_total, n2_len):
    n1_len = n_total // n2_len
    n1h = n1_len // 2
    k1 = np.arange(n1_len, dtype=np.int64)
    n2 = np.arange(n2_len, dtype=np.int64)
    m = (k1[None, :, None] * k1[None, None, :] * n2_len + n2[:, None, None] * k1[None, :, None]) % n_total
    ang = m.astype(np.float64) * (2.0 * math.pi / n_total)
    c, s = np.cos(ang), np.sin(ang)
    ch, sh = c[:, :, :n1h], s[:, :, :n1h]
    gf = np.concatenate([np.concatenate([ch, sh], 2), np.concatenate([-sh, ch], 2)], 1)
    gk = np.concatenate([ch, -sh], 1)
    cht, sht = ch.transpose(0, 2, 1), sh.transpose(0, 2, 1)
    gi = np.concatenate([np.concatenate([cht, -sht], 2), np.concatenate([sht, cht], 2)], 1)
    a2 = ((n2[:, None] * n2[None, :]) % n2_len).astype(np.float64) * (2.0 * math.pi / n2_len)
    c2, s2 = np.cos(a2), np.sin(a2)
    f2 = np.concatenate([np.concatenate([c2, s2], 1), np.concatenate([-s2, c2], 1)], 0)
    f2c = np.concatenate([np.concatenate([c2, -s2], 1), np.concatenate([s2, c2], 1)], 0)
    return tuple(np.asarray(t, np.float32) for t in (gf, gk, gi, f2, f2c))


def _fft_tables(n_total, n2_len):
    return tuple(jnp.asarray(t).astype(BF16) for t in _fft_tables_np(n_total, n2_len))


def _hy_shortconv_kernel(x_ref, w_ref, b_ref, o_ref):
    x = x_ref[0]
    l = x.shape[0]
    row = lax.broadcasted_iota(jnp.int32, x.shape, 0)
    prev = jnp.where(row == 0, 0.0, pltpu.roll(x, 1, axis=0))
    nxt = jnp.where(row == l - 1, 0.0, pltpu.roll(x, l - 1, axis=0))
    o_ref[0] = w_ref[0:1, :] * prev + w_ref[1:2, :] * x + w_ref[2:3, :] * nxt + b_ref[...]


def _hy_shortconv(p3, conv_w, conv_b):
    bsz, l, _ = p3.shape
    units = 3 * BR // LANE
    return pl.pallas_call(
        _hy_shortconv_kernel,
        grid=(bsz, units),
        in_specs=[pl.BlockSpec((1, l, LANE), lambda b, u: (b, 0, COL_HY_X + u)),
                  pl.BlockSpec((3, LANE), lambda b, u: (0, u)),
                  pl.BlockSpec((1, LANE), lambda b, u: (0, u))],
        out_specs=pl.BlockSpec((1, l, LANE), lambda b, u: (b, 0, u)),
        out_shape=jax.ShapeDtypeStruct((bsz, l, 3 * BR), F32),
        compiler_params=_cparams(("parallel", "parallel")),
        name="hy_shortconv",
    )(p3, conv_w, conv_b.reshape(1, 3 * BR))


def _fft_stage2(mid_ref, f2_ref, n1_len, n2_len, emit):
    def body(kp, _):
        def rhs(k1):
            return jnp.concatenate([mid_ref[pl.ds(k1, n2_len, stride=2 * n1_len), :],
                                    mid_ref[pl.ds(n1_len + k1, n2_len, stride=2 * n1_len), :]], axis=0)
        r = jnp.concatenate([rhs(2 * kp), rhs(2 * kp + 1)], axis=1).astype(BF16)
        xs = jnp.dot(f2_ref[...], r, preferred_element_type=F32)
        emit(2 * kp, xs[:, :LANE])
        emit(2 * kp + 1, xs[:, LANE:])
        return 0
    lax.fori_loop(0, n1_len // 2, body, 0, unroll=_FFT_UNROLL)


def _hy_kf_kernel(fwd_ref, bwd_ref, nrm_ref, gk_ref, f2_ref, o_ref, mid_ref, *, n1_len, n2_len):
    n1h = n1_len // 2
    row = lax.broadcasted_iota(jnp.int32, (n1h, LANE), 0)

    def stage1(n2, _):
        f = fwd_ref[pl.ds(n2, n1h, stride=n2_len), :]
        b = jnp.where((row == 0) & (n2 == 0), 0.0, bwd_ref[pl.ds(n2, n1h, stride=n2_len), :])
        a = jnp.dot(gk_ref[n2], jnp.concatenate([f, b], axis=1).astype(BF16), preferred_element_type=F32)
        r0 = pl.multiple_of(n2 * 2 * n1_len, 2 * n1_len)
        mid_ref[0, pl.ds(r0, 2 * n1_len), :] = a[:, :LANE]
        mid_ref[1, pl.ds(r0, 2 * n1_len), :] = a[:, LANE:]
        return 0
    lax.fori_loop(0, n2_len, stage1, 0, unroll=_FFT_UNROLL)
    scale = nrm_ref[...] * (1.0 / (n1_len * n2_len))

    def stage2(k1, _):
        def rhs(j):
            return jnp.concatenate([mid_ref[j, pl.ds(k1, n2_len, stride=2 * n1_len), :],
                                    mid_ref[j, pl.ds(n1_len + k1, n2_len, stride=2 * n1_len), :]], axis=0)
        r = jnp.concatenate([rhs(0), rhs(1)], axis=1).astype(BF16)
        xs = jnp.dot(f2_ref[...], r, preferred_element_type=F32)
        f, b = xs[:, :LANE], xs[:, LANE:]
        o_ref[0, k1] = (jnp.concatenate([f[:n2_len] + b[:n2_len], f[n2_len:] - b[n2_len:]], axis=0)
                        * scale).astype(BF16)
        return 0
    lax.fori_loop(0, n1_len, stage2, 0, unroll=_FFT_UNROLL)


def _hy_kf(h2d, nrm, gk, f2, n2_len):
    l, cols = h2d.shape
    n1_len = 2 * l // n2_len
    tiles = cols // (2 * LANE)
    return pl.pallas_call(
        functools.partial(_hy_kf_kernel, n1_len=n1_len, n2_len=n2_len),
        grid=(tiles,),
        in_specs=[pl.BlockSpec((l, LANE), lambda t: (0, t)),
                  pl.BlockSpec((l, LANE), lambda t: (0, tiles + t)),
                  pl.BlockSpec((1, LANE), lambda t: (0, t)),
                  pl.BlockSpec(gk.shape, lambda t: (0, 0, 0)),
                  pl.BlockSpec(f2.shape, lambda t: (0, 0))],
        out_specs=pl.BlockSpec((1, n1_len, 2 * n2_len, LANE), lambda t: (t, 0, 0, 0)),
        out_shape=jax.ShapeDtypeStruct((tiles, n1_len, 2 * n2_len, LANE), BF16),
        scratch_shapes=[pltpu.VMEM((2, 4 * l, LANE), F32)],
        compiler_params=_cparams(("parallel",)),
        name="hy_kf",
    )(h2d, h2d, nrm, gk, f2)


def _hy_conv_kernel(z_ref, g_ref, kf_ref, bias_ref, gf_ref, gi_ref, f2_ref, f2c_ref, o_ref,
                    mid_ref, spec_ref, *, n1_len, n2_len, order):
    n1h = n1_len // 2

    def stage1(n2, _):
        rhs = jnp.concatenate([z_ref[0, pl.ds(n2, n1h, stride=n2_len), :],
                               z_ref[1, pl.ds(n2, n1h, stride=n2_len), :]], axis=0).astype(BF16)
        r0 = pl.multiple_of(n2 * 2 * n1_len, 2 * n1_len)
        mid_ref[pl.ds(r0, 2 * n1_len), :] = jnp.dot(gf_ref[n2], rhs, preferred_element_type=F32)
        return 0
    lax.fori_loop(0, n2_len, stage1, 0, unroll=_FFT_UNROLL)

    def emit(k1, xs):
        kf = kf_ref[0, k1].astype(F32)
        xr, xi = xs[:n2_len], xs[n2_len:]
        kr, ki = kf[:n2_len], kf[n2_len:]
        spec_ref[k1] = jnp.concatenate([xr * kr - xi * ki, xr * ki + xi * kr], axis=0).astype(BF16)
    _fft_stage2(mid_ref, f2_ref, n1_len, n2_len, emit)

    def stage2i(kp, _):
        r = jnp.concatenate([spec_ref[2 * kp], spec_ref[2 * kp + 1]], axis=1)
        b = jnp.dot(f2c_ref[...], r, preferred_element_type=F32)
        r0 = pl.multiple_of(kp * 4 * n2_len, 4 * n2_len)
        mid_ref[pl.ds(r0, 2 * n2_len), :] = b[:, :LANE]
        mid_ref[pl.ds(r0 + 2 * n2_len, 2 * n2_len), :] = b[:, LANE:]
        return 0
    lax.fori_loop(0, n1_len // 2, stage2i, 0, unroll=_FFT_UNROLL)

    def stage1i(n2, _):
        rhs = jnp.concatenate([mid_ref[pl.ds(n2, n1_len, stride=2 * n2_len), :],
                               mid_ref[pl.ds(n2_len + n2, n1_len, stride=2 * n2_len), :]], axis=0).astype(BF16)
        o = jnp.dot(gi_ref[n2], rhs, preferred_element_type=F32)
        o_ref[0, pl.ds(n2, n1h, stride=n2_len), :] = o[:n1h]
        o_ref[1, pl.ds(n2, n1h, stride=n2_len), :] = o[n1h:]
        return 0
    lax.fori_loop(0, n2_len, stage1i, 0, unroll=_FFT_UNROLL)

    bias = bias_ref[order:order + 1, :]
    for j in range(2):
        o_ref[j] = g_ref[j] * (o_ref[j] + bias * z_ref[j])


def _hy_conv(z, zcol, g, gcol, kf, bias, tabs, n2_len, order):
    gf, _, gi, f2, f2c = tabs
    bsz, l, _ = z.shape
    n1_len = 2 * l // n2_len
    tiles = BR // LANE
    const2 = lambda *_: (0, 0)
    const3 = lambda *_: (0, 0, 0)
    one = pl.Buffered(1)
    return pl.pallas_call(
        functools.partial(_hy_conv_kernel, n1_len=n1_len, n2_len=n2_len, order=order),
        grid=(tiles, bsz // 2),
        in_specs=[pl.BlockSpec((2, l, LANE), lambda t, b: (b, 0, zcol + t)),
                  pl.BlockSpec((2, l, LANE), lambda t, b: (b, 0, gcol + t)),
                  pl.BlockSpec((1, n1_len, 2 * n2_len, LANE), lambda t, b: (order * tiles + t, 0, 0, 0),
                               pipeline_mode=one),
                  pl.BlockSpec((HY_ORDER, LANE), lambda t, b: (0, t)),
                  pl.BlockSpec(gf.shape, const3, pipeline_mode=one),
                  pl.BlockSpec(gi.shape, const3, pipeline_mode=one),
                  pl.BlockSpec(f2.shape, const2, pipeline_mode=one),
                  pl.BlockSpec(f2c.shape, const2, pipeline_mode=one)],
        out_specs=pl.BlockSpec((2, l, LANE), lambda t, b: (b, 0, t)),
        out_shape=jax.ShapeDtypeStruct((bsz, l, BR), F32),
        scratch_shapes=[pltpu.VMEM((4 * l, LANE), F32),
                        pltpu.VMEM((n1_len, 2 * n2_len, LANE), BF16)],
        compiler_params=_cparams(("parallel", "arbitrary")),
        name="hy_conv%d" % order,
    )(z, g, kf, bias, gf, gi, f2, f2c)


def _hyena_filters(l, w1, b1, f1, w2, b2, f2, w3, b3, f3, w4):
    hp = lax.Precision.HIGHEST
    t = jnp.linspace(0.0, 1.0, l, dtype=F32)[:, None]
    wpos = 2.0 * math.pi * jnp.arange(l, dtype=F32)[:, None] / l
    f = jnp.linspace(1e-4, HY_BANDS - 1, HY_BANDS, dtype=F32)[None, :]
    z = jnp.concatenate([t, jnp.cos(f * wpos), -jnp.sin(f * wpos)], axis=-1)
    h = jnp.sin(f1 * (jnp.dot(z, w1, precision=hp) + b1))
    h = jnp.sin(f2 * (jnp.dot(h, w2, precision=hp) + b2))
    h = jnp.sin(f3 * (jnp.dot(h, w3, precision=hp) + b3))
    h = jnp.dot(h, w4, precision=hp).reshape(l, 2, HY_ORDER, BR)
    deltas = jnp.linspace(math.log(HY_TARGET) / HY_SLOW_DECAY,
                          math.log(HY_TARGET) / HY_FAST_DECAY, BR, dtype=F32)
    h = h * jnp.exp(-t * jnp.abs(deltas))[:, None, None, :]
    ss = jnp.sum(h[:, 0] * h[:, 0], axis=0) + jnp.sum(h[1:, 1] * h[1:, 1], axis=0)
    return h.reshape(l, 2 * HY_ORDER * BR), lax.rsqrt(ss + EPS).reshape(1, HY_ORDER * BR)


def _hyena(p3, conv_w, conv_b, filt, bias, n2_len):
    bsz, l, _ = p3.shape
    tabs = _fft_tables(2 * l, n2_len)
    xs = _hy_shortconv(p3, conv_w, conv_b)
    h2d, nrm = _hyena_filters(l, *[w.astype(F32) for w in filt])
    kf = _hy_kf(h2d, nrm, tabs[1], tabs[3], n2_len)
    tiles = BR // LANE
    z1 = _hy_conv(xs, 0, xs, tiles, kf, bias, tabs, n2_len, 0)
    return _hy_conv(z1, 0, xs, 2 * tiles, kf, bias, tabs, n2_len, 1)


def _nt(a, b):
    return lax.dot_general(a, b, (((1,), (1,)), ((), ())), preferred_element_type=F32)


def _tn(a, b):
    return lax.dot_general(a, b, (((0,), (0,)), ((), ())), preferred_element_type=F32)


def _ret_kernel(q_ref, k_ref, v_ref, cos_ref, sin_ref, lg_ref, nw_ref, s0_ref, o_ref, sfin_ref,
                vb_ref, qf_ref, qb_ref, kf_ref, kb_ref, *, nchunks):
    c = ATT_CHUNK
    lg = lg_ref[0]
    jrow = lax.broadcasted_iota(jnp.int32, (c, c), 0)
    icol = lax.broadcasted_iota(jnp.int32, (c, c), 1)
    jf = jrow.astype(F32)
    dsym = jnp.exp(jnp.abs(jrow - icol).astype(F32) * lg) * jnp.where(jrow == icol, 2.0, 1.0)
    dq_f = jnp.exp((jf + 1.0) * lg)
    dq_b = jnp.exp((c - jf) * lg)
    dk_f = jnp.exp((c - 1.0 - jf) * lg)
    dk_b = jnp.exp(jf * lg)
    dchunk = jnp.exp(c * lg)

    def rope(x, r0):
        return x * cos_ref[pl.ds(r0, c), :] + pltpu.roll(x, RET_DK // 2, axis=1) * sin_ref[pl.ds(r0, c), :]

    def intra_step(n, _):
        r0 = pl.multiple_of(n * c, c)
        rows = pl.ds(r0, c)
        q = rope(q_ref[0, rows, :], r0)
        k = rope(k_ref[0, rows, :], r0) * (RET_DK ** -0.5)
        v = v_ref[0, rows, :].astype(BF16)
        att = _nt(q.astype(BF16), k.astype(BF16)) * dsym
        o_ref[0, rows, :] = jnp.dot(att.astype(BF16), v, preferred_element_type=F32)
        vb_ref[rows, :] = v
        qf_ref[rows, :] = (q * dq_f).astype(BF16)
        qb_ref[rows, :] = (q * dq_b).astype(BF16)
        kf_ref[rows, :] = (k * dk_f).astype(BF16)
        kb_ref[rows, :] = (k * dk_b).astype(BF16)
        return 0

    lax.fori_loop(0, nchunks, intra_step, 0, unroll=_ATT_UNROLL)

    def sweep(lo, hi, carry, finish):
        def step(i, carry):
            s_f, s_b = carry
            rf = pl.ds(pl.multiple_of(i * c, c), c)
            rb = pl.ds(pl.multiple_of((nchunks - 1 - i) * c, c), c)
            o_f = o_ref[0, rf, :] + jnp.dot(qf_ref[rf, :], s_f.astype(BF16), preferred_element_type=F32)
            o_b = o_ref[0, rb, :] + jnp.dot(qb_ref[rb, :], s_b.astype(BF16), preferred_element_type=F32)
            if finish:
                o_f = o_f * lax.rsqrt(jnp.mean(o_f * o_f, axis=-1, keepdims=True) + EPS) * nw_ref[...]
                o_b = o_b * lax.rsqrt(jnp.mean(o_b * o_b, axis=-1, keepdims=True) + EPS) * nw_ref[...]
            o_ref[0, rf, :] = o_f
            o_ref[0, rb, :] = o_b
            return (dchunk * s_f + _tn(kf_ref[rf, :], vb_ref[rf, :]),
                    dchunk * s_b + _tn(kb_ref[rb, :], vb_ref[rb, :]))
        return lax.fori_loop(lo, hi, step, carry)

    carry = sweep(0, nchunks // 2, (s0_ref[0, 0, 0], s0_ref[0, 0, 1]), False)
    s_f, s_b = sweep(nchunks // 2, nchunks, carry, True)
    sfin_ref[0, 0, 0] = s_f
    sfin_ref[0, 0, 1] = s_b


def _retention(p3, cos2, sin2, norm_w, s0):
    bsz, l, _ = p3.shape
    lgs = jnp.log(1.0 - 2.0 ** (-5.0 - jnp.arange(RET_HEADS, dtype=F32)))
    lgs = jnp.broadcast_to(lgs[:, None, None], (RET_HEADS, 1, LANE))
    tab = pl.BlockSpec((l, LANE), lambda b, h: (0, 0))
    st = pl.BlockSpec((1, 1, 2, RET_DK, LANE), lambda b, h: (b, h, 0, 0, 0))
    return pl.pallas_call(
        functools.partial(_ret_kernel, nchunks=l // ATT_CHUNK),
        grid=(bsz, RET_HEADS),
        in_specs=[pl.BlockSpec((1, l, LANE), lambda b, h: (b, 0, COL_RET_Q + h)),
                  pl.BlockSpec((1, l, LANE), lambda b, h: (b, 0, COL_RET_K + h)),
                  pl.BlockSpec((1, l, LANE), lambda b, h: (b, 0, COL_RET_V + h)),
                  tab, tab,
                  pl.BlockSpec((1, 1, LANE), lambda b, h: (h, 0, 0)),
                  pl.BlockSpec((1, LANE), lambda b, h: (0, h)),
                  st],
        out_specs=[pl.BlockSpec((1, l, LANE), lambda b, h: (b, 0, h)), st],
        out_shape=[jax.ShapeDtypeStruct((bsz, l, BR), F32),
                   jax.ShapeDtypeStruct((bsz, RET_HEADS, 2, RET_DK, LANE), F32)],
        scratch_shapes=[pltpu.VMEM((l, LANE), BF16)] * 5,
        compiler_params=_cparams(("parallel", "parallel")),
        name="retention",
    )(p3, p3, p3, cos2, sin2, lgs, norm_w.reshape(1, BR), s0)


def _rope_tables(ang):
    cos, sin = jnp.cos(ang), jnp.sin(ang)
    return jnp.concatenate([cos, cos], -1), jnp.concatenate([-sin, sin], -1)


def _latent_angles(rows):
    half = RET_DK // 4
    inv = ROPE_BASE ** (-jnp.arange(half, dtype=F32) / half)
    r = jnp.repeat(jnp.arange(rows, dtype=F32), GRID_W)
    cl = jnp.tile(jnp.arange(GRID_W, dtype=F32), rows)
    return jnp.concatenate([r[:, None] * inv, cl[:, None] * inv], axis=-1)


def _ctx_angles(n_ctx):
    n = RET_DK // 2
    inv = ROPE_BASE ** (-jnp.arange(n, dtype=F32) / n)
    return jnp.arange(n_ctx, dtype=F32)[:, None] * inv


def _gla_kernel(q_ref, k_ref, v_ref, lr_ref, gw_ref, gb_ref, nw_ref, s0_ref, o_ref, sfin_ref,
                vb_ref, qf_ref, qb_ref, kf_ref, kb_ref, df_ref, db_ref, *, nchunks):
    c = ATT_CHUNK
    jrow = lax.broadcasted_iota(jnp.int32, (c, c), 0)
    icol = lax.broadcasted_iota(jnp.int32, (c, c), 1)
    tri_f = jnp.where(icol <= jrow, 1.0, 0.0).astype(BF16)
    tri_b = jnp.where(icol >= jrow, 1.0, 0.0).astype(BF16)
    lane = lax.broadcasted_iota(jnp.int32, (1, LANE), 1)
    head0 = lane < GLA_DK

    def decays(r0, d):
        x = jnp.dot(lr_ref[0, pl.ds(r0, c), :].astype(BF16), gw_ref[d], preferred_element_type=F32) + gb_ref[d]
        la = (jnp.minimum(x, 0.0) - jnp.log(1.0 + jnp.exp(-jnp.abs(x)))) * (1.0 / GLA_TAU)
        hi = la.astype(BF16)
        lo = (la - hi.astype(F32)).astype(BF16)
        cs = jnp.dot(tri_f if d == 0 else tri_b, jnp.concatenate([hi, lo], axis=1), preferred_element_type=F32)
        return cs[:, :LANE] + cs[:, LANE:]

    def intra_step(n, _):
        r0 = pl.multiple_of(n * c, c)
        rows = pl.ds(r0, c)
        q = q_ref[0, rows, :] * (GLA_DK ** -0.5)
        k = k_ref[0, rows, :]
        bf = decays(r0, 0)
        bb = decays(r0, 1)
        ref_f = bf[c // 2 - 1:c // 2, :]
        ref_b = bb[c // 2:c // 2 + 1, :]
        qf = q * jnp.exp(bf - ref_f)
        kf = (k * jnp.exp(ref_f - bf)).astype(BF16)
        qb = q * jnp.exp(bb - ref_b)
        kb = (k * jnp.exp(ref_b - bb)).astype(BF16)
        for h in range(2):
            hm = head0 if h == 0 else jnp.logical_not(head0)
            v = v_ref[0, rows, h * GLA_DV:(h + 1) * GLA_DV].astype(BF16)
            att = (jnp.where(icol <= jrow, _nt(jnp.where(hm, qf, 0.0).astype(BF16), kf), 0.0)
                   + jnp.where(icol >= jrow, _nt(jnp.where(hm, qb, 0.0).astype(BF16), kb), 0.0))
            o_ref[0, rows, h * GLA_DV:(h + 1) * GLA_DV] = jnp.dot(att.astype(BF16), v, preferred_element_type=F32)
            vb_ref[rows, h * GLA_DV:(h + 1) * GLA_DV] = v
        bf_last, bb_last = bf[c - 1:c, :], bb[0:1, :]
        qf_ref[rows, :] = (q * jnp.exp(bf)).astype(BF16)
        qb_ref[rows, :] = (q * jnp.exp(bb)).astype(BF16)
        kf_ref[rows, :] = (k * jnp.exp(bf_last - bf)).astype(BF16)
        kb_ref[rows, :] = (k * jnp.exp(bb_last - bb)).astype(BF16)
        d0 = pl.multiple_of(n * 8, 8)
        df_ref[pl.ds(d0, 8), :] = jnp.broadcast_to(jnp.exp(bf_last), (8, LANE))
        db_ref[pl.ds(d0, 8), :] = jnp.broadcast_to(jnp.exp(bb_last), (8, LANE))
        return 0

    lax.fori_loop(0, nchunks, intra_step, 0, unroll=_ATT_UNROLL)

    def sweep(lo, hi, carry, finish):
        def step(i, carry):
            sts = list(carry)
            new = []
            for d in range(2):
                n = i if d == 0 else nchunks - 1 - i
                rows = pl.ds(pl.multiple_of(n * c, c), c)
                qs = (qf_ref if d == 0 else qb_ref)[rows, :]
                kd = (kf_ref if d == 0 else kb_ref)[rows, :]
                dec = (df_ref if d == 0 else db_ref)[pl.ds(pl.multiple_of(n * 8, 8), 1), :]
                for h in range(2):
                    hm = head0 if h == 0 else jnp.logical_not(head0)
                    cols = slice(h * GLA_DV, (h + 1) * GLA_DV)
                    st = sts[2 * d + h]
                    o = o_ref[0, rows, cols] + _nt(qs, st.astype(BF16))
                    if finish:
                        o = o * lax.rsqrt(jnp.mean(o * o, axis=-1, keepdims=True) + EPS) * nw_ref[:, cols]
                    o_ref[0, rows, cols] = o
                    new.append(jnp.where(hm, dec * st + _tn(vb_ref[rows, cols], kd), 0.0))
            return tuple(new)
        return lax.fori_loop(lo, hi, step, carry)

    carry = (s0_ref[0, 0, 0, 0], s0_ref[0, 0, 0, 1], s0_ref[0, 0, 1, 0], s0_ref[0, 0, 1, 1])
    carry = sweep(0, nchunks // 2, carry, False)
    s_fin = sweep(nchunks // 2, nchunks, carry, True)
    sfin_ref[0, 0, 0, 0] = s_fin[0]
    sfin_ref[0, 0, 0, 1] = s_fin[1]
    sfin_ref[0, 0, 1, 0] = s_fin[2]
    sfin_ref[0, 0, 1, 1] = s_fin[3]


def _gla(p3, gate_w, gate_b, norm_w, s0):
    bsz, l, _ = p3.shape
    pairs = GLA_HEADS // 2
    gw = jnp.zeros((2, LANE, GLA_HEADS * GLA_DK), F32)
    for d in range(2):
        gw = gw.at[d, d * GLA_LR:(d + 1) * GLA_LR, :].set(gate_w[d].astype(F32))
    st = pl.BlockSpec((1, 1, 2, 2, GLA_DV, LANE), lambda b, h: (b, h, 0, 0, 0, 0))
    return pl.pallas_call(
        functools.partial(_gla_kernel, nchunks=l // ATT_CHUNK),
        grid=(bsz, pairs),
        in_specs=[pl.BlockSpec((1, l, LANE), lambda b, h: (b, 0, COL_GLA_Q + h)),
                  pl.BlockSpec((1, l, LANE), lambda b, h: (b, 0, COL_GLA_K + h)),
                  pl.BlockSpec((1, l, 2 * GLA_DV), lambda b, h: (b, 0, COL_GLA_V // 2 + h)),
                  pl.BlockSpec((1, l, LANE), lambda b, h: (b, 0, COL_GLA_LR)),
                  pl.BlockSpec((2, LANE, LANE), lambda b, h: (0, 0, h)),
                  pl.BlockSpec((2, 1, LANE), lambda b, h: (0, 0, h)),
                  pl.BlockSpec((1, 2 * GLA_DV), lambda b, h: (0, h)),
                  st],
        out_specs=[pl.BlockSpec((1, l, 2 * GLA_DV), lambda b, h: (b, 0, h)), st],
        out_shape=[jax.ShapeDtypeStruct((bsz, l, BR), F32),
                   jax.ShapeDtypeStruct((bsz, pairs, 2, 2, GLA_DV, LANE), F32)],
        scratch_shapes=([pltpu.VMEM((l, 2 * GLA_DV), BF16)] + [pltpu.VMEM((l, LANE), BF16)] * 4
                        + [pltpu.VMEM((8 * l // ATT_CHUNK, LANE), F32)] * 2),
        compiler_params=_cparams(("parallel", "parallel")),
        name="gla",
    )(p3, p3, p3, p3, gw.astype(BF16), gate_b.astype(F32).reshape(2, 1, GLA_HEADS * GLA_DK),
      norm_w.reshape(1, BR), s0)


def _mixers(p2, xs, bsz, prm, i, s5_tabs, rope, s5_h0, ret_s0, gla_s0, with_hyena):
    m = p2.shape[0]
    l = m // bsz
    p3 = p2.reshape(bsz, l, NP)
    tabs = s5_tabs[:3] + (s5_tabs[3][:, :, :, :l // (2 * S5_T) + 1],)
    y, s5_fin = _s5_core(xs, tabs, s5_h0)
    s5 = _s5_glu(y, p2, prm['s5_d'][i], prm['s5_glu_w'][i], prm['s5_glu_b'][i])
    hy = None
    if with_hyena:
        filt = tuple(prm[n][i] for n in ('hy_w1', 'hy_b1', 'hy_f1', 'hy_w2', 'hy_b2', 'hy_f2',
                                         'hy_w3', 'hy_b3', 'hy_f3', 'hy_w4'))
        hy = _hyena(p3, prm['hy_conv_w'][i], prm['hy_conv_b'][i], filt, prm['hy_bias'][i].astype(F32),
                    64 if l >= 512 else 16).reshape(m, BR)
    ret, ret_fin = _retention(p3, rope[0], rope[1], prm['ret_norm_w'][i], ret_s0)
    gla, gla_fin = _gla(p3, prm['gla_gate_w'][i], prm['gla_gate_b'][i], prm['gla_norm_w'][i], gla_s0)
    return (s5, hy, ret.reshape(m, BR), gla.reshape(m, BR)), (s5_fin, ret_fin, gla_fin)


def kernel(x, c, ctx, c_ctx, norm_w, ada_w, ada_b, w_in, w_out, s5_a_re, s5_a_im, s5_log_dt, s5_b_re, s5_b_im, s5_c_re, s5_c_im, s5_d, s5_glu_w, s5_glu_b, hy_conv_w, hy_conv_b, hy_w1, hy_b1, hy_f1, hy_w2, hy_b2, hy_f2, hy_w3, hy_b3, hy_f3, hy_w4, hy_bias, ret_norm_w, gla_gate_w, gla_gate_b, gla_norm_w, final_norm_w):
    prm = dict(s5_a_re=s5_a_re, s5_a_im=s5_a_im, s5_log_dt=s5_log_dt, s5_b_re=s5_b_re, s5_b_im=s5_b_im,
               s5_c_re=s5_c_re, s5_c_im=s5_c_im, s5_d=s5_d, s5_glu_w=s5_glu_w, s5_glu_b=s5_glu_b,
               hy_conv_w=hy_conv_w, hy_conv_b=hy_conv_b, hy_w1=hy_w1, hy_b1=hy_b1, hy_f1=hy_f1,
               hy_w2=hy_w2, hy_b2=hy_b2, hy_f2=hy_f2, hy_w3=hy_w3, hy_b3=hy_b3, hy_f3=hy_f3,
               hy_w4=hy_w4, hy_bias=hy_bias, ret_norm_w=ret_norm_w, gla_gate_w=gla_gate_w,
               gla_gate_b=gla_gate_b, gla_norm_w=gla_norm_w)
    bsz, l, d = x.shape
    lc = ctx.shape[1]
    depth = w_in.shape[0]
    cc = jnp.concatenate([c, c_ctx[None, :], jnp.zeros((8 - bsz - 1, d), F32)], axis=0)
    mod = _ada(cc, ada_w, ada_b)
    rope_l = _rope_tables(_latent_angles(l // GRID_W))
    rope_c = _rope_tables(_ctx_angles(lc))
    s5_zero = jnp.zeros((S5_GROUPS // 2, 2, 2, 8, 2 * S5_STATE), F32)
    ret_zero = jnp.zeros((bsz, RET_HEADS, 2, RET_DK, LANE), F32)
    gla_zero = jnp.zeros((bsz, GLA_HEADS // 2, 2, 2, GLA_DV, LANE), F32)

    x_l = x.reshape(bsz * l, d)
    x_c = ctx.reshape(bsz * lc, d)
    for i in range(depth):
        last = i == depth - 1
        sh, sc, gt = mod[i, :, :d], mod[i, :, d:2 * d], mod[i, :, 2 * d:]
        w = _permute_w_in(w_in[i])
        wo = w_out[i].astype(BF16)
        p_c, xs_c = _inproj(x_c, norm_w[i], sc[bsz:bsz + 1, None, :], sh[bsz:bsz + 1, None, :], w, bsz * lc)
        p_l, xs_l = _inproj(x_l, norm_w[i], sc[:bsz, None, :], sh[:bsz, None, :], w, l)
        s5_tabs = _s5_tables(s5_a_re[i], s5_a_im[i], s5_log_dt[i], s5_b_re[i], s5_b_im[i], s5_c_re[i],
                             s5_c_im[i], l // (2 * S5_T))
        ys_c, (s5_fin, ret_fin, gla_fin) = _mixers(p_c, xs_c, bsz, prm, i, s5_tabs, rope_c, s5_zero, ret_zero,
                                                   gla_zero, with_hyena=not last)
        ys_l, _ = _mixers(p_l, xs_l, bsz, prm, i, s5_tabs, rope_l, _s5_next_h0(s5_fin), ret_fin, gla_fin,
                          with_hyena=True)
        x_l = _outproj(x_l, ys_l, p_l, wo, gt[:bsz, None, :], final_norm_w, l, final=last)
        if not last:
            x_c = _outproj(x_c, ys_c, p_c, wo, gt[bsz:bsz + 1, None, :], final_norm_w, bsz * lc, final=False)
    return x_l.reshape(bsz, l, d)
```

```python
import functools
import math

import numpy as np
import jax
import jax.numpy as jnp
from jax import lax
from jax.experimental import pallas as pl
from jax.experimental.pallas import tpu as pltpu

F32 = jnp.float32
BF16 = jnp.bfloat16

EPS = 1e-6
GRID_W = 64
BR = 512
S5_GSIZE = 16
S5_GROUPS = BR // S5_GSIZE
S5_STATE = 64
S5_T = 16
HY_ORDER = 2
HY_EMB = 33
HY_BANDS = (HY_EMB - 1) // 2
HY_FAST_DECAY = 0.3
HY_SLOW_DECAY = 1.5
HY_TARGET = 1e-2
RET_HEADS = 4
RET_DK = 128
GLA_HEADS = 4
GLA_DK = 64
GLA_DV = 128
GLA_LR = 16
GLA_TAU = 16.0
ROPE_BASE = 10000.0
ATT_CHUNK = 128
_FFT_UNROLL = 8
_ATT_UNROLL = 2

LANE = 128
VMEM_LIMIT = 56 * 1024 * 1024

COL_S5_U = 0
COL_RET_K = 4
COL_RET_V = 8
COL_GLA_V = 12
COL_GATES = 16
COL_HY_X = 32
COL_RET_Q = 44
COL_GLA_Q = 48
COL_GLA_K = 50
COL_GLA_LR = 52
NP_UNITS = 54
NP = NP_UNITS * LANE
_SEG_DST = (
    (BR, COL_S5_U), (BR, COL_GATES),
    (3 * BR, COL_HY_X), (BR, COL_GATES + 4),
    (BR, COL_RET_Q), (BR, COL_RET_K), (BR, COL_RET_V), (BR, COL_GATES + 8),
    (BR // 2, COL_GLA_Q), (BR // 2, COL_GLA_K), (BR, COL_GLA_V), (2 * GLA_LR, COL_GLA_LR),
    (BR, COL_GATES + 12),
)


def _cparams(sem):
    return pltpu.CompilerParams(dimension_semantics=sem, vmem_limit_bytes=VMEM_LIMIT)


def _silu(x):
    return x * jax.nn.sigmoid(x)


def _permute_w_in(w):
    lead = w.shape[:-1]
    segs, off = [], 0
    for size, unit in _SEG_DST:
        segs.append((unit * LANE, off, size))
        off += size
    pieces, pos = [], 0
    for dst, src, size in sorted(segs):
        if dst > pos:
            pieces.append(jnp.zeros(lead + (dst - pos,), BF16))
        pieces.append(w[..., src:src + size].astype(BF16))
        pos = dst + size
    if pos < NP:
        pieces.append(jnp.zeros(lead + (NP - pos,), BF16))
    return jnp.concatenate(pieces, axis=-1)


def _ada_kernel(c_ref, w_ref, b_ref, o_ref):
    a = _silu(c_ref[...]).astype(BF16)
    o_ref[0] = jnp.dot(a, w_ref[0].astype(BF16), preferred_element_type=F32) + b_ref[0]


def _ada(cc, ada_w, ada_b, tn=768):
    depth, d, n = ada_w.shape
    return pl.pallas_call(
        _ada_kernel,
        grid=(depth, n // tn),
        in_specs=[pl.BlockSpec((8, d), lambda i, j: (0, 0)),
                  pl.BlockSpec((1, d, tn), lambda i, j: (i, 0, j)),
                  pl.BlockSpec((1, 1, tn), lambda i, j: (i, 0, j))],
        out_specs=pl.BlockSpec((1, 8, tn), lambda i, j: (i, 0, j)),
        out_shape=jax.ShapeDtypeStruct((depth, 8, n), F32),
        compiler_params=_cparams(("parallel", "parallel")),
        name="ada_mod",
    )(cc, ada_w, ada_b.reshape(depth, 1, n))


_S5_SLOT = 2 * S5_GSIZE
_S5_PAIRS = S5_GROUPS // 2
_S5_ROW = S5_T * _S5_SLOT


def _s5_lane_perm():
    perm = np.zeros((4, _S5_ROW), np.int32)
    for s in range(4):
        for tau in range(S5_T):
            for c in range(_S5_SLOT):
                gl, i = divmod(c, S5_GSIZE)
                perm[s, LANE * (tau // 4) + _S5_SLOT * ((s + tau) % 4) + c] = (gl * S5_T + tau) * S5_GSIZE + i
    return perm


def _s5_to_chunk_rows(nat_ref, xs_ref, rows16):
    slot = lax.broadcasted_iota(jnp.int32, (rows16, LANE), 1) // _S5_SLOT
    for q in range(4):
        for qd in range(4):
            rolled = []
            for t in range(4):
                piece = nat_ref[q, pl.ds(4 * qd + t, rows16, stride=S5_T), :]
                rolled.append(pltpu.roll(piece, _S5_SLOT * t, axis=1) if t else piece)
            for s in range(4):
                acc = rolled[0]
                for t in range(1, 4):
                    acc = jnp.where(slot == (s + t) % 4, rolled[t], acc)
                xs_ref[4 * q + s, :, qd * LANE:(qd + 1) * LANE] = acc.astype(xs_ref.dtype)


def _s5_from_chunk_rows(y_ref, nat_ref, rows16):
    slot = lax.broadcasted_iota(jnp.int32, (rows16, LANE), 1) // _S5_SLOT
    for q in range(4):
        for qd in range(4):
            src = [y_ref[4 * q + s, :, qd * LANE:(qd + 1) * LANE] for s in range(4)]
            for t in range(4):
                acc = src[0]
                for s in range(1, 4):
                    acc = jnp.where(slot == (s + t) % 4, src[s], acc)
                if t:
                    acc = pltpu.roll(acc, LANE - _S5_SLOT * t, axis=1)
                nat_ref[q, pl.ds(4 * qd + t, rows16, stride=S5_T), :] = acc


def _inproj_kernel(x_ref, nw_ref, sc_ref, sh_ref, w_ref, o_ref, xs_ref, h_ref, nat_ref, *, rows16):
    j = pl.program_id(1)

    @pl.when(j == 0)
    def _():
        x = x_ref[...]
        y = x * lax.rsqrt(jnp.mean(x * x, axis=-1, keepdims=True) + EPS) * nw_ref[...]
        h_ref[...] = (y * (1.0 + sc_ref[0]) + sh_ref[0]).astype(BF16)

    res = jnp.dot(h_ref[...], w_ref[...], preferred_element_type=F32)
    o_ref[...] = res.astype(o_ref.dtype)

    @pl.when(j == 0)
    def _():
        for q in range(BR // LANE):
            nat_ref[q] = res[:, q * LANE:(q + 1) * LANE]
        _s5_to_chunk_rows(nat_ref, xs_ref, rows16)


def _inproj(x2, nw, sc, sh, w, rows_per_mod, tm=1024, tn=768):
    m, d = x2.shape
    tm = min(tm, rows_per_mod)
    per = rows_per_mod // tm
    rows16 = tm // S5_T
    assert COL_S5_U == 0 and tn >= BR
    return pl.pallas_call(
        functools.partial(_inproj_kernel, rows16=rows16),
        grid=(m // tm, NP // tn),
        in_specs=[pl.BlockSpec((tm, d), lambda i, j: (i, 0)),
                  pl.BlockSpec((1, d), lambda i, j: (0, 0)),
                  pl.BlockSpec((1, 1, d), lambda i, j: (i // per, 0, 0)),
                  pl.BlockSpec((1, 1, d), lambda i, j: (i // per, 0, 0)),
                  pl.BlockSpec((d, tn), lambda i, j: (0, j))],
        out_specs=[pl.BlockSpec((tm, tn), lambda i, j: (i, j)),
                   pl.BlockSpec((_S5_PAIRS, rows16, _S5_ROW), lambda i, j: (0, i, 0))],
        out_shape=[jax.ShapeDtypeStruct((m, NP), BF16),
                   jax.ShapeDtypeStruct((_S5_PAIRS, m // S5_T, _S5_ROW), BF16)],
        scratch_shapes=[pltpu.VMEM((tm, d), BF16), pltpu.VMEM((BR // LANE, tm, LANE), F32)],
        compiler_params=_cparams(("parallel", "arbitrary")),
        name="inproj",
    )(x2, nw.reshape(1, d), sc, sh, w)


def _outproj_kernel(x_ref, y0_ref, y1_ref, y2_ref, y3_ref, g_ref, w_ref, gt_ref, fw_ref, o_ref, *, final):
    acc = None
    for k, y_ref in enumerate((y0_ref, y1_ref, y2_ref, y3_ref)):
        gated = (y_ref[...] * _silu(g_ref[:, k * BR:(k + 1) * BR].astype(F32))).astype(BF16)
        part = jnp.dot(gated, w_ref[k * BR:(k + 1) * BR, :], preferred_element_type=F32)
        acc = part if acc is None else acc + part
    x = x_ref[...] + gt_ref[0] * acc
    if final:
        x = x * lax.rsqrt(jnp.mean(x * x, axis=-1, keepdims=True) + EPS) * fw_ref[...]
    o_ref[...] = x


def _outproj(x2, ys, p, w_out, gt, fw, rows_per_mod, final, tm=256):
    m, d = x2.shape
    tm = min(tm, rows_per_mod)
    per = rows_per_mod // tm
    gates_blk = COL_GATES * LANE // (4 * BR)
    yspec = pl.BlockSpec((tm, BR), lambda i: (i, 0))
    return pl.pallas_call(
        functools.partial(_outproj_kernel, final=final),
        grid=(m // tm,),
        in_specs=[pl.BlockSpec((tm, d), lambda i: (i, 0)), yspec, yspec, yspec, yspec,
                  pl.BlockSpec((tm, 4 * BR), lambda i: (i, gates_blk)),
                  pl.BlockSpec((4 * BR, d), lambda i: (0, 0)),
                  pl.BlockSpec((1, 1, d), lambda i: (i // per, 0, 0)),
                  pl.BlockSpec((1, d), lambda i: (0, 0))],
        out_specs=pl.BlockSpec((tm, d), lambda i: (i, 0)),
        out_shape=jax.ShapeDtypeStruct((m, d), F32),
        compiler_params=_cparams(("parallel",)),
        name="outproj",
    )(x2, *ys, p, w_out, gt, fw.reshape(1, d))


def _s5_tables(a_re, a_im, log_dt, b_re, b_im, c_re, c_im, nch):
    hp = lax.Precision.HIGHEST
    t_len, g_n, p_n, s_n = S5_T, S5_GROUPS, S5_STATE, S5_GSIZE
    a_re, a_im = a_re.astype(F32), a_im.astype(F32)
    dt = jnp.exp(log_dt.astype(F32))[:, :, None]
    lam_re, lam_im = a_re * dt, a_im * dt

    def power(tau):
        tau = tau.astype(F32)[:, None, None, None]
        mag = jnp.exp(lam_re[None] * tau)
        return mag * jnp.cos(lam_im[None] * tau), mag * jnp.sin(lam_im[None] * tau)

    ab_re, ab_im = power(jnp.ones((1,)))
    ab_re, ab_im = ab_re[0], ab_im[0]
    den = a_re * a_re + a_im * a_im
    nr = ab_re - 1.0
    co_re = (nr * a_re + ab_im * a_im) / den
    co_im = (ab_im * a_re - nr * a_im) / den
    b_re, b_im = b_re.astype(F32), b_im.astype(F32)
    bco_re = co_re[..., None] * b_re - co_im[..., None] * b_im
    bco_im = co_re[..., None] * b_im + co_im[..., None] * b_re
    c_re, c_im = c_re.astype(F32), c_im.astype(F32)

    pr, pi = power(jnp.arange(t_len + 1))
    ca_re = c_re[None] * pr[:, :, :, None, :] - c_im[None] * pi[:, :, :, None, :]
    ca_im = c_re[None] * pi[:, :, :, None, :] + c_im[None] * pr[:, :, :, None, :]
    ca_cat = jnp.concatenate([ca_re[:t_len], -ca_im[:t_len]], axis=-1).transpose(1, 2, 0, 3, 4)
    bco_cat = jnp.concatenate([bco_re, bco_im], axis=-2)
    kk = jnp.einsum('dgmp,dgpj->dgmj', ca_cat.reshape(2, g_n, t_len * s_n, 2 * p_n), bco_cat, precision=hp)
    kk = kk.reshape(2, g_n, t_len, s_n, s_n).transpose(2, 0, 1, 3, 4)
    kf, kb = kk[:, 0], kk[:, 1]
    kfull = jnp.concatenate([kb[:0:-1], (kf[0] + kb[0])[None], kf[1:]], axis=0)
    tt = jnp.arange(t_len)
    idx = tt[None, :] - tt[:, None] + (t_len - 1)
    mm = kfull[idx]
    mmat = mm.transpose(2, 0, 4, 1, 3).reshape(g_n, t_len * s_n, t_len * s_n)

    wexp = jnp.stack([pr[t_len - 1 - tt, 0], pr[tt, 1]], 0), jnp.stack([pi[t_len - 1 - tt, 0], pi[tt, 1]], 0)
    w_re = wexp[0][..., None] * bco_re[:, None] - wexp[1][..., None] * bco_im[:, None]
    w_im = wexp[0][..., None] * bco_im[:, None] + wexp[1][..., None] * bco_re[:, None]
    w4 = jnp.stack([w_re[0], w_im[0], w_re[1], w_im[1]], 0)
    w4 = w4.transpose(0, 2, 1, 4, 3).reshape(4, g_n // 2, 2, t_len * s_n, p_n)
    eye2 = jnp.eye(2, dtype=F32)
    wpair = jnp.einsum('kqlrp,lm->qlrkmp', w4, eye2).reshape(g_n // 2, 2 * t_len * s_n, 4 * 2 * p_n)

    vexp_f, vexp_b = tt + 1, t_len - tt
    v4 = jnp.stack([ca_re[vexp_f, 0], -ca_im[vexp_f, 0], ca_re[vexp_b, 1], -ca_im[vexp_b, 1]], 0)
    v4 = v4.transpose(0, 2, 4, 1, 3).reshape(4, g_n // 2, 2, p_n, t_len * s_n)
    vpair = jnp.einsum('kqlpr,lm->qklpmr', v4, eye2).reshape(g_n // 2, 4 * 2 * p_n, 2 * t_len * s_n)

    qr, qi = power(t_len * jnp.arange(nch + 1))
    pw = jnp.stack([qr, qi], 0)
    pw = pw.reshape(2, nch + 1, 2, g_n // 2, 2 * p_n).transpose(3, 2, 0, 1, 4)
    mpair = jnp.einsum('qlrc,lm->qlrmc', mmat.reshape(g_n // 2, 2, t_len * s_n, t_len * s_n), eye2)
    mpair = mpair.reshape(g_n // 2, _S5_ROW, _S5_ROW)
    onehot = jnp.asarray(np.eye(_S5_ROW, dtype=np.float32)[_s5_lane_perm()]).astype(BF16)

    def rows_to_lane_order(a):
        a4 = a.astype(BF16).reshape(g_n // 8, 4, _S5_ROW, a.shape[-1])
        return jnp.einsum('sno,qsoc->qsnc', onehot, a4, preferred_element_type=F32).astype(BF16).reshape(a.shape)

    def cols_to_lane_order(a):
        a4 = a.astype(BF16).reshape(g_n // 8, 4, a.shape[1], _S5_ROW)
        return jnp.einsum('qsro,sno->qsrn', a4, onehot, preferred_element_type=F32).astype(BF16).reshape(a.shape)

    return cols_to_lane_order(rows_to_lane_order(mpair)), rows_to_lane_order(wpair), cols_to_lane_order(vpair), pw


def _cmul(ar, ai, br, bi):
    return ar * br - ai * bi, ar * bi + ai * br


def _s5_kernel(x_ref, m_ref, w_ref, v_ref, pw_ref, h0_ref, y_ref, fin_ref, s_ref, h_ref, *, nch):
    ln = 2 * S5_STATE
    x = x_ref[0]
    s_all = jnp.dot(x, w_ref[0], preferred_element_type=F32)
    for k in range(4):
        s_ref[k] = s_all[:, k * ln:(k + 1) * ln]
    a_rf, a_if = pw_ref[0, 0, 0, 1:2, :], pw_ref[0, 0, 1, 1:2, :]
    a_rb, a_ib = pw_ref[0, 1, 0, 1:2, :], pw_ref[0, 1, 1, 1:2, :]

    def tile(c):
        return pl.ds(c, 8, stride=nch)

    def scan_step(s, carry):
        hrf, hif, hrb, hib = carry
        rf, rb = tile(s), tile(nch - 1 - s)
        h_ref[0, rf, :] = hrf
        h_ref[1, rf, :] = hif
        h_ref[2, rb, :] = hrb
        h_ref[3, rb, :] = hib
        pr, pi = _cmul(a_rf, a_if, hrf, hif)
        qr, qi = _cmul(a_rb, a_ib, hrb, hib)
        return (pr + s_ref[0, rf, :], pi + s_ref[1, rf, :], qr + s_ref[2, rb, :], qi + s_ref[3, rb, :])

    init = (h0_ref[0, 0, 0], h0_ref[0, 0, 1], h0_ref[0, 1, 0], h0_ref[0, 1, 1])
    hrf, hif, hrb, hib = lax.fori_loop(0, nch, scan_step, init, unroll=2)

    odd = (lax.broadcasted_iota(jnp.int32, (8, ln), 0) % 2) == 1
    crf = jnp.where(odd, pltpu.roll(hrf, 1, axis=0), 0.0)
    cif = jnp.where(odd, pltpu.roll(hif, 1, axis=0), 0.0)
    crb = jnp.where(odd, 0.0, pltpu.roll(hrb, 7, axis=0))
    cib = jnp.where(odd, 0.0, pltpu.roll(hib, 7, axis=0))

    def fix_step(s, _):
        rf, rb = tile(s), tile(nch - 1 - s)
        pr, pi = _cmul(pw_ref[0, 0, 0, pl.ds(s, 1), :], pw_ref[0, 0, 1, pl.ds(s, 1), :], crf, cif)
        qr, qi = _cmul(pw_ref[0, 1, 0, pl.ds(s, 1), :], pw_ref[0, 1, 1, pl.ds(s, 1), :], crb, cib)
        h_ref[0, rf, :] = h_ref[0, rf, :] + pr
        h_ref[1, rf, :] = h_ref[1, rf, :] + pi
        h_ref[2, rb, :] = h_ref[2, rb, :] + qr
        h_ref[3, rb, :] = h_ref[3, rb, :] + qi
        return 0

    lax.fori_loop(0, nch, fix_step, 0, unroll=2)
    pr, pi = _cmul(pw_ref[0, 0, 0, nch:nch + 1, :], pw_ref[0, 0, 1, nch:nch + 1, :], crf, cif)
    qr, qi = _cmul(pw_ref[0, 1, 0, nch:nch + 1, :], pw_ref[0, 1, 1, nch:nch + 1, :], crb, cib)
    fin_ref[0, 0, 0] = hrf + pr
    fin_ref[0, 0, 1] = hif + pi
    fin_ref[0, 1, 0] = hrb + qr
    fin_ref[0, 1, 1] = hib + qi

    h_all = jnp.concatenate([h_ref[k] for k in range(4)], axis=1).astype(BF16)
    y_ref[0] = (jnp.dot(x, m_ref[0], preferred_element_type=F32)
                + jnp.dot(h_all, v_ref[0], preferred_element_type=F32))


def _s5_core(xs, tables, h0):
    mpair, wpair, vpair, pw = tables
    gp_n, rows, wdt = xs.shape
    nch = rows // 8
    return pl.pallas_call(
        functools.partial(_s5_kernel, nch=nch),
        grid=(gp_n,),
        in_specs=[pl.BlockSpec((1, rows, wdt), lambda g: (g, 0, 0)),
                  pl.BlockSpec((1, wdt, wdt), lambda g: (g, 0, 0)),
                  pl.BlockSpec((1, wdt, wdt), lambda g: (g, 0, 0)),
                  pl.BlockSpec((1, wdt, wdt), lambda g: (g, 0, 0)),
                  pl.BlockSpec((1, 2, 2, nch + 1, 2 * S5_STATE), lambda g: (g, 0, 0, 0, 0)),
                  pl.BlockSpec((1, 2, 2, 8, 2 * S5_STATE), lambda g: (g, 0, 0, 0, 0))],
        out_specs=[pl.BlockSpec((1, rows, wdt), lambda g: (g, 0, 0)),
                   pl.BlockSpec((1, 2, 2, 8, 2 * S5_STATE), lambda g: (g, 0, 0, 0, 0))],
        out_shape=[jax.ShapeDtypeStruct((gp_n, rows, wdt), F32),
                   jax.ShapeDtypeStruct((gp_n, 2, 2, 8, 2 * S5_STATE), F32)],
        scratch_shapes=[pltpu.VMEM((4, rows, 2 * S5_STATE), F32), pltpu.VMEM((4, rows, 2 * S5_STATE), F32)],
        compiler_params=_cparams(("parallel",)),
        name="s5_core",
    )(xs, mpair, wpair, vpair, pw, h0)


def _s5_next_h0(fin):
    g, d, r, _, n = fin.shape
    sw = fin.reshape(g, d, r, 4, 2, n)[:, :, :, :, ::-1, :]
    keep = jnp.array([[1.0, 0.0], [0.0, 1.0]], F32)[None, :, None, None, :, None]
    return (sw * keep).reshape(fin.shape)


def _s5_glu_kernel(y_ref, u_ref, d_ref, w_ref, b_ref, o_ref, nat_ref, *, rows16):
    _s5_from_chunk_rows(y_ref, nat_ref, rows16)
    y = jnp.concatenate([nat_ref[q] for q in range(BR // LANE)], axis=1) + u_ref[...].astype(F32) * d_ref[...]
    g = jax.nn.gelu(y)
    z = jnp.dot(g.astype(BF16), w_ref[...], preferred_element_type=F32) + b_ref[...]
    o_ref[...] = g * jax.nn.sigmoid(z)


def _s5_glu(ys, p, d_skip, glu_w, glu_b, tm=512):
    m = p.shape[0]
    tm = min(tm, m)
    rows16 = tm // S5_T
    return pl.pallas_call(
        functools.partial(_s5_glu_kernel, rows16=rows16),
        grid=(m // tm,),
        in_specs=[pl.BlockSpec((_S5_PAIRS, rows16, _S5_ROW), lambda i: (0, i, 0)),
                  pl.BlockSpec((tm, BR), lambda i: (i, COL_S5_U * LANE // BR)),
                  pl.BlockSpec((1, BR), lambda i: (0, 0)),
                  pl.BlockSpec((BR, BR), lambda i: (0, 0)),
                  pl.BlockSpec((1, BR), lambda i: (0, 0))],
        out_specs=pl.BlockSpec((tm, BR), lambda i: (i, 0)),
        out_shape=jax.ShapeDtypeStruct((m, BR), F32),
        scratch_shapes=[pltpu.VMEM((BR // LANE, tm, LANE), F32)],
        compiler_params=_cparams(("parallel",)),
        name="s5_glu",
    )(ys, p, d_skip.reshape(1, BR), glu_w.astype(BF16), glu_b.reshape(1, BR))


@functools.lru_cache(maxsize=None)
def _fft_tables_np(n_total, n2_len):
    n1_len = n_total // n2_len
    n1h = n1_len // 2
    k1 = np.arange(n1_len, dtype=np.int64)
    n2 = np.arange(n2_len, dtype=np.int64)
    m = (k1[None, :, None] * k1[None, None, :] * n2_len + n2[:, None, None] * k1[None, :, None]) % n_total
    ang = m.astype(np.float64) * (2.0 * math.pi / n_total)
    c, s = np.cos(ang), np.sin(ang)
    ch, sh = c[:, :, :n1h], s[:, :, :n1h]
    gf = np.concatenate([np.concatenate([ch, sh], 2), np.concatenate([-sh, ch], 2)], 1)
    gk = np.concatenate([ch, -sh], 1)
    cht, sht = ch.transpose(0, 2, 1), sh.transpose(0, 2, 1)
    gi = np.concatenate([np.concatenate([cht, -sht], 2), np.concatenate([sht, cht], 2)], 1)
    a2 = ((n2[:, None] * n2[None, :]) % n2_len).astype(np.float64) * (2.0 * math.pi / n2_len)
    c2, s2 = np.cos(a2), np.sin(a2)
    f2 = np.concatenate([np.concatenate([c2, s2], 1), np.concatenate([-s2, c2], 1)], 0)
    f2c = np.concatenate([np.concatenate([c2, -s2], 1), np.concatenate([s2, c2], 1)], 0)
    return tuple(np.asarray(t, np.float32) for t in (gf, gk, gi, f2, f2c))


def _fft_tables(n_total, n2_len):
    return tuple(jnp.asarray(t).astype(BF16) for t in _fft_tables_np(n_total, n2_len))


def _hy_shortconv_kernel(x_ref, w_ref, b_ref, o_ref):
    x = x_ref[0].astype(F32)
    l = x.shape[0]
    row = lax.broadcasted_iota(jnp.int32, x.shape, 0)
    prev = jnp.where(row == 0, 0.0, pltpu.roll(x, 1, axis=0))
    nxt = jnp.where(row == l - 1, 0.0, pltpu.roll(x, l - 1, axis=0))
    o_ref[0] = w_ref[0:1, :] * prev + w_ref[1:2, :] * x + w_ref[2:3, :] * nxt + b_ref[...]


def _hy_shortconv(p3, conv_w, conv_b):
    bsz, l, _ = p3.shape
    units = 3 * BR // LANE
    return pl.pallas_call(
        _hy_shortconv_kernel,
        grid=(bsz, units),
        in_specs=[pl.BlockSpec((1, l, LANE), lambda b, u: (b, 0, COL_HY_X + u)),
                  pl.BlockSpec((3, LANE), lambda b, u: (0, u)),
                  pl.BlockSpec((1, LANE), lambda b, u: (0, u))],
        out_specs=pl.BlockSpec((1, l, LANE), lambda b, u: (b, 0, u)),
        out_shape=jax.ShapeDtypeStruct((bsz, l, 3 * BR), F32),
        compiler_params=_cparams(("parallel", "parallel")),
        name="hy_shortconv",
    )(p3, conv_w, conv_b.reshape(1, 3 * BR))


def _fft_stage2(mid_ref, f2_ref, n1_len, n2_len, emit):
    def body(kp, _):
        def rhs(k1):
            return jnp.concatenate([mid_ref[pl.ds(k1, n2_len, stride=2 * n1_len), :],
                                    mid_ref[pl.ds(n1_len + k1, n2_len, stride=2 * n1_len), :]], axis=0)
        r = jnp.concatenate([rhs(2 * kp), rhs(2 * kp + 1)], axis=1).astype(BF16)
        xs = jnp.dot(f2_ref[...], r, preferred_element_type=F32)
        emit(2 * kp, xs[:, :LANE])
        emit(2 * kp + 1, xs[:, LANE:])
        return 0
    lax.fori_loop(0, n1_len // 2, body, 0, unroll=_FFT_UNROLL)


def _hy_kf_kernel(fwd_ref, bwd_ref, nrm_ref, gk_ref, f2_ref, o_ref, mid_ref, *, n1_len, n2_len):
    n1h = n1_len // 2
    row = lax.broadcasted_iota(jnp.int32, (n1h, LANE), 0)

    def stage1(n2, _):
        f = fwd_ref[pl.ds(n2, n1h, stride=n2_len), :]
        b = jnp.where((row == 0) & (n2 == 0), 0.0, bwd_ref[pl.ds(n2, n1h, stride=n2_len), :])
        a = jnp.dot(gk_ref[n2], jnp.concatenate([f, b], axis=1).astype(BF16), preferred_element_type=F32)
        r0 = pl.multiple_of(n2 * 2 * n1_len, 2 * n1_len)
        mid_ref[0, pl.ds(r0, 2 * n1_len), :] = a[:, :LANE]
        mid_ref[1, pl.ds(r0, 2 * n1_len), :] = a[:, LANE:]
        return 0
    lax.fori_loop(0, n2_len, stage1, 0, unroll=_FFT_UNROLL)
    scale = nrm_ref[...] * (1.0 / (n1_len * n2_len))

    def stage2(k1, _):
        def rhs(j):
            return jnp.concatenate([mid_ref[j, pl.ds(k1, n2_len, stride=2 * n1_len), :],
                                    mid_ref[j, pl.ds(n1_len + k1, n2_len, stride=2 * n1_len), :]], axis=0)
        r = jnp.concatenate([rhs(0), rhs(1)], axis=1).astype(BF16)
        xs = jnp.dot(f2_ref[...], r, preferred_element_type=F32)
        f, b = xs[:, :LANE], xs[:, LANE:]
        o_ref[0, k1] = (jnp.concatenate([f[:n2_len] + b[:n2_len], f[n2_len:] - b[n2_len:]], axis=0)
                        * scale).astype(BF16)
        return 0
    lax.fori_loop(0, n1_len, stage2, 0, unroll=_FFT_UNROLL)


def _hy_kf(h2d, nrm, gk, f2, n2_len):
    l, cols = h2d.shape
    n1_len = 2 * l // n2_len
    tiles = cols // (2 * LANE)
    return pl.pallas_call(
        functools.partial(_hy_kf_kernel, n1_len=n1_len, n2_len=n2_len),
        grid=(tiles,),
        in_specs=[pl.BlockSpec((l, LANE), lambda t: (0, t)),
                  pl.BlockSpec((l, LANE), lambda t: (0, tiles + t)),
                  pl.BlockSpec((1, LANE), lambda t: (0, t)),
                  pl.BlockSpec(gk.shape, lambda t: (0, 0, 0)),
                  pl.BlockSpec(f2.shape, lambda t: (0, 0))],
        out_specs=pl.BlockSpec((1, n1_len, 2 * n2_len, LANE), lambda t: (t, 0, 0, 0)),
        out_shape=jax.ShapeDtypeStruct((tiles, n1_len, 2 * n2_len, LANE), BF16),
        scratch_shapes=[pltpu.VMEM((2, 4 * l, LANE), F32)],
        compiler_params=_cparams(("parallel",)),
        name="hy_kf",
    )(h2d, h2d, nrm, gk, f2)


def _hy_conv_kernel(z_ref, g_ref, kf_ref, bias_ref, gf_ref, gi_ref, f2_ref, f2c_ref, o_ref,
                    mid_ref, spec_ref, *, n1_len, n2_len, order):
    n1h = n1_len // 2

    def stage1(n2, _):
        rhs = jnp.concatenate([z_ref[0, pl.ds(n2, n1h, stride=n2_len), :],
                               z_ref[1, pl.ds(n2, n1h, stride=n2_len), :]], axis=0).astype(BF16)
        r0 = pl.multiple_of(n2 * 2 * n1_len, 2 * n1_len)
        mid_ref[pl.ds(r0, 2 * n1_len), :] = jnp.dot(gf_ref[n2], rhs, preferred_element_type=F32)
        return 0
    lax.fori_loop(0, n2_len, stage1, 0, unroll=_FFT_UNROLL)

    def emit(k1, xs):
        kf = kf_ref[0, k1].astype(F32)
        xr, xi = xs[:n2_len], xs[n2_len:]
        kr, ki = kf[:n2_len], kf[n2_len:]
        spec_ref[k1] = jnp.concatenate([xr * kr - xi * ki, xr * ki + xi * kr], axis=0).astype(BF16)
    _fft_stage2(mid_ref, f2_ref, n1_len, n2_len, emit)

    def stage2i(kp, _):
        r = jnp.concatenate([spec_ref[2 * kp], spec_ref[2 * kp + 1]], axis=1)
        b = jnp.dot(f2c_ref[...], r, preferred_element_type=F32)
        r0 = pl.multiple_of(kp * 4 * n2_len, 4 * n2_len)
        mid_ref[pl.ds(r0, 2 * n2_len), :] = b[:, :LANE]
        mid_ref[pl.ds(r0 + 2 * n2_len, 2 * n2_len), :] = b[:, LANE:]
        return 0
    lax.fori_loop(0, n1_len // 2, stage2i, 0, unroll=_FFT_UNROLL)

    def stage1i(n2, _):
        rhs = jnp.concatenate([mid_ref[pl.ds(n2, n1_len, stride=2 * n2_len), :],
                               mid_ref[pl.ds(n2_len + n2, n1_len, stride=2 * n2_len), :]], axis=0).astype(BF16)
        o = jnp.dot(gi_ref[n2], rhs, preferred_element_type=F32)
        o_ref[0, pl.ds(n2, n1h, stride=n2_len), :] = o[:n1h]
        o_ref[1, pl.ds(n2, n1h, stride=n2_len), :] = o[n1h:]
        return 0
    lax.fori_loop(0, n2_len, stage1i, 0, unroll=_FFT_UNROLL)

    bias = bias_ref[order:order + 1, :]
    for j in range(2):
        o_ref[j] = g_ref[j] * (o_ref[j] + bias * z_ref[j])


def _hy_conv(z, zcol, g, gcol, kf, bias, tabs, n2_len, order):
    gf, _, gi, f2, f2c = tabs
    bsz, l, _ = z.shape
    n1_len = 2 * l // n2_len
    tiles = BR // LANE
    const2 = lambda *_: (0, 0)
    const3 = lambda *_: (0, 0, 0)
    one = pl.Buffered(1)
    return pl.pallas_call(
        functools.partial(_hy_conv_kernel, n1_len=n1_len, n2_len=n2_len, order=order),
        grid=(tiles, bsz // 2),
        in_specs=[pl.BlockSpec((2, l, LANE), lambda t, b: (b, 0, zcol + t)),
                  pl.BlockSpec((2, l, LANE), lambda t, b: (b, 0, gcol + t)),
                  pl.BlockSpec((1, n1_len, 2 * n2_len, LANE), lambda t, b: (order * tiles + t, 0, 0, 0),
                               pipeline_mode=one),
                  pl.BlockSpec((HY_ORDER, LANE), lambda t, b: (0, t)),
                  pl.BlockSpec(gf.shape, const3, pipeline_mode=one),
                  pl.BlockSpec(gi.shape, const3, pipeline_mode=one),
                  pl.BlockSpec(f2.shape, const2, pipeline_mode=one),
                  pl.BlockSpec(f2c.shape, const2, pipeline_mode=one)],
        out_specs=pl.BlockSpec((2, l, LANE), lambda t, b: (b, 0, t)),
        out_shape=jax.ShapeDtypeStruct((bsz, l, BR), F32),
        scratch_shapes=[pltpu.VMEM((4 * l, LANE), F32),
                        pltpu.VMEM((n1_len, 2 * n2_len, LANE), BF16)],
        compiler_params=_cparams(("parallel", "arbitrary")),
        name="hy_conv%d" % order,
    )(z, g, kf, bias, gf, gi, f2, f2c)


def _hyena_filters(l, w1, b1, f1, w2, b2, f2, w3, b3, f3, w4):
    hp = lax.Precision.HIGHEST
    t = jnp.linspace(0.0, 1.0, l, dtype=F32)[:, None]
    wpos = 2.0 * math.pi * jnp.arange(l, dtype=F32)[:, None] / l
    f = jnp.linspace(1e-4, HY_BANDS - 1, HY_BANDS, dtype=F32)[None, :]
    z = jnp.concatenate([t, jnp.cos(f * wpos), -jnp.sin(f * wpos)], axis=-1)
    h = jnp.sin(f1 * (jnp.dot(z, w1, precision=hp) + b1))
    h = jnp.sin(f2 * (jnp.dot(h, w2, precision=hp) + b2))
    h = jnp.sin(f3 * (jnp.dot(h, w3, precision=hp) + b3))
    h = jnp.dot(h, w4, precision=hp).reshape(l, 2, HY_ORDER, BR)
    deltas = jnp.linspace(math.log(HY_TARGET) / HY_SLOW_DECAY,
                          math.log(HY_TARGET) / HY_FAST_DECAY, BR, dtype=F32)
    h = h * jnp.exp(-t * jnp.abs(deltas))[:, None, None, :]
    ss = jnp.sum(h[:, 0] * h[:, 0], axis=0) + jnp.sum(h[1:, 1] * h[1:, 1], axis=0)
    return h.reshape(l, 2 * HY_ORDER * BR), lax.rsqrt(ss + EPS).reshape(1, HY_ORDER * BR)


def _hyena(p3, conv_w, conv_b, filters, bias, n2_len):
    bsz, l, _ = p3.shape
    tabs = _fft_tables(2 * l, n2_len)
    xs = _hy_shortconv(p3, conv_w, conv_b)
    h2d, nrm = filters
    kf = _hy_kf(h2d, nrm, tabs[1], tabs[3], n2_len)
    tiles = BR // LANE
    z1 = _hy_conv(xs, 0, xs, tiles, kf, bias, tabs, n2_len, 0)
    return _hy_conv(z1, 0, xs, 2 * tiles, kf, bias, tabs, n2_len, 1)


def _nt(a, b):
    return lax.dot_general(a, b, (((1,), (1,)), ((), ())), preferred_element_type=F32)


def _tn(a, b):
    return lax.dot_general(a, b, (((0,), (0,)), ((), ())), preferred_element_type=F32)


def _ret_kernel(q_ref, k_ref, v_ref, cos_ref, sin_ref, lg_ref, nw_ref, s0_ref, o_ref, sfin_ref,
                vb_ref, qf_ref, qb_ref, kf_ref, kb_ref, *, nchunks):
    c = ATT_CHUNK
    lg = lg_ref[0]
    jrow = lax.broadcasted_iota(jnp.int32, (c, c), 0)
    icol = lax.broadcasted_iota(jnp.int32, (c, c), 1)
    jf = jrow.astype(F32)
    dsym = jnp.exp(jnp.abs(jrow - icol).astype(F32) * lg) * jnp.where(jrow == icol, 2.0, 1.0)
    dq_f = jnp.exp((jf + 1.0) * lg)
    dq_b = jnp.exp((c - jf) * lg)
    dk_f = jnp.exp((c - 1.0 - jf) * lg)
    dk_b = jnp.exp(jf * lg)
    dchunk = jnp.exp(c * lg)

    def rope(x, r0):
        return x * cos_ref[pl.ds(r0, c), :] + pltpu.roll(x, RET_DK // 2, axis=1) * sin_ref[pl.ds(r0, c), :]

    def intra_step(n, _):
        r0 = pl.multiple_of(n * c, c)
        rows = pl.ds(r0, c)
        q = rope(q_ref[0, rows, :].astype(F32), r0)
        k = rope(k_ref[0, rows, :].astype(F32), r0) * (RET_DK ** -0.5)
        v = v_ref[0, rows, :].astype(BF16)
        att = _nt(q.astype(BF16), k.astype(BF16)) * dsym
        o_ref[0, rows, :] = jnp.dot(att.astype(BF16), v, preferred_element_type=F32)
        vb_ref[rows, :] = v
        qf_ref[rows, :] = (q * dq_f).astype(BF16)
        qb_ref[rows, :] = (q * dq_b).astype(BF16)
        kf_ref[rows, :] = (k * dk_f).astype(BF16)
        kb_ref[rows, :] = (k * dk_b).astype(BF16)
        return 0

    lax.fori_loop(0, nchunks, intra_step, 0, unroll=_ATT_UNROLL)

    def sweep(lo, hi, carry, finish):
        def step(i, carry):
            s_f, s_b = carry
            rf = pl.ds(pl.multiple_of(i * c, c), c)
            rb = pl.ds(pl.multiple_of((nchunks - 1 - i) * c, c), c)
            o_f = o_ref[0, rf, :] + jnp.dot(qf_ref[rf, :], s_f.astype(BF16), preferred_element_type=F32)
            o_b = o_ref[0, rb, :] + jnp.dot(qb_ref[rb, :], s_b.astype(BF16), preferred_element_type=F32)
            if finish:
                o_f = o_f * lax.rsqrt(jnp.mean(o_f * o_f, axis=-1, keepdims=True) + EPS) * nw_ref[...]
                o_b = o_b * lax.rsqrt(jnp.mean(o_b * o_b, axis=-1, keepdims=True) + EPS) * nw_ref[...]
            o_ref[0, rf, :] = o_f
            o_ref[0, rb, :] = o_b
            return (dchunk * s_f + _tn(kf_ref[rf, :], vb_ref[rf, :]),
                    dchunk * s_b + _tn(kb_ref[rb, :], vb_ref[rb, :]))
        return lax.fori_loop(lo, hi, step, carry)

    carry = sweep(0, nchunks // 2, (s0_ref[0, 0, 0], s0_ref[0, 0, 1]), False)
    s_f, s_b = sweep(nchunks // 2, nchunks, carry, True)
    sfin_ref[0, 0, 0] = s_f
    sfin_ref[0, 0, 1] = s_b


def _retention(p3, cos2, sin2, norm_w, s0):
    bsz, l, _ = p3.shape
    lgs = jnp.log(1.0 - 2.0 ** (-5.0 - jnp.arange(RET_HEADS, dtype=F32)))
    lgs = jnp.broadcast_to(lgs[:, None, None], (RET_HEADS, 1, LANE))
    tab = pl.BlockSpec((l, LANE), lambda b, h: (0, 0))
    st = pl.BlockSpec((1, 1, 2, RET_DK, LANE), lambda b, h: (b, h, 0, 0, 0))
    return pl.pallas_call(
        functools.partial(_ret_kernel, nchunks=l // ATT_CHUNK),
        grid=(bsz, RET_HEADS),
        in_specs=[pl.BlockSpec((1, l, LANE), lambda b, h: (b, 0, COL_RET_Q + h)),
                  pl.BlockSpec((1, l, LANE), lambda b, h: (b, 0, COL_RET_K + h)),
                  pl.BlockSpec((1, l, LANE), lambda b, h: (b, 0, COL_RET_V + h)),
                  tab, tab,
                  pl.BlockSpec((1, 1, LANE), lambda b, h: (h, 0, 0)),
                  pl.BlockSpec((1, LANE), lambda b, h: (0, h)),
                  st],
        out_specs=[pl.BlockSpec((1, l, LANE), lambda b, h: (b, 0, h)), st],
        out_shape=[jax.ShapeDtypeStruct((bsz, l, BR), F32),
                   jax.ShapeDtypeStruct((bsz, RET_HEADS, 2, RET_DK, LANE), F32)],
        scratch_shapes=[pltpu.VMEM((l, LANE), BF16)] * 5,
        compiler_params=_cparams(("parallel", "parallel")),
        name="retention",
    )(p3, p3, p3, cos2, sin2, lgs, norm_w.reshape(1, BR), s0)


def _rope_tables(ang):
    cos, sin = jnp.cos(ang), jnp.sin(ang)
    return jnp.concatenate([cos, cos], -1), jnp.concatenate([-sin, sin], -1)


def _latent_angles(rows):
    half = RET_DK // 4
    inv = ROPE_BASE ** (-jnp.arange(half, dtype=F32) / half)
    r = jnp.repeat(jnp.arange(rows, dtype=F32), GRID_W)
    cl = jnp.tile(jnp.arange(GRID_W, dtype=F32), rows)
    return jnp.concatenate([r[:, None] * inv, cl[:, None] * inv], axis=-1)


def _ctx_angles(n_ctx):
    n = RET_DK // 2
    inv = ROPE_BASE ** (-jnp.arange(n, dtype=F32) / n)
    return jnp.arange(n_ctx, dtype=F32)[:, None] * inv


def _gla_kernel(q_ref, k_ref, v_ref, lr_ref, gw_ref, gb_ref, nw_ref, s0_ref, o_ref, sfin_ref,
                vb_ref, qf_ref, qb_ref, kf_ref, kb_ref, df_ref, db_ref, *, nchunks):
    c = ATT_CHUNK
    jrow = lax.broadcasted_iota(jnp.int32, (c, c), 0)
    icol = lax.broadcasted_iota(jnp.int32, (c, c), 1)
    tri_f = jnp.where(icol <= jrow, 1.0, 0.0).astype(BF16)
    tri_b = jnp.where(icol >= jrow, 1.0, 0.0).astype(BF16)
    lane = lax.broadcasted_iota(jnp.int32, (1, LANE), 1)
    head0 = lane < GLA_DK

    def decays(r0, d):
        x = jnp.dot(lr_ref[0, pl.ds(r0, c), :].astype(BF16), gw_ref[d], preferred_element_type=F32) + gb_ref[d]
        la = (jnp.minimum(x, 0.0) - jnp.log(1.0 + jnp.exp(-jnp.abs(x)))) * (1.0 / GLA_TAU)
        hi = la.astype(BF16)
        lo = (la - hi.astype(F32)).astype(BF16)
        cs = jnp.dot(tri_f if d == 0 else tri_b, jnp.concatenate([hi, lo], axis=1), preferred_element_type=F32)
        return cs[:, :LANE] + cs[:, LANE:]

    def intra_step(n, _):
        r0 = pl.multiple_of(n * c, c)
        rows = pl.ds(r0, c)
        q = q_ref[0, rows, :].astype(F32) * (GLA_DK ** -0.5)
        k = k_ref[0, rows, :].astype(F32)
        bf = decays(r0, 0)
        bb = decays(r0, 1)
        ref_f = bf[c // 2 - 1:c // 2, :]
        ref_b = bb[c // 2:c // 2 + 1, :]
        qf = q * jnp.exp(bf - ref_f)
        kf = (k * jnp.exp(ref_f - bf)).astype(BF16)
        qb = q * jnp.exp(bb - ref_b)
        kb = (k * jnp.exp(ref_b - bb)).astype(BF16)
        for h in range(2):
            hm = head0 if h == 0 else jnp.logical_not(head0)
            v = v_ref[0, rows, h * GLA_DV:(h + 1) * GLA_DV].astype(BF16)
            att = (jnp.where(icol <= jrow, _nt(jnp.where(hm, qf, 0.0).astype(BF16), kf), 0.0)
                   + jnp.where(icol >= jrow, _nt(jnp.where(hm, qb, 0.0).astype(BF16), kb), 0.0))
            o_ref[0, rows, h * GLA_DV:(h + 1) * GLA_DV] = jnp.dot(att.astype(BF16), v, preferred_element_type=F32)
            vb_ref[rows, h * GLA_DV:(h + 1) * GLA_DV] = v
        bf_last, bb_last = bf[c - 1:c, :], bb[0:1, :]
        qf_ref[rows, :] = (q * jnp.exp(bf)).astype(BF16)
        qb_ref[rows, :] = (q * jnp.exp(bb)).astype(BF16)
        kf_ref[rows, :] = (k * jnp.exp(bf_last - bf)).astype(BF16)
        kb_ref[rows, :] = (k * jnp.exp(bb_last - bb)).astype(BF16)
        d0 = pl.multiple_of(n * 8, 8)
        df_ref[pl.ds(d0, 8), :] = jnp.broadcast_to(jnp.exp(bf_last), (8, LANE))
        db_ref[pl.ds(d0, 8), :] = jnp.broadcast_to(jnp.exp(bb_last), (8, LANE))
        return 0

    lax.fori_loop(0, nchunks, intra_step, 0, unroll=_ATT_UNROLL)

    def sweep(lo, hi, carry, finish):
        def step(i, carry):
            sts = list(carry)
            new = []
            for d in range(2):
                n = i if d == 0 else nchunks - 1 - i
                rows = pl.ds(pl.multiple_of(n * c, c), c)
                qs = (qf_ref if d == 0 else qb_ref)[rows, :]
                kd = (kf_ref if d == 0 else kb_ref)[rows, :]
                dec = (df_ref if d == 0 else db_ref)[pl.ds(pl.multiple_of(n * 8, 8), 1), :]
                for h in range(2):
                    hm = head0 if h == 0 else jnp.logical_not(head0)
                    cols = slice(h * GLA_DV, (h + 1) * GLA_DV)
                    st = sts[2 * d + h]
                    o = o_ref[0, rows, cols] + _nt(qs, st.astype(BF16))
                    if finish:
                        o = o * lax.rsqrt(jnp.mean(o * o, axis=-1, keepdims=True) + EPS) * nw_ref[:, cols]
                    o_ref[0, rows, cols] = o
                    new.append(jnp.where(hm, dec * st + _tn(vb_ref[rows, cols], kd), 0.0))
            return tuple(new)
        return lax.fori_loop(lo, hi, step, carry)

    carry = (s0_ref[0, 0, 0, 0], s0_ref[0, 0, 0, 1], s0_ref[0, 0, 1, 0], s0_ref[0, 0, 1, 1])
    carry = sweep(0, nchunks // 2, carry, False)
    s_fin = sweep(nchunks // 2, nchunks, carry, True)
    sfin_ref[0, 0, 0, 0] = s_fin[0]
    sfin_ref[0, 0, 0, 1] = s_fin[1]
    sfin_ref[0, 0, 1, 0] = s_fin[2]
    sfin_ref[0, 0, 1, 1] = s_fin[3]


def _gla(p3, gate_w, gate_b, norm_w, s0):
    bsz, l, _ = p3.shape
    pairs = GLA_HEADS // 2
    gw = jnp.zeros((2, LANE, GLA_HEADS * GLA_DK), F32)
    for d in range(2):
        gw = gw.at[d, d * GLA_LR:(d + 1) * GLA_LR, :].set(gate_w[d].astype(F32))
    st = pl.BlockSpec((1, 1, 2, 2, GLA_DV, LANE), lambda b, h: (b, h, 0, 0, 0, 0))
    return pl.pallas_call(
        functools.partial(_gla_kernel, nchunks=l // ATT_CHUNK),
        grid=(bsz, pairs),
        in_specs=[pl.BlockSpec((1, l, LANE), lambda b, h: (b, 0, COL_GLA_Q + h)),
                  pl.BlockSpec((1, l, LANE), lambda b, h: (b, 0, COL_GLA_K + h)),
                  pl.BlockSpec((1, l, 2 * GLA_DV), lambda b, h: (b, 0, COL_GLA_V // 2 + h)),
                  pl.BlockSpec((1, l, LANE), lambda b, h: (b, 0, COL_GLA_LR)),
                  pl.BlockSpec((2, LANE, LANE), lambda b, h: (0, 0, h)),
                  pl.BlockSpec((2, 1, LANE), lambda b, h: (0, 0, h)),
                  pl.BlockSpec((1, 2 * GLA_DV), lambda b, h: (0, h)),
                  st],
        out_specs=[pl.BlockSpec((1, l, 2 * GLA_DV), lambda b, h: (b, 0, h)), st],
        out_shape=[jax.ShapeDtypeStruct((bsz, l, BR), F32),
                   jax.ShapeDtypeStruct((bsz, pairs, 2, 2, GLA_DV, LANE), F32)],
        scratch_shapes=([pltpu.VMEM((l, 2 * GLA_DV), BF16)] + [pltpu.VMEM((l, LANE), BF16)] * 4
                        + [pltpu.VMEM((8 * l // ATT_CHUNK, LANE), F32)] * 2),
        compiler_params=_cparams(("parallel", "parallel")),
        name="gla",
    )(p3, p3, p3, p3, gw.astype(BF16), gate_b.astype(F32).reshape(2, 1, GLA_HEADS * GLA_DK),
      norm_w.reshape(1, BR), s0)


def _mixers(p2, xs, bsz, prm, i, s5_tabs, hy_filters, rope, s5_h0, ret_s0, gla_s0, with_hyena):
    m = p2.shape[0]
    l = m // bsz
    p3 = p2.reshape(bsz, l, NP)
    tabs = s5_tabs[:3] + (s5_tabs[3][:, :, :, :l // (2 * S5_T) + 1],)
    y, s5_fin = _s5_core(xs, tabs, s5_h0)
    s5 = _s5_glu(y, p2, prm['s5_d'][i], prm['s5_glu_w'][i], prm['s5_glu_b'][i])
    hy = None
    if with_hyena:
        hy = _hyena(p3, prm['hy_conv_w'][i], prm['hy_conv_b'][i], hy_filters, prm['hy_bias'][i].astype(F32),
                    64 if l >= 512 else 16).reshape(m, BR)
    ret, ret_fin = _retention(p3, rope[0], rope[1], prm['ret_norm_w'][i], ret_s0)
    gla, gla_fin = _gla(p3, prm['gla_gate_w'][i], prm['gla_gate_b'][i], prm['gla_norm_w'][i], gla_s0)
    return (s5, hy, ret.reshape(m, BR), gla.reshape(m, BR)), (s5_fin, ret_fin, gla_fin)


def kernel(x, c, ctx, c_ctx, norm_w, ada_w, ada_b, w_in, w_out, s5_a_re, s5_a_im, s5_log_dt, s5_b_re, s5_b_im, s5_c_re, s5_c_im, s5_d, s5_glu_w, s5_glu_b, hy_conv_w, hy_conv_b, hy_w1, hy_b1, hy_f1, hy_w2, hy_b2, hy_f2, hy_w3, hy_b3, hy_f3, hy_w4, hy_bias, ret_norm_w, gla_gate_w, gla_gate_b, gla_norm_w, final_norm_w):
    prm = dict(s5_a_re=s5_a_re, s5_a_im=s5_a_im, s5_log_dt=s5_log_dt, s5_b_re=s5_b_re, s5_b_im=s5_b_im,
               s5_c_re=s5_c_re, s5_c_im=s5_c_im, s5_d=s5_d, s5_glu_w=s5_glu_w, s5_glu_b=s5_glu_b,
               hy_conv_w=hy_conv_w, hy_conv_b=hy_conv_b, hy_w1=hy_w1, hy_b1=hy_b1, hy_f1=hy_f1,
               hy_w2=hy_w2, hy_b2=hy_b2, hy_f2=hy_f2, hy_w3=hy_w3, hy_b3=hy_b3, hy_f3=hy_f3,
               hy_w4=hy_w4, hy_bias=hy_bias, ret_norm_w=ret_norm_w, gla_gate_w=gla_gate_w,
               gla_gate_b=gla_gate_b, gla_norm_w=gla_norm_w)
    bsz, l, d = x.shape
    lc = ctx.shape[1]
    depth = w_in.shape[0]
    cc = jnp.concatenate([c, c_ctx[None, :], jnp.zeros((8 - bsz - 1, d), F32)], axis=0)
    mod = _ada(cc, ada_w, ada_b)
    rope_l = _rope_tables(_latent_angles(l // GRID_W))
    rope_c = _rope_tables(_ctx_angles(lc))
    s5_zero = jnp.zeros((S5_GROUPS // 2, 2, 2, 8, 2 * S5_STATE), F32)
    ret_zero = jnp.zeros((bsz, RET_HEADS, 2, RET_DK, LANE), F32)
    gla_zero = jnp.zeros((bsz, GLA_HEADS // 2, 2, 2, GLA_DV, LANE), F32)

    w_all = _permute_w_in(w_in)
    wo_all = w_out.astype(BF16)
    s5_all = jax.vmap(functools.partial(_s5_tables, nch=l // (2 * S5_T)))(
        s5_a_re, s5_a_im, s5_log_dt, s5_b_re, s5_b_im, s5_c_re, s5_c_im)
    filt = tuple(w.astype(F32) for w in (hy_w1, hy_b1, hy_f1, hy_w2, hy_b2, hy_f2, hy_w3, hy_b3, hy_f3, hy_w4))
    hyf_l = jax.vmap(functools.partial(_hyena_filters, l))(*filt)
    hyf_c = jax.vmap(functools.partial(_hyena_filters, lc))(*[w[:depth - 1] for w in filt])

    x_l = x.reshape(bsz * l, d)
    x_c = ctx.reshape(bsz * lc, d)
    for i in range(depth):
        last = i == depth - 1
        sh, sc, gt = mod[i, :, :d], mod[i, :, d:2 * d], mod[i, :, 2 * d:]
        w, wo = w_all[i], wo_all[i]
        p_c, xs_c = _inproj(x_c, norm_w[i], sc[bsz:bsz + 1, None, :], sh[bsz:bsz + 1, None, :], w, bsz * lc)
        p_l, xs_l = _inproj(x_l, norm_w[i], sc[:bsz, None, :], sh[:bsz, None, :], w, l)
        s5_tabs = tuple(t[i] for t in s5_all)
        ys_c, (s5_fin, ret_fin, gla_fin) = _mixers(
            p_c, xs_c, bsz, prm, i, s5_tabs, None if last else (hyf_c[0][i], hyf_c[1][i]), rope_c, s5_zero,
            ret_zero, gla_zero, with_hyena=not last)
        ys_l, _ = _mixers(p_l, xs_l, bsz, prm, i, s5_tabs, (hyf_l[0][i], hyf_l[1][i]), rope_l,
                          _s5_next_h0(s5_fin), ret_fin, gla_fin, with_hyena=True)
        x_l = _outproj(x_l, ys_l, p_l, wo, gt[:bsz, None, :], final_norm_w, l, final=last)
        if not last:
            x_c = _outproj(x_c, ys_c, p_c, wo, gt[bsz:bsz + 1, None, :], final_norm_w, bsz * lc, final=False)
    return x_l.reshape(bsz, l, d)
```

```python
import functools
import math

import numpy as np
import jax
import jax.numpy as jnp
from jax import lax
from jax.experimental import pallas as pl
from jax.experimental.pallas import tpu as pltpu

F32 = jnp.float32
BF16 = jnp.bfloat16

EPS = 1e-6
GRID_W = 64
BR = 512
S5_GSIZE = 16
S5_GROUPS = BR // S5_GSIZE
S5_STATE = 64
S5_T = 16
HY_ORDER = 2
HY_EMB = 33
HY_BANDS = (HY_EMB - 1) // 2
HY_FAST_DECAY = 0.3
HY_SLOW_DECAY = 1.5
HY_TARGET = 1e-2
RET_HEADS = 4
RET_DK = 128
GLA_HEADS = 4
GLA_DK = 64
GLA_DV = 128
GLA_LR = 16
GLA_TAU = 16.0
ROPE_BASE = 10000.0
ATT_CHUNK = 128
_FFT_UNROLL = 8
_ATT_UNROLL = 2

LANE = 128
VMEM_LIMIT = 56 * 1024 * 1024

COL_S5_U = 0
COL_S5_G = 4
COL_HY_X = 8
COL_HY_G = 20
COL_RET_Q = 24
COL_RET_K = 28
COL_RET_V = 32
COL_RET_G = 36
COL_GLA_Q = 40
COL_GLA_K = 42
COL_GLA_V = 44
COL_GLA_G = 48
COL_GLA_LR = 52
NP_UNITS = 54
NP = NP_UNITS * LANE
W_TILE = 768
_ALIGNED = COL_GLA_G * LANE


def _cparams(sem):
    return pltpu.CompilerParams(dimension_semantics=sem, vmem_limit_bytes=VMEM_LIMIT)


def _silu(x):
    return x * jax.nn.sigmoid(x)


def _wprep_kernel(w_ref, tail_ref, o_ref):
    @pl.when(pl.program_id(1) < _ALIGNED // W_TILE)
    def _():
        o_ref[0] = w_ref[0].astype(BF16)

    @pl.when(pl.program_id(1) == _ALIGNED // W_TILE)
    def _():
        t = tail_ref[0]
        d = t.shape[0]
        lr = jnp.concatenate([t[:, :2 * GLA_LR], jnp.zeros((d, W_TILE - BR - 2 * GLA_LR), F32)], axis=1)
        o_ref[0] = jnp.concatenate([t[:, 2 * GLA_LR:], lr], axis=1).astype(BF16)


def _wprep(w_in):
    depth, d, in_w = w_in.shape
    n_al = _ALIGNED // W_TILE
    assert n_al * W_TILE == _ALIGNED and in_w == _ALIGNED + 2 * GLA_LR + BR and NP == _ALIGNED + W_TILE
    tail = w_in[:, :, _ALIGNED:]
    return pl.pallas_call(
        _wprep_kernel,
        grid=(depth, n_al + 1),
        in_specs=[pl.BlockSpec((1, d, W_TILE), lambda i, j: (i, 0, jnp.minimum(j, n_al - 1))),
                  pl.BlockSpec((1, d, in_w - _ALIGNED), lambda i, j: (i, 0, 0))],
        out_specs=pl.BlockSpec((1, d, W_TILE), lambda i, j: (i, 0, j)),
        out_shape=jax.ShapeDtypeStruct((depth, d, NP), BF16),
        compiler_params=_cparams(("parallel", "parallel")),
        name="w_prep",
    )(w_in, tail)


def _ada_kernel(c_ref, w_ref, b_ref, o_ref):
    a = _silu(c_ref[...]).astype(BF16)
    o_ref[0] = jnp.dot(a, w_ref[0].astype(BF16), preferred_element_type=F32) + b_ref[0]


def _ada(cc, ada_w, ada_b, tn=768):
    depth, d, n = ada_w.shape
    return pl.pallas_call(
        _ada_kernel,
        grid=(depth, n // tn),
        in_specs=[pl.BlockSpec((8, d), lambda i, j: (0, 0)),
                  pl.BlockSpec((1, d, tn), lambda i, j: (i, 0, j)),
                  pl.BlockSpec((1, 1, tn), lambda i, j: (i, 0, j))],
        out_specs=pl.BlockSpec((1, 8, tn), lambda i, j: (i, 0, j)),
        out_shape=jax.ShapeDtypeStruct((depth, 8, n), F32),
        compiler_params=_cparams(("parallel", "parallel")),
        name="ada_mod",
    )(cc, ada_w, ada_b.reshape(depth, 1, n))


_S5_SLOT = 2 * S5_GSIZE
_S5_PAIRS = S5_GROUPS // 2
_S5_ROW = S5_T * _S5_SLOT


def _s5_lane_perm():
    perm = np.zeros((4, _S5_ROW), np.int32)
    for s in range(4):
        for tau in range(S5_T):
            for c in range(_S5_SLOT):
                gl, i = divmod(c, S5_GSIZE)
                perm[s, LANE * (tau // 4) + _S5_SLOT * ((s + tau) % 4) + c] = (gl * S5_T + tau) * S5_GSIZE + i
    return perm


def _s5_to_chunk_rows(nat_ref, xs_ref, rows16):
    slot = lax.broadcasted_iota(jnp.int32, (rows16, LANE), 1) // _S5_SLOT
    for q in range(4):
        for qd in range(4):
            rolled = []
            for t in range(4):
                piece = nat_ref[q, pl.ds(4 * qd + t, rows16, stride=S5_T), :]
                rolled.append(pltpu.roll(piece, _S5_SLOT * t, axis=1) if t else piece)
            for s in range(4):
                acc = rolled[0]
                for t in range(1, 4):
                    acc = jnp.where(slot == (s + t) % 4, rolled[t], acc)
                xs_ref[4 * q + s, :, qd * LANE:(qd + 1) * LANE] = acc.astype(xs_ref.dtype)


def _s5_from_chunk_rows(y_ref, nat_ref, rows16):
    slot = lax.broadcasted_iota(jnp.int32, (rows16, LANE), 1) // _S5_SLOT
    for q in range(4):
        for qd in range(4):
            src = [y_ref[4 * q + s, :, qd * LANE:(qd + 1) * LANE] for s in range(4)]
            for t in range(4):
                acc = src[0]
                for s in range(1, 4):
                    acc = jnp.where(slot == (s + t) % 4, src[s], acc)
                if t:
                    acc = pltpu.roll(acc, LANE - _S5_SLOT * t, axis=1)
                nat_ref[q, pl.ds(4 * qd + t, rows16, stride=S5_T), :] = acc


def _inproj_kernel(x_ref, nw_ref, sc_ref, sh_ref, w_ref, o_ref, xs_ref, h_ref, nat_ref, *, rows16):
    j = pl.program_id(1)

    @pl.when(j == 0)
    def _():
        x = x_ref[...]
        y = x * lax.rsqrt(jnp.mean(x * x, axis=-1, keepdims=True) + EPS) * nw_ref[...]
        h_ref[...] = (y * (1.0 + sc_ref[0]) + sh_ref[0]).astype(BF16)

    res = jnp.dot(h_ref[...], w_ref[...], preferred_element_type=F32)
    o_ref[...] = res.astype(o_ref.dtype)

    @pl.when(j == 0)
    def _():
        for q in range(BR // LANE):
            nat_ref[q] = res[:, q * LANE:(q + 1) * LANE]
        _s5_to_chunk_rows(nat_ref, xs_ref, rows16)


def _inproj(x2, nw, sc, sh, w_all, layer, rows_per_mod, tm=1024, tn=W_TILE):
    m, d = x2.shape
    tm = min(tm, rows_per_mod)
    per = rows_per_mod // tm
    rows16 = tm // S5_T
    assert COL_S5_U == 0 and tn >= BR
    return pl.pallas_call(
        functools.partial(_inproj_kernel, rows16=rows16),
        grid=(m // tm, NP // tn),
        in_specs=[pl.BlockSpec((tm, d), lambda i, j: (i, 0)),
                  pl.BlockSpec((1, d), lambda i, j: (0, 0)),
                  pl.BlockSpec((1, 1, d), lambda i, j: (i // per, 0, 0)),
                  pl.BlockSpec((1, 1, d), lambda i, j: (i // per, 0, 0)),
                  pl.BlockSpec((None, d, tn), lambda i, j: (layer, 0, j))],
        out_specs=[pl.BlockSpec((tm, tn), lambda i, j: (i, j)),
                   pl.BlockSpec((_S5_PAIRS, rows16, _S5_ROW), lambda i, j: (0, i, 0))],
        out_shape=[jax.ShapeDtypeStruct((m, NP), BF16),
                   jax.ShapeDtypeStruct((_S5_PAIRS, m // S5_T, _S5_ROW), BF16)],
        scratch_shapes=[pltpu.VMEM((tm, d), BF16), pltpu.VMEM((BR // LANE, tm, LANE), F32)],
        compiler_params=_cparams(("parallel", "arbitrary")),
        name="inproj",
    )(x2, nw.reshape(1, d), sc, sh, w_all)


def _outproj_kernel(x_ref, y0_ref, y1_ref, y2_ref, y3_ref, g0_ref, g1_ref, g2_ref, g3_ref, w_ref, gt_ref, fw_ref,
                    o_ref, *, final):
    acc = None
    for k, (y_ref, g_ref) in enumerate(((y0_ref, g0_ref), (y1_ref, g1_ref), (y2_ref, g2_ref), (y3_ref, g3_ref))):
        gated = (y_ref[...] * _silu(g_ref[...].astype(F32))).astype(BF16)
        part = jnp.dot(gated, w_ref[k * BR:(k + 1) * BR, :], preferred_element_type=F32)
        acc = part if acc is None else acc + part
    x = x_ref[...] + gt_ref[0] * acc
    if final:
        x = x * lax.rsqrt(jnp.mean(x * x, axis=-1, keepdims=True) + EPS) * fw_ref[...]
    o_ref[...] = x


def _outproj(x2, ys, p, wo_all, layer, gt, fw, rows_per_mod, final, tm=256):
    m, d = x2.shape
    tm = min(tm, rows_per_mod)
    per = rows_per_mod // tm
    yspec = pl.BlockSpec((tm, BR), lambda i: (i, 0))

    def gate(col_unit):
        blk = col_unit * LANE // BR
        return pl.BlockSpec((tm, BR), lambda i: (i, blk))

    return pl.pallas_call(
        functools.partial(_outproj_kernel, final=final),
        grid=(m // tm,),
        in_specs=[pl.BlockSpec((tm, d), lambda i: (i, 0)), yspec, yspec, yspec, yspec,
                  gate(COL_S5_G), gate(COL_HY_G), gate(COL_RET_G), gate(COL_GLA_G),
                  pl.BlockSpec((None, 4 * BR, d), lambda i: (layer, 0, 0)),
                  pl.BlockSpec((1, 1, d), lambda i: (i // per, 0, 0)),
                  pl.BlockSpec((1, d), lambda i: (0, 0))],
        out_specs=pl.BlockSpec((tm, d), lambda i: (i, 0)),
        out_shape=jax.ShapeDtypeStruct((m, d), F32),
        compiler_params=_cparams(("parallel",)),
        name="outproj",
    )(x2, *ys, p, p, p, p, wo_all, gt, fw.reshape(1, d))


def _s5_tables(a_re, a_im, log_dt, b_re, b_im, c_re, c_im, nch):
    hp = lax.Precision.HIGHEST
    t_len, g_n, p_n, s_n = S5_T, S5_GROUPS, S5_STATE, S5_GSIZE
    a_re, a_im = a_re.astype(F32), a_im.astype(F32)
    dt = jnp.exp(log_dt.astype(F32))[:, :, None]
    lam_re, lam_im = a_re * dt, a_im * dt

    def power(tau):
        tau = tau.astype(F32)[:, None, None, None]
        mag = jnp.exp(lam_re[None] * tau)
        return mag * jnp.cos(lam_im[None] * tau), mag * jnp.sin(lam_im[None] * tau)

    ab_re, ab_im = power(jnp.ones((1,)))
    ab_re, ab_im = ab_re[0], ab_im[0]
    den = a_re * a_re + a_im * a_im
    nr = ab_re - 1.0
    co_re = (nr * a_re + ab_im * a_im) / den
    co_im = (ab_im * a_re - nr * a_im) / den
    b_re, b_im = b_re.astype(F32), b_im.astype(F32)
    bco_re = co_re[..., None] * b_re - co_im[..., None] * b_im
    bco_im = co_re[..., None] * b_im + co_im[..., None] * b_re
    c_re, c_im = c_re.astype(F32), c_im.astype(F32)

    pr, pi = power(jnp.arange(t_len + 1))
    ca_re = c_re[None] * pr[:, :, :, None, :] - c_im[None] * pi[:, :, :, None, :]
    ca_im = c_re[None] * pi[:, :, :, None, :] + c_im[None] * pr[:, :, :, None, :]
    ca_cat = jnp.concatenate([ca_re[:t_len], -ca_im[:t_len]], axis=-1).transpose(1, 2, 0, 3, 4)
    bco_cat = jnp.concatenate([bco_re, bco_im], axis=-2)
    kk = jnp.einsum('dgmp,dgpj->dgmj', ca_cat.reshape(2, g_n, t_len * s_n, 2 * p_n), bco_cat, precision=hp)
    kk = kk.reshape(2, g_n, t_len, s_n, s_n).transpose(2, 0, 1, 3, 4)
    kf, kb = kk[:, 0], kk[:, 1]
    kfull = jnp.concatenate([kb[:0:-1], (kf[0] + kb[0])[None], kf[1:]], axis=0)
    tt = jnp.arange(t_len)
    idx = tt[None, :] - tt[:, None] + (t_len - 1)
    mm = kfull[idx]
    mmat = mm.transpose(2, 0, 4, 1, 3).reshape(g_n, t_len * s_n, t_len * s_n)

    wexp = jnp.stack([pr[t_len - 1 - tt, 0], pr[tt, 1]], 0), jnp.stack([pi[t_len - 1 - tt, 0], pi[tt, 1]], 0)
    w_re = wexp[0][..., None] * bco_re[:, None] - wexp[1][..., None] * bco_im[:, None]
    w_im = wexp[0][..., None] * bco_im[:, None] + wexp[1][..., None] * bco_re[:, None]
    w4 = jnp.stack([w_re[0], w_im[0], w_re[1], w_im[1]], 0)
    w4 = w4.transpose(0, 2, 1, 4, 3).reshape(4, g_n // 2, 2, t_len * s_n, p_n)
    eye2 = jnp.eye(2, dtype=F32)
    wpair = jnp.einsum('kqlrp,lm->qlrkmp', w4, eye2).reshape(g_n // 2, 2 * t_len * s_n, 4 * 2 * p_n)

    vexp_f, vexp_b = tt + 1, t_len - tt
    v4 = jnp.stack([ca_re[vexp_f, 0], -ca_im[vexp_f, 0], ca_re[vexp_b, 1], -ca_im[vexp_b, 1]], 0)
    v4 = v4.transpose(0, 2, 4, 1, 3).reshape(4, g_n // 2, 2, p_n, t_len * s_n)
    vpair = jnp.einsum('kqlpr,lm->qklpmr', v4, eye2).reshape(g_n // 2, 4 * 2 * p_n, 2 * t_len * s_n)

    qr, qi = power(t_len * jnp.arange(nch + 1))
    pw = jnp.stack([qr, qi], 0)
    pw = pw.reshape(2, nch + 1, 2, g_n // 2, 2 * p_n).transpose(3, 2, 0, 1, 4)
    mpair = jnp.einsum('qlrc,lm->qlrmc', mmat.reshape(g_n // 2, 2, t_len * s_n, t_len * s_n), eye2)
    mpair = mpair.reshape(g_n // 2, _S5_ROW, _S5_ROW)
    onehot = jnp.asarray(np.eye(_S5_ROW, dtype=np.float32)[_s5_lane_perm()]).astype(BF16)

    def rows_to_lane_order(a):
        a4 = a.astype(BF16).reshape(g_n // 8, 4, _S5_ROW, a.shape[-1])
        return jnp.einsum('sno,qsoc->qsnc', onehot, a4, preferred_element_type=F32).astype(BF16).reshape(a.shape)

    def cols_to_lane_order(a):
        a4 = a.astype(BF16).reshape(g_n // 8, 4, a.shape[1], _S5_ROW)
        return jnp.einsum('qsro,sno->qsrn', a4, onehot, preferred_element_type=F32).astype(BF16).reshape(a.shape)

    return cols_to_lane_order(rows_to_lane_order(mpair)), rows_to_lane_order(wpair), cols_to_lane_order(vpair), pw


def _cmul(ar, ai, br, bi):
    return ar * br - ai * bi, ar * bi + ai * br


def _s5_kernel(x_ref, m_ref, w_ref, v_ref, pw_ref, h0_ref, y_ref, fin_ref, s_ref, h_ref, *, nch):
    ln = 2 * S5_STATE
    x = x_ref[0]
    s_all = jnp.dot(x, w_ref[0], preferred_element_type=F32)
    for k in range(4):
        s_ref[k] = s_all[:, k * ln:(k + 1) * ln]
    a_rf, a_if = pw_ref[0, 0, 0, 1:2, :], pw_ref[0, 0, 1, 1:2, :]
    a_rb, a_ib = pw_ref[0, 1, 0, 1:2, :], pw_ref[0, 1, 1, 1:2, :]

    def tile(c):
        return pl.ds(c, 8, stride=nch)

    def scan_step(s, carry):
        hrf, hif, hrb, hib = carry
        rf, rb = tile(s), tile(nch - 1 - s)
        h_ref[0, rf, :] = hrf
        h_ref[1, rf, :] = hif
        h_ref[2, rb, :] = hrb
        h_ref[3, rb, :] = hib
        pr, pi = _cmul(a_rf, a_if, hrf, hif)
        qr, qi = _cmul(a_rb, a_ib, hrb, hib)
        return (pr + s_ref[0, rf, :], pi + s_ref[1, rf, :], qr + s_ref[2, rb, :], qi + s_ref[3, rb, :])

    init = (h0_ref[0, 0, 0], h0_ref[0, 0, 1], h0_ref[0, 1, 0], h0_ref[0, 1, 1])
    hrf, hif, hrb, hib = lax.fori_loop(0, nch, scan_step, init, unroll=2)

    odd = (lax.broadcasted_iota(jnp.int32, (8, ln), 0) % 2) == 1
    crf = jnp.where(odd, pltpu.roll(hrf, 1, axis=0), 0.0)
    cif = jnp.where(odd, pltpu.roll(hif, 1, axis=0), 0.0)
    crb = jnp.where(odd, 0.0, pltpu.roll(hrb, 7, axis=0))
    cib = jnp.where(odd, 0.0, pltpu.roll(hib, 7, axis=0))

    def fix_step(s, _):
        rf, rb = tile(s), tile(nch - 1 - s)
        pr, pi = _cmul(pw_ref[0, 0, 0, pl.ds(s, 1), :], pw_ref[0, 0, 1, pl.ds(s, 1), :], crf, cif)
        qr, qi = _cmul(pw_ref[0, 1, 0, pl.ds(s, 1), :], pw_ref[0, 1, 1, pl.ds(s, 1), :], crb, cib)
        h_ref[0, rf, :] = h_ref[0, rf, :] + pr
        h_ref[1, rf, :] = h_ref[1, rf, :] + pi
        h_ref[2, rb, :] = h_ref[2, rb, :] + qr
        h_ref[3, rb, :] = h_ref[3, rb, :] + qi
        return 0

    lax.fori_loop(0, nch, fix_step, 0, unroll=2)
    pr, pi = _cmul(pw_ref[0, 0, 0, nch:nch + 1, :], pw_ref[0, 0, 1, nch:nch + 1, :], crf, cif)
    qr, qi = _cmul(pw_ref[0, 1, 0, nch:nch + 1, :], pw_ref[0, 1, 1, nch:nch + 1, :], crb, cib)
    fin_ref[0, 0, 0] = hrf + pr
    fin_ref[0, 0, 1] = hif + pi
    fin_ref[0, 1, 0] = hrb + qr
    fin_ref[0, 1, 1] = hib + qi

    h_all = jnp.concatenate([h_ref[k] for k in range(4)], axis=1).astype(BF16)
    y_ref[0] = (jnp.dot(x, m_ref[0], preferred_element_type=F32)
                + jnp.dot(h_all, v_ref[0], preferred_element_type=F32))


def _s5_core(xs, tables, h0):
    mpair, wpair, vpair, pw = tables
    gp_n, rows, wdt = xs.shape
    nch = rows // 8
    return pl.pallas_call(
        functools.partial(_s5_kernel, nch=nch),
        grid=(gp_n,),
        in_specs=[pl.BlockSpec((1, rows, wdt), lambda g: (g, 0, 0)),
                  pl.BlockSpec((1, wdt, wdt), lambda g: (g, 0, 0)),
                  pl.BlockSpec((1, wdt, wdt), lambda g: (g, 0, 0)),
                  pl.BlockSpec((1, wdt, wdt), lambda g: (g, 0, 0)),
                  pl.BlockSpec((1, 2, 2, nch + 1, 2 * S5_STATE), lambda g: (g, 0, 0, 0, 0)),
                  pl.BlockSpec((1, 2, 2, 8, 2 * S5_STATE), lambda g: (g, 0, 0, 0, 0))],
        out_specs=[pl.BlockSpec((1, rows, wdt), lambda g: (g, 0, 0)),
                   pl.BlockSpec((1, 2, 2, 8, 2 * S5_STATE), lambda g: (g, 0, 0, 0, 0))],
        out_shape=[jax.ShapeDtypeStruct((gp_n, rows, wdt), F32),
                   jax.ShapeDtypeStruct((gp_n, 2, 2, 8, 2 * S5_STATE), F32)],
        scratch_shapes=[pltpu.VMEM((4, rows, 2 * S5_STATE), F32), pltpu.VMEM((4, rows, 2 * S5_STATE), F32)],
        compiler_params=_cparams(("parallel",)),
        name="s5_core",
    )(xs, mpair, wpair, vpair, pw, h0)


def _s5_next_h0(fin):
    g, d, r, _, n = fin.shape
    sw = fin.reshape(g, d, r, 4, 2, n)[:, :, :, :, ::-1, :]
    keep = jnp.array([[1.0, 0.0], [0.0, 1.0]], F32)[None, :, None, None, :, None]
    return (sw * keep).reshape(fin.shape)


def _s5_glu_kernel(y_ref, u_ref, d_ref, w_ref, b_ref, o_ref, nat_ref, *, rows16):
    _s5_from_chunk_rows(y_ref, nat_ref, rows16)
    y = jnp.concatenate([nat_ref[q] for q in range(BR // LANE)], axis=1) + u_ref[...].astype(F32) * d_ref[...]
    g = jax.nn.gelu(y)
    z = jnp.dot(g.astype(BF16), w_ref[...], preferred_element_type=F32) + b_ref[...]
    o_ref[...] = g * jax.nn.sigmoid(z)


def _s5_glu(ys, p, d_skip, glu_w, glu_b, tm=512):
    m = p.shape[0]
    tm = min(tm, m)
    rows16 = tm // S5_T
    return pl.pallas_call(
        functools.partial(_s5_glu_kernel, rows16=rows16),
        grid=(m // tm,),
        in_specs=[pl.BlockSpec((_S5_PAIRS, rows16, _S5_ROW), lambda i: (0, i, 0)),
                  pl.BlockSpec((tm, BR), lambda i: (i, COL_S5_U * LANE // BR)),
                  pl.BlockSpec((1, BR), lambda i: (0, 0)),
                  pl.BlockSpec((BR, BR), lambda i: (0, 0)),
                  pl.BlockSpec((1, BR), lambda i: (0, 0))],
        out_specs=pl.BlockSpec((tm, BR), lambda i: (i, 0)),
        out_shape=jax.ShapeDtypeStruct((m, BR), F32),
        scratch_shapes=[pltpu.VMEM((BR // LANE, tm, LANE), F32)],
        compiler_params=_cparams(("parallel",)),
        name="s5_glu",
    )(ys, p, d_skip.reshape(1, BR), glu_w.astype(BF16), glu_b.reshape(1, BR))


@functools.lru_cache(maxsize=None)
def _fft_tables_np(n_total, n2_len):
    n1_len = n_total // n2_len
    n1h = n1_len // 2
    k1 = np.arange(n1_len, dtype=np.int64)
    n2 = np.arange(n2_len, dtype=np.int64)
    m = (k1[None, :, None] * k1[None, None, :] * n2_len + n2[:, None, None] * k1[None, :, None]) % n_total
    ang = m.astype(np.float64) * (2.0 * math.pi / n_total)
    c, s = np.cos(ang), np.sin(ang)
    ch, sh = c[:, :, :n1h], s[:, :, :n1h]
    gf = np.concatenate([np.concatenate([ch, sh], 2), np.concatenate([-sh, ch], 2)], 1)
    gk = np.concatenate([ch, -sh], 1)
    cht, sht = ch.transpose(0, 2, 1), sh.transpose(0, 2, 1)
    gi = np.concatenate([np.concatenate([cht, -sht], 2), np.concatenate([sht, cht], 2)], 1)
    a2 = ((n2[:, None] * n2[None, :]) % n2_len).astype(np.float64) * (2.0 * math.pi / n2_len)
    c2, s2 = np.cos(a2), np.sin(a2)
    f2 = np.concatenate([np.concatenate([c2, s2], 1), np.concatenate([-s2, c2], 1)], 0)
    f2c = np.concatenate([np.concatenate([c2, -s2], 1), np.concatenate([s2, c2], 1)], 0)
    return tuple(np.asarray(t, np.float32) for t in (gf, gk, gi, f2, f2c))


def _fft_tables(n_total, n2_len):
    return tuple(jnp.asarray(t).astype(BF16) for t in _fft_tables_np(n_total, n2_len))


def _hy_shortconv_kernel(x_ref, w_ref, b_ref, o_ref):
    x = x_ref[0].astype(F32)
    l = x.shape[0]
    row = lax.broadcasted_iota(jnp.int32, x.shape, 0)
    prev = jnp.where(row == 0, 0.0, pltpu.roll(x, 1, axis=0))
    nxt = jnp.where(row == l - 1, 0.0, pltpu.roll(x, l - 1, axis=0))
    o_ref[0] = w_ref[0:1, :] * prev + w_ref[1:2, :] * x + w_ref[2:3, :] * nxt + b_ref[...]


def _hy_shortconv(p3, conv_w, conv_b):
    bsz, l, _ = p3.shape
    units = 3 * BR // LANE
    return pl.pallas_call(
        _hy_shortconv_kernel,
        grid=(bsz, units),
        in_specs=[pl.BlockSpec((1, l, LANE), lambda b, u: (b, 0, COL_HY_X + u)),
                  pl.BlockSpec((3, LANE), lambda b, u: (0, u)),
                  pl.BlockSpec((1, LANE), lambda b, u: (0, u))],
        out_specs=pl.BlockSpec((1, l, LANE), lambda b, u: (b, 0, u)),
        out_shape=jax.ShapeDtypeStruct((bsz, l, 3 * BR), F32),
        compiler_params=_cparams(("parallel", "parallel")),
        name="hy_shortconv",
    )(p3, conv_w, conv_b.reshape(1, 3 * BR))


def _fft_stage2(mid_ref, f2_ref, n1_len, n2_len, emit):
    def body(kp, _):
        def rhs(k1):
            return jnp.concatenate([mid_ref[pl.ds(k1, n2_len, stride=2 * n1_len), :],
                                    mid_ref[pl.ds(n1_len + k1, n2_len, stride=2 * n1_len), :]], axis=0)
        r = jnp.concatenate([rhs(2 * kp), rhs(2 * kp + 1)], axis=1).astype(BF16)
        xs = jnp.dot(f2_ref[...], r, preferred_element_type=F32)
        emit(2 * kp, xs[:, :LANE])
        emit(2 * kp + 1, xs[:, LANE:])
        return 0
    lax.fori_loop(0, n1_len // 2, body, 0, unroll=_FFT_UNROLL)


def _hy_kf_kernel(fwd_ref, bwd_ref, nrm_ref, gk_ref, f2_ref, o_ref, mid_ref, *, n1_len, n2_len):
    n1h = n1_len // 2
    row = lax.broadcasted_iota(jnp.int32, (n1h, LANE), 0)

    def stage1(n2, _):
        f = fwd_ref[pl.ds(n2, n1h, stride=n2_len), :]
        b = jnp.where((row == 0) & (n2 == 0), 0.0, bwd_ref[pl.ds(n2, n1h, stride=n2_len), :])
        a = jnp.dot(gk_ref[n2], jnp.concatenate([f, b], axis=1).astype(BF16), preferred_element_type=F32)
        r0 = pl.multiple_of(n2 * 2 * n1_len, 2 * n1_len)
        mid_ref[0, pl.ds(r0, 2 * n1_len), :] = a[:, :LANE]
        mid_ref[1, pl.ds(r0, 2 * n1_len), :] = a[:, LANE:]
        return 0
    lax.fori_loop(0, n2_len, stage1, 0, unroll=_FFT_UNROLL)
    scale = nrm_ref[...] * (1.0 / (n1_len * n2_len))

    def stage2(k1, _):
        def rhs(j):
            return jnp.concatenate([mid_ref[j, pl.ds(k1, n2_len, stride=2 * n1_len), :],
                                    mid_ref[j, pl.ds(n1_len + k1, n2_len, stride=2 * n1_len), :]], axis=0)
        r = jnp.concatenate([rhs(0), rhs(1)], axis=1).astype(BF16)
        xs = jnp.dot(f2_ref[...], r, preferred_element_type=F32)
        f, b = xs[:, :LANE], xs[:, LANE:]
        o_ref[0, k1] = (jnp.concatenate([f[:n2_len] + b[:n2_len], f[n2_len:] - b[n2_len:]], axis=0)
                        * scale).astype(BF16)
        return 0
    lax.fori_loop(0, n1_len, stage2, 0, unroll=_FFT_UNROLL)


def _hy_kf(h2d, nrm, gk, f2, n2_len):
    l, cols = h2d.shape
    n1_len = 2 * l // n2_len
    tiles = cols // (2 * LANE)
    return pl.pallas_call(
        functools.partial(_hy_kf_kernel, n1_len=n1_len, n2_len=n2_len),
        grid=(tiles,),
        in_specs=[pl.BlockSpec((l, LANE), lambda t: (0, t)),
                  pl.BlockSpec((l, LANE), lambda t: (0, tiles + t)),
                  pl.BlockSpec((1, LANE), lambda t: (0, t)),
                  pl.BlockSpec(gk.shape, lambda t: (0, 0, 0)),
                  pl.BlockSpec(f2.shape, lambda t: (0, 0))],
        out_specs=pl.BlockSpec((1, n1_len, 2 * n2_len, LANE), lambda t: (t, 0, 0, 0)),
        out_shape=jax.ShapeDtypeStruct((tiles, n1_len, 2 * n2_len, LANE), BF16),
        scratch_shapes=[pltpu.VMEM((2, 4 * l, LANE), F32)],
        compiler_params=_cparams(("parallel",)),
        name="hy_kf",
    )(h2d, h2d, nrm, gk, f2)


def _hy_conv_kernel(z_ref, g_ref, kf_ref, bias_ref, gf_ref, gi_ref, f2_ref, f2c_ref, o_ref,
                    mid_ref, spec_ref, *, n1_len, n2_len, order):
    n1h = n1_len // 2

    def stage1(n2, _):
        rhs = jnp.concatenate([z_ref[0, pl.ds(n2, n1h, stride=n2_len), :],
                               z_ref[1, pl.ds(n2, n1h, stride=n2_len), :]], axis=0).astype(BF16)
        r0 = pl.multiple_of(n2 * 2 * n1_len, 2 * n1_len)
        mid_ref[pl.ds(r0, 2 * n1_len), :] = jnp.dot(gf_ref[n2], rhs, preferred_element_type=F32)
        return 0
    lax.fori_loop(0, n2_len, stage1, 0, unroll=_FFT_UNROLL)

    def emit(k1, xs):
        kf = kf_ref[0, k1].astype(F32)
        xr, xi = xs[:n2_len], xs[n2_len:]
        kr, ki = kf[:n2_len], kf[n2_len:]
        spec_ref[k1] = jnp.concatenate([xr * kr - xi * ki, xr * ki + xi * kr], axis=0).astype(BF16)
    _fft_stage2(mid_ref, f2_ref, n1_len, n2_len, emit)

    def stage2i(kp, _):
        r = jnp.concatenate([spec_ref[2 * kp], spec_ref[2 * kp + 1]], axis=1)
        b = jnp.dot(f2c_ref[...], r, preferred_element_type=F32)
        r0 = pl.multiple_of(kp * 4 * n2_len, 4 * n2_len)
        mid_ref[pl.ds(r0, 2 * n2_len), :] = b[:, :LANE]
        mid_ref[pl.ds(r0 + 2 * n2_len, 2 * n2_len), :] = b[:, LANE:]
        return 0
    lax.fori_loop(0, n1_len // 2, stage2i, 0, unroll=_FFT_UNROLL)

    def stage1i(n2, _):
        rhs = jnp.concatenate([mid_ref[pl.ds(n2, n1_len, stride=2 * n2_len), :],
                               mid_ref[pl.ds(n2_len + n2, n1_len, stride=2 * n2_len), :]], axis=0).astype(BF16)
        o = jnp.dot(gi_ref[n2], rhs, preferred_element_type=F32)
        o_ref[0, pl.ds(n2, n1h, stride=n2_len), :] = o[:n1h]
        o_ref[1, pl.ds(n2, n1h, stride=n2_len), :] = o[n1h:]
        return 0
    lax.fori_loop(0, n2_len, stage1i, 0, unroll=_FFT_UNROLL)

    bias = bias_ref[order:order + 1, :]
    for j in range(2):
        o_ref[j] = g_ref[j] * (o_ref[j] + bias * z_ref[j])


def _hy_conv(z, zcol, g, gcol, kf, bias, tabs, n2_len, order):
    gf, _, gi, f2, f2c = tabs
    bsz, l, _ = z.shape
    n1_len = 2 * l // n2_len
    tiles = BR // LANE
    const2 = lambda *_: (0, 0)
    const3 = lambda *_: (0, 0, 0)
    one = pl.Buffered(1)
    return pl.pallas_call(
        functools.partial(_hy_conv_kernel, n1_len=n1_len, n2_len=n2_len, order=order),
        grid=(tiles, bsz // 2),
        in_specs=[pl.BlockSpec((2, l, LANE), lambda t, b: (b, 0, zcol + t)),
                  pl.BlockSpec((2, l, LANE), lambda t, b: (b, 0, gcol + t)),
                  pl.BlockSpec((1, n1_len, 2 * n2_len, LANE), lambda t, b: (order * tiles + t, 0, 0, 0),
                               pipeline_mode=one),
                  pl.BlockSpec((HY_ORDER, LANE), lambda t, b: (0, t)),
                  pl.BlockSpec(gf.shape, const3, pipeline_mode=one),
                  pl.BlockSpec(gi.shape, const3, pipeline_mode=one),
                  pl.BlockSpec(f2.shape, const2, pipeline_mode=one),
                  pl.BlockSpec(f2c.shape, const2, pipeline_mode=one)],
        out_specs=pl.BlockSpec((2, l, LANE), lambda t, b: (b, 0, t)),
        out_shape=jax.ShapeDtypeStruct((bsz, l, BR), F32),
        scratch_shapes=[pltpu.VMEM((4 * l, LANE), F32),
                        pltpu.VMEM((n1_len, 2 * n2_len, LANE), BF16)],
        compiler_params=_cparams(("parallel", "arbitrary")),
        name="hy_conv%d" % order,
    )(z, g, kf, bias, gf, gi, f2, f2c)


def _hyena_filters(l, w1, b1, f1, w2, b2, f2, w3, b3, f3, w4):
    hp = lax.Precision.HIGHEST
    t = jnp.linspace(0.0, 1.0, l, dtype=F32)[:, None]
    wpos = 2.0 * math.pi * jnp.arange(l, dtype=F32)[:, None] / l
    f = jnp.linspace(1e-4, HY_BANDS - 1, HY_BANDS, dtype=F32)[None, :]
    z = jnp.concatenate([t, jnp.cos(f * wpos), -jnp.sin(f * wpos)], axis=-1)
    h = jnp.sin(f1 * (jnp.dot(z, w1, precision=hp) + b1))
    h = jnp.sin(f2 * (jnp.dot(h, w2, precision=hp) + b2))
    h = jnp.sin(f3 * (jnp.dot(h, w3, precision=hp) + b3))
    h = jnp.dot(h, w4, precision=hp).reshape(l, 2, HY_ORDER, BR)
    deltas = jnp.linspace(math.log(HY_TARGET) / HY_SLOW_DECAY,
                          math.log(HY_TARGET) / HY_FAST_DECAY, BR, dtype=F32)
    h = h * jnp.exp(-t * jnp.abs(deltas))[:, None, None, :]
    ss = jnp.sum(h[:, 0] * h[:, 0], axis=0) + jnp.sum(h[1:, 1] * h[1:, 1], axis=0)
    return h.reshape(l, 2 * HY_ORDER * BR), lax.rsqrt(ss + EPS).reshape(1, HY_ORDER * BR)


def _hyena(p3, conv_w, conv_b, filters, bias, n2_len):
    bsz, l, _ = p3.shape
    tabs = _fft_tables(2 * l, n2_len)
    xs = _hy_shortconv(p3, conv_w, conv_b)
    h2d, nrm = filters
    kf = _hy_kf(h2d, nrm, tabs[1], tabs[3], n2_len)
    tiles = BR // LANE
    z1 = _hy_conv(xs, 0, xs, tiles, kf, bias, tabs, n2_len, 0)
    return _hy_conv(z1, 0, xs, 2 * tiles, kf, bias, tabs, n2_len, 1)


def _nt(a, b):
    return lax.dot_general(a, b, (((1,), (1,)), ((), ())), preferred_element_type=F32)


def _tn(a, b):
    return lax.dot_general(a, b, (((0,), (0,)), ((), ())), preferred_element_type=F32)


def _ret_kernel(q_ref, k_ref, v_ref, cos_ref, sin_ref, lg_ref, nw_ref, s0_ref, o_ref, sfin_ref,
                vb_ref, qf_ref, qb_ref, kf_ref, kb_ref, *, nchunks):
    c = ATT_CHUNK
    lg = lg_ref[0]
    jrow = lax.broadcasted_iota(jnp.int32, (c, c), 0)
    icol = lax.broadcasted_iota(jnp.int32, (c, c), 1)
    jf = jrow.astype(F32)
    dsym = jnp.exp(jnp.abs(jrow - icol).astype(F32) * lg) * jnp.where(jrow == icol, 2.0, 1.0)
    dq_f = jnp.exp((jf + 1.0) * lg)
    dq_b = jnp.exp((c - jf) * lg)
    dk_f = jnp.exp((c - 1.0 - jf) * lg)
    dk_b = jnp.exp(jf * lg)
    dchunk = jnp.exp(c * lg)

    def rope(x, r0):
        return x * cos_ref[pl.ds(r0, c), :] + pltpu.roll(x, RET_DK // 2, axis=1) * sin_ref[pl.ds(r0, c), :]

    def intra_step(n, _):
        r0 = pl.multiple_of(n * c, c)
        rows = pl.ds(r0, c)
        q = rope(q_ref[0, rows, :].astype(F32), r0)
        k = rope(k_ref[0, rows, :].astype(F32), r0) * (RET_DK ** -0.5)
        v = v_ref[0, rows, :].astype(BF16)
        att = _nt(q.astype(BF16), k.astype(BF16)) * dsym
        o_ref[0, rows, :] = jnp.dot(att.astype(BF16), v, preferred_element_type=F32)
        vb_ref[rows, :] = v
        qf_ref[rows, :] = (q * dq_f).astype(BF16)
        qb_ref[rows, :] = (q * dq_b).astype(BF16)
        kf_ref[rows, :] = (k * dk_f).astype(BF16)
        kb_ref[rows, :] = (k * dk_b).astype(BF16)
        return 0

    lax.fori_loop(0, nchunks, intra_step, 0, unroll=_ATT_UNROLL)

    def sweep(lo, hi, carry, finish):
        def step(i, carry):
            s_f, s_b = carry
            rf = pl.ds(pl.multiple_of(i * c, c), c)
            rb = pl.ds(pl.multiple_of((nchunks - 1 - i) * c, c), c)
            o_f = o_ref[0, rf, :] + jnp.dot(qf_ref[rf, :], s_f.astype(BF16), preferred_element_type=F32)
            o_b = o_ref[0, rb, :] + jnp.dot(qb_ref[rb, :], s_b.astype(BF16), preferred_element_type=F32)
            if finish:
                o_f = o_f * lax.rsqrt(jnp.mean(o_f * o_f, axis=-1, keepdims=True) + EPS) * nw_ref[...]
                o_b = o_b * lax.rsqrt(jnp.mean(o_b * o_b, axis=-1, keepdims=True) + EPS) * nw_ref[...]
            o_ref[0, rf, :] = o_f
            o_ref[0, rb, :] = o_b
            return (dchunk * s_f + _tn(kf_ref[rf, :], vb_ref[rf, :]),
                    dchunk * s_b + _tn(kb_ref[rb, :], vb_ref[rb, :]))
        return lax.fori_loop(lo, hi, step, carry)

    carry = sweep(0, nchunks // 2, (s0_ref[0, 0, 0], s0_ref[0, 0, 1]), False)
    s_f, s_b = sweep(nchunks // 2, nchunks, carry, True)
    sfin_ref[0, 0, 0] = s_f
    sfin_ref[0, 0, 1] = s_b


def _retention(p3, cos2, sin2, norm_w, s0):
    bsz, l, _ = p3.shape
    lgs = jnp.log(1.0 - 2.0 ** (-5.0 - jnp.arange(RET_HEADS, dtype=F32)))
    lgs = jnp.broadcast_to(lgs[:, None, None], (RET_HEADS, 1, LANE))
    tab = pl.BlockSpec((l, LANE), lambda b, h: (0, 0))
    st = pl.BlockSpec((1, 1, 2, RET_DK, LANE), lambda b, h: (b, h, 0, 0, 0))
    return pl.pallas_call(
        functools.partial(_ret_kernel, nchunks=l // ATT_CHUNK),
        grid=(bsz, RET_HEADS),
        in_specs=[pl.BlockSpec((1, l, LANE), lambda b, h: (b, 0, COL_RET_Q + h)),
                  pl.BlockSpec((1, l, LANE), lambda b, h: (b, 0, COL_RET_K + h)),
                  pl.BlockSpec((1, l, LANE), lambda b, h: (b, 0, COL_RET_V + h)),
                  tab, tab,
                  pl.BlockSpec((1, 1, LANE), lambda b, h: (h, 0, 0)),
                  pl.BlockSpec((1, LANE), lambda b, h: (0, h)),
                  st],
        out_specs=[pl.BlockSpec((1, l, LANE), lambda b, h: (b, 0, h)), st],
        out_shape=[jax.ShapeDtypeStruct((bsz, l, BR), F32),
                   jax.ShapeDtypeStruct((bsz, RET_HEADS, 2, RET_DK, LANE), F32)],
        scratch_shapes=[pltpu.VMEM((l, LANE), BF16)] * 5,
        compiler_params=_cparams(("parallel", "parallel")),
        name="retention",
    )(p3, p3, p3, cos2, sin2, lgs, norm_w.reshape(1, BR), s0)


def _rope_tables(ang):
    cos, sin = jnp.cos(ang), jnp.sin(ang)
    return jnp.concatenate([cos, cos], -1), jnp.concatenate([-sin, sin], -1)


def _latent_angles(rows):
    half = RET_DK // 4
    inv = ROPE_BASE ** (-jnp.arange(half, dtype=F32) / half)
    r = jnp.repeat(jnp.arange(rows, dtype=F32), GRID_W)
    cl = jnp.tile(jnp.arange(GRID_W, dtype=F32), rows)
    return jnp.concatenate([r[:, None] * inv, cl[:, None] * inv], axis=-1)


def _ctx_angles(n_ctx):
    n = RET_DK // 2
    inv = ROPE_BASE ** (-jnp.arange(n, dtype=F32) / n)
    return jnp.arange(n_ctx, dtype=F32)[:, None] * inv


def _gla_kernel(q_ref, k_ref, v_ref, lr_ref, gw_ref, gb_ref, nw_ref, s0_ref, o_ref, sfin_ref,
                vb_ref, qf_ref, qb_ref, kf_ref, kb_ref, df_ref, db_ref, *, nchunks):
    c = ATT_CHUNK
    jrow = lax.broadcasted_iota(jnp.int32, (c, c), 0)
    icol = lax.broadcasted_iota(jnp.int32, (c, c), 1)
    tri_f = jnp.where(icol <= jrow, 1.0, 0.0).astype(BF16)
    tri_b = jnp.where(icol >= jrow, 1.0, 0.0).astype(BF16)
    lane = lax.broadcasted_iota(jnp.int32, (1, LANE), 1)
    head0 = lane < GLA_DK

    def decays(r0, d):
        x = jnp.dot(lr_ref[0, pl.ds(r0, c), :].astype(BF16), gw_ref[d], preferred_element_type=F32) + gb_ref[d]
        la = (jnp.minimum(x, 0.0) - jnp.log(1.0 + jnp.exp(-jnp.abs(x)))) * (1.0 / GLA_TAU)
        hi = la.astype(BF16)
        lo = (la - hi.astype(F32)).astype(BF16)
        cs = jnp.dot(tri_f if d == 0 else tri_b, jnp.concatenate([hi, lo], axis=1), preferred_element_type=F32)
        return cs[:, :LANE] + cs[:, LANE:]

    def intra_step(n, _):
        r0 = pl.multiple_of(n * c, c)
        rows = pl.ds(r0, c)
        q = q_ref[0, rows, :].astype(F32) * (GLA_DK ** -0.5)
        k = k_ref[0, rows, :].astype(F32)
        bf = decays(r0, 0)
        bb = decays(r0, 1)
        ref_f = bf[c // 2 - 1:c // 2, :]
        ref_b = bb[c // 2:c // 2 + 1, :]
        qf = q * jnp.exp(bf - ref_f)
        kf = (k * jnp.exp(ref_f - bf)).astype(BF16)
        qb = q * jnp.exp(bb - ref_b)
        kb = (k * jnp.exp(ref_b - bb)).astype(BF16)
        for h in range(2):
            hm = head0 if h == 0 else jnp.logical_not(head0)
            v = v_ref[0, rows, h * GLA_DV:(h + 1) * GLA_DV].astype(BF16)
            att = (jnp.where(icol <= jrow, _nt(jnp.where(hm, qf, 0.0).astype(BF16), kf), 0.0)
                   + jnp.where(icol >= jrow, _nt(jnp.where(hm, qb, 0.0).astype(BF16), kb), 0.0))
            o_ref[0, rows, h * GLA_DV:(h + 1) * GLA_DV] = jnp.dot(att.astype(BF16), v, preferred_element_type=F32)
            vb_ref[rows, h * GLA_DV:(h + 1) * GLA_DV] = v
        bf_last, bb_last = bf[c - 1:c, :], bb[0:1, :]
        qf_ref[rows, :] = (q * jnp.exp(bf)).astype(BF16)
        qb_ref[rows, :] = (q * jnp.exp(bb)).astype(BF16)
        kf_ref[rows, :] = (k * jnp.exp(bf_last - bf)).astype(BF16)
        kb_ref[rows, :] = (k * jnp.exp(bb_last - bb)).astype(BF16)
        d0 = pl.multiple_of(n * 8, 8)
        df_ref[pl.ds(d0, 8), :] = jnp.broadcast_to(jnp.exp(bf_last), (8, LANE))
        db_ref[pl.ds(d0, 8), :] = jnp.broadcast_to(jnp.exp(bb_last), (8, LANE))
        return 0

    lax.fori_loop(0, nchunks, intra_step, 0, unroll=_ATT_UNROLL)

    def sweep(lo, hi, carry, finish):
        def step(i, carry):
            sts = list(carry)
            new = []
            for d in range(2):
                n = i if d == 0 else nchunks - 1 - i
                rows = pl.ds(pl.multiple_of(n * c, c), c)
                qs = (qf_ref if d == 0 else qb_ref)[rows, :]
                kd = (kf_ref if d == 0 else kb_ref)[rows, :]
                dec = (df_ref if d == 0 else db_ref)[pl.ds(pl.multiple_of(n * 8, 8), 1), :]
                for h in range(2):
                    hm = head0 if h == 0 else jnp.logical_not(head0)
                    cols = slice(h * GLA_DV, (h + 1) * GLA_DV)
                    st = sts[2 * d + h]
                    o = o_ref[0, rows, cols] + _nt(qs, st.astype(BF16))
                    if finish:
                        o = o * lax.rsqrt(jnp.mean(o * o, axis=-1, keepdims=True) + EPS) * nw_ref[:, cols]
                    o_ref[0, rows, cols] = o
                    new.append(jnp.where(hm, dec * st + _tn(vb_ref[rows, cols], kd), 0.0))
            return tuple(new)
        return lax.fori_loop(lo, hi, step, carry)

    carry = (s0_ref[0, 0, 0, 0], s0_ref[0, 0, 0, 1], s0_ref[0, 0, 1, 0], s0_ref[0, 0, 1, 1])
    carry = sweep(0, nchunks // 2, carry, False)
    s_fin = sweep(nchunks // 2, nchunks, carry, True)
    sfin_ref[0, 0, 0, 0] = s_fin[0]
    sfin_ref[0, 0, 0, 1] = s_fin[1]
    sfin_ref[0, 0, 1, 0] = s_fin[2]
    sfin_ref[0, 0, 1, 1] = s_fin[3]


def _gla(p3, gate_w, gate_b, norm_w, s0):
    bsz, l, _ = p3.shape
    pairs = GLA_HEADS // 2
    gw = jnp.zeros((2, LANE, GLA_HEADS * GLA_DK), F32)
    for d in range(2):
        gw = gw.at[d, d * GLA_LR:(d + 1) * GLA_LR, :].set(gate_w[d].astype(F32))
    st = pl.BlockSpec((1, 1, 2, 2, GLA_DV, LANE), lambda b, h: (b, h, 0, 0, 0, 0))
    return pl.pallas_call(
        functools.partial(_gla_kernel, nchunks=l // ATT_CHUNK),
        grid=(bsz, pairs),
        in_specs=[pl.BlockSpec((1, l, LANE), lambda b, h: (b, 0, COL_GLA_Q + h)),
                  pl.BlockSpec((1, l, LANE), lambda b, h: (b, 0, COL_GLA_K + h)),
                  pl.BlockSpec((1, l, 2 * GLA_DV), lambda b, h: (b, 0, COL_GLA_V // 2 + h)),
                  pl.BlockSpec((1, l, LANE), lambda b, h: (b, 0, COL_GLA_LR)),
                  pl.BlockSpec((2, LANE, LANE), lambda b, h: (0, 0, h)),
                  pl.BlockSpec((2, 1, LANE), lambda b, h: (0, 0, h)),
                  pl.BlockSpec((1, 2 * GLA_DV), lambda b, h: (0, h)),
                  st],
        out_specs=[pl.BlockSpec((1, l, 2 * GLA_DV), lambda b, h: (b, 0, h)), st],
        out_shape=[jax.ShapeDtypeStruct((bsz, l, BR), F32),
                   jax.ShapeDtypeStruct((bsz, pairs, 2, 2, GLA_DV, LANE), F32)],
        scratch_shapes=([pltpu.VMEM((l, 2 * GLA_DV), BF16)] + [pltpu.VMEM((l, LANE), BF16)] * 4
                        + [pltpu.VMEM((8 * l // ATT_CHUNK, LANE), F32)] * 2),
        compiler_params=_cparams(("parallel", "parallel")),
        name="gla",
    )(p3, p3, p3, p3, gw.astype(BF16), gate_b.astype(F32).reshape(2, 1, GLA_HEADS * GLA_DK),
      norm_w.reshape(1, BR), s0)


def _mixers(p2, xs, bsz, prm, i, s5_tabs, hy_filters, rope, s5_h0, ret_s0, gla_s0, with_hyena):
    m = p2.shape[0]
    l = m // bsz
    p3 = p2.reshape(bsz, l, NP)
    tabs = s5_tabs[:3] + (s5_tabs[3][:, :, :, :l // (2 * S5_T) + 1],)
    y, s5_fin = _s5_core(xs, tabs, s5_h0)
    s5 = _s5_glu(y, p2, prm['s5_d'][i], prm['s5_glu_w'][i], prm['s5_glu_b'][i])
    hy = None
    if with_hyena:
        hy = _hyena(p3, prm['hy_conv_w'][i], prm['hy_conv_b'][i], hy_filters, prm['hy_bias'][i].astype(F32),
                    64 if l >= 512 else 16).reshape(m, BR)
    ret, ret_fin = _retention(p3, rope[0], rope[1], prm['ret_norm_w'][i], ret_s0)
    gla, gla_fin = _gla(p3, prm['gla_gate_w'][i], prm['gla_gate_b'][i], prm['gla_norm_w'][i], gla_s0)
    return (s5, hy, ret.reshape(m, BR), gla.reshape(m, BR)), (s5_fin, ret_fin, gla_fin)


def kernel(x, c, ctx, c_ctx, norm_w, ada_w, ada_b, w_in, w_out, s5_a_re, s5_a_im, s5_log_dt, s5_b_re, s5_b_im, s5_c_re, s5_c_im, s5_d, s5_glu_w, s5_glu_b, hy_conv_w, hy_conv_b, hy_w1, hy_b1, hy_f1, hy_w2, hy_b2, hy_f2, hy_w3, hy_b3, hy_f3, hy_w4, hy_bias, ret_norm_w, gla_gate_w, gla_gate_b, gla_norm_w, final_norm_w):
    prm = dict(s5_a_re=s5_a_re, s5_a_im=s5_a_im, s5_log_dt=s5_log_dt, s5_b_re=s5_b_re, s5_b_im=s5_b_im,
               s5_c_re=s5_c_re, s5_c_im=s5_c_im, s5_d=s5_d, s5_glu_w=s5_glu_w, s5_glu_b=s5_glu_b,
               hy_conv_w=hy_conv_w, hy_conv_b=hy_conv_b, hy_w1=hy_w1, hy_b1=hy_b1, hy_f1=hy_f1,
               hy_w2=hy_w2, hy_b2=hy_b2, hy_f2=hy_f2, hy_w3=hy_w3, hy_b3=hy_b3, hy_f3=hy_f3,
               hy_w4=hy_w4, hy_bias=hy_bias, ret_norm_w=ret_norm_w, gla_gate_w=gla_gate_w,
               gla_gate_b=gla_gate_b, gla_norm_w=gla_norm_w)
    bsz, l, d = x.shape
    lc = ctx.shape[1]
    depth = w_in.shape[0]
    cc = jnp.concatenate([c, c_ctx[None, :], jnp.zeros((8 - bsz - 1, d), F32)], axis=0)
    mod = _ada(cc, ada_w, ada_b)
    rope_l = _rope_tables(_latent_angles(l // GRID_W))
    rope_c = _rope_tables(_ctx_angles(lc))
    s5_zero = jnp.zeros((S5_GROUPS // 2, 2, 2, 8, 2 * S5_STATE), F32)
    ret_zero = jnp.zeros((bsz, RET_HEADS, 2, RET_DK, LANE), F32)
    gla_zero = jnp.zeros((bsz, GLA_HEADS // 2, 2, 2, GLA_DV, LANE), F32)

    w_all = _wprep(w_in)
    wo_all = w_out.astype(BF16)
    filt = (hy_w1, hy_b1, hy_f1, hy_w2, hy_b2, hy_f2, hy_w3, hy_b3, hy_f3, hy_w4)

    x_l = x.reshape(bsz * l, d)
    x_c = ctx.reshape(bsz * lc, d)
    for i in range(depth):
        last = i == depth - 1
        sh, sc, gt = mod[i, :, :d], mod[i, :, d:2 * d], mod[i, :, 2 * d:]
        p_c, xs_c = _inproj(x_c, norm_w[i], sc[bsz:bsz + 1, None, :], sh[bsz:bsz + 1, None, :], w_all, i,
                            bsz * lc)
        p_l, xs_l = _inproj(x_l, norm_w[i], sc[:bsz, None, :], sh[:bsz, None, :], w_all, i, l)
        s5_tabs = _s5_tables(s5_a_re[i], s5_a_im[i], s5_log_dt[i], s5_b_re[i], s5_b_im[i], s5_c_re[i],
                             s5_c_im[i], l // (2 * S5_T))
        filt_i = [w[i].astype(F32) for w in filt]
        ys_c, (s5_fin, ret_fin, gla_fin) = _mixers(
            p_c, xs_c, bsz, prm, i, s5_tabs, None if last else _hyena_filters(lc, *filt_i), rope_c, s5_zero,
            ret_zero, gla_zero, with_hyena=not last)
        ys_l, _ = _mixers(p_l, xs_l, bsz, prm, i, s5_tabs, _hyena_filters(l, *filt_i), rope_l,
                          _s5_next_h0(s5_fin), ret_fin, gla_fin, with_hyena=True)
        x_l = _outproj(x_l, ys_l, p_l, wo_all, i, gt[:bsz, None, :], final_norm_w, l, final=last)
        if not last:
            x_c = _outproj(x_c, ys_c, p_c, wo_all, i, gt[bsz:bsz + 1, None, :], final_norm_w, bsz * lc,
                           final=False)
    return x_l.reshape(bsz, l, d)
```

```python
import functools
import math

import numpy as np
import jax
import jax.numpy as jnp
from jax import lax
from jax.experimental import pallas as pl
from jax.experimental.pallas import tpu as pltpu

F32 = jnp.float32
BF16 = jnp.bfloat16

EPS = 1e-6
GRID_W = 64
BR = 512
S5_GSIZE = 16
S5_GROUPS = BR // S5_GSIZE
S5_STATE = 64
S5_T = 16
HY_ORDER = 2
HY_EMB = 33
HY_BANDS = (HY_EMB - 1) // 2
HY_FAST_DECAY = 0.3
HY_SLOW_DECAY = 1.5
HY_TARGET = 1e-2
RET_HEADS = 4
RET_DK = 128
GLA_HEADS = 4
GLA_DK = 64
GLA_DV = 128
GLA_LR = 16
GLA_TAU = 16.0
ROPE_BASE = 10000.0
ATT_CHUNK = 128
_FFT_UNROLL = 8
_ATT_UNROLL = 2

LANE = 128
VMEM_LIMIT = 56 * 1024 * 1024

COL_S5_U = 0
COL_S5_G = 4
COL_HY_X = 8
COL_HY_G = 20
COL_RET_Q = 24
COL_RET_K = 28
COL_RET_V = 32
COL_RET_G = 36
COL_GLA_Q = 40
COL_GLA_K = 42
COL_GLA_V = 44
COL_GLA_G = 48
COL_GLA_LR = 52
NP_UNITS = 54
NP = NP_UNITS * LANE
W_TILE = 768
_ALIGNED = COL_GLA_G * LANE


def _cparams(sem):
    return pltpu.CompilerParams(dimension_semantics=sem, vmem_limit_bytes=VMEM_LIMIT)


def _silu(x):
    return x * jax.nn.sigmoid(x)


def _wprep(w_in):
    depth, d, in_w = w_in.shape
    assert _ALIGNED % W_TILE == 0 and in_w == _ALIGNED + 2 * GLA_LR + BR and NP == _ALIGNED + W_TILE
    wb = w_in.astype(BF16)
    tail = jnp.concatenate([wb[:, :, _ALIGNED + 2 * GLA_LR:], wb[:, :, _ALIGNED:_ALIGNED + 2 * GLA_LR],
                            jnp.zeros((depth, d, W_TILE - BR - 2 * GLA_LR), BF16)], axis=-1)
    return wb, tail


def _ada_kernel(c_ref, w_ref, b_ref, o_ref):
    a = _silu(c_ref[...]).astype(BF16)
    o_ref[0] = jnp.dot(a, w_ref[0].astype(BF16), preferred_element_type=F32) + b_ref[0]


def _ada(cc, ada_w, ada_b, tn=768):
    depth, d, n = ada_w.shape
    return pl.pallas_call(
        _ada_kernel,
        grid=(depth, n // tn),
        in_specs=[pl.BlockSpec((8, d), lambda i, j: (0, 0)),
                  pl.BlockSpec((1, d, tn), lambda i, j: (i, 0, j)),
                  pl.BlockSpec((1, 1, tn), lambda i, j: (i, 0, j))],
        out_specs=pl.BlockSpec((1, 8, tn), lambda i, j: (i, 0, j)),
        out_shape=jax.ShapeDtypeStruct((depth, 8, n), F32),
        compiler_params=_cparams(("parallel", "parallel")),
        name="ada_mod",
    )(cc, ada_w, ada_b.reshape(depth, 1, n))


_S5_SLOT = 2 * S5_GSIZE
_S5_PAIRS = S5_GROUPS // 2
_S5_ROW = S5_T * _S5_SLOT


def _s5_lane_perm():
    perm = np.zeros((4, _S5_ROW), np.int32)
    for s in range(4):
        for tau in range(S5_T):
            for c in range(_S5_SLOT):
                gl, i = divmod(c, S5_GSIZE)
                perm[s, LANE * (tau // 4) + _S5_SLOT * ((s + tau) % 4) + c] = (gl * S5_T + tau) * S5_GSIZE + i
    return perm


def _s5_to_chunk_rows(nat_ref, xs_ref, rows16):
    slot = lax.broadcasted_iota(jnp.int32, (rows16, LANE), 1) // _S5_SLOT
    for q in range(4):
        for qd in range(4):
            rolled = []
            for t in range(4):
                piece = nat_ref[q, pl.ds(4 * qd + t, rows16, stride=S5_T), :]
                rolled.append(pltpu.roll(piece, _S5_SLOT * t, axis=1) if t else piece)
            for s in range(4):
                acc = rolled[0]
                for t in range(1, 4):
                    acc = jnp.where(slot == (s + t) % 4, rolled[t], acc)
                xs_ref[4 * q + s, :, qd * LANE:(qd + 1) * LANE] = acc.astype(xs_ref.dtype)


def _s5_from_chunk_rows(y_ref, nat_ref, rows16):
    slot = lax.broadcasted_iota(jnp.int32, (rows16, LANE), 1) // _S5_SLOT
    for q in range(4):
        for qd in range(4):
            src = [y_ref[4 * q + s, :, qd * LANE:(qd + 1) * LANE] for s in range(4)]
            for t in range(4):
                acc = src[0]
                for s in range(1, 4):
                    acc = jnp.where(slot == (s + t) % 4, src[s], acc)
                if t:
                    acc = pltpu.roll(acc, LANE - _S5_SLOT * t, axis=1)
                nat_ref[q, pl.ds(4 * qd + t, rows16, stride=S5_T), :] = acc


def _inproj_kernel(x_ref, nw_ref, sc_ref, sh_ref, w_ref, tail_ref, o_ref, xs_ref, h_ref, nat_ref, *, rows16):
    j = pl.program_id(1)
    last = pl.num_programs(1) - 1

    def project(w):
        res = jnp.dot(h_ref[...], w, preferred_element_type=F32)
        o_ref[...] = res.astype(o_ref.dtype)
        return res

    @pl.when(j == 0)
    def _():
        x = x_ref[...]
        y = x * lax.rsqrt(jnp.mean(x * x, axis=-1, keepdims=True) + EPS) * nw_ref[...]
        h_ref[...] = (y * (1.0 + sc_ref[0]) + sh_ref[0]).astype(BF16)
        res = project(w_ref[...])
        for q in range(BR // LANE):
            nat_ref[q] = res[:, q * LANE:(q + 1) * LANE]
        _s5_to_chunk_rows(nat_ref, xs_ref, rows16)

    @pl.when((j > 0) & (j < last))
    def _():
        project(w_ref[...])

    @pl.when(j == last)
    def _():
        project(tail_ref[...])


def _inproj(x2, nw, sc, sh, weights, layer, rows_per_mod, tm=1024, tn=W_TILE):
    w_all, w_tail = weights
    m, d = x2.shape
    tm = min(tm, rows_per_mod)
    per = rows_per_mod // tm
    rows16 = tm // S5_T
    n_al = _ALIGNED // tn
    assert COL_S5_U == 0 and tn >= BR and tn == W_TILE
    return pl.pallas_call(
        functools.partial(_inproj_kernel, rows16=rows16),
        grid=(m // tm, NP // tn),
        in_specs=[pl.BlockSpec((tm, d), lambda i, j: (i, 0)),
                  pl.BlockSpec((1, d), lambda i, j: (0, 0)),
                  pl.BlockSpec((1, 1, d), lambda i, j: (i // per, 0, 0)),
                  pl.BlockSpec((1, 1, d), lambda i, j: (i // per, 0, 0)),
                  pl.BlockSpec((None, d, tn), lambda i, j: (layer, 0, jnp.minimum(j, n_al - 1))),
                  pl.BlockSpec((None, d, tn), lambda i, j: (layer, 0, 0))],
        out_specs=[pl.BlockSpec((tm, tn), lambda i, j: (i, j)),
                   pl.BlockSpec((_S5_PAIRS, rows16, _S5_ROW), lambda i, j: (0, i, 0))],
        out_shape=[jax.ShapeDtypeStruct((m, NP), BF16),
                   jax.ShapeDtypeStruct((_S5_PAIRS, m // S5_T, _S5_ROW), BF16)],
        scratch_shapes=[pltpu.VMEM((tm, d), BF16), pltpu.VMEM((BR // LANE, tm, LANE), F32)],
        compiler_params=_cparams(("parallel", "arbitrary")),
        name="inproj",
    )(x2, nw.reshape(1, d), sc, sh, w_all, w_tail)


def _outproj_kernel(x_ref, y0_ref, y1_ref, y2_ref, y3_ref, g0_ref, g1_ref, g2_ref, g3_ref, w_ref, gt_ref, fw_ref,
                    o_ref, *, final):
    acc = None
    for k, (y_ref, g_ref) in enumerate(((y0_ref, g0_ref), (y1_ref, g1_ref), (y2_ref, g2_ref), (y3_ref, g3_ref))):
        gated = (y_ref[...] * _silu(g_ref[...].astype(F32))).astype(BF16)
        part = jnp.dot(gated, w_ref[k * BR:(k + 1) * BR, :], preferred_element_type=F32)
        acc = part if acc is None else acc + part
    x = x_ref[...] + gt_ref[0] * acc
    if final:
        x = x * lax.rsqrt(jnp.mean(x * x, axis=-1, keepdims=True) + EPS) * fw_ref[...]
    o_ref[...] = x


def _outproj(x2, ys, p, wo_all, layer, gt, fw, rows_per_mod, final, tm=256):
    m, d = x2.shape
    tm = min(tm, rows_per_mod)
    per = rows_per_mod // tm
    yspec = pl.BlockSpec((tm, BR), lambda i: (i, 0))

    def gate(col_unit):
        blk = col_unit * LANE // BR
        return pl.BlockSpec((tm, BR), lambda i: (i, blk))

    return pl.pallas_call(
        functools.partial(_outproj_kernel, final=final),
        grid=(m // tm,),
        in_specs=[pl.BlockSpec((tm, d), lambda i: (i, 0)), yspec, yspec, yspec, yspec,
                  gate(COL_S5_G), gate(COL_HY_G), gate(COL_RET_G), gate(COL_GLA_G),
                  pl.BlockSpec((None, 4 * BR, d), lambda i: (layer, 0, 0)),
                  pl.BlockSpec((1, 1, d), lambda i: (i // per, 0, 0)),
                  pl.BlockSpec((1, d), lambda i: (0, 0))],
        out_specs=pl.BlockSpec((tm, d), lambda i: (i, 0)),
        out_shape=jax.ShapeDtypeStruct((m, d), F32),
        compiler_params=_cparams(("parallel",)),
        name="outproj",
    )(x2, *ys, p, p, p, p, wo_all, gt, fw.reshape(1, d))


def _s5_tables(a_re, a_im, log_dt, b_re, b_im, c_re, c_im, nch):
    hp = lax.Precision.HIGHEST
    t_len, g_n, p_n, s_n = S5_T, S5_GROUPS, S5_STATE, S5_GSIZE
    a_re, a_im = a_re.astype(F32), a_im.astype(F32)
    dt = jnp.exp(log_dt.astype(F32))[:, :, None]
    lam_re, lam_im = a_re * dt, a_im * dt

    def power(tau):
        tau = tau.astype(F32)[:, None, None, None]
        mag = jnp.exp(lam_re[None] * tau)
        return mag * jnp.cos(lam_im[None] * tau), mag * jnp.sin(lam_im[None] * tau)

    ab_re, ab_im = power(jnp.ones((1,)))
    ab_re, ab_im = ab_re[0], ab_im[0]
    den = a_re * a_re + a_im * a_im
    nr = ab_re - 1.0
    co_re = (nr * a_re + ab_im * a_im) / den
    co_im = (ab_im * a_re - nr * a_im) / den
    b_re, b_im = b_re.astype(F32), b_im.astype(F32)
    bco_re = co_re[..., None] * b_re - co_im[..., None] * b_im
    bco_im = co_re[..., None] * b_im + co_im[..., None] * b_re
    c_re, c_im = c_re.astype(F32), c_im.astype(F32)

    pr, pi = power(jnp.arange(t_len + 1))
    ca_re = c_re[None] * pr[:, :, :, None, :] - c_im[None] * pi[:, :, :, None, :]
    ca_im = c_re[None] * pi[:, :, :, None, :] + c_im[None] * pr[:, :, :, None, :]
    ca_cat = jnp.concatenate([ca_re[:t_len], -ca_im[:t_len]], axis=-1).transpose(1, 2, 0, 3, 4)
    bco_cat = jnp.concatenate([bco_re, bco_im], axis=-2)
    kk = jnp.einsum('dgmp,dgpj->dgmj', ca_cat.reshape(2, g_n, t_len * s_n, 2 * p_n), bco_cat, precision=hp)
    kk = kk.reshape(2, g_n, t_len, s_n, s_n).transpose(2, 0, 1, 3, 4)
    kf, kb = kk[:, 0], kk[:, 1]
    kfull = jnp.concatenate([kb[:0:-1], (kf[0] + kb[0])[None], kf[1:]], axis=0)
    kcat = kfull.transpose(1, 3, 0, 2).reshape(g_n, s_n, (2 * t_len - 1) * s_n)
    kcat = jnp.pad(kcat, ((0, 0), (0, 0), (0, _S5_ROW - kcat.shape[-1])))
    tt = jnp.arange(t_len)

    wexp = jnp.stack([pr[t_len - 1 - tt, 0], pr[tt, 1]], 0), jnp.stack([pi[t_len - 1 - tt, 0], pi[tt, 1]], 0)
    w_re = wexp[0][..., None] * bco_re[:, None] - wexp[1][..., None] * bco_im[:, None]
    w_im = wexp[0][..., None] * bco_im[:, None] + wexp[1][..., None] * bco_re[:, None]
    w4 = jnp.stack([w_re[0], w_im[0], w_re[1], w_im[1]], 0)
    wcat = w4.transpose(2, 1, 4, 0, 3).reshape(g_n, t_len * s_n, 4 * p_n)

    vexp_f, vexp_b = tt + 1, t_len - tt
    v4 = jnp.stack([ca_re[vexp_f, 0], -ca_im[vexp_f, 0], ca_re[vexp_b, 1], -ca_im[vexp_b, 1]], 0)
    vcat = v4.transpose(2, 0, 4, 1, 3).reshape(g_n, 4 * p_n, t_len * s_n)

    qr, qi = power(t_len * jnp.arange(nch + 1))
    pw = jnp.stack([qr, qi], 0)
    pw = pw.reshape(2, nch + 1, 2, g_n // 2, 2 * p_n).transpose(3, 2, 0, 1, 4)
    return _s5_assemble(kcat, wcat, vcat) + (pw,)


def _s5_assemble_kernel(kcat_ref, wcat_ref, vcat_ref, e_ref, oh_ref, m_ref, w_ref, v_ref):
    half = S5_T * S5_GSIZE
    oh = oh_ref[0]
    zeros = jnp.zeros((half, half), F32)
    diag = []
    for gl in range(2):
        kc = kcat_ref[gl]
        diag.append(jnp.concatenate(
            [kc[:, (S5_T - 1 - t) * S5_GSIZE:(S5_T - 1 - t) * S5_GSIZE + half] for t in range(S5_T)], axis=0))
    m_old = jnp.concatenate([jnp.concatenate([diag[0], zeros], axis=1),
                             jnp.concatenate([zeros, diag[1]], axis=1)], axis=0).astype(BF16)
    m_rows = jnp.dot(oh, m_old, preferred_element_type=F32).astype(BF16)
    m_ref[0] = _nt(m_rows, oh).astype(BF16)
    w_old = jnp.concatenate([jnp.dot(wcat_ref[gl].astype(BF16), e_ref[gl], preferred_element_type=F32)
                             for gl in range(2)], axis=0).astype(BF16)
    w_ref[0] = jnp.dot(oh, w_old, preferred_element_type=F32).astype(BF16)
    v_old = jnp.concatenate([_tn(e_ref[gl], vcat_ref[gl].astype(BF16)) for gl in range(2)], axis=1).astype(BF16)
    v_ref[0] = _nt(v_old, oh).astype(BF16)


def _s5_assemble(kcat, wcat, vcat):
    half = S5_T * S5_GSIZE
    onehot = jnp.asarray(np.eye(_S5_ROW, dtype=np.float32)[_s5_lane_perm()]).astype(BF16)
    place = np.zeros((2, half, _S5_ROW), np.float32)
    for gl in range(2):
        for k in range(4):
            for p in range(S5_STATE):
                place[gl, k * S5_STATE + p, (2 * k + gl) * S5_STATE + p] = 1.0
    out = jax.ShapeDtypeStruct((_S5_PAIRS, _S5_ROW, _S5_ROW), BF16)
    big = pl.BlockSpec((1, _S5_ROW, _S5_ROW), lambda g: (g, 0, 0))
    return pl.pallas_call(
        _s5_assemble_kernel,
        grid=(_S5_PAIRS,),
        in_specs=[pl.BlockSpec((2, S5_GSIZE, _S5_ROW), lambda g: (g, 0, 0)),
                  pl.BlockSpec((2, half, half), lambda g: (g, 0, 0)),
                  pl.BlockSpec((2, half, half), lambda g: (g, 0, 0)),
                  pl.BlockSpec((2, half, _S5_ROW), lambda g: (0, 0, 0)),
                  pl.BlockSpec((1, _S5_ROW, _S5_ROW), lambda g: (g % 4, 0, 0))],
        out_specs=[big, big, big],
        out_shape=[out, out, out],
        compiler_params=_cparams(("parallel",)),
        name="s5_assemble",
    )(kcat, wcat, vcat, jnp.asarray(place).astype(BF16), onehot)


def _nt(a, b):
    return lax.dot_general(a, b, (((1,), (1,)), ((), ())), preferred_element_type=F32)


def _tn(a, b):
    return lax.dot_general(a, b, (((0,), (0,)), ((), ())), preferred_element_type=F32)


def _cmul(ar, ai, br, bi):
    return ar * br - ai * bi, ar * bi + ai * br


def _s5_kernel(x_ref, m_ref, w_ref, v_ref, pw_ref, h0_ref, y_ref, fin_ref, s_ref, h_ref, *, nch):
    ln = 2 * S5_STATE
    x = x_ref[0]
    s_all = jnp.dot(x, w_ref[0], preferred_element_type=F32)
    for k in range(4):
        s_ref[k] = s_all[:, k * ln:(k + 1) * ln]
    a_rf, a_if = pw_ref[0, 0, 0, 1:2, :], pw_ref[0, 0, 1, 1:2, :]
    a_rb, a_ib = pw_ref[0, 1, 0, 1:2, :], pw_ref[0, 1, 1, 1:2, :]

    def tile(c):
        return pl.ds(c, 8, stride=nch)

    def scan_step(s, carry):
        hrf, hif, hrb, hib = carry
        rf, rb = tile(s), tile(nch - 1 - s)
        h_ref[0, rf, :] = hrf
        h_ref[1, rf, :] = hif
        h_ref[2, rb, :] = hrb
        h_ref[3, rb, :] = hib
        pr, pi = _cmul(a_rf, a_if, hrf, hif)
        qr, qi = _cmul(a_rb, a_ib, hrb, hib)
        return (pr + s_ref[0, rf, :], pi + s_ref[1, rf, :], qr + s_ref[2, rb, :], qi + s_ref[3, rb, :])

    init = (h0_ref[0, 0, 0], h0_ref[0, 0, 1], h0_ref[0, 1, 0], h0_ref[0, 1, 1])
    hrf, hif, hrb, hib = lax.fori_loop(0, nch, scan_step, init, unroll=2)

    odd = (lax.broadcasted_iota(jnp.int32, (8, ln), 0) % 2) == 1
    crf = jnp.where(odd, pltpu.roll(hrf, 1, axis=0), 0.0)
    cif = jnp.where(odd, pltpu.roll(hif, 1, axis=0), 0.0)
    crb = jnp.where(odd, 0.0, pltpu.roll(hrb, 7, axis=0))
    cib = jnp.where(odd, 0.0, pltpu.roll(hib, 7, axis=0))

    def fix_step(s, _):
        rf, rb = tile(s), tile(nch - 1 - s)
        pr, pi = _cmul(pw_ref[0, 0, 0, pl.ds(s, 1), :], pw_ref[0, 0, 1, pl.ds(s, 1), :], crf, cif)
        qr, qi = _cmul(pw_ref[0, 1, 0, pl.ds(s, 1), :], pw_ref[0, 1, 1, pl.ds(s, 1), :], crb, cib)
        h_ref[0, rf, :] = h_ref[0, rf, :] + pr
        h_ref[1, rf, :] = h_ref[1, rf, :] + pi
        h_ref[2, rb, :] = h_ref[2, rb, :] + qr
        h_ref[3, rb, :] = h_ref[3, rb, :] + qi
        return 0

    lax.fori_loop(0, nch, fix_step, 0, unroll=2)
    pr, pi = _cmul(pw_ref[0, 0, 0, nch:nch + 1, :], pw_ref[0, 0, 1, nch:nch + 1, :], crf, cif)
    qr, qi = _cmul(pw_ref[0, 1, 0, nch:nch + 1, :], pw_ref[0, 1, 1, nch:nch + 1, :], crb, cib)
    fin_ref[0, 0, 0] = hrf + pr
    fin_ref[0, 0, 1] = hif + pi
    fin_ref[0, 1, 0] = hrb + qr
    fin_ref[0, 1, 1] = hib + qi

    h_all = jnp.concatenate([h_ref[k] for k in range(4)], axis=1).astype(BF16)
    y_ref[0] = (jnp.dot(x, m_ref[0], preferred_element_type=F32)
                + jnp.dot(h_all, v_ref[0], preferred_element_type=F32))


def _s5_core(xs, tables, h0):
    mpair, wpair, vpair, pw = tables
    gp_n, rows, wdt = xs.shape
    nch = rows // 8
    return pl.pallas_call(
        functools.partial(_s5_kernel, nch=nch),
        grid=(gp_n,),
        in_specs=[pl.BlockSpec((1, rows, wdt), lambda g: (g, 0, 0)),
                  pl.BlockSpec((1, wdt, wdt), lambda g: (g, 0, 0)),
                  pl.BlockSpec((1, wdt, wdt), lambda g: (g, 0, 0)),
                  pl.BlockSpec((1, wdt, wdt), lambda g: (g, 0, 0)),
                  pl.BlockSpec((1, 2, 2, nch + 1, 2 * S5_STATE), lambda g: (g, 0, 0, 0, 0)),
                  pl.BlockSpec((1, 2, 2, 8, 2 * S5_STATE), lambda g: (g, 0, 0, 0, 0))],
        out_specs=[pl.BlockSpec((1, rows, wdt), lambda g: (g, 0, 0)),
                   pl.BlockSpec((1, 2, 2, 8, 2 * S5_STATE), lambda g: (g, 0, 0, 0, 0))],
        out_shape=[jax.ShapeDtypeStruct((gp_n, rows, wdt), F32),
                   jax.ShapeDtypeStruct((gp_n, 2, 2, 8, 2 * S5_STATE), F32)],
        scratch_shapes=[pltpu.VMEM((4, rows, 2 * S5_STATE), F32), pltpu.VMEM((4, rows, 2 * S5_STATE), F32)],
        compiler_params=_cparams(("parallel",)),
        name="s5_core",
    )(xs, mpair, wpair, vpair, pw, h0)


def _s5_next_h0(fin):
    g, d, r, _, n = fin.shape
    sw = fin.reshape(g, d, r, 4, 2, n)[:, :, :, :, ::-1, :]
    keep = jnp.array([[1.0, 0.0], [0.0, 1.0]], F32)[None, :, None, None, :, None]
    return (sw * keep).reshape(fin.shape)


def _s5_glu_kernel(y_ref, u_ref, d_ref, w_ref, b_ref, o_ref, nat_ref, *, rows16):
    _s5_from_chunk_rows(y_ref, nat_ref, rows16)
    y = jnp.concatenate([nat_ref[q] for q in range(BR // LANE)], axis=1) + u_ref[...].astype(F32) * d_ref[...]
    g = jax.nn.gelu(y)
    z = jnp.dot(g.astype(BF16), w_ref[...], preferred_element_type=F32) + b_ref[...]
    o_ref[...] = g * jax.nn.sigmoid(z)


def _s5_glu(ys, p, d_skip, glu_w, glu_b, tm=512):
    m = p.shape[0]
    tm = min(tm, m)
    rows16 = tm // S5_T
    return pl.pallas_call(
        functools.partial(_s5_glu_kernel, rows16=rows16),
        grid=(m // tm,),
        in_specs=[pl.BlockSpec((_S5_PAIRS, rows16, _S5_ROW), lambda i: (0, i, 0)),
                  pl.BlockSpec((tm, BR), lambda i: (i, COL_S5_U * LANE // BR)),
                  pl.BlockSpec((1, BR), lambda i: (0, 0)),
                  pl.BlockSpec((BR, BR), lambda i: (0, 0)),
                  pl.BlockSpec((1, BR), lambda i: (0, 0))],
        out_specs=pl.BlockSpec((tm, BR), lambda i: (i, 0)),
        out_shape=jax.ShapeDtypeStruct((m, BR), F32),
        scratch_shapes=[pltpu.VMEM((BR // LANE, tm, LANE), F32)],
        compiler_params=_cparams(("parallel",)),
        name="s5_glu",
    )(ys, p, d_skip.reshape(1, BR), glu_w.astype(BF16), glu_b.reshape(1, BR))


@functools.lru_cache(maxsize=None)
def _fft_tables_np(n_total, n2_len):
    n1_len = n_total // n2_len
    n1h = n1_len // 2
    k1 = np.arange(n1_len, dtype=np.int64)
    n2 = np.arange(n2_len, dtype=np.int64)
    m = (k1[None, :, None] * k1[None, None, :] * n2_len + n2[:, None, None] * k1[None, :, None]) % n_total
    ang = m.astype(np.float64) * (2.0 * math.pi / n_total)
    c, s = np.cos(ang), np.sin(ang)
    ch, sh = c[:, :, :n1h], s[:, :, :n1h]
    gf = np.concatenate([np.concatenate([ch, sh], 2), np.concatenate([-sh, ch], 2)], 1)
    gk = np.concatenate([ch, -sh], 1)
    cht, sht = ch.transpose(0, 2, 1), sh.transpose(0, 2, 1)
    gi = np.concatenate([np.concatenate([cht, -sht], 2), np.concatenate([sht, cht], 2)], 1)
    a2 = ((n2[:, None] * n2[None, :]) % n2_len).astype(np.float64) * (2.0 * math.pi / n2_len)
    c2, s2 = np.cos(a2), np.sin(a2)
    f2 = np.concatenate([np.concatenate([c2, s2], 1), np.concatenate([-s2, c2], 1)], 0)
    f2c = np.concatenate([np.concatenate([c2, -s2], 1), np.concatenate([s2, c2], 1)], 0)
    return tuple(np.asarray(t, np.float32) for t in (gf, gk, gi, f2, f2c))


def _fft_tables(n_total, n2_len):
    return tuple(jnp.asarray(t).astype(BF16) for t in _fft_tables_np(n_total, n2_len))


def _hy_shortconv_kernel(x_ref, w_ref, b_ref, o_ref):
    x = x_ref[0].astype(F32)
    l = x.shape[0]
    row = lax.broadcasted_iota(jnp.int32, x.shape, 0)
    prev = jnp.where(row == 0, 0.0, pltpu.roll(x, 1, axis=0))
    nxt = jnp.where(row == l - 1, 0.0, pltpu.roll(x, l - 1, axis=0))
    o_ref[0] = w_ref[0:1, :] * prev + w_ref[1:2, :] * x + w_ref[2:3, :] * nxt + b_ref[...]


def _hy_shortconv(p3, conv_w, conv_b):
    bsz, l, _ = p3.shape
    units = 3 * BR // LANE
    return pl.pallas_call(
        _hy_shortconv_kernel,
        grid=(bsz, units),
        in_specs=[pl.BlockSpec((1, l, LANE), lambda b, u: (b, 0, COL_HY_X + u)),
                  pl.BlockSpec((3, LANE), lambda b, u: (0, u)),
                  pl.BlockSpec((1, LANE), lambda b, u: (0, u))],
        out_specs=pl.BlockSpec((1, l, LANE), lambda b, u: (b, 0, u)),
        out_shape=jax.ShapeDtypeStruct((bsz, l, 3 * BR), F32),
        compiler_params=_cparams(("parallel", "parallel")),
        name="hy_shortconv",
    )(p3, conv_w, conv_b.reshape(1, 3 * BR))


def _fft_stage2(mid_ref, f2_ref, n1_len, n2_len, emit):
    def body(kp, _):
        def rhs(k1):
            return jnp.concatenate([mid_ref[pl.ds(k1, n2_len, stride=2 * n1_len), :],
                                    mid_ref[pl.ds(n1_len + k1, n2_len, stride=2 * n1_len), :]], axis=0)
        r = jnp.concatenate([rhs(2 * kp), rhs(2 * kp + 1)], axis=1).astype(BF16)
        xs = jnp.dot(f2_ref[...], r, preferred_element_type=F32)
        emit(2 * kp, xs[:, :LANE])
        emit(2 * kp + 1, xs[:, LANE:])
        return 0
    lax.fori_loop(0, n1_len // 2, body, 0, unroll=_FFT_UNROLL)


def _hy_kf_kernel(fwd_ref, bwd_ref, nrm_ref, gk_ref, f2_ref, o_ref, mid_ref, *, n1_len, n2_len):
    n1h = n1_len // 2
    row = lax.broadcasted_iota(jnp.int32, (n1h, LANE), 0)

    def stage1(n2, _):
        f = fwd_ref[pl.ds(n2, n1h, stride=n2_len), :]
        b = jnp.where((row == 0) & (n2 == 0), 0.0, bwd_ref[pl.ds(n2, n1h, stride=n2_len), :])
        a = jnp.dot(gk_ref[n2], jnp.concatenate([f, b], axis=1).astype(BF16), preferred_element_type=F32)
        r0 = pl.multiple_of(n2 * 2 * n1_len, 2 * n1_len)
        mid_ref[0, pl.ds(r0, 2 * n1_len), :] = a[:, :LANE]
        mid_ref[1, pl.ds(r0, 2 * n1_len), :] = a[:, LANE:]
        return 0
    lax.fori_loop(0, n2_len, stage1, 0, unroll=_FFT_UNROLL)
    scale = nrm_ref[...] * (1.0 / (n1_len * n2_len))

    def stage2(k1, _):
        def rhs(j):
            return jnp.concatenate([mid_ref[j, pl.ds(k1, n2_len, stride=2 * n1_len), :],
                                    mid_ref[j, pl.ds(n1_len + k1, n2_len, stride=2 * n1_len), :]], axis=0)
        r = jnp.concatenate([rhs(0), rhs(1)], axis=1).astype(BF16)
        xs = jnp.dot(f2_ref[...], r, preferred_element_type=F32)
        f, b = xs[:, :LANE], xs[:, LANE:]
        o_ref[0, k1] = (jnp.concatenate([f[:n2_len] + b[:n2_len], f[n2_len:] - b[n2_len:]], axis=0)
                        * scale).astype(BF16)
        return 0
    lax.fori_loop(0, n1_len, stage2, 0, unroll=_FFT_UNROLL)


def _hy_kf(h2d, nrm, gk, f2, n2_len):
    l, cols = h2d.shape
    n1_len = 2 * l // n2_len
    tiles = cols // (2 * LANE)
    return pl.pallas_call(
        functools.partial(_hy_kf_kernel, n1_len=n1_len, n2_len=n2_len),
        grid=(tiles,),
        in_specs=[pl.BlockSpec((l, LANE), lambda t: (0, t)),
                  pl.BlockSpec((l, LANE), lambda t: (0, tiles + t)),
                  pl.BlockSpec((1, LANE), lambda t: (0, t)),
                  pl.BlockSpec(gk.shape, lambda t: (0, 0, 0)),
                  pl.BlockSpec(f2.shape, lambda t: (0, 0))],
        out_specs=pl.BlockSpec((1, n1_len, 2 * n2_len, LANE), lambda t: (t, 0, 0, 0)),
        out_shape=jax.ShapeDtypeStruct((tiles, n1_len, 2 * n2_len, LANE), BF16),
        scratch_shapes=[pltpu.VMEM((2, 4 * l, LANE), F32)],
        compiler_params=_cparams(("parallel",)),
        name="hy_kf",
    )(h2d, h2d, nrm, gk, f2)


def _hy_conv_kernel(z_ref, g_ref, kf_ref, bias_ref, gf_ref, gi_ref, f2_ref, f2c_ref, o_ref,
                    mid_ref, spec_ref, *, n1_len, n2_len, order):
    n1h = n1_len // 2

    def stage1(n2, _):
        rhs = jnp.concatenate([z_ref[0, pl.ds(n2, n1h, stride=n2_len), :],
                               z_ref[1, pl.ds(n2, n1h, stride=n2_len), :]], axis=0).astype(BF16)
        r0 = pl.multiple_of(n2 * 2 * n1_len, 2 * n1_len)
        mid_ref[pl.ds(r0, 2 * n1_len), :] = jnp.dot(gf_ref[n2], rhs, preferred_element_type=F32)
        return 0
    lax.fori_loop(0, n2_len, stage1, 0, unroll=_FFT_UNROLL)

    def emit(k1, xs):
        kf = kf_ref[0, k1].astype(F32)
        xr, xi = xs[:n2_len], xs[n2_len:]
        kr, ki = kf[:n2_len], kf[n2_len:]
        spec_ref[k1] = jnp.concatenate([xr * kr - xi * ki, xr * ki + xi * kr], axis=0).astype(BF16)
    _fft_stage2(mid_ref, f2_ref, n1_len, n2_len, emit)

    def stage2i(kp, _):
        r = jnp.concatenate([spec_ref[2 * kp], spec_ref[2 * kp + 1]], axis=1)
        b = jnp.dot(f2c_ref[...], r, preferred_element_type=F32)
        r0 = pl.multiple_of(kp * 4 * n2_len, 4 * n2_len)
        mid_ref[pl.ds(r0, 2 * n2_len), :] = b[:, :LANE]
        mid_ref[pl.ds(r0 + 2 * n2_len, 2 * n2_len), :] = b[:, LANE:]
        return 0
    lax.fori_loop(0, n1_len // 2, stage2i, 0, unroll=_FFT_UNROLL)

    def stage1i(n2, _):
        rhs = jnp.concatenate([mid_ref[pl.ds(n2, n1_len, stride=2 * n2_len), :],
                               mid_ref[pl.ds(n2_len + n2, n1_len, stride=2 * n2_len), :]], axis=0).astype(BF16)
        o = jnp.dot(gi_ref[n2], rhs, preferred_element_type=F32)
        o_ref[0, pl.ds(n2, n1h, stride=n2_len), :] = o[:n1h]
        o_ref[1, pl.ds(n2, n1h, stride=n2_len), :] = o[n1h:]
        return 0
    lax.fori_loop(0, n2_len, stage1i, 0, unroll=_FFT_UNROLL)

    bias = bias_ref[order:order + 1, :]
    for j in range(2):
        o_ref[j] = g_ref[j] * (o_ref[j] + bias * z_ref[j])


def _hy_conv(z, zcol, g, gcol, kf, bias, tabs, n2_len, order):
    gf, _, gi, f2, f2c = tabs
    bsz, l, _ = z.shape
    n1_len = 2 * l // n2_len
    tiles = BR // LANE
    const2 = lambda *_: (0, 0)
    const3 = lambda *_: (0, 0, 0)
    one = pl.Buffered(1)
    return pl.pallas_call(
        functools.partial(_hy_conv_kernel, n1_len=n1_len, n2_len=n2_len, order=order),
        grid=(tiles, bsz // 2),
        in_specs=[pl.BlockSpec((2, l, LANE), lambda t, b: (b, 0, zcol + t)),
                  pl.BlockSpec((2, l, LANE), lambda t, b: (b, 0, gcol + t)),
                  pl.BlockSpec((1, n1_len, 2 * n2_len, LANE), lambda t, b: (order * tiles + t, 0, 0, 0),
                               pipeline_mode=one),
                  pl.BlockSpec((HY_ORDER, LANE), lambda t, b: (0, t)),
                  pl.BlockSpec(gf.shape, const3, pipeline_mode=one),
                  pl.BlockSpec(gi.shape, const3, pipeline_mode=one),
                  pl.BlockSpec(f2.shape, const2, pipeline_mode=one),
                  pl.BlockSpec(f2c.shape, const2, pipeline_mode=one)],
        out_specs=pl.BlockSpec((2, l, LANE), lambda t, b: (b, 0, t)),
        out_shape=jax.ShapeDtypeStruct((bsz, l, BR), F32),
        scratch_shapes=[pltpu.VMEM((4 * l, LANE), F32),
                        pltpu.VMEM((n1_len, 2 * n2_len, LANE), BF16)],
        compiler_params=_cparams(("parallel", "arbitrary")),
        name="hy_conv%d" % order,
    )(z, g, kf, bias, gf, gi, f2, f2c)


def _hyena_filters(l, w1, b1, f1, w2, b2, f2, w3, b3, f3, w4):
    hp = lax.Precision.HIGHEST
    t = jnp.linspace(0.0, 1.0, l, dtype=F32)[:, None]
    wpos = 2.0 * math.pi * jnp.arange(l, dtype=F32)[:, None] / l
    f = jnp.linspace(1e-4, HY_BANDS - 1, HY_BANDS, dtype=F32)[None, :]
    z = jnp.concatenate([t, jnp.cos(f * wpos), -jnp.sin(f * wpos)], axis=-1)
    h = jnp.sin(f1 * (jnp.dot(z, w1, precision=hp) + b1))
    h = jnp.sin(f2 * (jnp.dot(h, w2, precision=hp) + b2))
    h = jnp.sin(f3 * (jnp.dot(h, w3, precision=hp) + b3))
    h = jnp.dot(h, w4, precision=hp).reshape(l, 2, HY_ORDER, BR)
    deltas = jnp.linspace(math.log(HY_TARGET) / HY_SLOW_DECAY,
                          math.log(HY_TARGET) / HY_FAST_DECAY, BR, dtype=F32)
    h = h * jnp.exp(-t * jnp.abs(deltas))[:, None, None, :]
    ss = jnp.sum(h[:, 0] * h[:, 0], axis=0) + jnp.sum(h[1:, 1] * h[1:, 1], axis=0)
    return h.reshape(l, 2 * HY_ORDER * BR), lax.rsqrt(ss + EPS).reshape(1, HY_ORDER * BR)


def _hyena(p3, conv_w, conv_b, filters, bias, n2_len):
    bsz, l, _ = p3.shape
    tabs = _fft_tables(2 * l, n2_len)
    xs = _hy_shortconv(p3, conv_w, conv_b)
    h2d, nrm = filters
    kf = _hy_kf(h2d, nrm, tabs[1], tabs[3], n2_len)
    tiles = BR // LANE
    z1 = _hy_conv(xs, 0, xs, tiles, kf, bias, tabs, n2_len, 0)
    return _hy_conv(z1, 0, xs, 2 * tiles, kf, bias, tabs, n2_len, 1)


def _ret_kernel(q_ref, k_ref, v_ref, cos_ref, sin_ref, lg_ref, nw_ref, s0_ref, o_ref, sfin_ref,
                vb_ref, qf_ref, qb_ref, kf_ref, kb_ref, *, nchunks):
    c = ATT_CHUNK
    lg = lg_ref[0]
    jrow = lax.broadcasted_iota(jnp.int32, (c, c), 0)
    icol = lax.broadcasted_iota(jnp.int32, (c, c), 1)
    jf = jrow.astype(F32)
    dsym = jnp.exp(jnp.abs(jrow - icol).astype(F32) * lg) * jnp.where(jrow == icol, 2.0, 1.0)
    dq_f = jnp.exp((jf + 1.0) * lg)
    dq_b = jnp.exp((c - jf) * lg)
    dk_f = jnp.exp((c - 1.0 - jf) * lg)
    dk_b = jnp.exp(jf * lg)
    dchunk = jnp.exp(c * lg)

    def rope(x, r0):
        return x * cos_ref[pl.ds(r0, c), :] + pltpu.roll(x, RET_DK // 2, axis=1) * sin_ref[pl.ds(r0, c), :]

    def intra_step(n, _):
        r0 = pl.multiple_of(n * c, c)
        rows = pl.ds(r0, c)
        q = rope(q_ref[0, rows, :].astype(F32), r0)
        k = rope(k_ref[0, rows, :].astype(F32), r0) * (RET_DK ** -0.5)
        v = v_ref[0, rows, :].astype(BF16)
        att = _nt(q.astype(BF16), k.astype(BF16)) * dsym
        o_ref[0, rows, :] = jnp.dot(att.astype(BF16), v, preferred_element_type=F32)
        vb_ref[rows, :] = v
        qf_ref[rows, :] = (q * dq_f).astype(BF16)
        qb_ref[rows, :] = (q * dq_b).astype(BF16)
        kf_ref[rows, :] = (k * dk_f).astype(BF16)
        kb_ref[rows, :] = (k * dk_b).astype(BF16)
        return 0

    lax.fori_loop(0, nchunks, intra_step, 0, unroll=_ATT_UNROLL)

    def sweep(lo, hi, carry, finish):
        def step(i, carry):
            s_f, s_b = carry
            rf = pl.ds(pl.multiple_of(i * c, c), c)
            rb = pl.ds(pl.multiple_of((nchunks - 1 - i) * c, c), c)
            o_f = o_ref[0, rf, :] + jnp.dot(qf_ref[rf, :], s_f.astype(BF16), preferred_element_type=F32)
            o_b = o_ref[0, rb, :] + jnp.dot(qb_ref[rb, :], s_b.astype(BF16), preferred_element_type=F32)
            if finish:
                o_f = o_f * lax.rsqrt(jnp.mean(o_f * o_f, axis=-1, keepdims=True) + EPS) * nw_ref[...]
                o_b = o_b * lax.rsqrt(jnp.mean(o_b * o_b, axis=-1, keepdims=True) + EPS) * nw_ref[...]
            o_ref[0, rf, :] = o_f
            o_ref[0, rb, :] = o_b
            return (dchunk * s_f + _tn(kf_ref[rf, :], vb_ref[rf, :]),
                    dchunk * s_b + _tn(kb_ref[rb, :], vb_ref[rb, :]))
        return lax.fori_loop(lo, hi, step, carry)

    carry = sweep(0, nchunks // 2, (s0_ref[0, 0, 0], s0_ref[0, 0, 1]), False)
    s_f, s_b = sweep(nchunks // 2, nchunks, carry, True)
    sfin_ref[0, 0, 0] = s_f
    sfin_ref[0, 0, 1] = s_b


def _retention(p3, cos2, sin2, norm_w, s0):
    bsz, l, _ = p3.shape
    lgs = jnp.log(1.0 - 2.0 ** (-5.0 - jnp.arange(RET_HEADS, dtype=F32)))
    lgs = jnp.broadcast_to(lgs[:, None, None], (RET_HEADS, 1, LANE))
    tab = pl.BlockSpec((l, LANE), lambda b, h: (0, 0))
    st = pl.BlockSpec((1, 1, 2, RET_DK, LANE), lambda b, h: (b, h, 0, 0, 0))
    return pl.pallas_call(
        functools.partial(_ret_kernel, nchunks=l // ATT_CHUNK),
        grid=(bsz, RET_HEADS),
        in_specs=[pl.BlockSpec((1, l, LANE), lambda b, h: (b, 0, COL_RET_Q + h)),
                  pl.BlockSpec((1, l, LANE), lambda b, h: (b, 0, COL_RET_K + h)),
                  pl.BlockSpec((1, l, LANE), lambda b, h: (b, 0, COL_RET_V + h)),
                  tab, tab,
                  pl.BlockSpec((1, 1, LANE), lambda b, h: (h, 0, 0)),
                  pl.BlockSpec((1, LANE), lambda b, h: (0, h)),
                  st],
        out_specs=[pl.BlockSpec((1, l, LANE), lambda b, h: (b, 0, h)), st],
        out_shape=[jax.ShapeDtypeStruct((bsz, l, BR), F32),
                   jax.ShapeDtypeStruct((bsz, RET_HEADS, 2, RET_DK, LANE), F32)],
        scratch_shapes=[pltpu.VMEM((l, LANE), BF16)] * 5,
        compiler_params=_cparams(("parallel", "parallel")),
        name="retention",
    )(p3, p3, p3, cos2, sin2, lgs, norm_w.reshape(1, BR), s0)


def _rope_tables(ang):
    cos, sin = jnp.cos(ang), jnp.sin(ang)
    return jnp.concatenate([cos, cos], -1), jnp.concatenate([-sin, sin], -1)


def _latent_angles(rows):
    half = RET_DK // 4
    inv = ROPE_BASE ** (-jnp.arange(half, dtype=F32) / half)
    r = jnp.repeat(jnp.arange(rows, dtype=F32), GRID_W)
    cl = jnp.tile(jnp.arange(GRID_W, dtype=F32), rows)
    return jnp.concatenate([r[:, None] * inv, cl[:, None] * inv], axis=-1)


def _ctx_angles(n_ctx):
    n = RET_DK // 2
    inv = ROPE_BASE ** (-jnp.arange(n, dtype=F32) / n)
    return jnp.arange(n_ctx, dtype=F32)[:, None] * inv


def _gla_kernel(q_ref, k_ref, v_ref, lr_ref, gw_ref, gb_ref, nw_ref, s0_ref, o_ref, sfin_ref,
                vb_ref, qf_ref, qb_ref, kf_ref, kb_ref, df_ref, db_ref, *, nchunks):
    c = ATT_CHUNK
    jrow = lax.broadcasted_iota(jnp.int32, (c, c), 0)
    icol = lax.broadcasted_iota(jnp.int32, (c, c), 1)
    tri_f = jnp.where(icol <= jrow, 1.0, 0.0).astype(BF16)
    tri_b = jnp.where(icol >= jrow, 1.0, 0.0).astype(BF16)
    lane = lax.broadcasted_iota(jnp.int32, (1, LANE), 1)
    head0 = lane < GLA_DK

    def decays(r0, d):
        x = jnp.dot(lr_ref[0, pl.ds(r0, c), :].astype(BF16), gw_ref[d], preferred_element_type=F32) + gb_ref[d]
        la = (jnp.minimum(x, 0.0) - jnp.log(1.0 + jnp.exp(-jnp.abs(x)))) * (1.0 / GLA_TAU)
        hi = la.astype(BF16)
        lo = (la - hi.astype(F32)).astype(BF16)
        cs = jnp.dot(tri_f if d == 0 else tri_b, jnp.concatenate([hi, lo], axis=1), preferred_element_type=F32)
        return cs[:, :LANE] + cs[:, LANE:]

    def intra_step(n, _):
        r0 = pl.multiple_of(n * c, c)
        rows = pl.ds(r0, c)
        q = q_ref[0, rows, :].astype(F32) * (GLA_DK ** -0.5)
        k = k_ref[0, rows, :].astype(F32)
        bf = decays(r0, 0)
        bb = decays(r0, 1)
        ref_f = bf[c // 2 - 1:c // 2, :]
        ref_b = bb[c // 2:c // 2 + 1, :]
        qf = q * jnp.exp(bf - ref_f)
        kf = (k * jnp.exp(ref_f - bf)).astype(BF16)
        qb = q * jnp.exp(bb - ref_b)
        kb = (k * jnp.exp(ref_b - bb)).astype(BF16)
        for h in range(2):
            hm = head0 if h == 0 else jnp.logical_not(head0)
            v = v_ref[0, rows, h * GLA_DV:(h + 1) * GLA_DV].astype(BF16)
            att = (jnp.where(icol <= jrow, _nt(jnp.where(hm, qf, 0.0).astype(BF16), kf), 0.0)
                   + jnp.where(icol >= jrow, _nt(jnp.where(hm, qb, 0.0).astype(BF16), kb), 0.0))
            o_ref[0, rows, h * GLA_DV:(h + 1) * GLA_DV] = jnp.dot(att.astype(BF16), v, preferred_element_type=F32)
            vb_ref[rows, h * GLA_DV:(h + 1) * GLA_DV] = v
        bf_last, bb_last = bf[c - 1:c, :], bb[0:1, :]
        qf_ref[rows, :] = (q * jnp.exp(bf)).astype(BF16)
        qb_ref[rows, :] = (q * jnp.exp(bb)).astype(BF16)
        kf_ref[rows, :] = (k * jnp.exp(bf_last - bf)).astype(BF16)
        kb_ref[rows, :] = (k * jnp.exp(bb_last - bb)).astype(BF16)
        d0 = pl.multiple_of(n * 8, 8)
        df_ref[pl.ds(d0, 8), :] = jnp.broadcast_to(jnp.exp(bf_last), (8, LANE))
        db_ref[pl.ds(d0, 8), :] = jnp.broadcast_to(jnp.exp(bb_last), (8, LANE))
        return 0

    lax.fori_loop(0, nchunks, intra_step, 0, unroll=_ATT_UNROLL)

    def sweep(lo, hi, carry, finish):
        def step(i, carry):
            sts = list(carry)
            new = []
            for d in range(2):
                n = i if d == 0 else nchunks - 1 - i
                rows = pl.ds(pl.multiple_of(n * c, c), c)
                qs = (qf_ref if d == 0 else qb_ref)[rows, :]
                kd = (kf_ref if d == 0 else kb_ref)[rows, :]
                dec = (df_ref if d == 0 else db_ref)[pl.ds(pl.multiple_of(n * 8, 8), 1), :]
                for h in range(2):
                    hm = head0 if h == 0 else jnp.logical_not(head0)
                    cols = slice(h * GLA_DV, (h + 1) * GLA_DV)
                    st = sts[2 * d + h]
                    o = o_ref[0, rows, cols] + _nt(qs, st.astype(BF16))
                    if finish:
                        o = o * lax.rsqrt(jnp.mean(o * o, axis=-1, keepdims=True) + EPS) * nw_ref[:, cols]
                    o_ref[0, rows, cols] = o
                    new.append(jnp.where(hm, dec * st + _tn(vb_ref[rows, cols], kd), 0.0))
            return tuple(new)
        return lax.fori_loop(lo, hi, step, carry)

    carry = (s0_ref[0, 0, 0, 0], s0_ref[0, 0, 0, 1], s0_ref[0, 0, 1, 0], s0_ref[0, 0, 1, 1])
    carry = sweep(0, nchunks // 2, carry, False)
    s_fin = sweep(nchunks // 2, nchunks, carry, True)
    sfin_ref[0, 0, 0, 0] = s_fin[0]
    sfin_ref[0, 0, 0, 1] = s_fin[1]
    sfin_ref[0, 0, 1, 0] = s_fin[2]
    sfin_ref[0, 0, 1, 1] = s_fin[3]


def _gla(p3, gate_w, gate_b, norm_w, s0):
    bsz, l, _ = p3.shape
    pairs = GLA_HEADS // 2
    gw = jnp.zeros((2, LANE, GLA_HEADS * GLA_DK), F32)
    for d in range(2):
        gw = gw.at[d, d * GLA_LR:(d + 1) * GLA_LR, :].set(gate_w[d].astype(F32))
    st = pl.BlockSpec((1, 1, 2, 2, GLA_DV, LANE), lambda b, h: (b, h, 0, 0, 0, 0))
    return pl.pallas_call(
        functools.partial(_gla_kernel, nchunks=l // ATT_CHUNK),
        grid=(bsz, pairs),
        in_specs=[pl.BlockSpec((1, l, LANE), lambda b, h: (b, 0, COL_GLA_Q + h)),
                  pl.BlockSpec((1, l, LANE), lambda b, h: (b, 0, COL_GLA_K + h)),
                  pl.BlockSpec((1, l, 2 * GLA_DV), lambda b, h: (b, 0, COL_GLA_V // 2 + h)),
                  pl.BlockSpec((1, l, LANE), lambda b, h: (b, 0, COL_GLA_LR)),
                  pl.BlockSpec((2, LANE, LANE), lambda b, h: (0, 0, h)),
                  pl.BlockSpec((2, 1, LANE), lambda b, h: (0, 0, h)),
                  pl.BlockSpec((1, 2 * GLA_DV), lambda b, h: (0, h)),
                  st],
        out_specs=[pl.BlockSpec((1, l, 2 * GLA_DV), lambda b, h: (b, 0, h)), st],
        out_shape=[jax.ShapeDtypeStruct((bsz, l, BR), F32),
                   jax.ShapeDtypeStruct((bsz, pairs, 2, 2, GLA_DV, LANE), F32)],
        scratch_shapes=([pltpu.VMEM((l, 2 * GLA_DV), BF16)] + [pltpu.VMEM((l, LANE), BF16)] * 4
                        + [pltpu.VMEM((8 * l // ATT_CHUNK, LANE), F32)] * 2),
        compiler_params=_cparams(("parallel", "parallel")),
        name="gla",
    )(p3, p3, p3, p3, gw.astype(BF16), gate_b.astype(F32).reshape(2, 1, GLA_HEADS * GLA_DK),
      norm_w.reshape(1, BR), s0)


def _mixers(p2, xs, bsz, prm, i, s5_tabs, hy_filters, rope, s5_h0, ret_s0, gla_s0, with_hyena):
    m = p2.shape[0]
    l = m // bsz
    p3 = p2.reshape(bsz, l, NP)
    tabs = s5_tabs[:3] + (s5_tabs[3][:, :, :, :l // (2 * S5_T) + 1],)
    y, s5_fin = _s5_core(xs, tabs, s5_h0)
    s5 = _s5_glu(y, p2, prm['s5_d'][i], prm['s5_glu_w'][i], prm['s5_glu_b'][i])
    hy = None
    if with_hyena:
        hy = _hyena(p3, prm['hy_conv_w'][i], prm['hy_conv_b'][i], hy_filters, prm['hy_bias'][i].astype(F32),
                    64 if l >= 512 else 16).reshape(m, BR)
    ret, ret_fin = _retention(p3, rope[0], rope[1], prm['ret_norm_w'][i], ret_s0)
    gla, gla_fin = _gla(p3, prm['gla_gate_w'][i], prm['gla_gate_b'][i], prm['gla_norm_w'][i], gla_s0)
    return (s5, hy, ret.reshape(m, BR), gla.reshape(m, BR)), (s5_fin, ret_fin, gla_fin)


def kernel(x, c, ctx, c_ctx, norm_w, ada_w, ada_b, w_in, w_out, s5_a_re, s5_a_im, s5_log_dt, s5_b_re, s5_b_im, s5_c_re, s5_c_im, s5_d, s5_glu_w, s5_glu_b, hy_conv_w, hy_conv_b, hy_w1, hy_b1, hy_f1, hy_w2, hy_b2, hy_f2, hy_w3, hy_b3, hy_f3, hy_w4, hy_bias, ret_norm_w, gla_gate_w, gla_gate_b, gla_norm_w, final_norm_w):
    prm = dict(s5_a_re=s5_a_re, s5_a_im=s5_a_im, s5_log_dt=s5_log_dt, s5_b_re=s5_b_re, s5_b_im=s5_b_im,
               s5_c_re=s5_c_re, s5_c_im=s5_c_im, s5_d=s5_d, s5_glu_w=s5_glu_w, s5_glu_b=s5_glu_b,
               hy_conv_w=hy_conv_w, hy_conv_b=hy_conv_b, hy_w1=hy_w1, hy_b1=hy_b1, hy_f1=hy_f1,
               hy_w2=hy_w2, hy_b2=hy_b2, hy_f2=hy_f2, hy_w3=hy_w3, hy_b3=hy_b3, hy_f3=hy_f3,
               hy_w4=hy_w4, hy_bias=hy_bias, ret_norm_w=ret_norm_w, gla_gate_w=gla_gate_w,
               gla_gate_b=gla_gate_b, gla_norm_w=gla_norm_w)
    bsz, l, d = x.shape
    lc = ctx.shape[1]
    depth = w_in.shape[0]
    cc = jnp.concatenate([c, c_ctx[None, :], jnp.zeros((8 - bsz - 1, d), F32)], axis=0)
    mod = _ada(cc, ada_w, ada_b)
    rope_l = _rope_tables(_latent_angles(l // GRID_W))
    rope_c = _rope_tables(_ctx_angles(lc))
    s5_zero = jnp.zeros((S5_GROUPS // 2, 2, 2, 8, 2 * S5_STATE), F32)
    ret_zero = jnp.zeros((bsz, RET_HEADS, 2, RET_DK, LANE), F32)
    gla_zero = jnp.zeros((bsz, GLA_HEADS // 2, 2, 2, GLA_DV, LANE), F32)

    w_all = _wprep(w_in)
    wo_all = w_out.astype(BF16)
    filt = (hy_w1, hy_b1, hy_f1, hy_w2, hy_b2, hy_f2, hy_w3, hy_b3, hy_f3, hy_w4)

    x_l = x.reshape(bsz * l, d)
    x_c = ctx.reshape(bsz * lc, d)
    for i in range(depth):
        last = i == depth - 1
        sh, sc, gt = mod[i, :, :d], mod[i, :, d:2 * d], mod[i, :, 2 * d:]
        p_c, xs_c = _inproj(x_c, norm_w[i], sc[bsz:bsz + 1, None, :], sh[bsz:bsz + 1, None, :], w_all, i,
                            bsz * lc)
        p_l, xs_l = _inproj(x_l, norm_w[i], sc[:bsz, None, :], sh[:bsz, None, :], w_all, i, l)
        s5_tabs = _s5_tables(s5_a_re[i], s5_a_im[i], s5_log_dt[i], s5_b_re[i], s5_b_im[i], s5_c_re[i],
                             s5_c_im[i], l // (2 * S5_T))
        filt_i = [w[i].astype(F32) for w in filt]
        ys_c, (s5_fin, ret_fin, gla_fin) = _mixers(
            p_c, xs_c, bsz, prm, i, s5_tabs, None if last else _hyena_filters(lc, *filt_i), rope_c, s5_zero,
            ret_zero, gla_zero, with_hyena=not last)
        ys_l, _ = _mixers(p_l, xs_l, bsz, prm, i, s5_tabs, _hyena_filters(l, *filt_i), rope_l,
                          _s5_next_h0(s5_fin), ret_fin, gla_fin, with_hyena=True)
        x_l = _outproj(x_l, ys_l, p_l, wo_all, i, gt[:bsz, None, :], final_norm_w, l, final=last)
        if not last:
            x_c = _outproj(x_c, ys_c, p_c, wo_all, i, gt[bsz:bsz + 1, None, :], final_norm_w, bsz * lc,
                           final=False)
    return x_l.reshape(bsz, l, d)
```

```python
import functools
import math

import numpy as np
import jax
import jax.numpy as jnp
from jax import lax
from jax.experimental import pallas as pl
from jax.experimental.pallas import tpu as pltpu

F32 = jnp.float32
BF16 = jnp.bfloat16

EPS = 1e-6
GRID_W = 64
BR = 512
S5_GSIZE = 16
S5_GROUPS = BR // S5_GSIZE
S5_STATE = 64
S5_T = 16
HY_ORDER = 2
HY_EMB = 33
HY_BANDS = (HY_EMB - 1) // 2
HY_FAST_DECAY = 0.3
HY_SLOW_DECAY = 1.5
HY_TARGET = 1e-2
RET_HEADS = 4
RET_DK = 128
GLA_HEADS = 4
GLA_DK = 64
GLA_DV = 128
GLA_LR = 16
GLA_TAU = 16.0
ROPE_BASE = 10000.0
ATT_CHUNK = 128
RET_CHUNK = 256
_FFT_UNROLL = 8
_ATT_UNROLL = 2

LANE = 128
VMEM_LIMIT = 56 * 1024 * 1024

COL_S5_U = 0
COL_S5_G = 4
COL_HY_X = 8
COL_HY_G = 20
COL_RET_Q = 24
COL_RET_K = 28
COL_RET_V = 32
COL_RET_G = 36
COL_GLA_Q = 40
COL_GLA_K = 42
COL_GLA_V = 44
COL_GLA_G = 48
COL_GLA_LR = 52
NP_UNITS = 54
NP = NP_UNITS * LANE
W_TILE = 768
_ALIGNED = COL_GLA_G * LANE


def _cparams(sem):
    return pltpu.CompilerParams(dimension_semantics=sem, vmem_limit_bytes=VMEM_LIMIT)


def _silu(x):
    return x * jax.nn.sigmoid(x)


def _wprep(w_in):
    depth, d, in_w = w_in.shape
    assert _ALIGNED % W_TILE == 0 and in_w == _ALIGNED + 2 * GLA_LR + BR and NP == _ALIGNED + W_TILE
    wb = w_in.astype(BF16)
    tail = jnp.concatenate([wb[:, :, _ALIGNED + 2 * GLA_LR:], wb[:, :, _ALIGNED:_ALIGNED + 2 * GLA_LR],
                            jnp.zeros((depth, d, W_TILE - BR - 2 * GLA_LR), BF16)], axis=-1)
    return wb, tail


def _ada_kernel(c_ref, w_ref, b_ref, o_ref):
    a = _silu(c_ref[...]).astype(BF16)
    o_ref[0] = jnp.dot(a, w_ref[0].astype(BF16), preferred_element_type=F32) + b_ref[0]


def _ada(cc, ada_w, ada_b, tn=768):
    depth, d, n = ada_w.shape
    return pl.pallas_call(
        _ada_kernel,
        grid=(depth, n // tn),
        in_specs=[pl.BlockSpec((8, d), lambda i, j: (0, 0)),
                  pl.BlockSpec((1, d, tn), lambda i, j: (i, 0, j)),
                  pl.BlockSpec((1, 1, tn), lambda i, j: (i, 0, j))],
        out_specs=pl.BlockSpec((1, 8, tn), lambda i, j: (i, 0, j)),
        out_shape=jax.ShapeDtypeStruct((depth, 8, n), F32),
        compiler_params=_cparams(("parallel", "parallel")),
        name="ada_mod",
    )(cc, ada_w, ada_b.reshape(depth, 1, n))


_S5_SLOT = 2 * S5_GSIZE
_S5_PAIRS = S5_GROUPS // 2
_S5_ROW = S5_T * _S5_SLOT


def _s5_lane_perm():
    perm = np.zeros((4, _S5_ROW), np.int32)
    for s in range(4):
        for tau in range(S5_T):
            for c in range(_S5_SLOT):
                gl, i = divmod(c, S5_GSIZE)
                perm[s, LANE * (tau // 4) + _S5_SLOT * ((s + tau) % 4) + c] = (gl * S5_T + tau) * S5_GSIZE + i
    return perm


def _s5_to_chunk_rows(nat_ref, xs_ref, rows16):
    slot = lax.broadcasted_iota(jnp.int32, (rows16, LANE), 1) // _S5_SLOT
    for q in range(4):
        for qd in range(4):
            rolled = []
            for t in range(4):
                piece = nat_ref[q, pl.ds(4 * qd + t, rows16, stride=S5_T), :]
                rolled.append(pltpu.roll(piece, _S5_SLOT * t, axis=1) if t else piece)
            for s in range(4):
                acc = rolled[0]
                for t in range(1, 4):
                    acc = jnp.where(slot == (s + t) % 4, rolled[t], acc)
                xs_ref[4 * q + s, :, qd * LANE:(qd + 1) * LANE] = acc.astype(xs_ref.dtype)


def _s5_from_chunk_rows(y_ref, nat_ref, rows16):
    slot = lax.broadcasted_iota(jnp.int32, (rows16, LANE), 1) // _S5_SLOT
    for q in range(4):
        for qd in range(4):
            src = [y_ref[4 * q + s, :, qd * LANE:(qd + 1) * LANE] for s in range(4)]
            for t in range(4):
                acc = src[0]
                for s in range(1, 4):
                    acc = jnp.where(slot == (s + t) % 4, src[s], acc)
                if t:
                    acc = pltpu.roll(acc, LANE - _S5_SLOT * t, axis=1)
                nat_ref[q, pl.ds(4 * qd + t, rows16, stride=S5_T), :] = acc


def _inproj_kernel(x_ref, nw_ref, sc_ref, sh_ref, w_ref, tail_ref, o_ref, xs_ref, h_ref, nat_ref, *, rows16):
    j = pl.program_id(1)
    last = pl.num_programs(1) - 1

    def project(w):
        res = jnp.dot(h_ref[...], w, preferred_element_type=F32)
        o_ref[...] = res.astype(o_ref.dtype)
        return res

    @pl.when(j == 0)
    def _():
        x = x_ref[...]
        y = x * lax.rsqrt(jnp.mean(x * x, axis=-1, keepdims=True) + EPS) * nw_ref[...]
        h_ref[...] = (y * (1.0 + sc_ref[0]) + sh_ref[0]).astype(BF16)
        res = project(w_ref[...])
        for q in range(BR // LANE):
            nat_ref[q] = res[:, q * LANE:(q + 1) * LANE]
        _s5_to_chunk_rows(nat_ref, xs_ref, rows16)

    @pl.when((j > 0) & (j < last))
    def _():
        project(w_ref[...])

    @pl.when(j == last)
    def _():
        project(tail_ref[...])


def _inproj(x2, nw, sc, sh, weights, layer, rows_per_mod, tm=1024, tn=W_TILE):
    w_all, w_tail = weights
    m, d = x2.shape
    tm = min(tm, rows_per_mod)
    per = rows_per_mod // tm
    rows16 = tm // S5_T
    n_al = _ALIGNED // tn
    assert COL_S5_U == 0 and tn >= BR and tn == W_TILE
    return pl.pallas_call(
        functools.partial(_inproj_kernel, rows16=rows16),
        grid=(m // tm, NP // tn),
        in_specs=[pl.BlockSpec((tm, d), lambda i, j: (i, 0)),
                  pl.BlockSpec((1, d), lambda i, j: (0, 0)),
                  pl.BlockSpec((1, 1, d), lambda i, j: (i // per, 0, 0)),
                  pl.BlockSpec((1, 1, d), lambda i, j: (i // per, 0, 0)),
                  pl.BlockSpec((None, d, tn), lambda i, j: (layer, 0, jnp.minimum(j, n_al - 1))),
                  pl.BlockSpec((None, d, tn), lambda i, j: (layer, 0, 0))],
        out_specs=[pl.BlockSpec((tm, tn), lambda i, j: (i, j)),
                   pl.BlockSpec((_S5_PAIRS, rows16, _S5_ROW), lambda i, j: (0, i, 0))],
        out_shape=[jax.ShapeDtypeStruct((m, NP), BF16),
                   jax.ShapeDtypeStruct((_S5_PAIRS, m // S5_T, _S5_ROW), BF16)],
        scratch_shapes=[pltpu.VMEM((tm, d), BF16), pltpu.VMEM((BR // LANE, tm, LANE), F32)],
        compiler_params=_cparams(("parallel", "arbitrary")),
        name="inproj",
    )(x2, nw.reshape(1, d), sc, sh, w_all, w_tail)


def _outproj_kernel(x_ref, y0_ref, y1_ref, y2_ref, y3_ref, g0_ref, g1_ref, g2_ref, g3_ref, w_ref, gt_ref, fw_ref,
                    o_ref, *, final):
    acc = None
    for k, (y_ref, g_ref) in enumerate(((y0_ref, g0_ref), (y1_ref, g1_ref), (y2_ref, g2_ref), (y3_ref, g3_ref))):
        gated = (y_ref[...] * _silu(g_ref[...].astype(F32))).astype(BF16)
        part = jnp.dot(gated, w_ref[k * BR:(k + 1) * BR, :], preferred_element_type=F32)
        acc = part if acc is None else acc + part
    x = x_ref[...] + gt_ref[0] * acc
    if final:
        x = x * lax.rsqrt(jnp.mean(x * x, axis=-1, keepdims=True) + EPS) * fw_ref[...]
    o_ref[...] = x


def _outproj(x2, ys, p, wo_all, layer, gt, fw, rows_per_mod, final, tm=256):
    m, d = x2.shape
    tm = min(tm, rows_per_mod)
    per = rows_per_mod // tm
    yspec = pl.BlockSpec((tm, BR), lambda i: (i, 0))

    def gate(col_unit):
        blk = col_unit * LANE // BR
        return pl.BlockSpec((tm, BR), lambda i: (i, blk))

    return pl.pallas_call(
        functools.partial(_outproj_kernel, final=final),
        grid=(m // tm,),
        in_specs=[pl.BlockSpec((tm, d), lambda i: (i, 0)), yspec, yspec, yspec, yspec,
                  gate(COL_S5_G), gate(COL_HY_G), gate(COL_RET_G), gate(COL_GLA_G),
                  pl.BlockSpec((None, 4 * BR, d), lambda i: (layer, 0, 0)),
                  pl.BlockSpec((1, 1, d), lambda i: (i // per, 0, 0)),
                  pl.BlockSpec((1, d), lambda i: (0, 0))],
        out_specs=pl.BlockSpec((tm, d), lambda i: (i, 0)),
        out_shape=jax.ShapeDtypeStruct((m, d), F32),
        compiler_params=_cparams(("parallel",)),
        name="outproj",
    )(x2, *ys, p, p, p, p, wo_all, gt, fw.reshape(1, d))


def _s5_tables(a_re, a_im, log_dt, b_re, b_im, c_re, c_im, nch):
    hp = lax.Precision.HIGHEST
    t_len, g_n, p_n, s_n = S5_T, S5_GROUPS, S5_STATE, S5_GSIZE
    a_re, a_im = a_re.astype(F32), a_im.astype(F32)
    dt = jnp.exp(log_dt.astype(F32))[:, :, None]
    lam_re, lam_im = a_re * dt, a_im * dt

    def power(tau):
        tau = tau.astype(F32)[:, None, None, None]
        mag = jnp.exp(lam_re[None] * tau)
        return mag * jnp.cos(lam_im[None] * tau), mag * jnp.sin(lam_im[None] * tau)

    ab_re, ab_im = power(jnp.ones((1,)))
    ab_re, ab_im = ab_re[0], ab_im[0]
    den = a_re * a_re + a_im * a_im
    nr = ab_re - 1.0
    co_re = (nr * a_re + ab_im * a_im) / den
    co_im = (ab_im * a_re - nr * a_im) / den
    b_re, b_im = b_re.astype(F32), b_im.astype(F32)
    bco_re = co_re[..., None] * b_re - co_im[..., None] * b_im
    bco_im = co_re[..., None] * b_im + co_im[..., None] * b_re
    c_re, c_im = c_re.astype(F32), c_im.astype(F32)

    pr, pi = power(jnp.arange(t_len + 1))
    ca_re = c_re[None] * pr[:, :, :, None, :] - c_im[None] * pi[:, :, :, None, :]
    ca_im = c_re[None] * pi[:, :, :, None, :] + c_im[None] * pr[:, :, :, None, :]
    ca_cat = jnp.concatenate([ca_re[:t_len], -ca_im[:t_len]], axis=-1).transpose(1, 2, 0, 3, 4)
    bco_cat = jnp.concatenate([bco_re, bco_im], axis=-2)
    kk = jnp.einsum('dgmp,dgpj->dgmj', ca_cat.reshape(2, g_n, t_len * s_n, 2 * p_n), bco_cat, precision=hp)
    kk = kk.reshape(2, g_n, t_len, s_n, s_n).transpose(2, 0, 1, 3, 4)
    kf, kb = kk[:, 0], kk[:, 1]
    kfull = jnp.concatenate([kb[:0:-1], (kf[0] + kb[0])[None], kf[1:]], axis=0)
    kcat = kfull.transpose(1, 3, 0, 2).reshape(g_n, s_n, (2 * t_len - 1) * s_n)
    kcat = jnp.pad(kcat, ((0, 0), (0, 0), (0, _S5_ROW - kcat.shape[-1])))
    tt = jnp.arange(t_len)

    wexp = jnp.stack([pr[t_len - 1 - tt, 0], pr[tt, 1]], 0), jnp.stack([pi[t_len - 1 - tt, 0], pi[tt, 1]], 0)
    w_re = wexp[0][..., None] * bco_re[:, None] - wexp[1][..., None] * bco_im[:, None]
    w_im = wexp[0][..., None] * bco_im[:, None] + wexp[1][..., None] * bco_re[:, None]
    w4 = jnp.stack([w_re[0], w_im[0], w_re[1], w_im[1]], 0)
    wcat = w4.transpose(2, 1, 4, 0, 3).reshape(g_n, t_len * s_n, 4 * p_n)

    vexp_f, vexp_b = tt + 1, t_len - tt
    v4 = jnp.stack([ca_re[vexp_f, 0], -ca_im[vexp_f, 0], ca_re[vexp_b, 1], -ca_im[vexp_b, 1]], 0)
    vcat = v4.transpose(2, 0, 4, 1, 3).reshape(g_n, 4 * p_n, t_len * s_n)

    qr, qi = power(t_len * jnp.arange(nch + 1))
    pw = jnp.stack([qr, qi], 0)
    pw = pw.reshape(2, nch + 1, 2, g_n // 2, 2 * p_n).transpose(3, 2, 0, 1, 4)
    return _s5_assemble(kcat, wcat, vcat) + (pw,)


def _s5_assemble_kernel(kcat_ref, wcat_ref, vcat_ref, e_ref, oh_ref, m_ref, w_ref, v_ref):
    half = S5_T * S5_GSIZE
    oh = oh_ref[0]
    zeros = jnp.zeros((half, half), F32)
    diag = []
    for gl in range(2):
        kc = kcat_ref[gl]
        diag.append(jnp.concatenate(
            [kc[:, (S5_T - 1 - t) * S5_GSIZE:(S5_T - 1 - t) * S5_GSIZE + half] for t in range(S5_T)], axis=0))
    m_old = jnp.concatenate([jnp.concatenate([diag[0], zeros], axis=1),
                             jnp.concatenate([zeros, diag[1]], axis=1)], axis=0).astype(BF16)
    m_rows = jnp.dot(oh, m_old, preferred_element_type=F32).astype(BF16)
    m_ref[0] = _nt(m_rows, oh).astype(BF16)
    w_old = jnp.concatenate([jnp.dot(wcat_ref[gl].astype(BF16), e_ref[gl], preferred_element_type=F32)
                             for gl in range(2)], axis=0).astype(BF16)
    w_ref[0] = jnp.dot(oh, w_old, preferred_element_type=F32).astype(BF16)
    v_old = jnp.concatenate([_tn(e_ref[gl], vcat_ref[gl].astype(BF16)) for gl in range(2)], axis=1).astype(BF16)
    v_ref[0] = _nt(v_old, oh).astype(BF16)


def _s5_assemble(kcat, wcat, vcat):
    half = S5_T * S5_GSIZE
    onehot = jnp.asarray(np.eye(_S5_ROW, dtype=np.float32)[_s5_lane_perm()]).astype(BF16)
    place = np.zeros((2, half, _S5_ROW), np.float32)
    for gl in range(2):
        for k in range(4):
            for p in range(S5_STATE):
                place[gl, k * S5_STATE + p, (2 * k + gl) * S5_STATE + p] = 1.0
    out = jax.ShapeDtypeStruct((_S5_PAIRS, _S5_ROW, _S5_ROW), BF16)
    big = pl.BlockSpec((1, _S5_ROW, _S5_ROW), lambda g: (g, 0, 0))
    return pl.pallas_call(
        _s5_assemble_kernel,
        grid=(_S5_PAIRS,),
        in_specs=[pl.BlockSpec((2, S5_GSIZE, _S5_ROW), lambda g: (g, 0, 0)),
                  pl.BlockSpec((2, half, half), lambda g: (g, 0, 0)),
                  pl.BlockSpec((2, half, half), lambda g: (g, 0, 0)),
                  pl.BlockSpec((2, half, _S5_ROW), lambda g: (0, 0, 0)),
                  pl.BlockSpec((1, _S5_ROW, _S5_ROW), lambda g: (g % 4, 0, 0))],
        out_specs=[big, big, big],
        out_shape=[out, out, out],
        compiler_params=_cparams(("parallel",)),
        name="s5_assemble",
    )(kcat, wcat, vcat, jnp.asarray(place).astype(BF16), onehot)


def _nt(a, b):
    return lax.dot_general(a, b, (((1,), (1,)), ((), ())), preferred_element_type=F32)


def _tn(a, b):
    return lax.dot_general(a, b, (((0,), (0,)), ((), ())), preferred_element_type=F32)


def _cmul(ar, ai, br, bi):
    return ar * br - ai * bi, ar * bi + ai * br


def _s5_kernel(x_ref, m_ref, w_ref, v_ref, pw_ref, h0_ref, y_ref, fin_ref, s_ref, h_ref, *, nch):
    ln = 2 * S5_STATE
    x = x_ref[0]
    s_all = jnp.dot(x, w_ref[0], preferred_element_type=F32)
    for k in range(4):
        s_ref[k] = s_all[:, k * ln:(k + 1) * ln]
    a_rf, a_if = pw_ref[0, 0, 0, 1:2, :], pw_ref[0, 0, 1, 1:2, :]
    a_rb, a_ib = pw_ref[0, 1, 0, 1:2, :], pw_ref[0, 1, 1, 1:2, :]

    nb = 4

    def tile(half, c):
        return pl.ds(half * nch + c, nb, stride=2 * nch)

    def sweep(half_f, half_b, carry):
        def step(s, carry):
            hrf, hif, hrb, hib = carry
            rf, rb = tile(half_f, s), tile(half_b, nch - 1 - s)
            h_ref[0, rf, :] = hrf
            h_ref[1, rf, :] = hif
            h_ref[2, rb, :] = hrb
            h_ref[3, rb, :] = hib
            pr, pi = _cmul(a_rf, a_if, hrf, hif)
            qr, qi = _cmul(a_rb, a_ib, hrb, hib)
            return (pr + s_ref[0, rf, :], pi + s_ref[1, rf, :], qr + s_ref[2, rb, :], qi + s_ref[3, rb, :])
        return lax.fori_loop(0, nch, step, carry, unroll=2)

    lead_f, lead_b = pl.ds(0, nb, stride=2), pl.ds(1, nb, stride=2)
    init = (h0_ref[0, 0, 0, lead_f, :], h0_ref[0, 0, 1, lead_f, :],
            h0_ref[0, 1, 0, lead_b, :], h0_ref[0, 1, 1, lead_b, :])
    hrf, hif, hrb, hib = sweep(1, 0, sweep(0, 1, init))
    fin_ref[...] = jnp.zeros(fin_ref.shape, F32)
    fin_ref[0, 0, 0, lead_b, :] = hrf
    fin_ref[0, 0, 1, lead_b, :] = hif
    fin_ref[0, 1, 0, lead_f, :] = hrb
    fin_ref[0, 1, 1, lead_f, :] = hib

    h_all = jnp.concatenate([h_ref[k] for k in range(4)], axis=1).astype(BF16)
    y_ref[0] = (jnp.dot(x, m_ref[0], preferred_element_type=F32)
                + jnp.dot(h_all, v_ref[0], preferred_element_type=F32))


def _s5_core(xs, tables, h0):
    mpair, wpair, vpair, pw = tables
    gp_n, rows, wdt = xs.shape
    nch = rows // 8
    return pl.pallas_call(
        functools.partial(_s5_kernel, nch=nch),
        grid=(gp_n,),
        in_specs=[pl.BlockSpec((1, rows, wdt), lambda g: (g, 0, 0)),
                  pl.BlockSpec((1, wdt, wdt), lambda g: (g, 0, 0)),
                  pl.BlockSpec((1, wdt, wdt), lambda g: (g, 0, 0)),
                  pl.BlockSpec((1, wdt, wdt), lambda g: (g, 0, 0)),
                  pl.BlockSpec((1,) + pw.shape[1:], lambda g: (g, 0, 0, 0, 0)),
                  pl.BlockSpec((1, 2, 2, 8, 2 * S5_STATE), lambda g: (g, 0, 0, 0, 0))],
        out_specs=[pl.BlockSpec((1, rows, wdt), lambda g: (g, 0, 0)),
                   pl.BlockSpec((1, 2, 2, 8, 2 * S5_STATE), lambda g: (g, 0, 0, 0, 0))],
        out_shape=[jax.ShapeDtypeStruct((gp_n, rows, wdt), F32),
                   jax.ShapeDtypeStruct((gp_n, 2, 2, 8, 2 * S5_STATE), F32)],
        scratch_shapes=[pltpu.VMEM((4, rows, 2 * S5_STATE), F32), pltpu.VMEM((4, rows, 2 * S5_STATE), F32)],
        compiler_params=_cparams(("parallel",)),
        name="s5_core",
    )(xs, mpair, wpair, vpair, pw, h0)


def _s5_next_h0(fin):
    g, d, r, _, n = fin.shape
    sw = fin.reshape(g, d, r, 4, 2, n)[:, :, :, :, ::-1, :]
    keep = jnp.array([[1.0, 0.0], [0.0, 1.0]], F32)[None, :, None, None, :, None]
    return (sw * keep).reshape(fin.shape)


def _s5_glu_kernel(y_ref, u_ref, d_ref, w_ref, b_ref, o_ref, nat_ref, *, rows16):
    _s5_from_chunk_rows(y_ref, nat_ref, rows16)
    y = jnp.concatenate([nat_ref[q] for q in range(BR // LANE)], axis=1) + u_ref[...].astype(F32) * d_ref[...]
    g = jax.nn.gelu(y)
    z = jnp.dot(g.astype(BF16), w_ref[...], preferred_element_type=F32) + b_ref[...]
    o_ref[...] = g * jax.nn.sigmoid(z)


def _s5_glu(ys, p, d_skip, glu_w, glu_b, tm=512):
    m = p.shape[0]
    tm = min(tm, m)
    rows16 = tm // S5_T
    return pl.pallas_call(
        functools.partial(_s5_glu_kernel, rows16=rows16),
        grid=(m // tm,),
        in_specs=[pl.BlockSpec((_S5_PAIRS, rows16, _S5_ROW), lambda i: (0, i, 0)),
                  pl.BlockSpec((tm, BR), lambda i: (i, COL_S5_U * LANE // BR)),
                  pl.BlockSpec((1, BR), lambda i: (0, 0)),
                  pl.BlockSpec((BR, BR), lambda i: (0, 0)),
                  pl.BlockSpec((1, BR), lambda i: (0, 0))],
        out_specs=pl.BlockSpec((tm, BR), lambda i: (i, 0)),
        out_shape=jax.ShapeDtypeStruct((m, BR), F32),
        scratch_shapes=[pltpu.VMEM((BR // LANE, tm, LANE), F32)],
        compiler_params=_cparams(("parallel",)),
        name="s5_glu",
    )(ys, p, d_skip.reshape(1, BR), glu_w.astype(BF16), glu_b.reshape(1, BR))


@functools.lru_cache(maxsize=None)
def _fft_tables_np(n_total, n2_len):
    n1_len = n_total // n2_len
    n1h = n1_len // 2
    k1 = np.arange(n1_len, dtype=np.int64)
    n2 = np.arange(n2_len, dtype=np.int64)
    m = (k1[None, :, None] * k1[None, None, :] * n2_len + n2[:, None, None] * k1[None, :, None]) % n_total
    ang = m.astype(np.float64) * (2.0 * math.pi / n_total)
    c, s = np.cos(ang), np.sin(ang)
    ch, sh = c[:, :, :n1h], s[:, :, :n1h]
    gf = np.concatenate([np.concatenate([ch, sh], 2), np.concatenate([-sh, ch], 2)], 1)
    gk = np.concatenate([ch, -sh], 1)
    cht, sht = ch.transpose(0, 2, 1), sh.transpose(0, 2, 1)
    gi = np.concatenate([np.concatenate([cht, -sht], 2), np.concatenate([sht, cht], 2)], 1)
    a2 = ((n2[:, None] * n2[None, :]) % n2_len).astype(np.float64) * (2.0 * math.pi / n2_len)
    c2, s2 = np.cos(a2), np.sin(a2)
    f2 = np.concatenate([np.concatenate([c2, s2], 1), np.concatenate([-s2, c2], 1)], 0)
    f2c = np.concatenate([np.concatenate([c2, -s2], 1), np.concatenate([s2, c2], 1)], 0)
    return tuple(np.asarray(t, np.float32) for t in (gf, gk, gi, f2, f2c))


def _fft_tables(n_total, n2_len):
    return tuple(jnp.asarray(t).astype(BF16) for t in _fft_tables_np(n_total, n2_len))


def _hy_shortconv_kernel(x_ref, w_ref, b_ref, o_ref):
    x = x_ref[0].astype(F32)
    l = x.shape[0]
    row = lax.broadcasted_iota(jnp.int32, x.shape, 0)
    prev = jnp.where(row == 0, 0.0, pltpu.roll(x, 1, axis=0))
    nxt = jnp.where(row == l - 1, 0.0, pltpu.roll(x, l - 1, axis=0))
    o_ref[0] = w_ref[0:1, :] * prev + w_ref[1:2, :] * x + w_ref[2:3, :] * nxt + b_ref[...]


def _hy_shortconv(p3, conv_w, conv_b):
    bsz, l, _ = p3.shape
    units = 3 * BR // LANE
    return pl.pallas_call(
        _hy_shortconv_kernel,
        grid=(bsz, units),
        in_specs=[pl.BlockSpec((1, l, LANE), lambda b, u: (b, 0, COL_HY_X + u)),
                  pl.BlockSpec((3, LANE), lambda b, u: (0, u)),
                  pl.BlockSpec((1, LANE), lambda b, u: (0, u))],
        out_specs=pl.BlockSpec((1, l, LANE), lambda b, u: (b, 0, u)),
        out_shape=jax.ShapeDtypeStruct((bsz, l, 3 * BR), F32),
        compiler_params=_cparams(("parallel", "parallel")),
        name="hy_shortconv",
    )(p3, conv_w, conv_b.reshape(1, 3 * BR))


def _fft_stage2(mid_ref, f2_ref, n1_len, n2_len, emit):
    def body(kp, _):
        def rhs(k1):
            return jnp.concatenate([mid_ref[pl.ds(k1, n2_len, stride=2 * n1_len), :],
                                    mid_ref[pl.ds(n1_len + k1, n2_len, stride=2 * n1_len), :]], axis=0)
        r = jnp.concatenate([rhs(2 * kp), rhs(2 * kp + 1)], axis=1).astype(BF16)
        xs = jnp.dot(f2_ref[...], r, preferred_element_type=F32)
        emit(2 * kp, xs[:, :LANE])
        emit(2 * kp + 1, xs[:, LANE:])
        return 0
    lax.fori_loop(0, n1_len // 2, body, 0, unroll=_FFT_UNROLL)


def _hy_kf_kernel(fwd_ref, bwd_ref, nrm_ref, gk_ref, f2_ref, o_ref, mid_ref, *, n1_len, n2_len):
    n1h = n1_len // 2
    row = lax.broadcasted_iota(jnp.int32, (n1h, LANE), 0)

    def stage1(n2, _):
        f = fwd_ref[pl.ds(n2, n1h, stride=n2_len), :]
        b = jnp.where((row == 0) & (n2 == 0), 0.0, bwd_ref[pl.ds(n2, n1h, stride=n2_len), :])
        a = jnp.dot(gk_ref[n2], jnp.concatenate([f, b], axis=1).astype(BF16), preferred_element_type=F32)
        r0 = pl.multiple_of(n2 * 2 * n1_len, 2 * n1_len)
        mid_ref[0, pl.ds(r0, 2 * n1_len), :] = a[:, :LANE]
        mid_ref[1, pl.ds(r0, 2 * n1_len), :] = a[:, LANE:]
        return 0
    lax.fori_loop(0, n2_len, stage1, 0, unroll=_FFT_UNROLL)
    scale = nrm_ref[...] * (1.0 / (n1_len * n2_len))

    def stage2(k1, _):
        def rhs(j):
            return jnp.concatenate([mid_ref[j, pl.ds(k1, n2_len, stride=2 * n1_len), :],
                                    mid_ref[j, pl.ds(n1_len + k1, n2_len, stride=2 * n1_len), :]], axis=0)
        r = jnp.concatenate([rhs(0), rhs(1)], axis=1).astype(BF16)
        xs = jnp.dot(f2_ref[...], r, preferred_element_type=F32)
        f, b = xs[:, :LANE], xs[:, LANE:]
        o_ref[0, k1] = (jnp.concatenate([f[:n2_len] + b[:n2_len], f[n2_len:] - b[n2_len:]], axis=0)
                        * scale).astype(BF16)
        return 0
    lax.fori_loop(0, n1_len, stage2, 0, unroll=_FFT_UNROLL)


def _hy_kf(h2d, nrm, gk, f2, n2_len):
    l, cols = h2d.shape
    n1_len = 2 * l // n2_len
    tiles = cols // (2 * LANE)
    return pl.pallas_call(
        functools.partial(_hy_kf_kernel, n1_len=n1_len, n2_len=n2_len),
        grid=(tiles,),
        in_specs=[pl.BlockSpec((l, LANE), lambda t: (0, t)),
                  pl.BlockSpec((l, LANE), lambda t: (0, tiles + t)),
                  pl.BlockSpec((1, LANE), lambda t: (0, t)),
                  pl.BlockSpec(gk.shape, lambda t: (0, 0, 0)),
                  pl.BlockSpec(f2.shape, lambda t: (0, 0))],
        out_specs=pl.BlockSpec((1, n1_len, 2 * n2_len, LANE), lambda t: (t, 0, 0, 0)),
        out_shape=jax.ShapeDtypeStruct((tiles, n1_len, 2 * n2_len, LANE), BF16),
        scratch_shapes=[pltpu.VMEM((2, 4 * l, LANE), F32)],
        compiler_params=_cparams(("parallel",)),
        name="hy_kf",
    )(h2d, h2d, nrm, gk, f2)


def _hy_conv_kernel(z_ref, g_ref, kf_ref, bias_ref, gf_ref, gi_ref, f2_ref, f2c_ref, o_ref,
                    mid_ref, spec_ref, *, n1_len, n2_len, order):
    n1h = n1_len // 2

    def stage1(n2, _):
        rhs = jnp.concatenate([z_ref[0, pl.ds(n2, n1h, stride=n2_len), :],
                               z_ref[1, pl.ds(n2, n1h, stride=n2_len), :]], axis=0).astype(BF16)
        r0 = pl.multiple_of(n2 * 2 * n1_len, 2 * n1_len)
        mid_ref[pl.ds(r0, 2 * n1_len), :] = jnp.dot(gf_ref[n2], rhs, preferred_element_type=F32)
        return 0
    lax.fori_loop(0, n2_len, stage1, 0, unroll=_FFT_UNROLL)

    def emit(k1, xs):
        kf = kf_ref[0, k1].astype(F32)
        xr, xi = xs[:n2_len], xs[n2_len:]
        kr, ki = kf[:n2_len], kf[n2_len:]
        spec_ref[k1] = jnp.concatenate([xr * kr - xi * ki, xr * ki + xi * kr], axis=0).astype(BF16)
    _fft_stage2(mid_ref, f2_ref, n1_len, n2_len, emit)

    def stage2i(kp, _):
        r = jnp.concatenate([spec_ref[2 * kp], spec_ref[2 * kp + 1]], axis=1)
        b = jnp.dot(f2c_ref[...], r, preferred_element_type=F32)
        r0 = pl.multiple_of(kp * 4 * n2_len, 4 * n2_len)
        mid_ref[pl.ds(r0, 2 * n2_len), :] = b[:, :LANE]
        mid_ref[pl.ds(r0 + 2 * n2_len, 2 * n2_len), :] = b[:, LANE:]
        return 0
    lax.fori_loop(0, n1_len // 2, stage2i, 0, unroll=_FFT_UNROLL)

    def stage1i(n2, _):
        rhs = jnp.concatenate([mid_ref[pl.ds(n2, n1_len, stride=2 * n2_len), :],
                               mid_ref[pl.ds(n2_len + n2, n1_len, stride=2 * n2_len), :]], axis=0).astype(BF16)
        o = jnp.dot(gi_ref[n2], rhs, preferred_element_type=F32)
        o_ref[0, pl.ds(n2, n1h, stride=n2_len), :] = o[:n1h]
        o_ref[1, pl.ds(n2, n1h, stride=n2_len), :] = o[n1h:]
        return 0
    lax.fori_loop(0, n2_len, stage1i, 0, unroll=_FFT_UNROLL)

    bias = bias_ref[order:order + 1, :]
    for j in range(2):
        o_ref[j] = g_ref[j] * (o_ref[j] + bias * z_ref[j])


def _hy_conv(z, zcol, g, gcol, kf, bias, tabs, n2_len, order):
    gf, _, gi, f2, f2c = tabs
    bsz, l, _ = z.shape
    n1_len = 2 * l // n2_len
    tiles = BR // LANE
    const2 = lambda *_: (0, 0)
    const3 = lambda *_: (0, 0, 0)
    one = pl.Buffered(1)
    return pl.pallas_call(
        functools.partial(_hy_conv_kernel, n1_len=n1_len, n2_len=n2_len, order=order),
        grid=(tiles, bsz // 2),
        in_specs=[pl.BlockSpec((2, l, LANE), lambda t, b: (b, 0, zcol + t)),
                  pl.BlockSpec((2, l, LANE), lambda t, b: (b, 0, gcol + t)),
                  pl.BlockSpec((1, n1_len, 2 * n2_len, LANE), lambda t, b: (order * tiles + t, 0, 0, 0),
                               pipeline_mode=one),
                  pl.BlockSpec((HY_ORDER, LANE), lambda t, b: (0, t)),
                  pl.BlockSpec(gf.shape, const3, pipeline_mode=one),
                  pl.BlockSpec(gi.shape, const3, pipeline_mode=one),
                  pl.BlockSpec(f2.shape, const2, pipeline_mode=one),
                  pl.BlockSpec(f2c.shape, const2, pipeline_mode=one)],
        out_specs=pl.BlockSpec((2, l, LANE), lambda t, b: (b, 0, t)),
        out_shape=jax.ShapeDtypeStruct((bsz, l, BR), F32),
        scratch_shapes=[pltpu.VMEM((4 * l, LANE), F32),
                        pltpu.VMEM((n1_len, 2 * n2_len, LANE), BF16)],
        compiler_params=_cparams(("parallel", "arbitrary")),
        name="hy_conv%d" % order,
    )(z, g, kf, bias, gf, gi, f2, f2c)


def _hyena_filters(l, w1, b1, f1, w2, b2, f2, w3, b3, f3, w4):
    hp = lax.Precision.HIGHEST
    t = jnp.linspace(0.0, 1.0, l, dtype=F32)[:, None]
    wpos = 2.0 * math.pi * jnp.arange(l, dtype=F32)[:, None] / l
    f = jnp.linspace(1e-4, HY_BANDS - 1, HY_BANDS, dtype=F32)[None, :]
    z = jnp.concatenate([t, jnp.cos(f * wpos), -jnp.sin(f * wpos)], axis=-1)
    h = jnp.sin(f1 * (jnp.dot(z, w1, precision=hp) + b1))
    h = jnp.sin(f2 * (jnp.dot(h, w2, precision=hp) + b2))
    h = jnp.sin(f3 * (jnp.dot(h, w3, precision=hp) + b3))
    h = jnp.dot(h, w4, precision=hp).reshape(l, 2, HY_ORDER, BR)
    deltas = jnp.linspace(math.log(HY_TARGET) / HY_SLOW_DECAY,
                          math.log(HY_TARGET) / HY_FAST_DECAY, BR, dtype=F32)
    h = h * jnp.exp(-t * jnp.abs(deltas))[:, None, None, :]
    ss = jnp.sum(h[:, 0] * h[:, 0], axis=0) + jnp.sum(h[1:, 1] * h[1:, 1], axis=0)
    return h.reshape(l, 2 * HY_ORDER * BR), lax.rsqrt(ss + EPS).reshape(1, HY_ORDER * BR)


def _hyena(p3, conv_w, conv_b, filters, bias, n2_len):
    bsz, l, _ = p3.shape
    tabs = _fft_tables(2 * l, n2_len)
    xs = _hy_shortconv(p3, conv_w, conv_b)
    h2d, nrm = filters
    kf = _hy_kf(h2d, nrm, tabs[1], tabs[3], n2_len)
    tiles = BR // LANE
    z1 = _hy_conv(xs, 0, xs, tiles, kf, bias, tabs, n2_len, 0)
    return _hy_conv(z1, 0, xs, 2 * tiles, kf, bias, tabs, n2_len, 1)


def _ret_kernel(q_ref, k_ref, v_ref, cos_ref, sin_ref, lg_ref, nw_ref, s0_ref, o_ref, sfin_ref,
                vb_ref, qf_ref, qb_ref, kf_ref, kb_ref, *, nchunks, c):
    lg = lg_ref[0]
    jrow = lax.broadcasted_iota(jnp.int32, (c, c), 0)
    icol = lax.broadcasted_iota(jnp.int32, (c, c), 1)
    jf = lax.broadcasted_iota(jnp.int32, (c, RET_DK), 0).astype(F32)
    dsym = jnp.exp(jnp.abs(jrow - icol).astype(F32) * lg[:, 0:1]) * jnp.where(jrow == icol, 2.0, 1.0)
    dq_f = jnp.exp((jf + 1.0) * lg)
    dq_b = jnp.exp((c - jf) * lg)
    dk_f = jnp.exp((c - 1.0 - jf) * lg)
    dk_b = jnp.exp(jf * lg)
    dchunk = jnp.exp(c * lg)

    def rope(x, r0):
        return x * cos_ref[pl.ds(r0, c), :] + pltpu.roll(x, RET_DK // 2, axis=1) * sin_ref[pl.ds(r0, c), :]

    def intra_step(n, _):
        r0 = pl.multiple_of(n * c, c)
        rows = pl.ds(r0, c)
        q = rope(q_ref[0, rows, :].astype(F32), r0)
        k = rope(k_ref[0, rows, :].astype(F32), r0) * (RET_DK ** -0.5)
        v = v_ref[0, rows, :].astype(BF16)
        att = _nt(q.astype(BF16), k.astype(BF16)) * dsym
        o_ref[0, rows, :] = jnp.dot(att.astype(BF16), v, preferred_element_type=F32)
        vb_ref[rows, :] = v
        qf_ref[rows, :] = (q * dq_f).astype(BF16)
        qb_ref[rows, :] = (q * dq_b).astype(BF16)
        kf_ref[rows, :] = (k * dk_f).astype(BF16)
        kb_ref[rows, :] = (k * dk_b).astype(BF16)
        return 0

    lax.fori_loop(0, nchunks, intra_step, 0, unroll=_ATT_UNROLL)

    def sweep(lo, hi, carry, finish):
        def step(i, carry):
            s_f, s_b = carry
            rf = pl.ds(pl.multiple_of(i * c, c), c)
            rb = pl.ds(pl.multiple_of((nchunks - 1 - i) * c, c), c)
            o_f = o_ref[0, rf, :] + jnp.dot(qf_ref[rf, :], s_f.astype(BF16), preferred_element_type=F32)
            o_b = o_ref[0, rb, :] + jnp.dot(qb_ref[rb, :], s_b.astype(BF16), preferred_element_type=F32)
            if finish:
                o_f = o_f * lax.rsqrt(jnp.mean(o_f * o_f, axis=-1, keepdims=True) + EPS) * nw_ref[...]
                o_b = o_b * lax.rsqrt(jnp.mean(o_b * o_b, axis=-1, keepdims=True) + EPS) * nw_ref[...]
            o_ref[0, rf, :] = o_f
            o_ref[0, rb, :] = o_b
            return (dchunk * s_f + _tn(kf_ref[rf, :], vb_ref[rf, :]),
                    dchunk * s_b + _tn(kb_ref[rb, :], vb_ref[rb, :]))
        return lax.fori_loop(lo, hi, step, carry)

    carry = sweep(0, nchunks // 2, (s0_ref[0, 0, 0], s0_ref[0, 0, 1]), False)
    s_f, s_b = sweep(nchunks // 2, nchunks, carry, True)
    sfin_ref[0, 0, 0] = s_f
    sfin_ref[0, 0, 1] = s_b


def _retention(p3, cos2, sin2, norm_w, s0):
    bsz, l, _ = p3.shape
    lgs = jnp.log(1.0 - 2.0 ** (-5.0 - jnp.arange(RET_HEADS, dtype=F32)))
    lgs = jnp.broadcast_to(lgs[:, None, None], (RET_HEADS, 1, LANE))
    tab = pl.BlockSpec((l, LANE), lambda b, h: (0, 0))
    st = pl.BlockSpec((1, 1, 2, RET_DK, LANE), lambda b, h: (b, h, 0, 0, 0))
    chunk = RET_CHUNK if l % (2 * RET_CHUNK) == 0 else ATT_CHUNK
    return pl.pallas_call(
        functools.partial(_ret_kernel, nchunks=l // chunk, c=chunk),
        grid=(bsz, RET_HEADS),
        in_specs=[pl.BlockSpec((1, l, LANE), lambda b, h: (b, 0, COL_RET_Q + h)),
                  pl.BlockSpec((1, l, LANE), lambda b, h: (b, 0, COL_RET_K + h)),
                  pl.BlockSpec((1, l, LANE), lambda b, h: (b, 0, COL_RET_V + h)),
                  tab, tab,
                  pl.BlockSpec((1, 1, LANE), lambda b, h: (h, 0, 0)),
                  pl.BlockSpec((1, LANE), lambda b, h: (0, h)),
                  st],
        out_specs=[pl.BlockSpec((1, l, LANE), lambda b, h: (b, 0, h)), st],
        out_shape=[jax.ShapeDtypeStruct((bsz, l, BR), F32),
                   jax.ShapeDtypeStruct((bsz, RET_HEADS, 2, RET_DK, LANE), F32)],
        scratch_shapes=[pltpu.VMEM((l, LANE), BF16)] * 5,
        compiler_params=_cparams(("parallel", "parallel")),
        name="retention",
    )(p3, p3, p3, cos2, sin2, lgs, norm_w.reshape(1, BR), s0)


def _rope_tables(ang):
    cos, sin = jnp.cos(ang), jnp.sin(ang)
    return jnp.concatenate([cos, cos], -1), jnp.concatenate([-sin, sin], -1)


def _latent_angles(rows):
    half = RET_DK // 4
    inv = ROPE_BASE ** (-jnp.arange(half, dtype=F32) / half)
    r = jnp.repeat(jnp.arange(rows, dtype=F32), GRID_W)
    cl = jnp.tile(jnp.arange(GRID_W, dtype=F32), rows)
    return jnp.concatenate([r[:, None] * inv, cl[:, None] * inv], axis=-1)


def _ctx_angles(n_ctx):
    n = RET_DK // 2
    inv = ROPE_BASE ** (-jnp.arange(n, dtype=F32) / n)
    return jnp.arange(n_ctx, dtype=F32)[:, None] * inv


def _gla_kernel(q_ref, k_ref, v_ref, lr_ref, gw_ref, gb_ref, nw_ref, s0_ref, o_ref, sfin_ref,
                vb_ref, qf_ref, qb_ref, kf_ref, kb_ref, df_ref, db_ref, *, nchunks):
    c = ATT_CHUNK
    jrow = lax.broadcasted_iota(jnp.int32, (c, c), 0)
    icol = lax.broadcasted_iota(jnp.int32, (c, c), 1)
    tri_f = jnp.where(icol <= jrow, 1.0, 0.0).astype(BF16)
    tri_b = jnp.where(icol >= jrow, 1.0, 0.0).astype(BF16)
    lane = lax.broadcasted_iota(jnp.int32, (1, LANE), 1)
    head0 = lane < GLA_DK

    gw_both = jnp.concatenate([gw_ref[0], gw_ref[1]], axis=1)
    gb_both = jnp.concatenate([gb_ref[0], gb_ref[1]], axis=1)

    def decays(r0):
        x = jnp.dot(lr_ref[0, pl.ds(r0, c), :].astype(BF16), gw_both, preferred_element_type=F32) + gb_both
        la = (jnp.minimum(x, 0.0) - jnp.log(1.0 + jnp.exp(-jnp.abs(x)))) * (1.0 / GLA_TAU)
        hi = la.astype(BF16)
        lo = (la - hi.astype(F32)).astype(BF16)
        out = []
        for d, tri in enumerate((tri_f, tri_b)):
            cols = slice(d * LANE, (d + 1) * LANE)
            cs = jnp.dot(tri, jnp.concatenate([hi[:, cols], lo[:, cols]], axis=1), preferred_element_type=F32)
            out.append(cs[:, :LANE] + cs[:, LANE:])
        return out

    def intra_step(n, _):
        r0 = pl.multiple_of(n * c, c)
        rows = pl.ds(r0, c)
        q = q_ref[0, rows, :].astype(F32) * (GLA_DK ** -0.5)
        k = k_ref[0, rows, :].astype(F32)
        bf, bb = decays(r0)
        ref_f = bf[c // 2 - 1:c // 2, :]
        ref_b = bb[c // 2:c // 2 + 1, :]
        qf = q * jnp.exp(bf - ref_f)
        kf = (k * jnp.exp(ref_f - bf)).astype(BF16)
        qb = q * jnp.exp(bb - ref_b)
        kb = (k * jnp.exp(ref_b - bb)).astype(BF16)
        for h in range(2):
            hm = head0 if h == 0 else jnp.logical_not(head0)
            v = v_ref[0, rows, h * GLA_DV:(h + 1) * GLA_DV].astype(BF16)
            att = (jnp.where(icol <= jrow, _nt(jnp.where(hm, qf, 0.0).astype(BF16), kf), 0.0)
                   + jnp.where(icol >= jrow, _nt(jnp.where(hm, qb, 0.0).astype(BF16), kb), 0.0))
            o_ref[0, rows, h * GLA_DV:(h + 1) * GLA_DV] = jnp.dot(att.astype(BF16), v, preferred_element_type=F32)
            vb_ref[rows, h * GLA_DV:(h + 1) * GLA_DV] = v
        bf_last, bb_last = bf[c - 1:c, :], bb[0:1, :]
        qf_ref[rows, :] = (q * jnp.exp(bf)).astype(BF16)
        qb_ref[rows, :] = (q * jnp.exp(bb)).astype(BF16)
        kf_ref[rows, :] = (k * jnp.exp(bf_last - bf)).astype(BF16)
        kb_ref[rows, :] = (k * jnp.exp(bb_last - bb)).astype(BF16)
        d0 = pl.multiple_of(n * 8, 8)
        df_ref[pl.ds(d0, 8), :] = jnp.broadcast_to(jnp.exp(bf_last), (8, LANE))
        db_ref[pl.ds(d0, 8), :] = jnp.broadcast_to(jnp.exp(bb_last), (8, LANE))
        return 0

    lax.fori_loop(0, nchunks, intra_step, 0, unroll=_ATT_UNROLL)

    def sweep(lo, hi, carry, finish):
        def step(i, carry):
            sts = list(carry)
            new = []
            for d in range(2):
                n = i if d == 0 else nchunks - 1 - i
                rows = pl.ds(pl.multiple_of(n * c, c), c)
                qs = (qf_ref if d == 0 else qb_ref)[rows, :]
                kd = (kf_ref if d == 0 else kb_ref)[rows, :]
                dec = (df_ref if d == 0 else db_ref)[pl.ds(pl.multiple_of(n * 8, 8), 1), :]
                for h in range(2):
                    hm = head0 if h == 0 else jnp.logical_not(head0)
                    cols = slice(h * GLA_DV, (h + 1) * GLA_DV)
                    st = sts[2 * d + h]
                    o = o_ref[0, rows, cols] + _nt(qs, st.astype(BF16))
                    if finish:
                        o = o * lax.rsqrt(jnp.mean(o * o, axis=-1, keepdims=True) + EPS) * nw_ref[:, cols]
                    o_ref[0, rows, cols] = o
                    new.append(jnp.where(hm, dec * st + _tn(vb_ref[rows, cols], kd), 0.0))
            return tuple(new)
        return lax.fori_loop(lo, hi, step, carry)

    carry = (s0_ref[0, 0, 0, 0], s0_ref[0, 0, 0, 1], s0_ref[0, 0, 1, 0], s0_ref[0, 0, 1, 1])
    carry = sweep(0, nchunks // 2, carry, False)
    s_fin = sweep(nchunks // 2, nchunks, carry, True)
    sfin_ref[0, 0, 0, 0] = s_fin[0]
    sfin_ref[0, 0, 0, 1] = s_fin[1]
    sfin_ref[0, 0, 1, 0] = s_fin[2]
    sfin_ref[0, 0, 1, 1] = s_fin[3]


def _gla(p3, gate_w, gate_b, norm_w, s0):
    bsz, l, _ = p3.shape
    pairs = GLA_HEADS // 2
    gw = jnp.zeros((2, LANE, GLA_HEADS * GLA_DK), F32)
    for d in range(2):
        gw = gw.at[d, d * GLA_LR:(d + 1) * GLA_LR, :].set(gate_w[d].astype(F32))
    st = pl.BlockSpec((1, 1, 2, 2, GLA_DV, LANE), lambda b, h: (b, h, 0, 0, 0, 0))
    return pl.pallas_call(
        functools.partial(_gla_kernel, nchunks=l // ATT_CHUNK),
        grid=(bsz, pairs),
        in_specs=[pl.BlockSpec((1, l, LANE), lambda b, h: (b, 0, COL_GLA_Q + h)),
                  pl.BlockSpec((1, l, LANE), lambda b, h: (b, 0, COL_GLA_K + h)),
                  pl.BlockSpec((1, l, 2 * GLA_DV), lambda b, h: (b, 0, COL_GLA_V // 2 + h)),
                  pl.BlockSpec((1, l, LANE), lambda b, h: (b, 0, COL_GLA_LR)),
                  pl.BlockSpec((2, LANE, LANE), lambda b, h: (0, 0, h)),
                  pl.BlockSpec((2, 1, LANE), lambda b, h: (0, 0, h)),
                  pl.BlockSpec((1, 2 * GLA_DV), lambda b, h: (0, h)),
                  st],
        out_specs=[pl.BlockSpec((1, l, 2 * GLA_DV), lambda b, h: (b, 0, h)), st],
        out_shape=[jax.ShapeDtypeStruct((bsz, l, BR), F32),
                   jax.ShapeDtypeStruct((bsz, pairs, 2, 2, GLA_DV, LANE), F32)],
        scratch_shapes=([pltpu.VMEM((l, 2 * GLA_DV), BF16)] + [pltpu.VMEM((l, LANE), BF16)] * 4
                        + [pltpu.VMEM((8 * l // ATT_CHUNK, LANE), F32)] * 2),
        compiler_params=_cparams(("parallel", "parallel")),
        name="gla",
    )(p3, p3, p3, p3, gw.astype(BF16), gate_b.astype(F32).reshape(2, 1, GLA_HEADS * GLA_DK),
      norm_w.reshape(1, BR), s0)


def _mixers(p2, xs, bsz, prm, i, s5_tabs, hy_filters, rope, s5_h0, ret_s0, gla_s0, with_hyena):
    m = p2.shape[0]
    l = m // bsz
    p3 = p2.reshape(bsz, l, NP)
    y, s5_fin = _s5_core(xs, s5_tabs, s5_h0)
    s5 = _s5_glu(y, p2, prm['s5_d'][i], prm['s5_glu_w'][i], prm['s5_glu_b'][i])
    hy = None
    if with_hyena:
        hy = _hyena(p3, prm['hy_conv_w'][i], prm['hy_conv_b'][i], hy_filters, prm['hy_bias'][i].astype(F32),
                    64 if l >= 512 else 16).reshape(m, BR)
    ret, ret_fin = _retention(p3, rope[0], rope[1], prm['ret_norm_w'][i], ret_s0)
    gla, gla_fin = _gla(p3, prm['gla_gate_w'][i], prm['gla_gate_b'][i], prm['gla_norm_w'][i], gla_s0)
    return (s5, hy, ret.reshape(m, BR), gla.reshape(m, BR)), (s5_fin, ret_fin, gla_fin)


def kernel(x, c, ctx, c_ctx, norm_w, ada_w, ada_b, w_in, w_out, s5_a_re, s5_a_im, s5_log_dt, s5_b_re, s5_b_im, s5_c_re, s5_c_im, s5_d, s5_glu_w, s5_glu_b, hy_conv_w, hy_conv_b, hy_w1, hy_b1, hy_f1, hy_w2, hy_b2, hy_f2, hy_w3, hy_b3, hy_f3, hy_w4, hy_bias, ret_norm_w, gla_gate_w, gla_gate_b, gla_norm_w, final_norm_w):
    prm = dict(s5_a_re=s5_a_re, s5_a_im=s5_a_im, s5_log_dt=s5_log_dt, s5_b_re=s5_b_re, s5_b_im=s5_b_im,
               s5_c_re=s5_c_re, s5_c_im=s5_c_im, s5_d=s5_d, s5_glu_w=s5_glu_w, s5_glu_b=s5_glu_b,
               hy_conv_w=hy_conv_w, hy_conv_b=hy_conv_b, hy_w1=hy_w1, hy_b1=hy_b1, hy_f1=hy_f1,
               hy_w2=hy_w2, hy_b2=hy_b2, hy_f2=hy_f2, hy_w3=hy_w3, hy_b3=hy_b3, hy_f3=hy_f3,
               hy_w4=hy_w4, hy_bias=hy_bias, ret_norm_w=ret_norm_w, gla_gate_w=gla_gate_w,
               gla_gate_b=gla_gate_b, gla_norm_w=gla_norm_w)
    bsz, l, d = x.shape
    lc = ctx.shape[1]
    depth = w_in.shape[0]
    cc = jnp.concatenate([c, c_ctx[None, :], jnp.zeros((8 - bsz - 1, d), F32)], axis=0)
    mod = _ada(cc, ada_w, ada_b)
    rope_l = _rope_tables(_latent_angles(l // GRID_W))
    rope_c = _rope_tables(_ctx_angles(lc))
    s5_zero = jnp.zeros((S5_GROUPS // 2, 2, 2, 8, 2 * S5_STATE), F32)
    ret_zero = jnp.zeros((bsz, RET_HEADS, 2, RET_DK, LANE), F32)
    gla_zero = jnp.zeros((bsz, GLA_HEADS // 2, 2, 2, GLA_DV, LANE), F32)

    w_all = _wprep(w_in)
    wo_all = w_out.astype(BF16)
    filt = (hy_w1, hy_b1, hy_f1, hy_w2, hy_b2, hy_f2, hy_w3, hy_b3, hy_f3, hy_w4)

    x_l = x.reshape(bsz * l, d)
    x_c = ctx.reshape(bsz * lc, d)
    for i in range(depth):
        last = i == depth - 1
        sh, sc, gt = mod[i, :, :d], mod[i, :, d:2 * d], mod[i, :, 2 * d:]
        p_c, xs_c = _inproj(x_c, norm_w[i], sc[bsz:bsz + 1, None, :], sh[bsz:bsz + 1, None, :], w_all, i,
                            bsz * lc)
        p_l, xs_l = _inproj(x_l, norm_w[i], sc[:bsz, None, :], sh[:bsz, None, :], w_all, i, l)
        s5_tabs = _s5_tables(s5_a_re[i], s5_a_im[i], s5_log_dt[i], s5_b_re[i], s5_b_im[i], s5_c_re[i],
                             s5_c_im[i], 1)
        filt_i = [w[i].astype(F32) for w in filt]
        ys_c, (s5_fin, ret_fin, gla_fin) = _mixers(
            p_c, xs_c, bsz, prm, i, s5_tabs, None if last else _hyena_filters(lc, *filt_i), rope_c, s5_zero,
            ret_zero, gla_zero, with_hyena=not last)
        ys_l, _ = _mixers(p_l, xs_l, bsz, prm, i, s5_tabs, _hyena_filters(l, *filt_i), rope_l,
                          _s5_next_h0(s5_fin), ret_fin, gla_fin, with_hyena=True)
        x_l = _outproj(x_l, ys_l, p_l, wo_all, i, gt[:bsz, None, :], final_norm_w, l, final=last)
        if not last:
            x_c = _outproj(x_c, ys_c, p_c, wo_all, i, gt[bsz:bsz + 1, None, :], final_norm_w, bsz * lc,
                           final=False)
    return x_l.reshape(bsz, l, d)
```

```python
import functools
import math

import numpy as np
import jax
import jax.numpy as jnp
from jax import lax
from jax.experimental import pallas as pl
from jax.experimental.pallas import tpu as pltpu

F32 = jnp.float32
BF16 = jnp.bfloat16

EPS = 1e-6
GRID_W = 64
BR = 512
S5_GSIZE = 16
S5_GROUPS = BR // S5_GSIZE
S5_STATE = 64
S5_T = 16
HY_ORDER = 2
HY_EMB = 33
HY_BANDS = (HY_EMB - 1) // 2
HY_FAST_DECAY = 0.3
HY_SLOW_DECAY = 1.5
HY_TARGET = 1e-2
RET_HEADS = 4
RET_DK = 128
GLA_HEADS = 4
GLA_DK = 64
GLA_DV = 128
GLA_LR = 16
GLA_TAU = 16.0
ROPE_BASE = 10000.0
ATT_CHUNK = 128
RET_CHUNK = 256
_FFT_UNROLL = 8
_ATT_UNROLL = 2

LANE = 128
VMEM_LIMIT = 56 * 1024 * 1024

COL_S5_U = 0
COL_S5_G = 4
COL_HY_X = 8
COL_HY_G = 20
COL_RET_Q = 24
COL_RET_K = 28
COL_RET_V = 32
COL_RET_G = 36
COL_GLA_Q = 40
COL_GLA_K = 42
COL_GLA_V = 44
COL_GLA_G = 48
COL_GLA_LR = 52
NP_UNITS = 54
NP = NP_UNITS * LANE
W_TILE = 768
_ALIGNED = COL_GLA_G * LANE


def _cparams(sem):
    return pltpu.CompilerParams(dimension_semantics=sem, vmem_limit_bytes=VMEM_LIMIT)


def _silu(x):
    return x * jax.nn.sigmoid(x)


def _wprep(w_in):
    depth, d, in_w = w_in.shape
    assert _ALIGNED % W_TILE == 0 and in_w == _ALIGNED + 2 * GLA_LR + BR and NP == _ALIGNED + W_TILE
    wb = w_in.astype(BF16)
    tail = jnp.concatenate([wb[:, :, _ALIGNED + 2 * GLA_LR:], wb[:, :, _ALIGNED:_ALIGNED + 2 * GLA_LR],
                            jnp.zeros((depth, d, W_TILE - BR - 2 * GLA_LR), BF16)], axis=-1)
    return wb, tail


def _ada_kernel(c_ref, w_ref, b_ref, o_ref):
    a = _silu(c_ref[...]).astype(BF16)
    o_ref[0] = jnp.dot(a, w_ref[0].astype(BF16), preferred_element_type=F32) + b_ref[0]


def _ada(cc, ada_w, ada_b, tn=768):
    depth, d, n = ada_w.shape
    return pl.pallas_call(
        _ada_kernel,
        grid=(depth, n // tn),
        in_specs=[pl.BlockSpec((8, d), lambda i, j: (0, 0)),
                  pl.BlockSpec((1, d, tn), lambda i, j: (i, 0, j)),
                  pl.BlockSpec((1, 1, tn), lambda i, j: (i, 0, j))],
        out_specs=pl.BlockSpec((1, 8, tn), lambda i, j: (i, 0, j)),
        out_shape=jax.ShapeDtypeStruct((depth, 8, n), F32),
        compiler_params=_cparams(("parallel", "parallel")),
        name="ada_mod",
    )(cc, ada_w, ada_b.reshape(depth, 1, n))


_S5_SLOT = 2 * S5_GSIZE
_S5_PAIRS = S5_GROUPS // 2
_S5_ROW = S5_T * _S5_SLOT


def _s5_lane_perm():
    perm = np.zeros((4, _S5_ROW), np.int32)
    for s in range(4):
        for tau in range(S5_T):
            for c in range(_S5_SLOT):
                gl, i = divmod(c, S5_GSIZE)
                perm[s, LANE * (tau // 4) + _S5_SLOT * ((s + tau) % 4) + c] = (gl * S5_T + tau) * S5_GSIZE + i
    return perm


def _s5_to_chunk_rows(nat_ref, xs_ref, rows16):
    slot = lax.broadcasted_iota(jnp.int32, (rows16, LANE), 1) // _S5_SLOT
    for q in range(4):
        for qd in range(4):
            rolled = []
            for t in range(4):
                piece = nat_ref[q, pl.ds(4 * qd + t, rows16, stride=S5_T), :]
                rolled.append(pltpu.roll(piece, _S5_SLOT * t, axis=1) if t else piece)
            for s in range(4):
                acc = rolled[0]
                for t in range(1, 4):
                    acc = jnp.where(slot == (s + t) % 4, rolled[t], acc)
                xs_ref[4 * q + s, :, qd * LANE:(qd + 1) * LANE] = acc.astype(xs_ref.dtype)


def _s5_from_chunk_rows(y_ref, nat_ref, rows16):
    slot = lax.broadcasted_iota(jnp.int32, (rows16, LANE), 1) // _S5_SLOT
    for q in range(4):
        for qd in range(4):
            src = [y_ref[4 * q + s, :, qd * LANE:(qd + 1) * LANE] for s in range(4)]
            for t in range(4):
                acc = src[0]
                for s in range(1, 4):
                    acc = jnp.where(slot == (s + t) % 4, src[s], acc)
                if t:
                    acc = pltpu.roll(acc, LANE - _S5_SLOT * t, axis=1)
                nat_ref[q, pl.ds(4 * qd + t, rows16, stride=S5_T), :] = acc


def _inproj_kernel(x_ref, nw_ref, sc_ref, sh_ref, w_ref, tail_ref, o_ref, xs_ref, h_ref, nat_ref, *, rows16):
    j = pl.program_id(1)
    last = pl.num_programs(1) - 1

    def project(w):
        res = jnp.dot(h_ref[...], w, preferred_element_type=F32)
        o_ref[...] = res.astype(o_ref.dtype)
        return res

    @pl.when(j == 0)
    def _():
        x = x_ref[...]
        y = x * lax.rsqrt(jnp.mean(x * x, axis=-1, keepdims=True) + EPS) * nw_ref[...]
        h_ref[...] = (y * (1.0 + sc_ref[0]) + sh_ref[0]).astype(BF16)
        res = project(w_ref[...])
        for q in range(BR // LANE):
            nat_ref[q] = res[:, q * LANE:(q + 1) * LANE]
        _s5_to_chunk_rows(nat_ref, xs_ref, rows16)

    @pl.when((j > 0) & (j < last))
    def _():
        project(w_ref[...])

    @pl.when(j == last)
    def _():
        project(tail_ref[...])


def _inproj(x2, nw, sc, sh, weights, layer, rows_per_mod, tm=1024, tn=W_TILE):
    w_all, w_tail = weights
    m, d = x2.shape
    tm = min(tm, rows_per_mod)
    per = rows_per_mod // tm
    rows16 = tm // S5_T
    n_al = _ALIGNED // tn
    assert COL_S5_U == 0 and tn >= BR and tn == W_TILE
    return pl.pallas_call(
        functools.partial(_inproj_kernel, rows16=rows16),
        grid=(m // tm, NP // tn),
        in_specs=[pl.BlockSpec((tm, d), lambda i, j: (i, 0)),
                  pl.BlockSpec((1, d), lambda i, j: (0, 0)),
                  pl.BlockSpec((1, 1, d), lambda i, j: (i // per, 0, 0)),
                  pl.BlockSpec((1, 1, d), lambda i, j: (i // per, 0, 0)),
                  pl.BlockSpec((None, d, tn), lambda i, j: (layer, 0, jnp.minimum(j, n_al - 1))),
                  pl.BlockSpec((None, d, tn), lambda i, j: (layer, 0, 0))],
        out_specs=[pl.BlockSpec((tm, tn), lambda i, j: (i, j)),
                   pl.BlockSpec((_S5_PAIRS, rows16, _S5_ROW), lambda i, j: (0, i, 0))],
        out_shape=[jax.ShapeDtypeStruct((m, NP), BF16),
                   jax.ShapeDtypeStruct((_S5_PAIRS, m // S5_T, _S5_ROW), BF16)],
        scratch_shapes=[pltpu.VMEM((tm, d), BF16), pltpu.VMEM((BR // LANE, tm, LANE), F32)],
        compiler_params=_cparams(("parallel", "arbitrary")),
        name="inproj",
    )(x2, nw.reshape(1, d), sc, sh, w_all, w_tail)


def _outproj_kernel(x_ref, y0_ref, y1_ref, y2_ref, y3_ref, g0_ref, g1_ref, g2_ref, g3_ref, w_ref, gt_ref, fw_ref,
                    o_ref, *, final):
    acc = None
    for k, (y_ref, g_ref) in enumerate(((y0_ref, g0_ref), (y1_ref, g1_ref), (y2_ref, g2_ref), (y3_ref, g3_ref))):
        gated = (y_ref[...] * _silu(g_ref[...].astype(F32))).astype(BF16)
        part = jnp.dot(gated, w_ref[k * BR:(k + 1) * BR, :], preferred_element_type=F32)
        acc = part if acc is None else acc + part
    x = x_ref[...] + gt_ref[0] * acc
    if final:
        x = x * lax.rsqrt(jnp.mean(x * x, axis=-1, keepdims=True) + EPS) * fw_ref[...]
    o_ref[...] = x


def _outproj(x2, ys, p, wo_all, layer, gt, fw, rows_per_mod, final, tm=256):
    m, d = x2.shape
    tm = min(tm, rows_per_mod)
    per = rows_per_mod // tm
    yspec = pl.BlockSpec((tm, BR), lambda i: (i, 0))

    def gate(col_unit):
        blk = col_unit * LANE // BR
        return pl.BlockSpec((tm, BR), lambda i: (i, blk))

    return pl.pallas_call(
        functools.partial(_outproj_kernel, final=final),
        grid=(m // tm,),
        in_specs=[pl.BlockSpec((tm, d), lambda i: (i, 0)), yspec, yspec, yspec, yspec,
                  gate(COL_S5_G), gate(COL_HY_G), gate(COL_RET_G), gate(COL_GLA_G),
                  pl.BlockSpec((None, 4 * BR, d), lambda i: (layer, 0, 0)),
                  pl.BlockSpec((1, 1, d), lambda i: (i // per, 0, 0)),
                  pl.BlockSpec((1, d), lambda i: (0, 0))],
        out_specs=pl.BlockSpec((tm, d), lambda i: (i, 0)),
        out_shape=jax.ShapeDtypeStruct((m, d), F32),
        compiler_params=_cparams(("parallel",)),
        name="outproj",
    )(x2, *ys, p, p, p, p, wo_all, gt, fw.reshape(1, d))


def _s5_tables(a_re, a_im, log_dt, b_re, b_im, c_re, c_im, nch):
    hp = lax.Precision.HIGHEST
    t_len, g_n, p_n, s_n = S5_T, S5_GROUPS, S5_STATE, S5_GSIZE
    a_re, a_im = a_re.astype(F32), a_im.astype(F32)
    dt = jnp.exp(log_dt.astype(F32))[:, :, None]
    lam_re, lam_im = a_re * dt, a_im * dt

    def power(tau):
        tau = tau.astype(F32)[:, None, None, None]
        mag = jnp.exp(lam_re[None] * tau)
        return mag * jnp.cos(lam_im[None] * tau), mag * jnp.sin(lam_im[None] * tau)

    ab_re, ab_im = power(jnp.ones((1,)))
    ab_re, ab_im = ab_re[0], ab_im[0]
    den = a_re * a_re + a_im * a_im
    nr = ab_re - 1.0
    co_re = (nr * a_re + ab_im * a_im) / den
    co_im = (ab_im * a_re - nr * a_im) / den
    b_re, b_im = b_re.astype(F32), b_im.astype(F32)
    bco_re = co_re[..., None] * b_re - co_im[..., None] * b_im
    bco_im = co_re[..., None] * b_im + co_im[..., None] * b_re
    c_re, c_im = c_re.astype(F32), c_im.astype(F32)

    pr, pi = power(jnp.arange(t_len + 1))
    ca_re = c_re[None] * pr[:, :, :, None, :] - c_im[None] * pi[:, :, :, None, :]
    ca_im = c_re[None] * pi[:, :, :, None, :] + c_im[None] * pr[:, :, :, None, :]
    ca_cat = jnp.concatenate([ca_re[:t_len], -ca_im[:t_len]], axis=-1).transpose(1, 2, 0, 3, 4)
    bco_cat = jnp.concatenate([bco_re, bco_im], axis=-2)
    kk = jnp.einsum('dgmp,dgpj->dgmj', ca_cat.reshape(2, g_n, t_len * s_n, 2 * p_n), bco_cat, precision=hp)
    kk = kk.reshape(2, g_n, t_len, s_n, s_n).transpose(2, 0, 1, 3, 4)
    kf, kb = kk[:, 0], kk[:, 1]
    kfull = jnp.concatenate([kb[:0:-1], (kf[0] + kb[0])[None], kf[1:]], axis=0)
    kcat = kfull.transpose(1, 3, 0, 2).reshape(g_n, s_n, (2 * t_len - 1) * s_n)
    kcat = jnp.pad(kcat, ((0, 0), (0, 0), (0, _S5_ROW - kcat.shape[-1])))
    tt = jnp.arange(t_len)

    wexp = jnp.stack([pr[t_len - 1 - tt, 0], pr[tt, 1]], 0), jnp.stack([pi[t_len - 1 - tt, 0], pi[tt, 1]], 0)
    w_re = wexp[0][..., None] * bco_re[:, None] - wexp[1][..., None] * bco_im[:, None]
    w_im = wexp[0][..., None] * bco_im[:, None] + wexp[1][..., None] * bco_re[:, None]
    w4 = jnp.stack([w_re[0], w_im[0], w_re[1], w_im[1]], 0)
    wcat = w4.transpose(2, 1, 4, 0, 3).reshape(g_n, t_len * s_n, 4 * p_n)

    vexp_f, vexp_b = tt + 1, t_len - tt
    v4 = jnp.stack([ca_re[vexp_f, 0], -ca_im[vexp_f, 0], ca_re[vexp_b, 1], -ca_im[vexp_b, 1]], 0)
    vcat = v4.transpose(2, 0, 4, 1, 3).reshape(g_n, 4 * p_n, t_len * s_n)

    qr, qi = power(t_len * jnp.arange(nch + 1))
    pw = jnp.stack([qr, qi], 0)
    pw = pw.reshape(2, nch + 1, 2, g_n // 2, 2 * p_n).transpose(3, 2, 0, 1, 4)
    return _s5_assemble(kcat, wcat, vcat) + (pw,)


def _s5_assemble_kernel(kcat_ref, wcat_ref, vcat_ref, e_ref, oh_ref, m_ref, w_ref, v_ref):
    half = S5_T * S5_GSIZE
    oh = oh_ref[0]
    zeros = jnp.zeros((half, half), F32)
    diag = []
    for gl in range(2):
        kc = kcat_ref[gl]
        diag.append(jnp.concatenate(
            [kc[:, (S5_T - 1 - t) * S5_GSIZE:(S5_T - 1 - t) * S5_GSIZE + half] for t in range(S5_T)], axis=0))
    m_old = jnp.concatenate([jnp.concatenate([diag[0], zeros], axis=1),
                             jnp.concatenate([zeros, diag[1]], axis=1)], axis=0).astype(BF16)
    m_rows = jnp.dot(oh, m_old, preferred_element_type=F32).astype(BF16)
    m_ref[0] = _nt(m_rows, oh).astype(BF16)
    w_old = jnp.concatenate([jnp.dot(wcat_ref[gl].astype(BF16), e_ref[gl], preferred_element_type=F32)
                             for gl in range(2)], axis=0).astype(BF16)
    w_ref[0] = jnp.dot(oh, w_old, preferred_element_type=F32).astype(BF16)
    v_old = jnp.concatenate([_tn(e_ref[gl], vcat_ref[gl].astype(BF16)) for gl in range(2)], axis=1).astype(BF16)
    v_ref[0] = _nt(v_old, oh).astype(BF16)


def _s5_assemble(kcat, wcat, vcat):
    half = S5_T * S5_GSIZE
    onehot = jnp.asarray(np.eye(_S5_ROW, dtype=np.float32)[_s5_lane_perm()]).astype(BF16)
    place = np.zeros((2, half, _S5_ROW), np.float32)
    for gl in range(2):
        for k in range(4):
            for p in range(S5_STATE):
                place[gl, k * S5_STATE + p, (2 * k + gl) * S5_STATE + p] = 1.0
    out = jax.ShapeDtypeStruct((_S5_PAIRS, _S5_ROW, _S5_ROW), BF16)
    big = pl.BlockSpec((1, _S5_ROW, _S5_ROW), lambda g: (g, 0, 0))
    return pl.pallas_call(
        _s5_assemble_kernel,
        grid=(_S5_PAIRS,),
        in_specs=[pl.BlockSpec((2, S5_GSIZE, _S5_ROW), lambda g: (g, 0, 0)),
                  pl.BlockSpec((2, half, half), lambda g: (g, 0, 0)),
                  pl.BlockSpec((2, half, half), lambda g: (g, 0, 0)),
                  pl.BlockSpec((2, half, _S5_ROW), lambda g: (0, 0, 0)),
                  pl.BlockSpec((1, _S5_ROW, _S5_ROW), lambda g: (g % 4, 0, 0))],
        out_specs=[big, big, big],
        out_shape=[out, out, out],
        compiler_params=_cparams(("parallel",)),
        name="s5_assemble",
    )(kcat, wcat, vcat, jnp.asarray(place).astype(BF16), onehot)


def _nt(a, b):
    return lax.dot_general(a, b, (((1,), (1,)), ((), ())), preferred_element_type=F32)


def _tn(a, b):
    return lax.dot_general(a, b, (((0,), (0,)), ((), ())), preferred_element_type=F32)


def _cmul(ar, ai, br, bi):
    return ar * br - ai * bi, ar * bi + ai * br


def _s5_kernel(x_ref, m_ref, w_ref, v_ref, pw_ref, h0_ref, y_ref, fin_ref, s_ref, h_ref, *, nch):
    ln = 2 * S5_STATE
    x = x_ref[0]
    s_all = jnp.dot(x, w_ref[0], preferred_element_type=F32)
    for k in range(4):
        s_ref[k] = s_all[:, k * ln:(k + 1) * ln]
    a_rf, a_if = pw_ref[0, 0, 0, 1:2, :], pw_ref[0, 0, 1, 1:2, :]
    a_rb, a_ib = pw_ref[0, 1, 0, 1:2, :], pw_ref[0, 1, 1, 1:2, :]

    nb = 4

    def tile(half, c):
        return pl.ds(half * nch + c, nb, stride=2 * nch)

    def sweep(half_f, half_b, carry):
        def step(s, carry):
            hrf, hif, hrb, hib = carry
            rf, rb = tile(half_f, s), tile(half_b, nch - 1 - s)
            h_ref[0, rf, :] = hrf
            h_ref[1, rf, :] = hif
            h_ref[2, rb, :] = hrb
            h_ref[3, rb, :] = hib
            pr, pi = _cmul(a_rf, a_if, hrf, hif)
            qr, qi = _cmul(a_rb, a_ib, hrb, hib)
            return (pr + s_ref[0, rf, :], pi + s_ref[1, rf, :], qr + s_ref[2, rb, :], qi + s_ref[3, rb, :])
        return lax.fori_loop(0, nch, step, carry, unroll=2)

    lead_f, lead_b = pl.ds(0, nb, stride=2), pl.ds(1, nb, stride=2)
    init = (h0_ref[0, 0, 0, lead_f, :], h0_ref[0, 0, 1, lead_f, :],
            h0_ref[0, 1, 0, lead_b, :], h0_ref[0, 1, 1, lead_b, :])
    hrf, hif, hrb, hib = sweep(1, 0, sweep(0, 1, init))
    fin_ref[...] = jnp.zeros(fin_ref.shape, F32)
    fin_ref[0, 0, 0, lead_b, :] = hrf
    fin_ref[0, 0, 1, lead_b, :] = hif
    fin_ref[0, 1, 0, lead_f, :] = hrb
    fin_ref[0, 1, 1, lead_f, :] = hib

    h_all = jnp.concatenate([h_ref[k] for k in range(4)], axis=1).astype(BF16)
    y_ref[0] = (jnp.dot(x, m_ref[0], preferred_element_type=F32)
                + jnp.dot(h_all, v_ref[0], preferred_element_type=F32))


def _s5_core(xs, tables, h0):
    mpair, wpair, vpair, pw = tables
    gp_n, rows, wdt = xs.shape
    nch = rows // 8
    return pl.pallas_call(
        functools.partial(_s5_kernel, nch=nch),
        grid=(gp_n,),
        in_specs=[pl.BlockSpec((1, rows, wdt), lambda g: (g, 0, 0)),
                  pl.BlockSpec((1, wdt, wdt), lambda g: (g, 0, 0)),
                  pl.BlockSpec((1, wdt, wdt), lambda g: (g, 0, 0)),
                  pl.BlockSpec((1, wdt, wdt), lambda g: (g, 0, 0)),
                  pl.BlockSpec((1,) + pw.shape[1:], lambda g: (g, 0, 0, 0, 0)),
                  pl.BlockSpec((1, 2, 2, 8, 2 * S5_STATE), lambda g: (g, 0, 0, 0, 0))],
        out_specs=[pl.BlockSpec((1, rows, wdt), lambda g: (g, 0, 0)),
                   pl.BlockSpec((1, 2, 2, 8, 2 * S5_STATE), lambda g: (g, 0, 0, 0, 0))],
        out_shape=[jax.ShapeDtypeStruct((gp_n, rows, wdt), F32),
                   jax.ShapeDtypeStruct((gp_n, 2, 2, 8, 2 * S5_STATE), F32)],
        scratch_shapes=[pltpu.VMEM((4, rows, 2 * S5_STATE), F32), pltpu.VMEM((4, rows, 2 * S5_STATE), F32)],
        compiler_params=_cparams(("parallel",)),
        name="s5_core",
    )(xs, mpair, wpair, vpair, pw, h0)


def _s5_next_h0(fin):
    g, d, r, _, n = fin.shape
    sw = fin.reshape(g, d, r, 4, 2, n)[:, :, :, :, ::-1, :]
    keep = jnp.array([[1.0, 0.0], [0.0, 1.0]], F32)[None, :, None, None, :, None]
    return (sw * keep).reshape(fin.shape)


def _s5_glu_kernel(y_ref, u_ref, d_ref, w_ref, b_ref, o_ref, nat_ref, *, rows16):
    _s5_from_chunk_rows(y_ref, nat_ref, rows16)
    y = jnp.concatenate([nat_ref[q] for q in range(BR // LANE)], axis=1) + u_ref[...].astype(F32) * d_ref[...]
    g = jax.nn.gelu(y)
    z = jnp.dot(g.astype(BF16), w_ref[...], preferred_element_type=F32) + b_ref[...]
    o_ref[...] = g * jax.nn.sigmoid(z)


def _s5_glu(ys, p, d_skip, glu_w, glu_b, tm=512):
    m = p.shape[0]
    tm = min(tm, m)
    rows16 = tm // S5_T
    return pl.pallas_call(
        functools.partial(_s5_glu_kernel, rows16=rows16),
        grid=(m // tm,),
        in_specs=[pl.BlockSpec((_S5_PAIRS, rows16, _S5_ROW), lambda i: (0, i, 0)),
                  pl.BlockSpec((tm, BR), lambda i: (i, COL_S5_U * LANE // BR)),
                  pl.BlockSpec((1, BR), lambda i: (0, 0)),
                  pl.BlockSpec((BR, BR), lambda i: (0, 0)),
                  pl.BlockSpec((1, BR), lambda i: (0, 0))],
        out_specs=pl.BlockSpec((tm, BR), lambda i: (i, 0)),
        out_shape=jax.ShapeDtypeStruct((m, BR), F32),
        scratch_shapes=[pltpu.VMEM((BR // LANE, tm, LANE), F32)],
        compiler_params=_cparams(("parallel",)),
        name="s5_glu",
    )(ys, p, d_skip.reshape(1, BR), glu_w.astype(BF16), glu_b.reshape(1, BR))


@functools.lru_cache(maxsize=None)
def _fft_tables_np(n_total, n2_len):
    n1_len = n_total // n2_len
    n1h = n1_len // 2
    k1 = np.arange(n1_len, dtype=np.int64)
    n2 = np.arange(n2_len, dtype=np.int64)
    m = (k1[None, :, None] * k1[None, None, :] * n2_len + n2[:, None, None] * k1[None, :, None]) % n_total
    ang = m.astype(np.float64) * (2.0 * math.pi / n_total)
    c, s = np.cos(ang), np.sin(ang)
    ch, sh = c[:, :, :n1h], s[:, :, :n1h]
    gf = np.concatenate([np.concatenate([ch, sh], 2), np.concatenate([-sh, ch], 2)], 1)
    gk = np.concatenate([ch, -sh], 1)
    cht, sht = ch.transpose(0, 2, 1), sh.transpose(0, 2, 1)
    gi = np.concatenate([np.concatenate([cht, -sht], 2), np.concatenate([sht, cht], 2)], 1)
    a2 = ((n2[:, None] * n2[None, :]) % n2_len).astype(np.float64) * (2.0 * math.pi / n2_len)
    c2, s2 = np.cos(a2), np.sin(a2)
    f2 = np.concatenate([np.concatenate([c2, s2], 1), np.concatenate([-s2, c2], 1)], 0)
    f2c = np.concatenate([np.concatenate([c2, -s2], 1), np.concatenate([s2, c2], 1)], 0)
    return tuple(np.asarray(t, np.float32) for t in (gf, gk, gi, f2, f2c))


def _fft_tables(n_total, n2_len):
    return tuple(jnp.asarray(t).astype(BF16) for t in _fft_tables_np(n_total, n2_len))


def _hy_shortconv_kernel(x_ref, w_ref, b_ref, o_ref):
    x = x_ref[0].astype(F32)
    l = x.shape[0]
    row = lax.broadcasted_iota(jnp.int32, x.shape, 0)
    prev = jnp.where(row == 0, 0.0, pltpu.roll(x, 1, axis=0))
    nxt = jnp.where(row == l - 1, 0.0, pltpu.roll(x, l - 1, axis=0))
    o_ref[0] = w_ref[0:1, :] * prev + w_ref[1:2, :] * x + w_ref[2:3, :] * nxt + b_ref[...]


def _hy_shortconv(p3, conv_w, conv_b):
    bsz, l, _ = p3.shape
    units = 3 * BR // LANE
    return pl.pallas_call(
        _hy_shortconv_kernel,
        grid=(bsz, units),
        in_specs=[pl.BlockSpec((1, l, LANE), lambda b, u: (b, 0, COL_HY_X + u)),
                  pl.BlockSpec((3, LANE), lambda b, u: (0, u)),
                  pl.BlockSpec((1, LANE), lambda b, u: (0, u))],
        out_specs=pl.BlockSpec((1, l, LANE), lambda b, u: (b, 0, u)),
        out_shape=jax.ShapeDtypeStruct((bsz, l, 3 * BR), F32),
        compiler_params=_cparams(("parallel", "parallel")),
        name="hy_shortconv",
    )(p3, conv_w, conv_b.reshape(1, 3 * BR))


_GRP = 8


def _tile_rows(x3):
    t = jnp.swapaxes(x3, 0, 1)
    return [t[r] for r in range(_GRP)]


def _hy_kf_kernel(fwd_ref, bwd_ref, nrm_ref, gk_ref, f2_ref, o_ref, mid_ref, *, n1_len, n2_len):
    n1h = n1_len // 2
    ta = 2 * n1_len // _GRP
    row = lax.broadcasted_iota(jnp.int32, (n1h, LANE), 0)

    def stage1(m, _):
        cols = pl.ds(pl.multiple_of(m * _GRP, _GRP), _GRP)
        fs, bs = _tile_rows(fwd_ref[:, cols, :]), _tile_rows(bwd_ref[:, cols, :])
        bs[0] = jnp.where((row == 0) & (m == 0), 0.0, bs[0])
        for r in range(_GRP):
            n2 = m * _GRP + r
            a = jnp.dot(gk_ref[n2], jnp.concatenate([fs[r], bs[r]], axis=1).astype(BF16),
                        preferred_element_type=F32)
            mid_ref[0, pl.ds(n2 * ta, ta)] = a[:, :LANE].reshape(ta, _GRP, LANE)
            mid_ref[1, pl.ds(n2 * ta, ta)] = a[:, LANE:].reshape(ta, _GRP, LANE)
        return 0
    lax.fori_loop(0, n2_len // _GRP, stage1, 0)
    scale = nrm_ref[...] * (1.0 / (n1_len * n2_len))

    def stage2(mk, _):
        parts = [[_tile_rows(mid_ref[j, pl.ds(off + mk, n2_len, stride=ta)]) for off in (0, n1_len // _GRP)]
                 for j in range(2)]
        for r in range(_GRP):
            rhs = jnp.concatenate([jnp.concatenate([parts[j][0][r], parts[j][1][r]], axis=0) for j in range(2)],
                                  axis=1).astype(BF16)
            xs = jnp.dot(f2_ref[...], rhs, preferred_element_type=F32)
            f, b = xs[:, :LANE], xs[:, LANE:]
            o_ref[0, mk * _GRP + r] = (jnp.concatenate([f[:n2_len] + b[:n2_len], f[n2_len:] - b[n2_len:]], axis=0)
                                       * scale).astype(BF16)
        return 0
    lax.fori_loop(0, n1_len // _GRP, stage2, 0)


def _hy_kf(h2d, nrm, gk, f2, n2_len):
    l, cols = h2d.shape
    n1_len = 2 * l // n2_len
    n1h = n1_len // 2
    tiles = cols // (2 * LANE)
    h2d = h2d.reshape(n1h, n2_len, cols)
    return pl.pallas_call(
        functools.partial(_hy_kf_kernel, n1_len=n1_len, n2_len=n2_len),
        grid=(tiles,),
        in_specs=[pl.BlockSpec((n1h, n2_len, LANE), lambda t: (0, 0, t)),
                  pl.BlockSpec((n1h, n2_len, LANE), lambda t: (0, 0, tiles + t)),
                  pl.BlockSpec((1, LANE), lambda t: (0, t)),
                  pl.BlockSpec(gk.shape, lambda t: (0, 0, 0)),
                  pl.BlockSpec(f2.shape, lambda t: (0, 0))],
        out_specs=pl.BlockSpec((1, n1_len, 2 * n2_len, LANE), lambda t: (t, 0, 0, 0)),
        out_shape=jax.ShapeDtypeStruct((tiles, n1_len, 2 * n2_len, LANE), BF16),
        scratch_shapes=[pltpu.VMEM((2, 4 * l // _GRP, _GRP, LANE), F32)],
        compiler_params=_cparams(("parallel",)),
        name="hy_kf",
    )(h2d, h2d, nrm, gk, f2)


def _hy_conv_kernel(z_ref, g_ref, kf_ref, bias_ref, gf_ref, gi_ref, f2_ref, f2c_ref, o_ref,
                    mid_ref, spec_ref, *, n1_len, n2_len, order):
    n1h = n1_len // 2
    ta = 2 * n1_len // _GRP
    tb = 2 * n2_len // _GRP

    def stage1(m, _):
        zr = _tile_rows(z_ref[0, :, pl.ds(pl.multiple_of(m * _GRP, _GRP), _GRP), :])
        zi = _tile_rows(z_ref[1, :, pl.ds(pl.multiple_of(m * _GRP, _GRP), _GRP), :])
        for r in range(_GRP):
            n2 = m * _GRP + r
            rhs = jnp.concatenate([zr[r], zi[r]], axis=0).astype(BF16)
            a = jnp.dot(gf_ref[n2], rhs, preferred_element_type=F32)
            mid_ref[pl.ds(n2 * ta, ta)] = a.reshape(ta, _GRP, LANE)
        return 0
    lax.fori_loop(0, n2_len // _GRP, stage1, 0)

    def stage2(mk, _):
        xr = _tile_rows(mid_ref[pl.ds(mk, n2_len, stride=ta)])
        xi = _tile_rows(mid_ref[pl.ds(n1_len // _GRP + mk, n2_len, stride=ta)])
        for rp in range(_GRP // 2):
            cols = [jnp.concatenate([xr[2 * rp + j], xi[2 * rp + j]], axis=0) for j in range(2)]
            xs = jnp.dot(f2_ref[...], jnp.concatenate(cols, axis=1).astype(BF16), preferred_element_type=F32)
            for j in range(2):
                k1 = mk * _GRP + 2 * rp + j
                x = xs[:, j * LANE:(j + 1) * LANE]
                kf = kf_ref[0, k1].astype(F32)
                xre, xim = x[:n2_len], x[n2_len:]
                kr, ki = kf[:n2_len], kf[n2_len:]
                spec_ref[k1] = jnp.concatenate([xre * kr - xim * ki, xre * ki + xim * kr], axis=0).astype(BF16)
        return 0
    lax.fori_loop(0, n1_len // _GRP, stage2, 0)

    def stage2i(kp, _):
        r = jnp.concatenate([spec_ref[2 * kp], spec_ref[2 * kp + 1]], axis=1)
        b = jnp.dot(f2c_ref[...], r, preferred_element_type=F32)
        mid_ref[pl.ds(2 * kp * tb, tb)] = b[:, :LANE].reshape(tb, _GRP, LANE)
        mid_ref[pl.ds((2 * kp + 1) * tb, tb)] = b[:, LANE:].reshape(tb, _GRP, LANE)
        return 0
    lax.fori_loop(0, n1_len // 2, stage2i, 0, unroll=_FFT_UNROLL)

    def stage1i(m, _):
        br = _tile_rows(mid_ref[pl.ds(m, n1_len, stride=tb)])
        bi = _tile_rows(mid_ref[pl.ds(n2_len // _GRP + m, n1_len, stride=tb)])
        outs = []
        for r in range(_GRP):
            rhs = jnp.concatenate([br[r], bi[r]], axis=0).astype(BF16)
            outs.append(jnp.dot(gi_ref[m * _GRP + r], rhs, preferred_element_type=F32))
        cols = pl.ds(pl.multiple_of(m * _GRP, _GRP), _GRP)
        o_ref[0, :, cols, :] = jnp.swapaxes(jnp.stack([o[:n1h] for o in outs], axis=0), 0, 1)
        o_ref[1, :, cols, :] = jnp.swapaxes(jnp.stack([o[n1h:] for o in outs], axis=0), 0, 1)
        return 0
    lax.fori_loop(0, n2_len // _GRP, stage1i, 0)

    bias = bias_ref[order:order + 1, :]
    for j in range(2):
        o_ref[j] = g_ref[j] * (o_ref[j] + bias * z_ref[j])


def _hy_conv(z, zcol, g, gcol, kf, bias, tabs, n2_len, order):
    gf, _, gi, f2, f2c = tabs
    bsz, l, _ = z.shape
    n1_len = 2 * l // n2_len
    tiles = BR // LANE
    const2 = lambda *_: (0, 0)
    const3 = lambda *_: (0, 0, 0)
    one = pl.Buffered(1)
    n1h = n1_len // 2
    z = z.reshape(bsz, n1h, n2_len, z.shape[-1])
    g = g.reshape(bsz, n1h, n2_len, g.shape[-1])
    out = pl.pallas_call(
        functools.partial(_hy_conv_kernel, n1_len=n1_len, n2_len=n2_len, order=order),
        grid=(tiles, bsz // 2),
        in_specs=[pl.BlockSpec((2, n1h, n2_len, LANE), lambda t, b: (b, 0, 0, zcol + t)),
                  pl.BlockSpec((2, n1h, n2_len, LANE), lambda t, b: (b, 0, 0, gcol + t)),
                  pl.BlockSpec((1, n1_len, 2 * n2_len, LANE), lambda t, b: (order * tiles + t, 0, 0, 0),
                               pipeline_mode=one),
                  pl.BlockSpec((HY_ORDER, LANE), lambda t, b: (0, t)),
                  pl.BlockSpec(gf.shape, const3, pipeline_mode=one),
                  pl.BlockSpec(gi.shape, const3, pipeline_mode=one),
                  pl.BlockSpec(f2.shape, const2, pipeline_mode=one),
                  pl.BlockSpec(f2c.shape, const2, pipeline_mode=one)],
        out_specs=pl.BlockSpec((2, n1h, n2_len, LANE), lambda t, b: (b, 0, 0, t)),
        out_shape=jax.ShapeDtypeStruct((bsz, n1h, n2_len, BR), F32),
        scratch_shapes=[pltpu.VMEM((4 * l // _GRP, _GRP, LANE), F32),
                        pltpu.VMEM((n1_len, 2 * n2_len, LANE), BF16)],
        compiler_params=_cparams(("parallel", "arbitrary")),
        name="hy_conv%d" % order,
    )(z, g, kf, bias, gf, gi, f2, f2c)
    return out.reshape(bsz, l, BR)


def _hyena_filters(l, w1, b1, f1, w2, b2, f2, w3, b3, f3, w4):
    hp = lax.Precision.HIGHEST
    t = jnp.linspace(0.0, 1.0, l, dtype=F32)[:, None]
    wpos = 2.0 * math.pi * jnp.arange(l, dtype=F32)[:, None] / l
    f = jnp.linspace(1e-4, HY_BANDS - 1, HY_BANDS, dtype=F32)[None, :]
    z = jnp.concatenate([t, jnp.cos(f * wpos), -jnp.sin(f * wpos)], axis=-1)
    h = jnp.sin(f1 * (jnp.dot(z, w1, precision=hp) + b1))
    h = jnp.sin(f2 * (jnp.dot(h, w2, precision=hp) + b2))
    h = jnp.sin(f3 * (jnp.dot(h, w3, precision=hp) + b3))
    deltas = jnp.linspace(math.log(HY_TARGET) / HY_SLOW_DECAY,
                          math.log(HY_TARGET) / HY_FAST_DECAY, BR, dtype=F32)
    window = jnp.exp(-t * jnp.tile(jnp.abs(deltas), 2 * HY_ORDER)[None, :])
    h = jnp.dot(h, w4, precision=hp) * window
    half = HY_ORDER * BR
    row0 = (jnp.arange(l) == 0)[:, None]
    ss = (jnp.sum(h[:, :half] * h[:, :half], axis=0)
          + jnp.sum(jnp.where(row0, 0.0, h[:, half:] * h[:, half:]), axis=0))
    return h, lax.rsqrt(ss + EPS).reshape(1, half)


def _hyena(p3, conv_w, conv_b, filters, bias, n2_len):
    bsz, l, _ = p3.shape
    tabs = _fft_tables(2 * l, n2_len)
    xs = _hy_shortconv(p3, conv_w, conv_b)
    h2d, nrm = filters
    kf = _hy_kf(h2d, nrm, tabs[1], tabs[3], n2_len)
    tiles = BR // LANE
    z1 = _hy_conv(xs, 0, xs, tiles, kf, bias, tabs, n2_len, 0)
    return _hy_conv(z1, 0, xs, 2 * tiles, kf, bias, tabs, n2_len, 1)


def _ret_kernel(q_ref, k_ref, v_ref, cos_ref, sin_ref, lg_ref, nw_ref, s0_ref, o_ref, sfin_ref,
                vb_ref, qf_ref, qb_ref, kf_ref, kb_ref, *, nchunks, c):
    lg = lg_ref[0]
    jrow = lax.broadcasted_iota(jnp.int32, (c, c), 0)
    icol = lax.broadcasted_iota(jnp.int32, (c, c), 1)
    jf = lax.broadcasted_iota(jnp.int32, (c, RET_DK), 0).astype(F32)
    dsym = jnp.exp(jnp.abs(jrow - icol).astype(F32) * lg[:, 0:1]) * jnp.where(jrow == icol, 2.0, 1.0)
    dq_f = jnp.exp((jf + 1.0) * lg)
    dq_b = jnp.exp((c - jf) * lg)
    dk_f = jnp.exp((c - 1.0 - jf) * lg)
    dk_b = jnp.exp(jf * lg)
    dchunk = jnp.exp(c * lg)

    def rope(x, r0):
        return x * cos_ref[pl.ds(r0, c), :] + pltpu.roll(x, RET_DK // 2, axis=1) * sin_ref[pl.ds(r0, c), :]

    def intra_step(n, _):
        r0 = pl.multiple_of(n * c, c)
        rows = pl.ds(r0, c)
        q = rope(q_ref[0, rows, :].astype(F32), r0)
        k = rope(k_ref[0, rows, :].astype(F32), r0) * (RET_DK ** -0.5)
        v = v_ref[0, rows, :].astype(BF16)
        att = _nt(q.astype(BF16), k.astype(BF16)) * dsym
        o_ref[0, rows, :] = jnp.dot(att.astype(BF16), v, preferred_element_type=F32)
        vb_ref[rows, :] = v
        qf_ref[rows, :] = (q * dq_f).astype(BF16)
        qb_ref[rows, :] = (q * dq_b).astype(BF16)
        kf_ref[rows, :] = (k * dk_f).astype(BF16)
        kb_ref[rows, :] = (k * dk_b).astype(BF16)
        return 0

    lax.fori_loop(0, nchunks, intra_step, 0, unroll=_ATT_UNROLL)

    def sweep(lo, hi, carry, finish):
        def step(i, carry):
            s_f, s_b = carry
            rf = pl.ds(pl.multiple_of(i * c, c), c)
            rb = pl.ds(pl.multiple_of((nchunks - 1 - i) * c, c), c)
            o_f = o_ref[0, rf, :] + jnp.dot(qf_ref[rf, :], s_f.astype(BF16), preferred_element_type=F32)
            o_b = o_ref[0, rb, :] + jnp.dot(qb_ref[rb, :], s_b.astype(BF16), preferred_element_type=F32)
            if finish:
                o_f = o_f * lax.rsqrt(jnp.mean(o_f * o_f, axis=-1, keepdims=True) + EPS) * nw_ref[...]
                o_b = o_b * lax.rsqrt(jnp.mean(o_b * o_b, axis=-1, keepdims=True) + EPS) * nw_ref[...]
            o_ref[0, rf, :] = o_f
            o_ref[0, rb, :] = o_b
            return (dchunk * s_f + _tn(kf_ref[rf, :], vb_ref[rf, :]),
                    dchunk * s_b + _tn(kb_ref[rb, :], vb_ref[rb, :]))
        return lax.fori_loop(lo, hi, step, carry)

    carry = sweep(0, nchunks // 2, (s0_ref[0, 0, 0], s0_ref[0, 0, 1]), False)
    s_f, s_b = sweep(nchunks // 2, nchunks, carry, True)
    sfin_ref[0, 0, 0] = s_f
    sfin_ref[0, 0, 1] = s_b


def _retention(p3, cos2, sin2, norm_w, s0):
    bsz, l, _ = p3.shape
    lgs = jnp.log(1.0 - 2.0 ** (-5.0 - jnp.arange(RET_HEADS, dtype=F32)))
    lgs = jnp.broadcast_to(lgs[:, None, None], (RET_HEADS, 1, LANE))
    tab = pl.BlockSpec((l, LANE), lambda b, h: (0, 0))
    st = pl.BlockSpec((1, 1, 2, RET_DK, LANE), lambda b, h: (b, h, 0, 0, 0))
    chunk = RET_CHUNK if l % (2 * RET_CHUNK) == 0 else ATT_CHUNK
    return pl.pallas_call(
        functools.partial(_ret_kernel, nchunks=l // chunk, c=chunk),
        grid=(bsz, RET_HEADS),
        in_specs=[pl.BlockSpec((1, l, LANE), lambda b, h: (b, 0, COL_RET_Q + h)),
                  pl.BlockSpec((1, l, LANE), lambda b, h: (b, 0, COL_RET_K + h)),
                  pl.BlockSpec((1, l, LANE), lambda b, h: (b, 0, COL_RET_V + h)),
                  tab, tab,
                  pl.BlockSpec((1, 1, LANE), lambda b, h: (h, 0, 0)),
                  pl.BlockSpec((1, LANE), lambda b, h: (0, h)),
                  st],
        out_specs=[pl.BlockSpec((1, l, LANE), lambda b, h: (b, 0, h)), st],
        out_shape=[jax.ShapeDtypeStruct((bsz, l, BR), F32),
                   jax.ShapeDtypeStruct((bsz, RET_HEADS, 2, RET_DK, LANE), F32)],
        scratch_shapes=[pltpu.VMEM((l, LANE), BF16)] * 5,
        compiler_params=_cparams(("parallel", "parallel")),
        name="retention",
    )(p3, p3, p3, cos2, sin2, lgs, norm_w.reshape(1, BR), s0)


def _rope_tables(ang):
    cos, sin = jnp.cos(ang), jnp.sin(ang)
    return jnp.concatenate([cos, cos], -1), jnp.concatenate([-sin, sin], -1)


def _latent_angles(rows):
    half = RET_DK // 4
    inv = ROPE_BASE ** (-jnp.arange(half, dtype=F32) / half)
    r = jnp.repeat(jnp.arange(rows, dtype=F32), GRID_W)
    cl = jnp.tile(jnp.arange(GRID_W, dtype=F32), rows)
    return jnp.concatenate([r[:, None] * inv, cl[:, None] * inv], axis=-1)


def _ctx_angles(n_ctx):
    n = RET_DK // 2
    inv = ROPE_BASE ** (-jnp.arange(n, dtype=F32) / n)
    return jnp.arange(n_ctx, dtype=F32)[:, None] * inv


def _gla_kernel(q_ref, k_ref, v_ref, lr_ref, gw_ref, gb_ref, nw_ref, s0_ref, o_ref, sfin_ref,
                vb_ref, qf_ref, qb_ref, kf_ref, kb_ref, df_ref, db_ref, *, nchunks):
    c = ATT_CHUNK
    jrow = lax.broadcasted_iota(jnp.int32, (c, c), 0)
    icol = lax.broadcasted_iota(jnp.int32, (c, c), 1)
    tri_f = jnp.where(icol <= jrow, 1.0, 0.0).astype(BF16)
    tri_b = jnp.where(icol >= jrow, 1.0, 0.0).astype(BF16)
    lane = lax.broadcasted_iota(jnp.int32, (1, LANE), 1)
    head0 = lane < GLA_DK

    gw_both = jnp.concatenate([gw_ref[0], gw_ref[1]], axis=1)
    gb_both = jnp.concatenate([gb_ref[0], gb_ref[1]], axis=1)

    def decays(r0):
        x = jnp.dot(lr_ref[0, pl.ds(r0, c), :].astype(BF16), gw_both, preferred_element_type=F32) + gb_both
        la = (jnp.minimum(x, 0.0) - jnp.log(1.0 + jnp.exp(-jnp.abs(x)))) * (1.0 / GLA_TAU)
        hi = la.astype(BF16)
        lo = (la - hi.astype(F32)).astype(BF16)
        out = []
        for d, tri in enumerate((tri_f, tri_b)):
            cols = slice(d * LANE, (d + 1) * LANE)
            cs = jnp.dot(tri, jnp.concatenate([hi[:, cols], lo[:, cols]], axis=1), preferred_element_type=F32)
            out.append(cs[:, :LANE] + cs[:, LANE:])
        return out

    def intra_step(n, _):
        r0 = pl.multiple_of(n * c, c)
        rows = pl.ds(r0, c)
        q = q_ref[0, rows, :].astype(F32) * (GLA_DK ** -0.5)
        k = k_ref[0, rows, :].astype(F32)
        bf, bb = decays(r0)
        ref_f = bf[c // 2 - 1:c // 2, :]
        ref_b = bb[c // 2:c // 2 + 1, :]
        qf = q * jnp.exp(bf - ref_f)
        kf = (k * jnp.exp(ref_f - bf)).astype(BF16)
        qb = q * jnp.exp(bb - ref_b)
        kb = (k * jnp.exp(ref_b - bb)).astype(BF16)
        for h in range(2):
            hm = head0 if h == 0 else jnp.logical_not(head0)
            v = v_ref[0, rows, h * GLA_DV:(h + 1) * GLA_DV].astype(BF16)
            att = (jnp.where(icol <= jrow, _nt(jnp.where(hm, qf, 0.0).astype(BF16), kf), 0.0)
                   + jnp.where(icol >= jrow, _nt(jnp.where(hm, qb, 0.0).astype(BF16), kb), 0.0))
            o_ref[0, rows, h * GLA_DV:(h + 1) * GLA_DV] = jnp.dot(att.astype(BF16), v, preferred_element_type=F32)
            vb_ref[rows, h * GLA_DV:(h + 1) * GLA_DV] = v
        bf_last, bb_last = bf[c - 1:c, :], bb[0:1, :]
        qf_ref[rows, :] = (q * jnp.exp(bf)).astype(BF16)
        qb_ref[rows, :] = (q * jnp.exp(bb)).astype(BF16)
        kf_ref[rows, :] = (k * jnp.exp(bf_last - bf)).astype(BF16)
        kb_ref[rows, :] = (k * jnp.exp(bb_last - bb)).astype(BF16)
        d0 = pl.multiple_of(n * 8, 8)
        df_ref[pl.ds(d0, 8), :] = jnp.broadcast_to(jnp.exp(bf_last), (8, LANE))
        db_ref[pl.ds(d0, 8), :] = jnp.broadcast_to(jnp.exp(bb_last), (8, LANE))
        return 0

    lax.fori_loop(0, nchunks, intra_step, 0, unroll=_ATT_UNROLL)

    def sweep(lo, hi, carry, finish):
        def step(i, carry):
            sts = list(carry)
            new = []
            for d in range(2):
                n = i if d == 0 else nchunks - 1 - i
                rows = pl.ds(pl.multiple_of(n * c, c), c)
                qs = (qf_ref if d == 0 else qb_ref)[rows, :]
                kd = (kf_ref if d == 0 else kb_ref)[rows, :]
                dec = (df_ref if d == 0 else db_ref)[pl.ds(pl.multiple_of(n * 8, 8), 1), :]
                for h in range(2):
                    hm = head0 if h == 0 else jnp.logical_not(head0)
                    cols = slice(h * GLA_DV, (h + 1) * GLA_DV)
                    st = sts[2 * d + h]
                    o = o_ref[0, rows, cols] + _nt(qs, st.astype(BF16))
                    if finish:
                        o = o * lax.rsqrt(jnp.mean(o * o, axis=-1, keepdims=True) + EPS) * nw_ref[:, cols]
                    o_ref[0, rows, cols] = o
                    new.append(jnp.where(hm, dec * st + _tn(vb_ref[rows, cols], kd), 0.0))
            return tuple(new)
        return lax.fori_loop(lo, hi, step, carry)

    carry = (s0_ref[0, 0, 0, 0], s0_ref[0, 0, 0, 1], s0_ref[0, 0, 1, 0], s0_ref[0, 0, 1, 1])
    carry = sweep(0, nchunks // 2, carry, False)
    s_fin = sweep(nchunks // 2, nchunks, carry, True)
    sfin_ref[0, 0, 0, 0] = s_fin[0]
    sfin_ref[0, 0, 0, 1] = s_fin[1]
    sfin_ref[0, 0, 1, 0] = s_fin[2]
    sfin_ref[0, 0, 1, 1] = s_fin[3]


def _gla(p3, gate_w, gate_b, norm_w, s0):
    bsz, l, _ = p3.shape
    pairs = GLA_HEADS // 2
    gw = jnp.zeros((2, LANE, GLA_HEADS * GLA_DK), F32)
    for d in range(2):
        gw = gw.at[d, d * GLA_LR:(d + 1) * GLA_LR, :].set(gate_w[d].astype(F32))
    st = pl.BlockSpec((1, 1, 2, 2, GLA_DV, LANE), lambda b, h: (b, h, 0, 0, 0, 0))
    return pl.pallas_call(
        functools.partial(_gla_kernel, nchunks=l // ATT_CHUNK),
        grid=(bsz, pairs),
        in_specs=[pl.BlockSpec((1, l, LANE), lambda b, h: (b, 0, COL_GLA_Q + h)),
                  pl.BlockSpec((1, l, LANE), lambda b, h: (b, 0, COL_GLA_K + h)),
                  pl.BlockSpec((1, l, 2 * GLA_DV), lambda b, h: (b, 0, COL_GLA_V // 2 + h)),
                  pl.BlockSpec((1, l, LANE), lambda b, h: (b, 0, COL_GLA_LR)),
                  pl.BlockSpec((2, LANE, LANE), lambda b, h: (0, 0, h)),
                  pl.BlockSpec((2, 1, LANE), lambda b, h: (0, 0, h)),
                  pl.BlockSpec((1, 2 * GLA_DV), lambda b, h: (0, h)),
                  st],
        out_specs=[pl.BlockSpec((1, l, 2 * GLA_DV), lambda b, h: (b, 0, h)), st],
        out_shape=[jax.ShapeDtypeStruct((bsz, l, BR), F32),
                   jax.ShapeDtypeStruct((bsz, pairs, 2, 2, GLA_DV, LANE), F32)],
        scratch_shapes=([pltpu.VMEM((l, 2 * GLA_DV), BF16)] + [pltpu.VMEM((l, LANE), BF16)] * 4
                        + [pltpu.VMEM((8 * l // ATT_CHUNK, LANE), F32)] * 2),
        compiler_params=_cparams(("parallel", "parallel")),
        name="gla",
    )(p3, p3, p3, p3, gw.astype(BF16), gate_b.astype(F32).reshape(2, 1, GLA_HEADS * GLA_DK),
      norm_w.reshape(1, BR), s0)


def _mixers(p2, xs, bsz, prm, i, s5_tabs, hy_filters, rope, s5_h0, ret_s0, gla_s0, with_hyena):
    m = p2.shape[0]
    l = m // bsz
    p3 = p2.reshape(bsz, l, NP)
    y, s5_fin = _s5_core(xs, s5_tabs, s5_h0)
    s5 = _s5_glu(y, p2, prm['s5_d'][i], prm['s5_glu_w'][i], prm['s5_glu_b'][i])
    hy = None
    if with_hyena:
        hy = _hyena(p3, prm['hy_conv_w'][i], prm['hy_conv_b'][i], hy_filters, prm['hy_bias'][i].astype(F32),
                    64 if l >= 512 else 16).reshape(m, BR)
    ret, ret_fin = _retention(p3, rope[0], rope[1], prm['ret_norm_w'][i], ret_s0)
    gla, gla_fin = _gla(p3, prm['gla_gate_w'][i], prm['gla_gate_b'][i], prm['gla_norm_w'][i], gla_s0)
    return (s5, hy, ret.reshape(m, BR), gla.reshape(m, BR)), (s5_fin, ret_fin, gla_fin)


def kernel(x, c, ctx, c_ctx, norm_w, ada_w, ada_b, w_in, w_out, s5_a_re, s5_a_im, s5_log_dt, s5_b_re, s5_b_im, s5_c_re, s5_c_im, s5_d, s5_glu_w, s5_glu_b, hy_conv_w, hy_conv_b, hy_w1, hy_b1, hy_f1, hy_w2, hy_b2, hy_f2, hy_w3, hy_b3, hy_f3, hy_w4, hy_bias, ret_norm_w, gla_gate_w, gla_gate_b, gla_norm_w, final_norm_w):
    prm = dict(s5_a_re=s5_a_re, s5_a_im=s5_a_im, s5_log_dt=s5_log_dt, s5_b_re=s5_b_re, s5_b_im=s5_b_im,
               s5_c_re=s5_c_re, s5_c_im=s5_c_im, s5_d=s5_d, s5_glu_w=s5_glu_w, s5_glu_b=s5_glu_b,
               hy_conv_w=hy_conv_w, hy_conv_b=hy_conv_b, hy_w1=hy_w1, hy_b1=hy_b1, hy_f1=hy_f1,
               hy_w2=hy_w2, hy_b2=hy_b2, hy_f2=hy_f2, hy_w3=hy_w3, hy_b3=hy_b3, hy_f3=hy_f3,
               hy_w4=hy_w4, hy_bias=hy_bias, ret_norm_w=ret_norm_w, gla_gate_w=gla_gate_w,
               gla_gate_b=gla_gate_b, gla_norm_w=gla_norm_w)
    bsz, l, d = x.shape
    lc = ctx.shape[1]
    depth = w_in.shape[0]
    cc = jnp.concatenate([c, c_ctx[None, :], jnp.zeros((8 - bsz - 1, d), F32)], axis=0)
    mod = _ada(cc, ada_w, ada_b)
    rope_l = _rope_tables(_latent_angles(l // GRID_W))
    rope_c = _rope_tables(_ctx_angles(lc))
    s5_zero = jnp.zeros((S5_GROUPS // 2, 2, 2, 8, 2 * S5_STATE), F32)
    ret_zero = jnp.zeros((bsz, RET_HEADS, 2, RET_DK, LANE), F32)
    gla_zero = jnp.zeros((bsz, GLA_HEADS // 2, 2, 2, GLA_DV, LANE), F32)

    w_all = _wprep(w_in)
    wo_all = w_out.astype(BF16)
    filt = (hy_w1, hy_b1, hy_f1, hy_w2, hy_b2, hy_f2, hy_w3, hy_b3, hy_f3, hy_w4)

    x_l = x.reshape(bsz * l, d)
    x_c = ctx.reshape(bsz * lc, d)
    for i in range(depth):
        last = i == depth - 1
        sh, sc, gt = mod[i, :, :d], mod[i, :, d:2 * d], mod[i, :, 2 * d:]
        p_c, xs_c = _inproj(x_c, norm_w[i], sc[bsz:bsz + 1, None, :], sh[bsz:bsz + 1, None, :], w_all, i,
                            bsz * lc)
        p_l, xs_l = _inproj(x_l, norm_w[i], sc[:bsz, None, :], sh[:bsz, None, :], w_all, i, l)
        s5_tabs = _s5_tables(s5_a_re[i], s5_a_im[i], s5_log_dt[i], s5_b_re[i], s5_b_im[i], s5_c_re[i],
                             s5_c_im[i], 1)
        filt_i = [w[i].astype(F32) for w in filt]
        ys_c, (s5_fin, ret_fin, gla_fin) = _mixers(
            p_c, xs_c, bsz, prm, i, s5_tabs, None if last else _hyena_filters(lc, *filt_i), rope_c, s5_zero,
            ret_zero, gla_zero, with_hyena=not last)
        ys_l, _ = _mixers(p_l, xs_l, bsz, prm, i, s5_tabs, _hyena_filters(l, *filt_i), rope_l,
                          _s5_next_h0(s5_fin), ret_fin, gla_fin, with_hyena=True)
        x_l = _outproj(x_l, ys_l, p_l, wo_all, i, gt[:bsz, None, :], final_norm_w, l, final=last)
        if not last:
            x_c = _outproj(x_c, ys_c, p_c, wo_all, i, gt[bsz:bsz + 1, None, :], final_norm_w, bsz * lc,
                           final=False)
    return x_l.reshape(bsz, l, d)
```

```python
import functools
import math

import numpy as np
import jax
import jax.numpy as jnp
from jax import lax
from jax.experimental import pallas as pl
from jax.experimental.pallas import tpu as pltpu

F32 = jnp.float32
BF16 = jnp.bfloat16

EPS = 1e-6
GRID_W = 64
BR = 512
S5_GSIZE = 16
S5_GROUPS = BR // S5_GSIZE
S5_STATE = 64
S5_T = 16
HY_ORDER = 2
HY_EMB = 33
HY_BANDS = (HY_EMB - 1) // 2
HY_FAST_DECAY = 0.3
HY_SLOW_DECAY = 1.5
HY_TARGET = 1e-2
RET_HEADS = 4
RET_DK = 128
GLA_HEADS = 4
GLA_DK = 64
GLA_DV = 128
GLA_LR = 16
GLA_TAU = 16.0
ROPE_BASE = 10000.0
ATT_CHUNK = 128
RET_CHUNK = 256
_FFT_UNROLL = 8
_ATT_UNROLL = 2

LANE = 128
VMEM_LIMIT = 56 * 1024 * 1024

COL_S5_U = 0
COL_S5_G = 4
COL_HY_X = 8
COL_HY_G = 20
COL_RET_Q = 24
COL_RET_K = 28
COL_RET_V = 32
COL_RET_G = 36
COL_GLA_Q = 40
COL_GLA_K = 42
COL_GLA_V = 44
COL_GLA_G = 48
COL_GLA_LR = 52
NP_UNITS = 54
NP = NP_UNITS * LANE
W_TILE = 768
_ALIGNED = COL_GLA_G * LANE


def _cparams(sem):
    return pltpu.CompilerParams(dimension_semantics=sem, vmem_limit_bytes=VMEM_LIMIT)


def _silu(x):
    return x * jax.nn.sigmoid(x)


def _wprep(w_in):
    depth, d, in_w = w_in.shape
    assert _ALIGNED % W_TILE == 0 and in_w == _ALIGNED + 2 * GLA_LR + BR and NP == _ALIGNED + W_TILE
    wb = w_in.astype(BF16)
    tail = jnp.concatenate([wb[:, :, _ALIGNED + 2 * GLA_LR:], wb[:, :, _ALIGNED:_ALIGNED + 2 * GLA_LR],
                            jnp.zeros((depth, d, W_TILE - BR - 2 * GLA_LR), BF16)], axis=-1)
    return wb, tail


def _ada_kernel(c_ref, w_ref, b_ref, o_ref):
    a = _silu(c_ref[...]).astype(BF16)
    o_ref[0] = jnp.dot(a, w_ref[0].astype(BF16), preferred_element_type=F32) + b_ref[0]


def _ada(cc, ada_w, ada_b, tn=768):
    depth, d, n = ada_w.shape
    return pl.pallas_call(
        _ada_kernel,
        grid=(depth, n // tn),
        in_specs=[pl.BlockSpec((8, d), lambda i, j: (0, 0)),
                  pl.BlockSpec((1, d, tn), lambda i, j: (i, 0, j)),
                  pl.BlockSpec((1, 1, tn), lambda i, j: (i, 0, j))],
        out_specs=pl.BlockSpec((1, 8, tn), lambda i, j: (i, 0, j)),
        out_shape=jax.ShapeDtypeStruct((depth, 8, n), F32),
        compiler_params=_cparams(("parallel", "parallel")),
        name="ada_mod",
    )(cc, ada_w, ada_b.reshape(depth, 1, n))


_S5_SLOT = 2 * S5_GSIZE
_S5_PAIRS = S5_GROUPS // 2
_S5_ROW = S5_T * _S5_SLOT


def _s5_lane_perm():
    perm = np.zeros((4, _S5_ROW), np.int32)
    for s in range(4):
        for tau in range(S5_T):
            for c in range(_S5_SLOT):
                gl, i = divmod(c, S5_GSIZE)
                perm[s, LANE * (tau // 4) + _S5_SLOT * ((s + tau) % 4) + c] = (gl * S5_T + tau) * S5_GSIZE + i
    return perm


def _s5_to_chunk_rows(nat_ref, xs_ref, rows16):
    slot = lax.broadcasted_iota(jnp.int32, (rows16, LANE), 1) // _S5_SLOT
    for q in range(4):
        for qd in range(4):
            rolled = []
            for t in range(4):
                piece = nat_ref[q, pl.ds(4 * qd + t, rows16, stride=S5_T), :]
                rolled.append(pltpu.roll(piece, _S5_SLOT * t, axis=1) if t else piece)
            for s in range(4):
                acc = rolled[0]
                for t in range(1, 4):
                    acc = jnp.where(slot == (s + t) % 4, rolled[t], acc)
                xs_ref[4 * q + s, :, qd * LANE:(qd + 1) * LANE] = acc.astype(xs_ref.dtype)


def _s5_from_chunk_rows(y_ref, nat_ref, rows16):
    slot = lax.broadcasted_iota(jnp.int32, (rows16, LANE), 1) // _S5_SLOT
    for q in range(4):
        for qd in range(4):
            src = [y_ref[4 * q + s, :, qd * LANE:(qd + 1) * LANE] for s in range(4)]
            for t in range(4):
                acc = src[0]
                for s in range(1, 4):
                    acc = jnp.where(slot == (s + t) % 4, src[s], acc)
                if t:
                    acc = pltpu.roll(acc, LANE - _S5_SLOT * t, axis=1)
                nat_ref[q, pl.ds(4 * qd + t, rows16, stride=S5_T), :] = acc


def _inproj_kernel(x_ref, nw_ref, sc_ref, sh_ref, w_ref, tail_ref, o_ref, xs_ref, h_ref, nat_ref, *, rows16):
    j = pl.program_id(1)
    last = pl.num_programs(1) - 1

    def project(w):
        res = jnp.dot(h_ref[...], w, preferred_element_type=F32)
        o_ref[...] = res.astype(o_ref.dtype)
        return res

    @pl.when(j == 0)
    def _():
        x = x_ref[...]
        y = x * lax.rsqrt(jnp.mean(x * x, axis=-1, keepdims=True) + EPS) * nw_ref[...]
        h_ref[...] = (y * (1.0 + sc_ref[0]) + sh_ref[0]).astype(BF16)
        res = project(w_ref[...])
        for q in range(BR // LANE):
            nat_ref[q] = res[:, q * LANE:(q + 1) * LANE]
        _s5_to_chunk_rows(nat_ref, xs_ref, rows16)

    @pl.when((j > 0) & (j < last))
    def _():
        project(w_ref[...])

    @pl.when(j == last)
    def _():
        project(tail_ref[...])


def _inproj(x2, nw, sc, sh, weights, layer, rows_per_mod, tm=1024, tn=W_TILE):
    w_all, w_tail = weights
    m, d = x2.shape
    tm = min(tm, rows_per_mod)
    per = rows_per_mod // tm
    rows16 = tm // S5_T
    n_al = _ALIGNED // tn
    assert COL_S5_U == 0 and tn >= BR and tn == W_TILE
    return pl.pallas_call(
        functools.partial(_inproj_kernel, rows16=rows16),
        grid=(m // tm, NP // tn),
        in_specs=[pl.BlockSpec((tm, d), lambda i, j: (i, 0)),
                  pl.BlockSpec((1, d), lambda i, j: (0, 0)),
                  pl.BlockSpec((1, 1, d), lambda i, j: (i // per, 0, 0)),
                  pl.BlockSpec((1, 1, d), lambda i, j: (i // per, 0, 0)),
                  pl.BlockSpec((None, d, tn), lambda i, j: (layer, 0, jnp.minimum(j, n_al - 1))),
                  pl.BlockSpec((None, d, tn), lambda i, j: (layer, 0, 0))],
        out_specs=[pl.BlockSpec((tm, tn), lambda i, j: (i, j)),
                   pl.BlockSpec((_S5_PAIRS, rows16, _S5_ROW), lambda i, j: (0, i, 0))],
        out_shape=[jax.ShapeDtypeStruct((m, NP), BF16),
                   jax.ShapeDtypeStruct((_S5_PAIRS, m // S5_T, _S5_ROW), BF16)],
        scratch_shapes=[pltpu.VMEM((tm, d), BF16), pltpu.VMEM((BR // LANE, tm, LANE), F32)],
        compiler_params=_cparams(("parallel", "arbitrary")),
        name="inproj",
    )(x2, nw.reshape(1, d), sc, sh, w_all, w_tail)


def _outproj_kernel(x_ref, y0_ref, y1_ref, y2_ref, y3_ref, g0_ref, g1_ref, g2_ref, g3_ref, w_ref, gt_ref, fw_ref,
                    o_ref, *, final):
    acc = None
    for k, (y_ref, g_ref) in enumerate(((y0_ref, g0_ref), (y1_ref, g1_ref), (y2_ref, g2_ref), (y3_ref, g3_ref))):
        gated = (y_ref[...] * _silu(g_ref[...].astype(F32))).astype(BF16)
        part = jnp.dot(gated, w_ref[k * BR:(k + 1) * BR, :], preferred_element_type=F32)
        acc = part if acc is None else acc + part
    x = x_ref[...] + gt_ref[0] * acc
    if final:
        x = x * lax.rsqrt(jnp.mean(x * x, axis=-1, keepdims=True) + EPS) * fw_ref[...]
    o_ref[...] = x


def _outproj(x2, ys, p, wo_all, layer, gt, fw, rows_per_mod, final, tm=256):
    m, d = x2.shape
    tm = min(tm, rows_per_mod)
    per = rows_per_mod // tm
    yspec = pl.BlockSpec((tm, BR), lambda i: (i, 0))

    def gate(col_unit):
        blk = col_unit * LANE // BR
        return pl.BlockSpec((tm, BR), lambda i: (i, blk))

    return pl.pallas_call(
        functools.partial(_outproj_kernel, final=final),
        grid=(m // tm,),
        in_specs=[pl.BlockSpec((tm, d), lambda i: (i, 0)), yspec, yspec, yspec, yspec,
                  gate(COL_S5_G), gate(COL_HY_G), gate(COL_RET_G), gate(COL_GLA_G),
                  pl.BlockSpec((None, 4 * BR, d), lambda i: (layer, 0, 0)),
                  pl.BlockSpec((1, 1, d), lambda i: (i // per, 0, 0)),
                  pl.BlockSpec((1, d), lambda i: (0, 0))],
        out_specs=pl.BlockSpec((tm, d), lambda i: (i, 0)),
        out_shape=jax.ShapeDtypeStruct((m, d), F32),
        compiler_params=_cparams(("parallel",)),
        name="outproj",
    )(x2, *ys, p, p, p, p, wo_all, gt, fw.reshape(1, d))


def _s5_tables(a_re, a_im, log_dt, b_re, b_im, c_re, c_im):
    hp = lax.Precision.HIGHEST
    t_len, g_n, p_n, s_n = S5_T, S5_GROUPS, S5_STATE, S5_GSIZE
    a_re, a_im = a_re.astype(F32), a_im.astype(F32)
    dt = jnp.exp(log_dt.astype(F32))[:, :, None]
    lam_re, lam_im = a_re * dt, a_im * dt

    tau = jnp.arange(t_len + 1, dtype=F32)[None, None, :, None]
    mag = jnp.exp(lam_re[:, :, None, :] * tau)
    pr, pi = mag * jnp.cos(lam_im[:, :, None, :] * tau), mag * jnp.sin(lam_im[:, :, None, :] * tau)
    ab_re, ab_im = pr[:, :, 1], pi[:, :, 1]
    den = a_re * a_re + a_im * a_im
    nr = ab_re - 1.0
    co_re = (nr * a_re + ab_im * a_im) / den
    co_im = (ab_im * a_re - nr * a_im) / den
    b_re, b_im = b_re.astype(F32), b_im.astype(F32)
    bco_re = co_re[..., None] * b_re - co_im[..., None] * b_im
    bco_im = co_re[..., None] * b_im + co_im[..., None] * b_re
    c_re, c_im = c_re.astype(F32), c_im.astype(F32)

    ca_re = c_re[:, :, None] * pr[:, :, :, None, :] - c_im[:, :, None] * pi[:, :, :, None, :]
    ca_im = c_re[:, :, None] * pi[:, :, :, None, :] + c_im[:, :, None] * pr[:, :, :, None, :]
    ca_cat = jnp.concatenate([ca_re[:, :, :t_len], -ca_im[:, :, :t_len]], axis=-1)
    bco_cat = jnp.concatenate([bco_re, bco_im], axis=-2)
    kk = jnp.einsum('dgmp,dgpj->dgmj', ca_cat.reshape(2, g_n, t_len * s_n, 2 * p_n), bco_cat, precision=hp)
    kk = kk.reshape(2, g_n, t_len, s_n, s_n)
    kf, kb = kk[0], kk[1]
    kfull = jnp.concatenate([kb[:, :0:-1], (kf[:, 0] + kb[:, 0])[:, None], kf[:, 1:]], axis=1)
    kcat = kfull.transpose(0, 3, 1, 2).reshape(g_n, s_n, (2 * t_len - 1) * s_n)
    kcat = jnp.pad(kcat, ((0, 0), (0, 0), (0, _S5_ROW - kcat.shape[-1])))

    we_re = jnp.stack([pr[0, :, :t_len][:, ::-1], pr[1, :, :t_len]], 0)
    we_im = jnp.stack([pi[0, :, :t_len][:, ::-1], pi[1, :, :t_len]], 0)
    bt_re, bt_im = bco_re.transpose(0, 1, 3, 2)[:, :, None], bco_im.transpose(0, 1, 3, 2)[:, :, None]
    w_re = we_re[:, :, :, None, :] * bt_re - we_im[:, :, :, None, :] * bt_im
    w_im = we_re[:, :, :, None, :] * bt_im + we_im[:, :, :, None, :] * bt_re
    wcat = jnp.concatenate([w_re[0], w_im[0], w_re[1], w_im[1]], axis=-1).reshape(g_n, t_len * s_n, 4 * p_n)

    vcat_t = jnp.concatenate([ca_re[0, :, 1:], -ca_im[0, :, 1:], ca_re[1, :, 1:][:, ::-1], -ca_im[1, :, 1:][:, ::-1]],
                             axis=-1).reshape(g_n, t_len * s_n, 4 * p_n)

    pw = jnp.stack([pr[:, :, t_len], pi[:, :, t_len]], 0).reshape(2, 2, g_n // 2, 1, 2 * p_n).transpose(2, 1, 0, 3, 4)
    return _s5_assemble(kcat, wcat, vcat_t) + (pw,)


def _s5_assemble_kernel(kcat_ref, wcat_ref, vcat_ref, e_ref, oh_ref, oht_ref, m_ref, w_ref, v_ref):
    half = S5_T * S5_GSIZE
    oh = oh_ref[0]
    zeros = jnp.zeros((half, half), F32)
    diag = []
    for gl in range(2):
        kc = kcat_ref[gl]
        diag.append(jnp.concatenate(
            [kc[:, (S5_T - 1 - t) * S5_GSIZE:(S5_T - 1 - t) * S5_GSIZE + half] for t in range(S5_T)], axis=0))
    m_old = jnp.concatenate([jnp.concatenate([diag[0], zeros], axis=1),
                             jnp.concatenate([zeros, diag[1]], axis=1)], axis=0).astype(BF16)
    m_rows = jnp.dot(oh, m_old, preferred_element_type=F32).astype(BF16)
    m_ref[0] = _nt(m_rows, oh).astype(BF16)
    w_old = jnp.concatenate([jnp.dot(wcat_ref[gl].astype(BF16), e_ref[gl], preferred_element_type=F32)
                             for gl in range(2)], axis=0).astype(BF16)
    w_ref[0] = jnp.dot(oh, w_old, preferred_element_type=F32).astype(BF16)
    v_old_t = jnp.concatenate([jnp.dot(vcat_ref[gl].astype(BF16), e_ref[gl], preferred_element_type=F32)
                               for gl in range(2)], axis=0).astype(BF16)
    v_ref[0] = _tn(v_old_t, oht_ref[0]).astype(BF16)


def _s5_assemble(kcat, wcat, vcat_t):
    half = S5_T * S5_GSIZE
    perm = np.eye(_S5_ROW, dtype=np.float32)[_s5_lane_perm()]
    onehot = jnp.asarray(perm).astype(BF16)
    onehot_t = jnp.asarray(np.ascontiguousarray(perm.transpose(0, 2, 1))).astype(BF16)
    place = np.zeros((2, half, _S5_ROW), np.float32)
    for gl in range(2):
        for k in range(4):
            for p in range(S5_STATE):
                place[gl, k * S5_STATE + p, (2 * k + gl) * S5_STATE + p] = 1.0
    out = jax.ShapeDtypeStruct((_S5_PAIRS, _S5_ROW, _S5_ROW), BF16)
    big = pl.BlockSpec((1, _S5_ROW, _S5_ROW), lambda g: (g, 0, 0))
    return pl.pallas_call(
        _s5_assemble_kernel,
        grid=(_S5_PAIRS,),
        in_specs=[pl.BlockSpec((2, S5_GSIZE, _S5_ROW), lambda g: (g, 0, 0)),
                  pl.BlockSpec((2, half, half), lambda g: (g, 0, 0)),
                  pl.BlockSpec((2, half, half), lambda g: (g, 0, 0)),
                  pl.BlockSpec((2, half, _S5_ROW), lambda g: (0, 0, 0)),
                  pl.BlockSpec((1, _S5_ROW, _S5_ROW), lambda g: (g % 4, 0, 0)),
                  pl.BlockSpec((1, _S5_ROW, _S5_ROW), lambda g: (g % 4, 0, 0))],
        out_specs=[big, big, big],
        out_shape=[out, out, out],
        compiler_params=_cparams(("parallel",)),
        name="s5_assemble",
    )(kcat, wcat, vcat_t, jnp.asarray(place).astype(BF16), onehot, onehot_t)


def _nt(a, b):
    return lax.dot_general(a, b, (((1,), (1,)), ((), ())), preferred_element_type=F32)


def _tn(a, b):
    return lax.dot_general(a, b, (((0,), (0,)), ((), ())), preferred_element_type=F32)


def _cmul(ar, ai, br, bi):
    return ar * br - ai * bi, ar * bi + ai * br


def _s5_kernel(x_ref, m_ref, w_ref, v_ref, pw_ref, h0_ref, y_ref, fin_ref, s_ref, h_ref, *, nch):
    ln = 2 * S5_STATE
    x = x_ref[0]
    s_all = jnp.dot(x, w_ref[0], preferred_element_type=F32)
    for k in range(4):
        s_ref[k] = s_all[:, k * ln:(k + 1) * ln]
    a_rf, a_if = pw_ref[0, 0, 0], pw_ref[0, 0, 1]
    a_rb, a_ib = pw_ref[0, 1, 0], pw_ref[0, 1, 1]

    nb = 4

    def tile(half, c):
        return pl.ds(half * nch + c, nb, stride=2 * nch)

    def sweep(half_f, half_b, carry):
        def step(s, carry):
            hrf, hif, hrb, hib = carry
            rf, rb = tile(half_f, s), tile(half_b, nch - 1 - s)
            h_ref[0, rf, :] = hrf
            h_ref[1, rf, :] = hif
            h_ref[2, rb, :] = hrb
            h_ref[3, rb, :] = hib
            pr, pi = _cmul(a_rf, a_if, hrf, hif)
            qr, qi = _cmul(a_rb, a_ib, hrb, hib)
            return (pr + s_ref[0, rf, :], pi + s_ref[1, rf, :], qr + s_ref[2, rb, :], qi + s_ref[3, rb, :])
        return lax.fori_loop(0, nch, step, carry, unroll=2)

    lead_f, lead_b = pl.ds(0, nb, stride=2), pl.ds(1, nb, stride=2)
    init = (h0_ref[0, 0, 0, lead_f, :], h0_ref[0, 0, 1, lead_f, :],
            h0_ref[0, 1, 0, lead_b, :], h0_ref[0, 1, 1, lead_b, :])
    hrf, hif, hrb, hib = sweep(1, 0, sweep(0, 1, init))
    fin_ref[...] = jnp.zeros(fin_ref.shape, F32)
    fin_ref[0, 0, 0, lead_b, :] = hrf
    fin_ref[0, 0, 1, lead_b, :] = hif
    fin_ref[0, 1, 0, lead_f, :] = hrb
    fin_ref[0, 1, 1, lead_f, :] = hib

    h_all = jnp.concatenate([h_ref[k] for k in range(4)], axis=1).astype(BF16)
    y_ref[0] = (jnp.dot(x, m_ref[0], preferred_element_type=F32)
                + jnp.dot(h_all, v_ref[0], preferred_element_type=F32))


def _s5_core(xs, tables, h0):
    mpair, wpair, vpair, pw = tables
    gp_n, rows, wdt = xs.shape
    nch = rows // 8
    return pl.pallas_call(
        functools.partial(_s5_kernel, nch=nch),
        grid=(gp_n,),
        in_specs=[pl.BlockSpec((1, rows, wdt), lambda g: (g, 0, 0)),
                  pl.BlockSpec((1, wdt, wdt), lambda g: (g, 0, 0)),
                  pl.BlockSpec((1, wdt, wdt), lambda g: (g, 0, 0)),
                  pl.BlockSpec((1, wdt, wdt), lambda g: (g, 0, 0)),
                  pl.BlockSpec((1,) + pw.shape[1:], lambda g: (g, 0, 0, 0, 0)),
                  pl.BlockSpec((1, 2, 2, 8, 2 * S5_STATE), lambda g: (g, 0, 0, 0, 0))],
        out_specs=[pl.BlockSpec((1, rows, wdt), lambda g: (g, 0, 0)),
                   pl.BlockSpec((1, 2, 2, 8, 2 * S5_STATE), lambda g: (g, 0, 0, 0, 0))],
        out_shape=[jax.ShapeDtypeStruct((gp_n, rows, wdt), F32),
                   jax.ShapeDtypeStruct((gp_n, 2, 2, 8, 2 * S5_STATE), F32)],
        scratch_shapes=[pltpu.VMEM((4, rows, 2 * S5_STATE), F32), pltpu.VMEM((4, rows, 2 * S5_STATE), F32)],
        compiler_params=_cparams(("parallel",)),
        name="s5_core",
    )(xs, mpair, wpair, vpair, pw, h0)


def _s5_next_h0(fin):
    g, d, r, _, n = fin.shape
    sw = fin.reshape(g, d, r, 4, 2, n)[:, :, :, :, ::-1, :]
    keep = jnp.array([[1.0, 0.0], [0.0, 1.0]], F32)[None, :, None, None, :, None]
    return (sw * keep).reshape(fin.shape)


def _s5_glu_kernel(y_ref, u_ref, d_ref, w_ref, b_ref, o_ref, nat_ref, *, rows16):
    _s5_from_chunk_rows(y_ref, nat_ref, rows16)
    y = jnp.concatenate([nat_ref[q] for q in range(BR // LANE)], axis=1) + u_ref[...].astype(F32) * d_ref[...]
    g = jax.nn.gelu(y)
    z = jnp.dot(g.astype(BF16), w_ref[...], preferred_element_type=F32) + b_ref[...]
    o_ref[...] = g * jax.nn.sigmoid(z)


def _s5_glu(ys, p, d_skip, glu_w, glu_b, tm=512):
    m = p.shape[0]
    tm = min(tm, m)
    rows16 = tm // S5_T
    return pl.pallas_call(
        functools.partial(_s5_glu_kernel, rows16=rows16),
        grid=(m // tm,),
        in_specs=[pl.BlockSpec((_S5_PAIRS, rows16, _S5_ROW), lambda i: (0, i, 0)),
                  pl.BlockSpec((tm, BR), lambda i: (i, COL_S5_U * LANE // BR)),
                  pl.BlockSpec((1, BR), lambda i: (0, 0)),
                  pl.BlockSpec((BR, BR), lambda i: (0, 0)),
                  pl.BlockSpec((1, BR), lambda i: (0, 0))],
        out_specs=pl.BlockSpec((tm, BR), lambda i: (i, 0)),
        out_shape=jax.ShapeDtypeStruct((m, BR), F32),
        scratch_shapes=[pltpu.VMEM((BR // LANE, tm, LANE), F32)],
        compiler_params=_cparams(("parallel",)),
        name="s5_glu",
    )(ys, p, d_skip.reshape(1, BR), glu_w.astype(BF16), glu_b.reshape(1, BR))


@functools.lru_cache(maxsize=None)
def _fft_tables_np(n_total, n2_len):
    n1_len = n_total // n2_len
    n1h = n1_len // 2
    k1 = np.arange(n1_len, dtype=np.int64)
    n2 = np.arange(n2_len, dtype=np.int64)
    m = (k1[None, :, None] * k1[None, None, :] * n2_len + n2[:, None, None] * k1[None, :, None]) % n_total
    ang = m.astype(np.float64) * (2.0 * math.pi / n_total)
    c, s = np.cos(ang), np.sin(ang)
    ch, sh = c[:, :, :n1h], s[:, :, :n1h]
    gf = np.concatenate([np.concatenate([ch, sh], 2), np.concatenate([-sh, ch], 2)], 1)
    gk = np.concatenate([ch, -sh], 1)
    cht, sht = ch.transpose(0, 2, 1), sh.transpose(0, 2, 1)
    gi = np.concatenate([np.concatenate([cht, -sht], 2), np.concatenate([sht, cht], 2)], 1)
    a2 = ((n2[:, None] * n2[None, :]) % n2_len).astype(np.float64) * (2.0 * math.pi / n2_len)
    c2, s2 = np.cos(a2), np.sin(a2)
    f2 = np.concatenate([np.concatenate([c2, s2], 1), np.concatenate([-s2, c2], 1)], 0)
    f2c = np.concatenate([np.concatenate([c2, -s2], 1), np.concatenate([s2, c2], 1)], 0)
    return tuple(np.asarray(t, np.float32) for t in (gf, gk, gi, f2, f2c))


def _fft_tables(n_total, n2_len):
    return tuple(jnp.asarray(t).astype(BF16) for t in _fft_tables_np(n_total, n2_len))


def _hy_shortconv_kernel(x_ref, w_ref, b_ref, o_ref):
    x = x_ref[0].astype(F32)
    l = x.shape[0]
    row = lax.broadcasted_iota(jnp.int32, x.shape, 0)
    prev = jnp.where(row == 0, 0.0, pltpu.roll(x, 1, axis=0))
    nxt = jnp.where(row == l - 1, 0.0, pltpu.roll(x, l - 1, axis=0))
    o_ref[0] = w_ref[0:1, :] * prev + w_ref[1:2, :] * x + w_ref[2:3, :] * nxt + b_ref[...]


def _hy_shortconv(p3, conv_w, conv_b):
    bsz, l, _ = p3.shape
    units = 3 * BR // LANE
    return pl.pallas_call(
        _hy_shortconv_kernel,
        grid=(bsz, units),
        in_specs=[pl.BlockSpec((1, l, LANE), lambda b, u: (b, 0, COL_HY_X + u)),
                  pl.BlockSpec((3, LANE), lambda b, u: (0, u)),
                  pl.BlockSpec((1, LANE), lambda b, u: (0, u))],
        out_specs=pl.BlockSpec((1, l, LANE), lambda b, u: (b, 0, u)),
        out_shape=jax.ShapeDtypeStruct((bsz, l, 3 * BR), F32),
        compiler_params=_cparams(("parallel", "parallel")),
        name="hy_shortconv",
    )(p3, conv_w, conv_b.reshape(1, 3 * BR))


_GRP = 8


def _tile_rows(x3):
    t = jnp.swapaxes(x3, 0, 1)
    return [t[r] for r in range(_GRP)]


def _hy_kf_kernel(fwd_ref, bwd_ref, nrm_ref, gk_ref, f2_ref, o_ref, mid_ref, *, n1_len, n2_len):
    n1h = n1_len // 2
    ta = 2 * n1_len // _GRP
    row = lax.broadcasted_iota(jnp.int32, (n1h, LANE), 0)

    def stage1(m, _):
        cols = pl.ds(pl.multiple_of(m * _GRP, _GRP), _GRP)
        fs, bs = _tile_rows(fwd_ref[:, cols, :]), _tile_rows(bwd_ref[:, cols, :])
        bs[0] = jnp.where((row == 0) & (m == 0), 0.0, bs[0])
        for r in range(_GRP):
            n2 = m * _GRP + r
            a = jnp.dot(gk_ref[n2], jnp.concatenate([fs[r], bs[r]], axis=1).astype(BF16),
                        preferred_element_type=F32)
            mid_ref[0, pl.ds(n2 * ta, ta)] = a[:, :LANE].reshape(ta, _GRP, LANE)
            mid_ref[1, pl.ds(n2 * ta, ta)] = a[:, LANE:].reshape(ta, _GRP, LANE)
        return 0
    lax.fori_loop(0, n2_len // _GRP, stage1, 0)
    scale = nrm_ref[...] * (1.0 / (n1_len * n2_len))

    def stage2(mk, _):
        parts = [[_tile_rows(mid_ref[j, pl.ds(off + mk, n2_len, stride=ta)]) for off in (0, n1_len // _GRP)]
                 for j in range(2)]
        for r in range(_GRP):
            rhs = jnp.concatenate([jnp.concatenate([parts[j][0][r], parts[j][1][r]], axis=0) for j in range(2)],
                                  axis=1).astype(BF16)
            xs = jnp.dot(f2_ref[...], rhs, preferred_element_type=F32)
            f, b = xs[:, :LANE], xs[:, LANE:]
            o_ref[0, mk * _GRP + r] = (jnp.concatenate([f[:n2_len] + b[:n2_len], f[n2_len:] - b[n2_len:]], axis=0)
                                       * scale).astype(BF16)
        return 0
    lax.fori_loop(0, n1_len // _GRP, stage2, 0)


def _hy_kf(h2d, nrm, gk, f2, n2_len):
    l, cols = h2d.shape
    n1_len = 2 * l // n2_len
    n1h = n1_len // 2
    tiles = cols // (2 * LANE)
    h2d = h2d.reshape(n1h, n2_len, cols)
    return pl.pallas_call(
        functools.partial(_hy_kf_kernel, n1_len=n1_len, n2_len=n2_len),
        grid=(tiles,),
        in_specs=[pl.BlockSpec((n1h, n2_len, LANE), lambda t: (0, 0, t)),
                  pl.BlockSpec((n1h, n2_len, LANE), lambda t: (0, 0, tiles + t)),
                  pl.BlockSpec((1, LANE), lambda t: (0, t)),
                  pl.BlockSpec(gk.shape, lambda t: (0, 0, 0)),
                  pl.BlockSpec(f2.shape, lambda t: (0, 0))],
        out_specs=pl.BlockSpec((1, n1_len, 2 * n2_len, LANE), lambda t: (t, 0, 0, 0)),
        out_shape=jax.ShapeDtypeStruct((tiles, n1_len, 2 * n2_len, LANE), BF16),
        scratch_shapes=[pltpu.VMEM((2, 4 * l // _GRP, _GRP, LANE), F32)],
        compiler_params=_cparams(("parallel",)),
        name="hy_kf",
    )(h2d, h2d, nrm, gk, f2)


def _hy_conv_kernel(z_ref, g_ref, kf_ref, bias_ref, gf_ref, gi_ref, f2_ref, f2c_ref, o_ref,
                    mid_ref, spec_ref, *, n1_len, n2_len, order):
    n1h = n1_len // 2
    ta = 2 * n1_len // _GRP
    tb = 2 * n2_len // _GRP

    def stage1(m, _):
        zr = _tile_rows(z_ref[0, :, pl.ds(pl.multiple_of(m * _GRP, _GRP), _GRP), :])
        zi = _tile_rows(z_ref[1, :, pl.ds(pl.multiple_of(m * _GRP, _GRP), _GRP), :])
        for r in range(_GRP):
            n2 = m * _GRP + r
            rhs = jnp.concatenate([zr[r], zi[r]], axis=0).astype(BF16)
            a = jnp.dot(gf_ref[n2], rhs, preferred_element_type=F32)
            mid_ref[pl.ds(n2 * ta, ta)] = a.reshape(ta, _GRP, LANE)
        return 0
    lax.fori_loop(0, n2_len // _GRP, stage1, 0)

    def stage2(mk, _):
        xr = _tile_rows(mid_ref[pl.ds(mk, n2_len, stride=ta)])
        xi = _tile_rows(mid_ref[pl.ds(n1_len // _GRP + mk, n2_len, stride=ta)])
        for rp in range(_GRP // 2):
            cols = [jnp.concatenate([xr[2 * rp + j], xi[2 * rp + j]], axis=0) for j in range(2)]
            xs = jnp.dot(f2_ref[...], jnp.concatenate(cols, axis=1).astype(BF16), preferred_element_type=F32)
            for j in range(2):
                k1 = mk * _GRP + 2 * rp + j
                x = xs[:, j * LANE:(j + 1) * LANE]
                kf = kf_ref[0, k1].astype(F32)
                xre, xim = x[:n2_len], x[n2_len:]
                kr, ki = kf[:n2_len], kf[n2_len:]
                spec_ref[k1] = jnp.concatenate([xre * kr - xim * ki, xre * ki + xim * kr], axis=0).astype(BF16)
        return 0
    lax.fori_loop(0, n1_len // _GRP, stage2, 0)

    def stage2i(kp, _):
        r = jnp.concatenate([spec_ref[2 * kp], spec_ref[2 * kp + 1]], axis=1)
        b = jnp.dot(f2c_ref[...], r, preferred_element_type=F32)
        mid_ref[pl.ds(2 * kp * tb, tb)] = b[:, :LANE].reshape(tb, _GRP, LANE)
        mid_ref[pl.ds((2 * kp + 1) * tb, tb)] = b[:, LANE:].reshape(tb, _GRP, LANE)
        return 0
    lax.fori_loop(0, n1_len // 2, stage2i, 0, unroll=_FFT_UNROLL)

    def stage1i(m, _):
        br = _tile_rows(mid_ref[pl.ds(m, n1_len, stride=tb)])
        bi = _tile_rows(mid_ref[pl.ds(n2_len // _GRP + m, n1_len, stride=tb)])
        outs = []
        for r in range(_GRP):
            rhs = jnp.concatenate([br[r], bi[r]], axis=0).astype(BF16)
            outs.append(jnp.dot(gi_ref[m * _GRP + r], rhs, preferred_element_type=F32))
        cols = pl.ds(pl.multiple_of(m * _GRP, _GRP), _GRP)
        o_ref[0, :, cols, :] = jnp.swapaxes(jnp.stack([o[:n1h] for o in outs], axis=0), 0, 1)
        o_ref[1, :, cols, :] = jnp.swapaxes(jnp.stack([o[n1h:] for o in outs], axis=0), 0, 1)
        return 0
    lax.fori_loop(0, n2_len // _GRP, stage1i, 0)

    bias = bias_ref[order:order + 1, :]
    for j in range(2):
        o_ref[j] = g_ref[j] * (o_ref[j] + bias * z_ref[j])


def _hy_conv(z, zcol, g, gcol, kf, bias, tabs, n2_len, order):
    gf, _, gi, f2, f2c = tabs
    bsz, l, _ = z.shape
    n1_len = 2 * l // n2_len
    tiles = BR // LANE
    const2 = lambda *_: (0, 0)
    const3 = lambda *_: (0, 0, 0)
    one = pl.Buffered(1)
    n1h = n1_len // 2
    z = z.reshape(bsz, n1h, n2_len, z.shape[-1])
    g = g.reshape(bsz, n1h, n2_len, g.shape[-1])
    out = pl.pallas_call(
        functools.partial(_hy_conv_kernel, n1_len=n1_len, n2_len=n2_len, order=order),
        grid=(tiles, bsz // 2),
        in_specs=[pl.BlockSpec((2, n1h, n2_len, LANE), lambda t, b: (b, 0, 0, zcol + t)),
                  pl.BlockSpec((2, n1h, n2_len, LANE), lambda t, b: (b, 0, 0, gcol + t)),
                  pl.BlockSpec((1, n1_len, 2 * n2_len, LANE), lambda t, b: (order * tiles + t, 0, 0, 0),
                               pipeline_mode=one),
                  pl.BlockSpec((HY_ORDER, LANE), lambda t, b: (0, t)),
                  pl.BlockSpec(gf.shape, const3, pipeline_mode=one),
                  pl.BlockSpec(gi.shape, const3, pipeline_mode=one),
                  pl.BlockSpec(f2.shape, const2, pipeline_mode=one),
                  pl.BlockSpec(f2c.shape, const2, pipeline_mode=one)],
        out_specs=pl.BlockSpec((2, n1h, n2_len, LANE), lambda t, b: (b, 0, 0, t)),
        out_shape=jax.ShapeDtypeStruct((bsz, n1h, n2_len, BR), F32),
        scratch_shapes=[pltpu.VMEM((4 * l // _GRP, _GRP, LANE), F32),
                        pltpu.VMEM((n1_len, 2 * n2_len, LANE), BF16)],
        compiler_params=_cparams(("parallel", "arbitrary")),
        name="hy_conv%d" % order,
    )(z, g, kf, bias, gf, gi, f2, f2c)
    return out.reshape(bsz, l, BR)


def _hyena_filters(l, w1, b1, f1, w2, b2, f2, w3, b3, f3, w4):
    hp = lax.Precision.HIGHEST
    t = jnp.linspace(0.0, 1.0, l, dtype=F32)[:, None]
    wpos = 2.0 * math.pi * jnp.arange(l, dtype=F32)[:, None] / l
    f = jnp.linspace(1e-4, HY_BANDS - 1, HY_BANDS, dtype=F32)[None, :]
    z = jnp.concatenate([t, jnp.cos(f * wpos), -jnp.sin(f * wpos)], axis=-1)
    h = jnp.sin(f1 * (jnp.dot(z, w1, precision=hp) + b1))
    h = jnp.sin(f2 * (jnp.dot(h, w2, precision=hp) + b2))
    h = jnp.sin(f3 * (jnp.dot(h, w3, precision=hp) + b3))
    deltas = jnp.linspace(math.log(HY_TARGET) / HY_SLOW_DECAY,
                          math.log(HY_TARGET) / HY_FAST_DECAY, BR, dtype=F32)
    window = jnp.exp(-t * jnp.tile(jnp.abs(deltas), 2 * HY_ORDER)[None, :])
    h = jnp.dot(h, w4, precision=hp) * window
    half = HY_ORDER * BR
    row0 = (jnp.arange(l) == 0)[:, None]
    ss = (jnp.sum(h[:, :half] * h[:, :half], axis=0)
          + jnp.sum(jnp.where(row0, 0.0, h[:, half:] * h[:, half:]), axis=0))
    return h, lax.rsqrt(ss + EPS).reshape(1, half)


def _hyena(p3, conv_w, conv_b, filters, bias, n2_len):
    bsz, l, _ = p3.shape
    tabs = _fft_tables(2 * l, n2_len)
    xs = _hy_shortconv(p3, conv_w, conv_b)
    h2d, nrm = filters
    kf = _hy_kf(h2d, nrm, tabs[1], tabs[3], n2_len)
    tiles = BR // LANE
    z1 = _hy_conv(xs, 0, xs, tiles, kf, bias, tabs, n2_len, 0)
    return _hy_conv(z1, 0, xs, 2 * tiles, kf, bias, tabs, n2_len, 1)


def _ret_kernel(q_ref, k_ref, v_ref, cos_ref, sin_ref, lg_ref, nw_ref, s0_ref, o_ref, sfin_ref,
                vb_ref, qf_ref, qb_ref, kf_ref, kb_ref, *, nchunks, c):
    lg = lg_ref[0]
    jrow = lax.broadcasted_iota(jnp.int32, (c, c), 0)
    icol = lax.broadcasted_iota(jnp.int32, (c, c), 1)
    jf = lax.broadcasted_iota(jnp.int32, (c, RET_DK), 0).astype(F32)
    dsym = jnp.exp(jnp.abs(jrow - icol).astype(F32) * lg[:, 0:1]) * jnp.where(jrow == icol, 2.0, 1.0)
    dq_f = jnp.exp((jf + 1.0) * lg)
    dq_b = jnp.exp((c - jf) * lg)
    dk_f = jnp.exp((c - 1.0 - jf) * lg)
    dk_b = jnp.exp(jf * lg)
    dchunk = jnp.exp(c * lg)

    def rope(x, r0):
        return x * cos_ref[pl.ds(r0, c), :] + pltpu.roll(x, RET_DK // 2, axis=1) * sin_ref[pl.ds(r0, c), :]

    def intra_step(n, _):
        r0 = pl.multiple_of(n * c, c)
        rows = pl.ds(r0, c)
        q = rope(q_ref[0, rows, :].astype(F32), r0)
        k = rope(k_ref[0, rows, :].astype(F32), r0) * (RET_DK ** -0.5)
        v = v_ref[0, rows, :].astype(BF16)
        att = _nt(q.astype(BF16), k.astype(BF16)) * dsym
        o_ref[0, rows, :] = jnp.dot(att.astype(BF16), v, preferred_element_type=F32)
        vb_ref[rows, :] = v
        qf_ref[rows, :] = (q * dq_f).astype(BF16)
        qb_ref[rows, :] = (q * dq_b).astype(BF16)
        kf_ref[rows, :] = (k * dk_f).astype(BF16)
        kb_ref[rows, :] = (k * dk_b).astype(BF16)
        return 0

    lax.fori_loop(0, nchunks, intra_step, 0, unroll=_ATT_UNROLL)

    def sweep(lo, hi, carry, finish):
        def step(i, carry):
            s_f, s_b = carry
            rf = pl.ds(pl.multiple_of(i * c, c), c)
            rb = pl.ds(pl.multiple_of((nchunks - 1 - i) * c, c), c)
            o_f = o_ref[0, rf, :] + jnp.dot(qf_ref[rf, :], s_f.astype(BF16), preferred_element_type=F32)
            o_b = o_ref[0, rb, :] + jnp.dot(qb_ref[rb, :], s_b.astype(BF16), preferred_element_type=F32)
            if finish:
                o_f = o_f * lax.rsqrt(jnp.mean(o_f * o_f, axis=-1, keepdims=True) + EPS) * nw_ref[...]
                o_b = o_b * lax.rsqrt(jnp.mean(o_b * o_b, axis=-1, keepdims=True) + EPS) * nw_ref[...]
            o_ref[0, rf, :] = o_f
            o_ref[0, rb, :] = o_b
            return (dchunk * s_f + _tn(kf_ref[rf, :], vb_ref[rf, :]),
                    dchunk * s_b + _tn(kb_ref[rb, :], vb_ref[rb, :]))
        return lax.fori_loop(lo, hi, step, carry)

    carry = sweep(0, nchunks // 2, (s0_ref[0, 0, 0], s0_ref[0, 0, 1]), False)
    s_f, s_b = sweep(nchunks // 2, nchunks, carry, True)
    sfin_ref[0, 0, 0] = s_f
    sfin_ref[0, 0, 1] = s_b


def _retention(p3, cos2, sin2, norm_w, s0):
    bsz, l, _ = p3.shape
    lgs = jnp.log(1.0 - 2.0 ** (-5.0 - jnp.arange(RET_HEADS, dtype=F32)))
    lgs = jnp.broadcast_to(lgs[:, None, None], (RET_HEADS, 1, LANE))
    tab = pl.BlockSpec((l, LANE), lambda b, h: (0, 0))
    st = pl.BlockSpec((1, 1, 2, RET_DK, LANE), lambda b, h: (b, h, 0, 0, 0))
    chunk = RET_CHUNK if l % (2 * RET_CHUNK) == 0 else ATT_CHUNK
    return pl.pallas_call(
        functools.partial(_ret_kernel, nchunks=l // chunk, c=chunk),
        grid=(bsz, RET_HEADS),
        in_specs=[pl.BlockSpec((1, l, LANE), lambda b, h: (b, 0, COL_RET_Q + h)),
                  pl.BlockSpec((1, l, LANE), lambda b, h: (b, 0, COL_RET_K + h)),
                  pl.BlockSpec((1, l, LANE), lambda b, h: (b, 0, COL_RET_V + h)),
                  tab, tab,
                  pl.BlockSpec((1, 1, LANE), lambda b, h: (h, 0, 0)),
                  pl.BlockSpec((1, LANE), lambda b, h: (0, h)),
                  st],
        out_specs=[pl.BlockSpec((1, l, LANE), lambda b, h: (b, 0, h)), st],
        out_shape=[jax.ShapeDtypeStruct((bsz, l, BR), F32),
                   jax.ShapeDtypeStruct((bsz, RET_HEADS, 2, RET_DK, LANE), F32)],
        scratch_shapes=[pltpu.VMEM((l, LANE), BF16)] * 5,
        compiler_params=_cparams(("parallel", "parallel")),
        name="retention",
    )(p3, p3, p3, cos2, sin2, lgs, norm_w.reshape(1, BR), s0)


def _rope_tables(ang):
    cos, sin = jnp.cos(ang), jnp.sin(ang)
    return jnp.concatenate([cos, cos], -1), jnp.concatenate([-sin, sin], -1)


def _latent_angles(rows):
    half = RET_DK // 4
    inv = ROPE_BASE ** (-jnp.arange(half, dtype=F32) / half)
    r = jnp.repeat(jnp.arange(rows, dtype=F32), GRID_W)
    cl = jnp.tile(jnp.arange(GRID_W, dtype=F32), rows)
    return jnp.concatenate([r[:, None] * inv, cl[:, None] * inv], axis=-1)


def _ctx_angles(n_ctx):
    n = RET_DK // 2
    inv = ROPE_BASE ** (-jnp.arange(n, dtype=F32) / n)
    return jnp.arange(n_ctx, dtype=F32)[:, None] * inv


def _gla_kernel(q_ref, k_ref, v_ref, lr_ref, gw_ref, gb_ref, nw_ref, s0_ref, o_ref, sfin_ref,
                vb_ref, qf_ref, qb_ref, kf_ref, kb_ref, df_ref, db_ref, *, nchunks):
    c = ATT_CHUNK
    jrow = lax.broadcasted_iota(jnp.int32, (c, c), 0)
    icol = lax.broadcasted_iota(jnp.int32, (c, c), 1)
    tri_f = jnp.where(icol <= jrow, 1.0, 0.0).astype(BF16)
    tri_b = jnp.where(icol >= jrow, 1.0, 0.0).astype(BF16)
    lane = lax.broadcasted_iota(jnp.int32, (1, LANE), 1)
    head0 = lane < GLA_DK

    gw_both = jnp.concatenate([gw_ref[0], gw_ref[1]], axis=1)
    gb_both = jnp.concatenate([gb_ref[0], gb_ref[1]], axis=1)

    def decays(r0):
        x = jnp.dot(lr_ref[0, pl.ds(r0, c), :].astype(BF16), gw_both, preferred_element_type=F32) + gb_both
        la = (jnp.minimum(x, 0.0) - jnp.log(1.0 + jnp.exp(-jnp.abs(x)))) * (1.0 / GLA_TAU)
        hi = la.astype(BF16)
        lo = (la - hi.astype(F32)).astype(BF16)
        out = []
        for d, tri in enumerate((tri_f, tri_b)):
            cols = slice(d * LANE, (d + 1) * LANE)
            cs = jnp.dot(tri, jnp.concatenate([hi[:, cols], lo[:, cols]], axis=1), preferred_element_type=F32)
            out.append(cs[:, :LANE] + cs[:, LANE:])
        return out

    def intra_step(n, _):
        r0 = pl.multiple_of(n * c, c)
        rows = pl.ds(r0, c)
        q = q_ref[0, rows, :].astype(F32) * (GLA_DK ** -0.5)
        k = k_ref[0, rows, :].astype(F32)
        bf, bb = decays(r0)
        ref_f = bf[c // 2 - 1:c // 2, :]
        ref_b = bb[c // 2:c // 2 + 1, :]
        qf = q * jnp.exp(bf - ref_f)
        kf = (k * jnp.exp(ref_f - bf)).astype(BF16)
        qb = q * jnp.exp(bb - ref_b)
        kb = (k * jnp.exp(ref_b - bb)).astype(BF16)
        for h in range(2):
            hm = head0 if h == 0 else jnp.logical_not(head0)
            v = v_ref[0, rows, h * GLA_DV:(h + 1) * GLA_DV].astype(BF16)
            att = (jnp.where(icol <= jrow, _nt(jnp.where(hm, qf, 0.0).astype(BF16), kf), 0.0)
                   + jnp.where(icol >= jrow, _nt(jnp.where(hm, qb, 0.0).astype(BF16), kb), 0.0))
            o_ref[0, rows, h * GLA_DV:(h + 1) * GLA_DV] = jnp.dot(att.astype(BF16), v, preferred_element_type=F32)
            vb_ref[rows, h * GLA_DV:(h + 1) * GLA_DV] = v
        bf_last, bb_last = bf[c - 1:c, :], bb[0:1, :]
        qf_ref[rows, :] = (q * jnp.exp(bf)).astype(BF16)
        qb_ref[rows, :] = (q * jnp.exp(bb)).astype(BF16)
        kf_ref[rows, :] = (k * jnp.exp(bf_last - bf)).astype(BF16)
        kb_ref[rows, :] = (k * jnp.exp(bb_last - bb)).astype(BF16)
        d0 = pl.multiple_of(n * 8, 8)
        df_ref[pl.ds(d0, 8), :] = jnp.broadcast_to(jnp.exp(bf_last), (8, LANE))
        db_ref[pl.ds(d0, 8), :] = jnp.broadcast_to(jnp.exp(bb_last), (8, LANE))
        return 0

    lax.fori_loop(0, nchunks, intra_step, 0, unroll=_ATT_UNROLL)

    def sweep(lo, hi, carry, finish):
        def step(i, carry):
            sts = list(carry)
            new = []
            for d in range(2):
                n = i if d == 0 else nchunks - 1 - i
                rows = pl.ds(pl.multiple_of(n * c, c), c)
                qs = (qf_ref if d == 0 else qb_ref)[rows, :]
                kd = (kf_ref if d == 0 else kb_ref)[rows, :]
                dec = (df_ref if d == 0 else db_ref)[pl.ds(pl.multiple_of(n * 8, 8), 1), :]
                for h in range(2):
                    hm = head0 if h == 0 else jnp.logical_not(head0)
                    cols = slice(h * GLA_DV, (h + 1) * GLA_DV)
                    st = sts[2 * d + h]
                    o = o_ref[0, rows, cols] + _nt(qs, st.astype(BF16))
                    if finish:
                        o = o * lax.rsqrt(jnp.mean(o * o, axis=-1, keepdims=True) + EPS) * nw_ref[:, cols]
                    o_ref[0, rows, cols] = o
                    new.append(jnp.where(hm, dec * st + _tn(vb_ref[rows, cols], kd), 0.0))
            return tuple(new)
        return lax.fori_loop(lo, hi, step, carry)

    carry = (s0_ref[0, 0, 0, 0], s0_ref[0, 0, 0, 1], s0_ref[0, 0, 1, 0], s0_ref[0, 0, 1, 1])
    carry = sweep(0, nchunks // 2, carry, False)
    s_fin = sweep(nchunks // 2, nchunks, carry, True)
    sfin_ref[0, 0, 0, 0] = s_fin[0]
    sfin_ref[0, 0, 0, 1] = s_fin[1]
    sfin_ref[0, 0, 1, 0] = s_fin[2]
    sfin_ref[0, 0, 1, 1] = s_fin[3]


def _gla(p3, gate_w, gate_b, norm_w, s0):
    bsz, l, _ = p3.shape
    pairs = GLA_HEADS // 2
    gw = jnp.zeros((2, LANE, GLA_HEADS * GLA_DK), F32)
    for d in range(2):
        gw = gw.at[d, d * GLA_LR:(d + 1) * GLA_LR, :].set(gate_w[d].astype(F32))
    st = pl.BlockSpec((1, 1, 2, 2, GLA_DV, LANE), lambda b, h: (b, h, 0, 0, 0, 0))
    return pl.pallas_call(
        functools.partial(_gla_kernel, nchunks=l // ATT_CHUNK),
        grid=(bsz, pairs),
        in_specs=[pl.BlockSpec((1, l, LANE), lambda b, h: (b, 0, COL_GLA_Q + h)),
                  pl.BlockSpec((1, l, LANE), lambda b, h: (b, 0, COL_GLA_K + h)),
                  pl.BlockSpec((1, l, 2 * GLA_DV), lambda b, h: (b, 0, COL_GLA_V // 2 + h)),
                  pl.BlockSpec((1, l, LANE), lambda b, h: (b, 0, COL_GLA_LR)),
                  pl.BlockSpec((2, LANE, LANE), lambda b, h: (0, 0, h)),
                  pl.BlockSpec((2, 1, LANE), lambda b, h: (0, 0, h)),
                  pl.BlockSpec((1, 2 * GLA_DV), lambda b, h: (0, h)),
                  st],
        out_specs=[pl.BlockSpec((1, l, 2 * GLA_DV), lambda b, h: (b, 0, h)), st],
        out_shape=[jax.ShapeDtypeStruct((bsz, l, BR), F32),
                   jax.ShapeDtypeStruct((bsz, pairs, 2, 2, GLA_DV, LANE), F32)],
        scratch_shapes=([pltpu.VMEM((l, 2 * GLA_DV), BF16)] + [pltpu.VMEM((l, LANE), BF16)] * 4
                        + [pltpu.VMEM((8 * l // ATT_CHUNK, LANE), F32)] * 2),
        compiler_params=_cparams(("parallel", "parallel")),
        name="gla",
    )(p3, p3, p3, p3, gw.astype(BF16), gate_b.astype(F32).reshape(2, 1, GLA_HEADS * GLA_DK),
      norm_w.reshape(1, BR), s0)


def _mixers(p2, xs, bsz, prm, i, s5_tabs, hy_filters, rope, s5_h0, ret_s0, gla_s0, with_hyena):
    m = p2.shape[0]
    l = m // bsz
    p3 = p2.reshape(bsz, l, NP)
    y, s5_fin = _s5_core(xs, s5_tabs, s5_h0)
    s5 = _s5_glu(y, p2, prm['s5_d'][i], prm['s5_glu_w'][i], prm['s5_glu_b'][i])
    hy = None
    if with_hyena:
        hy = _hyena(p3, prm['hy_conv_w'][i], prm['hy_conv_b'][i], hy_filters, prm['hy_bias'][i].astype(F32),
                    64 if l >= 512 else 16).reshape(m, BR)
    ret, ret_fin = _retention(p3, rope[0], rope[1], prm['ret_norm_w'][i], ret_s0)
    gla, gla_fin = _gla(p3, prm['gla_gate_w'][i], prm['gla_gate_b'][i], prm['gla_norm_w'][i], gla_s0)
    return (s5, hy, ret.reshape(m, BR), gla.reshape(m, BR)), (s5_fin, ret_fin, gla_fin)


def kernel(x, c, ctx, c_ctx, norm_w, ada_w, ada_b, w_in, w_out, s5_a_re, s5_a_im, s5_log_dt, s5_b_re, s5_b_im, s5_c_re, s5_c_im, s5_d, s5_glu_w, s5_glu_b, hy_conv_w, hy_conv_b, hy_w1, hy_b1, hy_f1, hy_w2, hy_b2, hy_f2, hy_w3, hy_b3, hy_f3, hy_w4, hy_bias, ret_norm_w, gla_gate_w, gla_gate_b, gla_norm_w, final_norm_w):
    prm = dict(s5_a_re=s5_a_re, s5_a_im=s5_a_im, s5_log_dt=s5_log_dt, s5_b_re=s5_b_re, s5_b_im=s5_b_im,
               s5_c_re=s5_c_re, s5_c_im=s5_c_im, s5_d=s5_d, s5_glu_w=s5_glu_w, s5_glu_b=s5_glu_b,
               hy_conv_w=hy_conv_w, hy_conv_b=hy_conv_b, hy_w1=hy_w1, hy_b1=hy_b1, hy_f1=hy_f1,
               hy_w2=hy_w2, hy_b2=hy_b2, hy_f2=hy_f2, hy_w3=hy_w3, hy_b3=hy_b3, hy_f3=hy_f3,
               hy_w4=hy_w4, hy_bias=hy_bias, ret_norm_w=ret_norm_w, gla_gate_w=gla_gate_w,
               gla_gate_b=gla_gate_b, gla_norm_w=gla_norm_w)
    bsz, l, d = x.shape
    lc = ctx.shape[1]
    depth = w_in.shape[0]
    cc = jnp.concatenate([c, c_ctx[None, :], jnp.zeros((8 - bsz - 1, d), F32)], axis=0)
    mod = _ada(cc, ada_w, ada_b)
    rope_l = _rope_tables(_latent_angles(l // GRID_W))
    rope_c = _rope_tables(_ctx_angles(lc))
    s5_zero = jnp.zeros((S5_GROUPS // 2, 2, 2, 8, 2 * S5_STATE), F32)
    ret_zero = jnp.zeros((bsz, RET_HEADS, 2, RET_DK, LANE), F32)
    gla_zero = jnp.zeros((bsz, GLA_HEADS // 2, 2, 2, GLA_DV, LANE), F32)

    w_all = _wprep(w_in)
    wo_all = w_out.astype(BF16)
    filt = (hy_w1, hy_b1, hy_f1, hy_w2, hy_b2, hy_f2, hy_w3, hy_b3, hy_f3, hy_w4)

    x_l = x.reshape(bsz * l, d)
    x_c = ctx.reshape(bsz * lc, d)
    for i in range(depth):
        last = i == depth - 1
        sh, sc, gt = mod[i, :, :d], mod[i, :, d:2 * d], mod[i, :, 2 * d:]
        p_c, xs_c = _inproj(x_c, norm_w[i], sc[bsz:bsz + 1, None, :], sh[bsz:bsz + 1, None, :], w_all, i,
                            bsz * lc)
        p_l, xs_l = _inproj(x_l, norm_w[i], sc[:bsz, None, :], sh[:bsz, None, :], w_all, i, l)
        s5_tabs = _s5_tables(s5_a_re[i], s5_a_im[i], s5_log_dt[i], s5_b_re[i], s5_b_im[i], s5_c_re[i],
                             s5_c_im[i])
        filt_i = [w[i].astype(F32) for w in filt]
        ys_c, (s5_fin, ret_fin, gla_fin) = _mixers(
            p_c, xs_c, bsz, prm, i, s5_tabs, None if last else _hyena_filters(lc, *filt_i), rope_c, s5_zero,
            ret_zero, gla_zero, with_hyena=not last)
        ys_l, _ = _mixers(p_l, xs_l, bsz, prm, i, s5_tabs, _hyena_filters(l, *filt_i), rope_l,
                          _s5_next_h0(s5_fin), ret_fin, gla_fin, with_hyena=True)
        x_l = _outproj(x_l, ys_l, p_l, wo_all, i, gt[:bsz, None, :], final_norm_w, l, final=last)
        if not last:
            x_c = _outproj(x_c, ys_c, p_c, wo_all, i, gt[bsz:bsz + 1, None, :], final_norm_w, bsz * lc,
                           final=False)
    return x_l.reshape(bsz, l, d)
```

```python
import functools
import math

import numpy as np
import jax
import jax.numpy as jnp
from jax import lax
from jax.experimental import pallas as pl
from jax.experimental.pallas import tpu as pltpu

F32 = jnp.float32
BF16 = jnp.bfloat16

EPS = 1e-6
GRID_W = 64
BR = 512
S5_GSIZE = 16
S5_GROUPS = BR // S5_GSIZE
S5_STATE = 64
S5_T = 16
HY_ORDER = 2
HY_EMB = 33
HY_BANDS = (HY_EMB - 1) // 2
HY_FAST_DECAY = 0.3
HY_SLOW_DECAY = 1.5
HY_TARGET = 1e-2
RET_HEADS = 4
RET_DK = 128
GLA_HEADS = 4
GLA_DK = 64
GLA_DV = 128
GLA_LR = 16
GLA_TAU = 16.0
ROPE_BASE = 10000.0
ATT_CHUNK = 128
RET_CHUNK = 256
_FFT_UNROLL = 8
_ATT_UNROLL = 2

LANE = 128
VMEM_LIMIT = 56 * 1024 * 1024

COL_S5_U = 0
COL_S5_G = 4
COL_HY_X = 8
COL_HY_G = 20
COL_RET_Q = 24
COL_RET_K = 28
COL_RET_V = 32
COL_RET_G = 36
COL_GLA_Q = 40
COL_GLA_K = 42
COL_GLA_V = 44
COL_GLA_G = 48
COL_GLA_LR = 52
NP_UNITS = 54
NP = NP_UNITS * LANE
W_TILE = 768
_ALIGNED = COL_GLA_G * LANE


def _cparams(sem):
    return pltpu.CompilerParams(dimension_semantics=sem, vmem_limit_bytes=VMEM_LIMIT)


def _silu(x):
    return x * jax.nn.sigmoid(x)


def _wprep(w_in):
    depth, d, in_w = w_in.shape
    assert _ALIGNED % W_TILE == 0 and in_w == _ALIGNED + 2 * GLA_LR + BR and NP == _ALIGNED + W_TILE
    wb = w_in.astype(BF16)
    tail = jnp.concatenate([wb[:, :, _ALIGNED + 2 * GLA_LR:], wb[:, :, _ALIGNED:_ALIGNED + 2 * GLA_LR],
                            jnp.zeros((depth, d, W_TILE - BR - 2 * GLA_LR), BF16)], axis=-1)
    return wb, tail


def _ada_kernel(c_ref, w_ref, b_ref, o_ref):
    a = _silu(c_ref[...]).astype(BF16)
    o_ref[0] = jnp.dot(a, w_ref[0].astype(BF16), preferred_element_type=F32) + b_ref[0]


def _ada(cc, ada_w, ada_b, tn=768):
    depth, d, n = ada_w.shape
    return pl.pallas_call(
        _ada_kernel,
        grid=(depth, n // tn),
        in_specs=[pl.BlockSpec((8, d), lambda i, j: (0, 0)),
                  pl.BlockSpec((1, d, tn), lambda i, j: (i, 0, j)),
                  pl.BlockSpec((1, 1, tn), lambda i, j: (i, 0, j))],
        out_specs=pl.BlockSpec((1, 8, tn), lambda i, j: (i, 0, j)),
        out_shape=jax.ShapeDtypeStruct((depth, 8, n), F32),
        compiler_params=_cparams(("parallel", "parallel")),
        name="ada_mod",
    )(cc, ada_w, ada_b.reshape(depth, 1, n))


_S5_SLOT = 2 * S5_GSIZE
_S5_PAIRS = S5_GROUPS // 2
_S5_ROW = S5_T * _S5_SLOT


def _s5_lane_perm():
    perm = np.zeros((4, _S5_ROW), np.int32)
    for s in range(4):
        for tau in range(S5_T):
            for c in range(_S5_SLOT):
                gl, i = divmod(c, S5_GSIZE)
                perm[s, LANE * (tau // 4) + _S5_SLOT * ((s + tau) % 4) + c] = (gl * S5_T + tau) * S5_GSIZE + i
    return perm


def _s5_to_chunk_rows(nat_ref, xs_ref, rows16):
    slot = lax.broadcasted_iota(jnp.int32, (rows16, LANE), 1) // _S5_SLOT
    for q in range(4):
        for qd in range(4):
            rolled = []
            for t in range(4):
                piece = nat_ref[q, pl.ds(4 * qd + t, rows16, stride=S5_T), :]
                rolled.append(pltpu.roll(piece, _S5_SLOT * t, axis=1) if t else piece)
            for s in range(4):
                acc = rolled[0]
                for t in range(1, 4):
                    acc = jnp.where(slot == (s + t) % 4, rolled[t], acc)
                xs_ref[4 * q + s, :, qd * LANE:(qd + 1) * LANE] = acc.astype(xs_ref.dtype)


def _s5_from_chunk_rows(y_ref, nat_ref, rows16):
    slot = lax.broadcasted_iota(jnp.int32, (rows16, LANE), 1) // _S5_SLOT
    for q in range(4):
        for qd in range(4):
            src = [y_ref[4 * q + s, :, qd * LANE:(qd + 1) * LANE] for s in range(4)]
            for t in range(4):
                acc = src[0]
                for s in range(1, 4):
                    acc = jnp.where(slot == (s + t) % 4, src[s], acc)
                if t:
                    acc = pltpu.roll(acc, LANE - _S5_SLOT * t, axis=1)
                nat_ref[q, pl.ds(4 * qd + t, rows16, stride=S5_T), :] = acc


def _inproj_kernel(x_ref, nw_ref, sc_ref, sh_ref, w_ref, tail_ref, o_ref, xs_ref, h_ref, nat_ref, *, rows16):
    j = pl.program_id(1)
    last = pl.num_programs(1) - 1

    def project(w):
        res = jnp.dot(h_ref[...], w, preferred_element_type=F32)
        o_ref[...] = res.astype(o_ref.dtype)
        return res

    @pl.when(j == 0)
    def _():
        x = x_ref[...]
        y = x * lax.rsqrt(jnp.mean(x * x, axis=-1, keepdims=True) + EPS) * nw_ref[...]
        h_ref[...] = (y * (1.0 + sc_ref[0]) + sh_ref[0]).astype(BF16)
        res = project(w_ref[...])
        for q in range(BR // LANE):
            nat_ref[q] = res[:, q * LANE:(q + 1) * LANE]
        _s5_to_chunk_rows(nat_ref, xs_ref, rows16)

    @pl.when((j > 0) & (j < last))
    def _():
        project(w_ref[...])

    @pl.when(j == last)
    def _():
        project(tail_ref[...])


def _inproj(x2, nw, sc, sh, weights, layer, rows_per_mod, tm=1024, tn=W_TILE):
    w_all, w_tail = weights
    m, d = x2.shape
    tm = min(tm, rows_per_mod)
    per = rows_per_mod // tm
    rows16 = tm // S5_T
    n_al = _ALIGNED // tn
    assert COL_S5_U == 0 and tn >= BR and tn == W_TILE
    return pl.pallas_call(
        functools.partial(_inproj_kernel, rows16=rows16),
        grid=(m // tm, NP // tn),
        in_specs=[pl.BlockSpec((tm, d), lambda i, j: (i, 0)),
                  pl.BlockSpec((1, d), lambda i, j: (0, 0)),
                  pl.BlockSpec((1, 1, d), lambda i, j: (i // per, 0, 0)),
                  pl.BlockSpec((1, 1, d), lambda i, j: (i // per, 0, 0)),
                  pl.BlockSpec((None, d, tn), lambda i, j: (layer, 0, jnp.minimum(j, n_al - 1))),
                  pl.BlockSpec((None, d, tn), lambda i, j: (layer, 0, 0))],
        out_specs=[pl.BlockSpec((tm, tn), lambda i, j: (i, j)),
                   pl.BlockSpec((_S5_PAIRS, rows16, _S5_ROW), lambda i, j: (0, i, 0))],
        out_shape=[jax.ShapeDtypeStruct((m, NP), BF16),
                   jax.ShapeDtypeStruct((_S5_PAIRS, m // S5_T, _S5_ROW), BF16)],
        scratch_shapes=[pltpu.VMEM((tm, d), BF16), pltpu.VMEM((BR // LANE, tm, LANE), F32)],
        compiler_params=_cparams(("parallel", "arbitrary")),
        name="inproj",
    )(x2, nw.reshape(1, d), sc, sh, w_all, w_tail)


def _outproj_kernel(x_ref, y0_ref, y1_ref, y2_ref, y3_ref, g0_ref, g1_ref, g2_ref, g3_ref, w_ref, gt_ref, fw_ref,
                    o_ref, *, final):
    acc = None
    for k, (y_ref, g_ref) in enumerate(((y0_ref, g0_ref), (y1_ref, g1_ref), (y2_ref, g2_ref), (y3_ref, g3_ref))):
        gated = (y_ref[...] * _silu(g_ref[...].astype(F32))).astype(BF16)
        part = jnp.dot(gated, w_ref[k * BR:(k + 1) * BR, :], preferred_element_type=F32)
        acc = part if acc is None else acc + part
    x = x_ref[...] + gt_ref[0] * acc
    if final:
        x = x * lax.rsqrt(jnp.mean(x * x, axis=-1, keepdims=True) + EPS) * fw_ref[...]
    o_ref[...] = x


def _outproj(x2, ys, p, wo_all, layer, gt, fw, rows_per_mod, final, tm=256):
    m, d = x2.shape
    tm = min(tm, rows_per_mod)
    per = rows_per_mod // tm
    yspec = pl.BlockSpec((tm, BR), lambda i: (i, 0))

    def gate(col_unit):
        blk = col_unit * LANE // BR
        return pl.BlockSpec((tm, BR), lambda i: (i, blk))

    return pl.pallas_call(
        functools.partial(_outproj_kernel, final=final),
        grid=(m // tm,),
        in_specs=[pl.BlockSpec((tm, d), lambda i: (i, 0)), yspec, yspec, yspec, yspec,
                  gate(COL_S5_G), gate(COL_HY_G), gate(COL_RET_G), gate(COL_GLA_G),
                  pl.BlockSpec((None, 4 * BR, d), lambda i: (layer, 0, 0)),
                  pl.BlockSpec((1, 1, d), lambda i: (i // per, 0, 0)),
                  pl.BlockSpec((1, d), lambda i: (0, 0))],
        out_specs=pl.BlockSpec((tm, d), lambda i: (i, 0)),
        out_shape=jax.ShapeDtypeStruct((m, d), F32),
        compiler_params=_cparams(("parallel",)),
        name="outproj",
    )(x2, *ys, p, p, p, p, wo_all, gt, fw.reshape(1, d))


def _s5_tables(a_re, a_im, log_dt, b_re, b_im, c_re, c_im):
    hp = lax.Precision.HIGHEST
    t_len, g_n, p_n, s_n = S5_T, S5_GROUPS, S5_STATE, S5_GSIZE
    a_re, a_im = a_re.astype(F32), a_im.astype(F32)
    dt = jnp.exp(log_dt.astype(F32))[:, :, None]
    lam_re, lam_im = a_re * dt, a_im * dt

    tau = jnp.arange(t_len + 1, dtype=F32)[None, None, :, None]
    mag = jnp.exp(lam_re[:, :, None, :] * tau)
    pr, pi = mag * jnp.cos(lam_im[:, :, None, :] * tau), mag * jnp.sin(lam_im[:, :, None, :] * tau)
    ab_re, ab_im = pr[:, :, 1], pi[:, :, 1]
    den = a_re * a_re + a_im * a_im
    nr = ab_re - 1.0
    co_re = (nr * a_re + ab_im * a_im) / den
    co_im = (ab_im * a_re - nr * a_im) / den
    b_re, b_im = b_re.astype(F32), b_im.astype(F32)
    bco_re = co_re[..., None] * b_re - co_im[..., None] * b_im
    bco_im = co_re[..., None] * b_im + co_im[..., None] * b_re
    c_re, c_im = c_re.astype(F32), c_im.astype(F32)

    ca_re = c_re[:, :, None] * pr[:, :, :, None, :] - c_im[:, :, None] * pi[:, :, :, None, :]
    ca_im = c_re[:, :, None] * pi[:, :, :, None, :] + c_im[:, :, None] * pr[:, :, :, None, :]
    ca_cat = jnp.concatenate([ca_re[:, :, :t_len], -ca_im[:, :, :t_len]], axis=-1)
    bco_cat = jnp.concatenate([bco_re, bco_im], axis=-2)
    kk = jnp.einsum('dgmp,dgpj->dgmj', ca_cat.reshape(2, g_n, t_len * s_n, 2 * p_n), bco_cat, precision=hp)
    kk = kk.reshape(2, g_n, t_len, s_n, s_n)
    kf, kb = kk[0], kk[1]
    kfull = jnp.concatenate([kb[:, :0:-1], (kf[:, 0] + kb[:, 0])[:, None], kf[:, 1:]], axis=1)
    kcat = kfull.transpose(0, 3, 1, 2).reshape(g_n, s_n, (2 * t_len - 1) * s_n)
    kcat = jnp.pad(kcat, ((0, 0), (0, 0), (0, _S5_ROW - kcat.shape[-1])))

    we_re = jnp.stack([pr[0, :, :t_len][:, ::-1], pr[1, :, :t_len]], 0)
    we_im = jnp.stack([pi[0, :, :t_len][:, ::-1], pi[1, :, :t_len]], 0)
    bt_re, bt_im = bco_re.transpose(0, 1, 3, 2)[:, :, None], bco_im.transpose(0, 1, 3, 2)[:, :, None]
    w_re = we_re[:, :, :, None, :] * bt_re - we_im[:, :, :, None, :] * bt_im
    w_im = we_re[:, :, :, None, :] * bt_im + we_im[:, :, :, None, :] * bt_re
    wcat = jnp.concatenate([w_re[0], w_im[0], w_re[1], w_im[1]], axis=-1).reshape(g_n, t_len * s_n, 4 * p_n)

    vcat_t = jnp.concatenate([ca_re[0, :, 1:], -ca_im[0, :, 1:], ca_re[1, :, 1:][:, ::-1], -ca_im[1, :, 1:][:, ::-1]],
                             axis=-1).reshape(g_n, t_len * s_n, 4 * p_n)

    pw = jnp.stack([pr[:, :, t_len], pi[:, :, t_len]], 0).reshape(2, 2, g_n // 2, 1, 2 * p_n).transpose(2, 1, 0, 3, 4)
    return _s5_assemble(kcat, wcat, vcat_t) + (pw,)


def _s5_assemble_kernel(kcat_ref, wcat_ref, vcat_ref, e_ref, oh_ref, oht_ref, m_ref, w_ref, v_ref):
    half = S5_T * S5_GSIZE
    oh = oh_ref[0]
    zeros = jnp.zeros((half, half), F32)
    diag = []
    for gl in range(2):
        kc = kcat_ref[gl]
        diag.append(jnp.concatenate(
            [kc[:, (S5_T - 1 - t) * S5_GSIZE:(S5_T - 1 - t) * S5_GSIZE + half] for t in range(S5_T)], axis=0))
    m_old = jnp.concatenate([jnp.concatenate([diag[0], zeros], axis=1),
                             jnp.concatenate([zeros, diag[1]], axis=1)], axis=0).astype(BF16)
    m_rows = jnp.dot(oh, m_old, preferred_element_type=F32).astype(BF16)
    m_ref[0] = _nt(m_rows, oh).astype(BF16)
    w_old = jnp.concatenate([jnp.dot(wcat_ref[gl].astype(BF16), e_ref[gl], preferred_element_type=F32)
                             for gl in range(2)], axis=0).astype(BF16)
    w_ref[0] = jnp.dot(oh, w_old, preferred_element_type=F32).astype(BF16)
    v_old_t = jnp.concatenate([jnp.dot(vcat_ref[gl].astype(BF16), e_ref[gl], preferred_element_type=F32)
                               for gl in range(2)], axis=0).astype(BF16)
    v_ref[0] = _tn(v_old_t, oht_ref[0]).astype(BF16)


def _s5_assemble(kcat, wcat, vcat_t):
    half = S5_T * S5_GSIZE
    perm = np.eye(_S5_ROW, dtype=np.float32)[_s5_lane_perm()]
    onehot = jnp.asarray(perm).astype(BF16)
    onehot_t = jnp.asarray(np.ascontiguousarray(perm.transpose(0, 2, 1))).astype(BF16)
    place = np.zeros((2, half, _S5_ROW), np.float32)
    for gl in range(2):
        for k in range(4):
            for p in range(S5_STATE):
                place[gl, k * S5_STATE + p, (2 * k + gl) * S5_STATE + p] = 1.0
    out = jax.ShapeDtypeStruct((_S5_PAIRS, _S5_ROW, _S5_ROW), BF16)
    big = pl.BlockSpec((1, _S5_ROW, _S5_ROW), lambda g: (g, 0, 0))
    return pl.pallas_call(
        _s5_assemble_kernel,
        grid=(_S5_PAIRS,),
        in_specs=[pl.BlockSpec((2, S5_GSIZE, _S5_ROW), lambda g: (g, 0, 0)),
                  pl.BlockSpec((2, half, half), lambda g: (g, 0, 0)),
                  pl.BlockSpec((2, half, half), lambda g: (g, 0, 0)),
                  pl.BlockSpec((2, half, _S5_ROW), lambda g: (0, 0, 0)),
                  pl.BlockSpec((1, _S5_ROW, _S5_ROW), lambda g: (g % 4, 0, 0)),
                  pl.BlockSpec((1, _S5_ROW, _S5_ROW), lambda g: (g % 4, 0, 0))],
        out_specs=[big, big, big],
        out_shape=[out, out, out],
        compiler_params=_cparams(("parallel",)),
        name="s5_assemble",
    )(kcat, wcat, vcat_t, jnp.asarray(place).astype(BF16), onehot, onehot_t)


def _nt(a, b):
    return lax.dot_general(a, b, (((1,), (1,)), ((), ())), preferred_element_type=F32)


def _tn(a, b):
    return lax.dot_general(a, b, (((0,), (0,)), ((), ())), preferred_element_type=F32)


def _cmul(ar, ai, br, bi):
    return ar * br - ai * bi, ar * bi + ai * br


def _s5_kernel(x_ref, m_ref, w_ref, v_ref, pw_ref, h0_ref, y_ref, fin_ref, s_ref, h_ref, *, nch):
    ln = 2 * S5_STATE
    x = x_ref[0]
    s_all = jnp.dot(x, w_ref[0], preferred_element_type=F32)
    for k in range(4):
        s_ref[k] = s_all[:, k * ln:(k + 1) * ln]
    a_rf, a_if = pw_ref[0, 0, 0], pw_ref[0, 0, 1]
    a_rb, a_ib = pw_ref[0, 1, 0], pw_ref[0, 1, 1]

    nb = 4

    def tile(half, c):
        return pl.ds(half * nch + c, nb, stride=2 * nch)

    def sweep(half_f, half_b, carry):
        def step(s, carry):
            hrf, hif, hrb, hib = carry
            rf, rb = tile(half_f, s), tile(half_b, nch - 1 - s)
            h_ref[0, rf, :] = hrf
            h_ref[1, rf, :] = hif
            h_ref[2, rb, :] = hrb
            h_ref[3, rb, :] = hib
            pr, pi = _cmul(a_rf, a_if, hrf, hif)
            qr, qi = _cmul(a_rb, a_ib, hrb, hib)
            return (pr + s_ref[0, rf, :], pi + s_ref[1, rf, :], qr + s_ref[2, rb, :], qi + s_ref[3, rb, :])
        return lax.fori_loop(0, nch, step, carry, unroll=2)

    lead_f, lead_b = pl.ds(0, nb, stride=2), pl.ds(1, nb, stride=2)
    init = (h0_ref[0, 0, 0, lead_f, :], h0_ref[0, 0, 1, lead_f, :],
            h0_ref[0, 1, 0, lead_b, :], h0_ref[0, 1, 1, lead_b, :])
    hrf, hif, hrb, hib = sweep(1, 0, sweep(0, 1, init))
    fin_ref[...] = jnp.zeros(fin_ref.shape, F32)
    fin_ref[0, 0, 0, lead_b, :] = hrf
    fin_ref[0, 0, 1, lead_b, :] = hif
    fin_ref[0, 1, 0, lead_f, :] = hrb
    fin_ref[0, 1, 1, lead_f, :] = hib

    h_all = jnp.concatenate([h_ref[k] for k in range(4)], axis=1).astype(BF16)
    y_ref[0] = (jnp.dot(x, m_ref[0], preferred_element_type=F32)
                + jnp.dot(h_all, v_ref[0], preferred_element_type=F32))


def _s5_core(xs, tables, h0):
    mpair, wpair, vpair, pw = tables
    gp_n, rows, wdt = xs.shape
    nch = rows // 8
    return pl.pallas_call(
        functools.partial(_s5_kernel, nch=nch),
        grid=(gp_n,),
        in_specs=[pl.BlockSpec((1, rows, wdt), lambda g: (g, 0, 0)),
                  pl.BlockSpec((1, wdt, wdt), lambda g: (g, 0, 0)),
                  pl.BlockSpec((1, wdt, wdt), lambda g: (g, 0, 0)),
                  pl.BlockSpec((1, wdt, wdt), lambda g: (g, 0, 0)),
                  pl.BlockSpec((1,) + pw.shape[1:], lambda g: (g, 0, 0, 0, 0)),
                  pl.BlockSpec((1, 2, 2, 8, 2 * S5_STATE), lambda g: (g, 0, 0, 0, 0))],
        out_specs=[pl.BlockSpec((1, rows, wdt), lambda g: (g, 0, 0)),
                   pl.BlockSpec((1, 2, 2, 8, 2 * S5_STATE), lambda g: (g, 0, 0, 0, 0))],
        out_shape=[jax.ShapeDtypeStruct((gp_n, rows, wdt), F32),
                   jax.ShapeDtypeStruct((gp_n, 2, 2, 8, 2 * S5_STATE), F32)],
        scratch_shapes=[pltpu.VMEM((4, rows, 2 * S5_STATE), F32), pltpu.VMEM((4, rows, 2 * S5_STATE), F32)],
        compiler_params=_cparams(("parallel",)),
        name="s5_core",
    )(xs, mpair, wpair, vpair, pw, h0)


def _s5_next_h0(fin):
    g, d, r, _, n = fin.shape
    sw = fin.reshape(g, d, r, 4, 2, n)[:, :, :, :, ::-1, :]
    keep = jnp.array([[1.0, 0.0], [0.0, 1.0]], F32)[None, :, None, None, :, None]
    return (sw * keep).reshape(fin.shape)


def _s5_glu_kernel(y_ref, u_ref, d_ref, w_ref, b_ref, o_ref, nat_ref, *, rows16):
    _s5_from_chunk_rows(y_ref, nat_ref, rows16)
    y = jnp.concatenate([nat_ref[q] for q in range(BR // LANE)], axis=1) + u_ref[...].astype(F32) * d_ref[...]
    g = jax.nn.gelu(y)
    z = jnp.dot(g.astype(BF16), w_ref[...], preferred_element_type=F32) + b_ref[...]
    o_ref[...] = g * jax.nn.sigmoid(z)


def _s5_glu(ys, p, d_skip, glu_w, glu_b, tm=512):
    m = p.shape[0]
    tm = min(tm, m)
    rows16 = tm // S5_T
    return pl.pallas_call(
        functools.partial(_s5_glu_kernel, rows16=rows16),
        grid=(m // tm,),
        in_specs=[pl.BlockSpec((_S5_PAIRS, rows16, _S5_ROW), lambda i: (0, i, 0)),
                  pl.BlockSpec((tm, BR), lambda i: (i, COL_S5_U * LANE // BR)),
                  pl.BlockSpec((1, BR), lambda i: (0, 0)),
                  pl.BlockSpec((BR, BR), lambda i: (0, 0)),
                  pl.BlockSpec((1, BR), lambda i: (0, 0))],
        out_specs=pl.BlockSpec((tm, BR), lambda i: (i, 0)),
        out_shape=jax.ShapeDtypeStruct((m, BR), F32),
        scratch_shapes=[pltpu.VMEM((BR // LANE, tm, LANE), F32)],
        compiler_params=_cparams(("parallel",)),
        name="s5_glu",
    )(ys, p, d_skip.reshape(1, BR), glu_w.astype(BF16), glu_b.reshape(1, BR))


@functools.lru_cache(maxsize=None)
def _fft_tables_np(n_total, n2_len):
    n1_len = n_total // n2_len
    n1h = n1_len // 2
    k1 = np.arange(n1_len, dtype=np.int64)
    n2 = np.arange(n2_len, dtype=np.int64)
    m = (k1[None, :, None] * k1[None, None, :] * n2_len + n2[:, None, None] * k1[None, :, None]) % n_total
    ang = m.astype(np.float64) * (2.0 * math.pi / n_total)
    c, s = np.cos(ang), np.sin(ang)
    ch, sh = c[:, :, :n1h], s[:, :, :n1h]
    gf = np.concatenate([np.concatenate([ch, sh], 2), np.concatenate([-sh, ch], 2)], 1)
    gk = np.concatenate([ch, -sh], 1)
    cht, sht = ch.transpose(0, 2, 1), sh.transpose(0, 2, 1)
    gi = np.concatenate([np.concatenate([cht, -sht], 2), np.concatenate([sht, cht], 2)], 1)
    a2 = ((n2[:, None] * n2[None, :]) % n2_len).astype(np.float64) * (2.0 * math.pi / n2_len)
    c2, s2 = np.cos(a2), np.sin(a2)
    f2 = np.concatenate([np.concatenate([c2, s2], 1), np.concatenate([-s2, c2], 1)], 0)
    f2c = np.concatenate([np.concatenate([c2, -s2], 1), np.concatenate([s2, c2], 1)], 0)
    return tuple(np.asarray(t, np.float32) for t in (gf, gk, gi, f2, f2c))


def _fft_tables(n_total, n2_len):
    return tuple(jnp.asarray(t).astype(BF16) for t in _fft_tables_np(n_total, n2_len))


_GRP = 8


def _tile_rows(x3):
    t = jnp.swapaxes(x3, 0, 1)
    return [t[r] for r in range(_GRP)]


def _hy_kf_kernel(fwd_ref, bwd_ref, nrm_ref, gk_ref, f2_ref, o_ref, mid_ref, *, n1_len, n2_len):
    n1h = n1_len // 2
    ta = 2 * n1_len // _GRP
    row = lax.broadcasted_iota(jnp.int32, (n1h, LANE), 0)

    def stage1(m, _):
        cols = pl.ds(pl.multiple_of(m * _GRP, _GRP), _GRP)
        fs, bs = _tile_rows(fwd_ref[:, cols, :]), _tile_rows(bwd_ref[:, cols, :])
        bs[0] = jnp.where((row == 0) & (m == 0), 0.0, bs[0])
        for r in range(_GRP):
            n2 = m * _GRP + r
            a = jnp.dot(gk_ref[n2], jnp.concatenate([fs[r], bs[r]], axis=1).astype(BF16),
                        preferred_element_type=F32)
            mid_ref[0, pl.ds(n2 * ta, ta)] = a[:, :LANE].reshape(ta, _GRP, LANE)
            mid_ref[1, pl.ds(n2 * ta, ta)] = a[:, LANE:].reshape(ta, _GRP, LANE)
        return 0
    lax.fori_loop(0, n2_len // _GRP, stage1, 0)
    scale = nrm_ref[...] * (1.0 / (n1_len * n2_len))

    def stage2(mk, _):
        parts = [[_tile_rows(mid_ref[j, pl.ds(off + mk, n2_len, stride=ta)]) for off in (0, n1_len // _GRP)]
                 for j in range(2)]
        for r in range(_GRP):
            rhs = jnp.concatenate([jnp.concatenate([parts[j][0][r], parts[j][1][r]], axis=0) for j in range(2)],
                                  axis=1).astype(BF16)
            xs = jnp.dot(f2_ref[...], rhs, preferred_element_type=F32)
            f, b = xs[:, :LANE], xs[:, LANE:]
            o_ref[0, mk * _GRP + r] = (jnp.concatenate([f[:n2_len] + b[:n2_len], f[n2_len:] - b[n2_len:]], axis=0)
                                       * scale).astype(BF16)
        return 0
    lax.fori_loop(0, n1_len // _GRP, stage2, 0)


def _hy_kf(h2d, nrm, gk, f2, n2_len):
    l, cols = h2d.shape
    n1_len = 2 * l // n2_len
    n1h = n1_len // 2
    tiles = cols // (2 * LANE)
    h2d = h2d.reshape(n1h, n2_len, cols)
    return pl.pallas_call(
        functools.partial(_hy_kf_kernel, n1_len=n1_len, n2_len=n2_len),
        grid=(tiles,),
        in_specs=[pl.BlockSpec((n1h, n2_len, LANE), lambda t: (0, 0, t)),
                  pl.BlockSpec((n1h, n2_len, LANE), lambda t: (0, 0, tiles + t)),
                  pl.BlockSpec((1, LANE), lambda t: (0, t)),
                  pl.BlockSpec(gk.shape, lambda t: (0, 0, 0)),
                  pl.BlockSpec(f2.shape, lambda t: (0, 0))],
        out_specs=pl.BlockSpec((1, n1_len, 2 * n2_len, LANE), lambda t: (t, 0, 0, 0)),
        out_shape=jax.ShapeDtypeStruct((tiles, n1_len, 2 * n2_len, LANE), BF16),
        scratch_shapes=[pltpu.VMEM((2, 4 * l // _GRP, _GRP, LANE), F32)],
        compiler_params=_cparams(("parallel",)),
        name="hy_kf",
    )(h2d, h2d, nrm, gk, f2)


def _hy_conv_kernel(zin_ref, gin_ref, cwz_ref, cbz_ref, cwg_ref, cbg_ref, kf_ref, bias_ref, gf_ref, gi_ref,
                    f2_ref, f2c_ref, o_ref, mid_ref, spec_ref, g_ref, *zs, n1_len, n2_len, order, z_raw):
    n1h = n1_len // 2
    l = n1h * n2_len
    ta = 2 * n1_len // _GRP
    tb = 2 * n2_len // _GRP

    def shortconv(x4, w_ref, b_ref):
        x = x4.astype(F32).reshape(l, LANE)
        row = lax.broadcasted_iota(jnp.int32, (l, LANE), 0)
        prev = jnp.where(row == 0, 0.0, pltpu.roll(x, 1, axis=0))
        nxt = jnp.where(row == l - 1, 0.0, pltpu.roll(x, l - 1, axis=0))
        y = w_ref[0:1, :] * prev + w_ref[1:2, :] * x + w_ref[2:3, :] * nxt + b_ref[...]
        return y.reshape(n1h, n2_len, LANE)

    for j in range(2):
        g_ref[j] = shortconv(gin_ref[j], cwg_ref, cbg_ref)
        if z_raw:
            zs[0][j] = shortconv(zin_ref[j], cwz_ref, cbz_ref)
    z_ref = zs[0] if z_raw else zin_ref

    def stage1(m, _):
        zr = _tile_rows(z_ref[0, :, pl.ds(pl.multiple_of(m * _GRP, _GRP), _GRP), :])
        zi = _tile_rows(z_ref[1, :, pl.ds(pl.multiple_of(m * _GRP, _GRP), _GRP), :])
        for r in range(_GRP):
            n2 = m * _GRP + r
            rhs = jnp.concatenate([zr[r], zi[r]], axis=0).astype(BF16)
            a = jnp.dot(gf_ref[n2], rhs, preferred_element_type=F32)
            mid_ref[pl.ds(n2 * ta, ta)] = a.reshape(ta, _GRP, LANE)
        return 0
    lax.fori_loop(0, n2_len // _GRP, stage1, 0)

    def stage2(mk, _):
        xr = _tile_rows(mid_ref[pl.ds(mk, n2_len, stride=ta)])
        xi = _tile_rows(mid_ref[pl.ds(n1_len // _GRP + mk, n2_len, stride=ta)])
        for rp in range(_GRP // 2):
            cols = [jnp.concatenate([xr[2 * rp + j], xi[2 * rp + j]], axis=0) for j in range(2)]
            xs = jnp.dot(f2_ref[...], jnp.concatenate(cols, axis=1).astype(BF16), preferred_element_type=F32)
            for j in range(2):
                k1 = mk * _GRP + 2 * rp + j
                x = xs[:, j * LANE:(j + 1) * LANE]
                kf = kf_ref[0, k1].astype(F32)
                xre, xim = x[:n2_len], x[n2_len:]
                kr, ki = kf[:n2_len], kf[n2_len:]
                spec_ref[k1] = jnp.concatenate([xre * kr - xim * ki, xre * ki + xim * kr], axis=0).astype(BF16)
        return 0
    lax.fori_loop(0, n1_len // _GRP, stage2, 0)

    def stage2i(kp, _):
        r = jnp.concatenate([spec_ref[2 * kp], spec_ref[2 * kp + 1]], axis=1)
        b = jnp.dot(f2c_ref[...], r, preferred_element_type=F32)
        mid_ref[pl.ds(2 * kp * tb, tb)] = b[:, :LANE].reshape(tb, _GRP, LANE)
        mid_ref[pl.ds((2 * kp + 1) * tb, tb)] = b[:, LANE:].reshape(tb, _GRP, LANE)
        return 0
    lax.fori_loop(0, n1_len // 2, stage2i, 0, unroll=_FFT_UNROLL)

    def stage1i(m, _):
        br = _tile_rows(mid_ref[pl.ds(m, n1_len, stride=tb)])
        bi = _tile_rows(mid_ref[pl.ds(n2_len // _GRP + m, n1_len, stride=tb)])
        outs = []
        for r in range(_GRP):
            rhs = jnp.concatenate([br[r], bi[r]], axis=0).astype(BF16)
            outs.append(jnp.dot(gi_ref[m * _GRP + r], rhs, preferred_element_type=F32))
        cols = pl.ds(pl.multiple_of(m * _GRP, _GRP), _GRP)
        o_ref[0, :, cols, :] = jnp.swapaxes(jnp.stack([o[:n1h] for o in outs], axis=0), 0, 1)
        o_ref[1, :, cols, :] = jnp.swapaxes(jnp.stack([o[n1h:] for o in outs], axis=0), 0, 1)
        return 0
    lax.fori_loop(0, n2_len // _GRP, stage1i, 0)

    bias = bias_ref[order:order + 1, :]
    for j in range(2):
        o_ref[j] = g_ref[j] * (o_ref[j] + bias * z_ref[j])


def _hy_conv(z, zcol, z_unit, g, gcol, g_unit, conv_w, conv_b, kf, bias, tabs, n2_len, order):
    gf, _, gi, f2, f2c = tabs
    bsz, l, _ = z.shape
    n1_len = 2 * l // n2_len
    tiles = BR // LANE
    const2 = lambda *_: (0, 0)
    const3 = lambda *_: (0, 0, 0)
    one = pl.Buffered(1)
    n1h = n1_len // 2
    z_raw = z_unit is not None
    zu = z_unit if z_raw else 0
    z = z.reshape(bsz, n1h, n2_len, z.shape[-1])
    g = g.reshape(bsz, n1h, n2_len, g.shape[-1])
    seq = pltpu.VMEM((2, n1h, n2_len, LANE), F32)
    out = pl.pallas_call(
        functools.partial(_hy_conv_kernel, n1_len=n1_len, n2_len=n2_len, order=order, z_raw=z_raw),
        grid=(tiles, bsz // 2),
        in_specs=[pl.BlockSpec((2, n1h, n2_len, LANE), lambda t, b: (b, 0, 0, zcol + t)),
                  pl.BlockSpec((2, n1h, n2_len, LANE), lambda t, b: (b, 0, 0, gcol + t)),
                  pl.BlockSpec((3, LANE), lambda t, b: (0, zu + t)),
                  pl.BlockSpec((1, LANE), lambda t, b: (0, zu + t)),
                  pl.BlockSpec((3, LANE), lambda t, b: (0, g_unit + t)),
                  pl.BlockSpec((1, LANE), lambda t, b: (0, g_unit + t)),
                  pl.BlockSpec((1, n1_len, 2 * n2_len, LANE), lambda t, b: (order * tiles + t, 0, 0, 0),
                               pipeline_mode=one),
                  pl.BlockSpec((HY_ORDER, LANE), lambda t, b: (0, t)),
                  pl.BlockSpec(gf.shape, const3, pipeline_mode=one),
                  pl.BlockSpec(gi.shape, const3, pipeline_mode=one),
                  pl.BlockSpec(f2.shape, const2, pipeline_mode=one),
                  pl.BlockSpec(f2c.shape, const2, pipeline_mode=one)],
        out_specs=pl.BlockSpec((2, n1h, n2_len, LANE), lambda t, b: (b, 0, 0, t)),
        out_shape=jax.ShapeDtypeStruct((bsz, n1h, n2_len, BR), F32),
        scratch_shapes=[pltpu.VMEM((4 * l // _GRP, _GRP, LANE), F32),
                        pltpu.VMEM((n1_len, 2 * n2_len, LANE), BF16), seq] + ([seq] if z_raw else []),
        compiler_params=_cparams(("parallel", "arbitrary")),
        name="hy_conv%d" % order,
    )(z, g, conv_w, conv_b.reshape(1, -1), conv_w, conv_b.reshape(1, -1), kf, bias, gf, gi, f2, f2c)
    return out.reshape(bsz, l, BR)


def _hyena_filters(l, w1, b1, f1, w2, b2, f2, w3, b3, f3, w4):
    hp = lax.Precision.HIGHEST
    t = jnp.linspace(0.0, 1.0, l, dtype=F32)[:, None]
    wpos = 2.0 * math.pi * jnp.arange(l, dtype=F32)[:, None] / l
    f = jnp.linspace(1e-4, HY_BANDS - 1, HY_BANDS, dtype=F32)[None, :]
    z = jnp.concatenate([t, jnp.cos(f * wpos), -jnp.sin(f * wpos)], axis=-1)
    h = jnp.sin(f1 * (jnp.dot(z, w1, precision=hp) + b1))
    h = jnp.sin(f2 * (jnp.dot(h, w2, precision=hp) + b2))
    h = jnp.sin(f3 * (jnp.dot(h, w3, precision=hp) + b3))
    deltas = jnp.linspace(math.log(HY_TARGET) / HY_SLOW_DECAY,
                          math.log(HY_TARGET) / HY_FAST_DECAY, BR, dtype=F32)
    window = jnp.exp(-t * jnp.tile(jnp.abs(deltas), 2 * HY_ORDER)[None, :])
    h = jnp.dot(h, w4, precision=hp) * window
    half = HY_ORDER * BR
    row0 = (jnp.arange(l) == 0)[:, None]
    ss = (jnp.sum(h[:, :half] * h[:, :half], axis=0)
          + jnp.sum(jnp.where(row0, 0.0, h[:, half:] * h[:, half:]), axis=0))
    return h, lax.rsqrt(ss + EPS).reshape(1, half)


def _hyena(p3, conv_w, conv_b, filters, bias, n2_len):
    bsz, l, _ = p3.shape
    tabs = _fft_tables(2 * l, n2_len)
    h2d, nrm = filters
    kf = _hy_kf(h2d, nrm, tabs[1], tabs[3], n2_len)
    tiles = BR // LANE
    cw, cb = conv_w.astype(F32), conv_b.astype(F32)
    z1 = _hy_conv(p3, COL_HY_X, 0, p3, COL_HY_X + tiles, tiles, cw, cb, kf, bias, tabs, n2_len, 0)
    return _hy_conv(z1, 0, None, p3, COL_HY_X + 2 * tiles, 2 * tiles, cw, cb, kf, bias, tabs, n2_len, 1)


def _ret_kernel(q_ref, k_ref, v_ref, cos_ref, sin_ref, lg_ref, nw_ref, s0_ref, o_ref, sfin_ref,
                vb_ref, qf_ref, qb_ref, kf_ref, kb_ref, *, nchunks, c):
    lg = lg_ref[0]
    jrow = lax.broadcasted_iota(jnp.int32, (c, c), 0)
    icol = lax.broadcasted_iota(jnp.int32, (c, c), 1)
    jf = lax.broadcasted_iota(jnp.int32, (c, RET_DK), 0).astype(F32)
    dsym = jnp.exp(jnp.abs(jrow - icol).astype(F32) * lg[:, 0:1]) * jnp.where(jrow == icol, 2.0, 1.0)
    dq_f = jnp.exp((jf + 1.0) * lg)
    dq_b = jnp.exp((c - jf) * lg)
    dk_f = jnp.exp((c - 1.0 - jf) * lg)
    dk_b = jnp.exp(jf * lg)
    dchunk = jnp.exp(c * lg)

    def rope(x, r0):
        return x * cos_ref[pl.ds(r0, c), :] + pltpu.roll(x, RET_DK // 2, axis=1) * sin_ref[pl.ds(r0, c), :]

    def intra_step(n, _):
        r0 = pl.multiple_of(n * c, c)
        rows = pl.ds(r0, c)
        q = rope(q_ref[0, rows, :].astype(F32), r0)
        k = rope(k_ref[0, rows, :].astype(F32), r0) * (RET_DK ** -0.5)
        v = v_ref[0, rows, :].astype(BF16)
        att = _nt(q.astype(BF16), k.astype(BF16)) * dsym
        o_ref[0, rows, :] = jnp.dot(att.astype(BF16), v, preferred_element_type=F32)
        vb_ref[rows, :] = v
        qf_ref[rows, :] = (q * dq_f).astype(BF16)
        qb_ref[rows, :] = (q * dq_b).astype(BF16)
        kf_ref[rows, :] = (k * dk_f).astype(BF16)
        kb_ref[rows, :] = (k * dk_b).astype(BF16)
        return 0

    lax.fori_loop(0, nchunks, intra_step, 0, unroll=_ATT_UNROLL)

    def sweep(lo, hi, carry, finish):
        def step(i, carry):
            s_f, s_b = carry
            rf = pl.ds(pl.multiple_of(i * c, c), c)
            rb = pl.ds(pl.multiple_of((nchunks - 1 - i) * c, c), c)
            o_f = o_ref[0, rf, :] + jnp.dot(qf_ref[rf, :], s_f.astype(BF16), preferred_element_type=F32)
            o_b = o_ref[0, rb, :] + jnp.dot(qb_ref[rb, :], s_b.astype(BF16), preferred_element_type=F32)
            if finish:
                o_f = o_f * lax.rsqrt(jnp.mean(o_f * o_f, axis=-1, keepdims=True) + EPS) * nw_ref[...]
                o_b = o_b * lax.rsqrt(jnp.mean(o_b * o_b, axis=-1, keepdims=True) + EPS) * nw_ref[...]
            o_ref[0, rf, :] = o_f
            o_ref[0, rb, :] = o_b
            return (dchunk * s_f + _tn(kf_ref[rf, :], vb_ref[rf, :]),
                    dchunk * s_b + _tn(kb_ref[rb, :], vb_ref[rb, :]))
        return lax.fori_loop(lo, hi, step, carry)

    carry = sweep(0, nchunks // 2, (s0_ref[0, 0, 0], s0_ref[0, 0, 1]), False)
    s_f, s_b = sweep(nchunks // 2, nchunks, carry, True)
    sfin_ref[0, 0, 0] = s_f
    sfin_ref[0, 0, 1] = s_b


def _retention(p3, cos2, sin2, norm_w, s0):
    bsz, l, _ = p3.shape
    lgs = jnp.log(1.0 - 2.0 ** (-5.0 - jnp.arange(RET_HEADS, dtype=F32)))
    lgs = jnp.broadcast_to(lgs[:, None, None], (RET_HEADS, 1, LANE))
    tab = pl.BlockSpec((l, LANE), lambda b, h: (0, 0))
    st = pl.BlockSpec((1, 1, 2, RET_DK, LANE), lambda b, h: (b, h, 0, 0, 0))
    chunk = RET_CHUNK if l % (2 * RET_CHUNK) == 0 else ATT_CHUNK
    return pl.pallas_call(
        functools.partial(_ret_kernel, nchunks=l // chunk, c=chunk),
        grid=(bsz, RET_HEADS),
        in_specs=[pl.BlockSpec((1, l, LANE), lambda b, h: (b, 0, COL_RET_Q + h)),
                  pl.BlockSpec((1, l, LANE), lambda b, h: (b, 0, COL_RET_K + h)),
                  pl.BlockSpec((1, l, LANE), lambda b, h: (b, 0, COL_RET_V + h)),
                  tab, tab,
                  pl.BlockSpec((1, 1, LANE), lambda b, h: (h, 0, 0)),
                  pl.BlockSpec((1, LANE), lambda b, h: (0, h)),
                  st],
        out_specs=[pl.BlockSpec((1, l, LANE), lambda b, h: (b, 0, h)), st],
        out_shape=[jax.ShapeDtypeStruct((bsz, l, BR), F32),
                   jax.ShapeDtypeStruct((bsz, RET_HEADS, 2, RET_DK, LANE), F32)],
        scratch_shapes=[pltpu.VMEM((l, LANE), BF16)] * 5,
        compiler_params=_cparams(("parallel", "parallel")),
        name="retention",
    )(p3, p3, p3, cos2, sin2, lgs, norm_w.reshape(1, BR), s0)


def _rope_tables(ang):
    cos, sin = jnp.cos(ang), jnp.sin(ang)
    return jnp.concatenate([cos, cos], -1), jnp.concatenate([-sin, sin], -1)


def _latent_angles(rows):
    half = RET_DK // 4
    inv = ROPE_BASE ** (-jnp.arange(half, dtype=F32) / half)
    r = jnp.repeat(jnp.arange(rows, dtype=F32), GRID_W)
    cl = jnp.tile(jnp.arange(GRID_W, dtype=F32), rows)
    return jnp.concatenate([r[:, None] * inv, cl[:, None] * inv], axis=-1)


def _ctx_angles(n_ctx):
    n = RET_DK // 2
    inv = ROPE_BASE ** (-jnp.arange(n, dtype=F32) / n)
    return jnp.arange(n_ctx, dtype=F32)[:, None] * inv


def _gla_kernel(q_ref, k_ref, v_ref, lr_ref, gw_ref, gb_ref, nw_ref, s0_ref, o_ref, sfin_ref,
                vb_ref, qf_ref, qb_ref, kf_ref, kb_ref, df_ref, db_ref, *, nchunks):
    c = ATT_CHUNK
    jrow = lax.broadcasted_iota(jnp.int32, (c, c), 0)
    icol = lax.broadcasted_iota(jnp.int32, (c, c), 1)
    tri_f = jnp.where(icol <= jrow, 1.0, 0.0).astype(BF16)
    tri_b = jnp.where(icol >= jrow, 1.0, 0.0).astype(BF16)
    lane = lax.broadcasted_iota(jnp.int32, (1, LANE), 1)
    head0 = lane < GLA_DK

    gw_both = jnp.concatenate([gw_ref[0], gw_ref[1]], axis=1)
    gb_both = jnp.concatenate([gb_ref[0], gb_ref[1]], axis=1)

    def decays(r0):
        x = jnp.dot(lr_ref[0, pl.ds(r0, c), :].astype(BF16), gw_both, preferred_element_type=F32) + gb_both
        la = (jnp.minimum(x, 0.0) - jnp.log(1.0 + jnp.exp(-jnp.abs(x)))) * (1.0 / GLA_TAU)
        hi = la.astype(BF16)
        lo = (la - hi.astype(F32)).astype(BF16)
        out = []
        for d, tri in enumerate((tri_f, tri_b)):
            cols = slice(d * LANE, (d + 1) * LANE)
            cs = jnp.dot(tri, jnp.concatenate([hi[:, cols], lo[:, cols]], axis=1), preferred_element_type=F32)
            out.append(cs[:, :LANE] + cs[:, LANE:])
        return out

    def intra_step(n, _):
        r0 = pl.multiple_of(n * c, c)
        rows = pl.ds(r0, c)
        q = q_ref[0, rows, :].astype(F32) * (GLA_DK ** -0.5)
        k = k_ref[0, rows, :].astype(F32)
        bf, bb = decays(r0)
        ref_f = bf[c // 2 - 1:c // 2, :]
        ref_b = bb[c // 2:c // 2 + 1, :]
        qf = q * jnp.exp(bf - ref_f)
        kf = (k * jnp.exp(ref_f - bf)).astype(BF16)
        qb = q * jnp.exp(bb - ref_b)
        kb = (k * jnp.exp(ref_b - bb)).astype(BF16)
        for h in range(2):
            hm = head0 if h == 0 else jnp.logical_not(head0)
            v = v_ref[0, rows, h * GLA_DV:(h + 1) * GLA_DV].astype(BF16)
            att = (jnp.where(icol <= jrow, _nt(jnp.where(hm, qf, 0.0).astype(BF16), kf), 0.0)
                   + jnp.where(icol >= jrow, _nt(jnp.where(hm, qb, 0.0).astype(BF16), kb), 0.0))
            o_ref[0, rows, h * GLA_DV:(h + 1) * GLA_DV] = jnp.dot(att.astype(BF16), v, preferred_element_type=F32)
            vb_ref[rows, h * GLA_DV:(h + 1) * GLA_DV] = v
        bf_last, bb_last = bf[c - 1:c, :], bb[0:1, :]
        qf_ref[rows, :] = (q * jnp.exp(bf)).astype(BF16)
        qb_ref[rows, :] = (q * jnp.exp(bb)).astype(BF16)
        kf_ref[rows, :] = (k * jnp.exp(bf_last - bf)).astype(BF16)
        kb_ref[rows, :] = (k * jnp.exp(bb_last - bb)).astype(BF16)
        d0 = pl.multiple_of(n * 8, 8)
        df_ref[pl.ds(d0, 8), :] = jnp.broadcast_to(jnp.exp(bf_last), (8, LANE))
        db_ref[pl.ds(d0, 8), :] = jnp.broadcast_to(jnp.exp(bb_last), (8, LANE))
        return 0

    lax.fori_loop(0, nchunks, intra_step, 0, unroll=_ATT_UNROLL)

    def sweep(lo, hi, carry, finish):
        def step(i, carry):
            sts = list(carry)
            new = []
            for d in range(2):
                n = i if d == 0 else nchunks - 1 - i
                rows = pl.ds(pl.multiple_of(n * c, c), c)
                qs = (qf_ref if d == 0 else qb_ref)[rows, :]
                kd = (kf_ref if d == 0 else kb_ref)[rows, :]
                dec = (df_ref if d == 0 else db_ref)[pl.ds(pl.multiple_of(n * 8, 8), 1), :]
                for h in range(2):
                    hm = head0 if h == 0 else jnp.logical_not(head0)
                    cols = slice(h * GLA_DV, (h + 1) * GLA_DV)
                    st = sts[2 * d + h]
                    o = o_ref[0, rows, cols] + _nt(qs, st.astype(BF16))
                    if finish:
                        o = o * lax.rsqrt(jnp.mean(o * o, axis=-1, keepdims=True) + EPS) * nw_ref[:, cols]
                    o_ref[0, rows, cols] = o
                    new.append(jnp.where(hm, dec * st + _tn(vb_ref[rows, cols], kd), 0.0))
            return tuple(new)
        return lax.fori_loop(lo, hi, step, carry)

    carry = (s0_ref[0, 0, 0, 0], s0_ref[0, 0, 0, 1], s0_ref[0, 0, 1, 0], s0_ref[0, 0, 1, 1])
    carry = sweep(0, nchunks // 2, carry, False)
    s_fin = sweep(nchunks // 2, nchunks, carry, True)
    sfin_ref[0, 0, 0, 0] = s_fin[0]
    sfin_ref[0, 0, 0, 1] = s_fin[1]
    sfin_ref[0, 0, 1, 0] = s_fin[2]
    sfin_ref[0, 0, 1, 1] = s_fin[3]


def _gla(p3, gate_w, gate_b, norm_w, s0):
    bsz, l, _ = p3.shape
    pairs = GLA_HEADS // 2
    gw = jnp.zeros((2, LANE, GLA_HEADS * GLA_DK), F32)
    for d in range(2):
        gw = gw.at[d, d * GLA_LR:(d + 1) * GLA_LR, :].set(gate_w[d].astype(F32))
    st = pl.BlockSpec((1, 1, 2, 2, GLA_DV, LANE), lambda b, h: (b, h, 0, 0, 0, 0))
    return pl.pallas_call(
        functools.partial(_gla_kernel, nchunks=l // ATT_CHUNK),
        grid=(bsz, pairs),
        in_specs=[pl.BlockSpec((1, l, LANE), lambda b, h: (b, 0, COL_GLA_Q + h)),
                  pl.BlockSpec((1, l, LANE), lambda b, h: (b, 0, COL_GLA_K + h)),
                  pl.BlockSpec((1, l, 2 * GLA_DV), lambda b, h: (b, 0, COL_GLA_V // 2 + h)),
                  pl.BlockSpec((1, l, LANE), lambda b, h: (b, 0, COL_GLA_LR)),
                  pl.BlockSpec((2, LANE, LANE), lambda b, h: (0, 0, h)),
                  pl.BlockSpec((2, 1, LANE), lambda b, h: (0, 0, h)),
                  pl.BlockSpec((1, 2 * GLA_DV), lambda b, h: (0, h)),
                  st],
        out_specs=[pl.BlockSpec((1, l, 2 * GLA_DV), lambda b, h: (b, 0, h)), st],
        out_shape=[jax.ShapeDtypeStruct((bsz, l, BR), F32),
                   jax.ShapeDtypeStruct((bsz, pairs, 2, 2, GLA_DV, LANE), F32)],
        scratch_shapes=([pltpu.VMEM((l, 2 * GLA_DV), BF16)] + [pltpu.VMEM((l, LANE), BF16)] * 4
                        + [pltpu.VMEM((8 * l // ATT_CHUNK, LANE), F32)] * 2),
        compiler_params=_cparams(("parallel", "parallel")),
        name="gla",
    )(p3, p3, p3, p3, gw.astype(BF16), gate_b.astype(F32).reshape(2, 1, GLA_HEADS * GLA_DK),
      norm_w.reshape(1, BR), s0)


def _mixers(p2, xs, bsz, prm, i, s5_tabs, hy_filters, rope, s5_h0, ret_s0, gla_s0, with_hyena):
    m = p2.shape[0]
    l = m // bsz
    p3 = p2.reshape(bsz, l, NP)
    y, s5_fin = _s5_core(xs, s5_tabs, s5_h0)
    s5 = _s5_glu(y, p2, prm['s5_d'][i], prm['s5_glu_w'][i], prm['s5_glu_b'][i])
    hy = None
    if with_hyena:
        hy = _hyena(p3, prm['hy_conv_w'][i], prm['hy_conv_b'][i], hy_filters, prm['hy_bias'][i].astype(F32),
                    64 if l >= 512 else 16).reshape(m, BR)
    ret, ret_fin = _retention(p3, rope[0], rope[1], prm['ret_norm_w'][i], ret_s0)
    gla, gla_fin = _gla(p3, prm['gla_gate_w'][i], prm['gla_gate_b'][i], prm['gla_norm_w'][i], gla_s0)
    return (s5, hy, ret.reshape(m, BR), gla.reshape(m, BR)), (s5_fin, ret_fin, gla_fin)


def kernel(x, c, ctx, c_ctx, norm_w, ada_w, ada_b, w_in, w_out, s5_a_re, s5_a_im, s5_log_dt, s5_b_re, s5_b_im, s5_c_re, s5_c_im, s5_d, s5_glu_w, s5_glu_b, hy_conv_w, hy_conv_b, hy_w1, hy_b1, hy_f1, hy_w2, hy_b2, hy_f2, hy_w3, hy_b3, hy_f3, hy_w4, hy_bias, ret_norm_w, gla_gate_w, gla_gate_b, gla_norm_w, final_norm_w):
    prm = dict(s5_a_re=s5_a_re, s5_a_im=s5_a_im, s5_log_dt=s5_log_dt, s5_b_re=s5_b_re, s5_b_im=s5_b_im,
               s5_c_re=s5_c_re, s5_c_im=s5_c_im, s5_d=s5_d, s5_glu_w=s5_glu_w, s5_glu_b=s5_glu_b,
               hy_conv_w=hy_conv_w, hy_conv_b=hy_conv_b, hy_w1=hy_w1, hy_b1=hy_b1, hy_f1=hy_f1,
               hy_w2=hy_w2, hy_b2=hy_b2, hy_f2=hy_f2, hy_w3=hy_w3, hy_b3=hy_b3, hy_f3=hy_f3,
               hy_w4=hy_w4, hy_bias=hy_bias, ret_norm_w=ret_norm_w, gla_gate_w=gla_gate_w,
               gla_gate_b=gla_gate_b, gla_norm_w=gla_norm_w)
    bsz, l, d = x.shape
    lc = ctx.shape[1]
    depth = w_in.shape[0]
    cc = jnp.concatenate([c, c_ctx[None, :], jnp.zeros((8 - bsz - 1, d), F32)], axis=0)
    mod = _ada(cc, ada_w, ada_b)
    rope_l = _rope_tables(_latent_angles(l // GRID_W))
    rope_c = _rope_tables(_ctx_angles(lc))
    s5_zero = jnp.zeros((S5_GROUPS // 2, 2, 2, 8, 2 * S5_STATE), F32)
    ret_zero = jnp.zeros((bsz, RET_HEADS, 2, RET_DK, LANE), F32)
    gla_zero = jnp.zeros((bsz, GLA_HEADS // 2, 2, 2, GLA_DV, LANE), F32)

    w_all = _wprep(w_in)
    wo_all = w_out.astype(BF16)
    filt = (hy_w1, hy_b1, hy_f1, hy_w2, hy_b2, hy_f2, hy_w3, hy_b3, hy_f3, hy_w4)

    x_l = x.reshape(bsz * l, d)
    x_c = ctx.reshape(bsz * lc, d)
    for i in range(depth):
        last = i == depth - 1
        sh, sc, gt = mod[i, :, :d], mod[i, :, d:2 * d], mod[i, :, 2 * d:]
        p_c, xs_c = _inproj(x_c, norm_w[i], sc[bsz:bsz + 1, None, :], sh[bsz:bsz + 1, None, :], w_all, i,
                            bsz * lc)
        p_l, xs_l = _inproj(x_l, norm_w[i], sc[:bsz, None, :], sh[:bsz, None, :], w_all, i, l)
        s5_tabs = _s5_tables(s5_a_re[i], s5_a_im[i], s5_log_dt[i], s5_b_re[i], s5_b_im[i], s5_c_re[i],
                             s5_c_im[i])
        filt_i = [w[i].astype(F32) for w in filt]
        ys_c, (s5_fin, ret_fin, gla_fin) = _mixers(
            p_c, xs_c, bsz, prm, i, s5_tabs, None if last else _hyena_filters(lc, *filt_i), rope_c, s5_zero,
            ret_zero, gla_zero, with_hyena=not last)
        ys_l, _ = _mixers(p_l, xs_l, bsz, prm, i, s5_tabs, _hyena_filters(l, *filt_i), rope_l,
                          _s5_next_h0(s5_fin), ret_fin, gla_fin, with_hyena=True)
        x_l = _outproj(x_l, ys_l, p_l, wo_all, i, gt[:bsz, None, :], final_norm_w, l, final=last)
        if not last:
            x_c = _outproj(x_c, ys_c, p_c, wo_all, i, gt[bsz:bsz + 1, None, :], final_norm_w, bsz * lc,
                           final=False)
    return x_l.reshape(bsz, l, d)
```

```python
import functools
import math

import numpy as np
import jax
import jax.numpy as jnp
from jax import lax
from jax.experimental import pallas as pl
from jax.experimental.pallas import tpu as pltpu

F32 = jnp.float32
BF16 = jnp.bfloat16

EPS = 1e-6
GRID_W = 64
BR = 512
S5_GSIZE = 16
S5_GROUPS = BR // S5_GSIZE
S5_STATE = 64
S5_T = 16
HY_ORDER = 2
HY_EMB = 33
HY_BANDS = (HY_EMB - 1) // 2
HY_FAST_DECAY = 0.3
HY_SLOW_DECAY = 1.5
HY_TARGET = 1e-2
RET_HEADS = 4
RET_DK = 128
GLA_HEADS = 4
GLA_DK = 64
GLA_DV = 128
GLA_LR = 16
GLA_TAU = 16.0
ROPE_BASE = 10000.0
ATT_CHUNK = 128
RET_CHUNK = 256
_FFT_UNROLL = 8
_ATT_UNROLL = 2

LANE = 128
VMEM_LIMIT = 56 * 1024 * 1024

COL_S5_U = 0
COL_S5_G = 4
COL_HY_X = 8
COL_HY_G = 20
COL_RET_Q = 24
COL_RET_K = 28
COL_RET_V = 32
COL_RET_G = 36
COL_GLA_Q = 40
COL_GLA_K = 42
COL_GLA_V = 44
COL_GLA_G = 48
COL_GLA_LR = 52
NP_UNITS = 54
NP = NP_UNITS * LANE
W_TILE = 768
_ALIGNED = COL_GLA_G * LANE


def _cparams(sem):
    return pltpu.CompilerParams(dimension_semantics=sem, vmem_limit_bytes=VMEM_LIMIT)


def _silu(x):
    return x * jax.nn.sigmoid(x)


def _wprep(w_in):
    depth, d, in_w = w_in.shape
    assert _ALIGNED % W_TILE == 0 and in_w == _ALIGNED + 2 * GLA_LR + BR and NP == _ALIGNED + W_TILE
    wb = w_in.astype(BF16)
    tail = jnp.concatenate([wb[:, :, _ALIGNED + 2 * GLA_LR:], wb[:, :, _ALIGNED:_ALIGNED + 2 * GLA_LR],
                            jnp.zeros((depth, d, W_TILE - BR - 2 * GLA_LR), BF16)], axis=-1)
    return wb, tail


def _ada_kernel(c_ref, w_ref, b_ref, o_ref):
    a = _silu(c_ref[...]).astype(BF16)
    o_ref[0] = jnp.dot(a, w_ref[0].astype(BF16), preferred_element_type=F32) + b_ref[0]


def _ada(cc, ada_w, ada_b, tn=768):
    depth, d, n = ada_w.shape
    return pl.pallas_call(
        _ada_kernel,
        grid=(depth, n // tn),
        in_specs=[pl.BlockSpec((8, d), lambda i, j: (0, 0)),
                  pl.BlockSpec((1, d, tn), lambda i, j: (i, 0, j)),
                  pl.BlockSpec((1, 1, tn), lambda i, j: (i, 0, j))],
        out_specs=pl.BlockSpec((1, 8, tn), lambda i, j: (i, 0, j)),
        out_shape=jax.ShapeDtypeStruct((depth, 8, n), F32),
        compiler_params=_cparams(("parallel", "parallel")),
        name="ada_mod",
    )(cc, ada_w, ada_b.reshape(depth, 1, n))


_S5_SLOT = 2 * S5_GSIZE
_S5_PAIRS = S5_GROUPS // 2
_S5_ROW = S5_T * _S5_SLOT


def _s5_lane_perm():
    perm = np.zeros((4, _S5_ROW), np.int32)
    for s in range(4):
        for tau in range(S5_T):
            for c in range(_S5_SLOT):
                gl, i = divmod(c, S5_GSIZE)
                perm[s, LANE * (tau // 4) + _S5_SLOT * ((s + tau) % 4) + c] = (gl * S5_T + tau) * S5_GSIZE + i
    return perm


def _s5_to_chunk_rows(nat_ref, xs_ref, rows16):
    slot = lax.broadcasted_iota(jnp.int32, (rows16, LANE), 1) // _S5_SLOT
    for q in range(4):
        for qd in range(4):
            rolled = []
            for t in range(4):
                piece = nat_ref[q, pl.ds(4 * qd + t, rows16, stride=S5_T), :]
                rolled.append(pltpu.roll(piece, _S5_SLOT * t, axis=1) if t else piece)
            for s in range(4):
                acc = rolled[0]
                for t in range(1, 4):
                    acc = jnp.where(slot == (s + t) % 4, rolled[t], acc)
                xs_ref[4 * q + s, :, qd * LANE:(qd + 1) * LANE] = acc.astype(xs_ref.dtype)


def _s5_from_chunk_rows(y_ref, nat_ref, rows16):
    slot = lax.broadcasted_iota(jnp.int32, (rows16, LANE), 1) // _S5_SLOT
    for q in range(4):
        for qd in range(4):
            src = [y_ref[4 * q + s, :, qd * LANE:(qd + 1) * LANE] for s in range(4)]
            for t in range(4):
                acc = src[0]
                for s in range(1, 4):
                    acc = jnp.where(slot == (s + t) % 4, src[s], acc)
                if t:
                    acc = pltpu.roll(acc, LANE - _S5_SLOT * t, axis=1)
                nat_ref[q, pl.ds(4 * qd + t, rows16, stride=S5_T), :] = acc


def _inproj_kernel(x_ref, nw_ref, sc_ref, sh_ref, w_ref, tail_ref, o_ref, xs_ref, h_ref, nat_ref, *, rows16):
    j = pl.program_id(1)
    last = pl.num_programs(1) - 1

    def project(w):
        res = jnp.dot(h_ref[...], w, preferred_element_type=F32)
        o_ref[...] = res.astype(o_ref.dtype)
        return res

    @pl.when(j == 0)
    def _():
        x = x_ref[...]
        y = x * lax.rsqrt(jnp.mean(x * x, axis=-1, keepdims=True) + EPS) * nw_ref[...]
        h_ref[...] = (y * (1.0 + sc_ref[0]) + sh_ref[0]).astype(BF16)
        res = project(w_ref[...])
        for q in range(BR // LANE):
            nat_ref[q] = res[:, q * LANE:(q + 1) * LANE]
        _s5_to_chunk_rows(nat_ref, xs_ref, rows16)

    @pl.when((j > 0) & (j < last))
    def _():
        project(w_ref[...])

    @pl.when(j == last)
    def _():
        project(tail_ref[...])


def _inproj(x2, nw, sc, sh, weights, layer, rows_per_mod, tm=1024, tn=W_TILE):
    w_all, w_tail = weights
    m, d = x2.shape
    tm = min(tm, rows_per_mod)
    per = rows_per_mod // tm
    rows16 = tm // S5_T
    n_al = _ALIGNED // tn
    assert COL_S5_U == 0 and tn >= BR and tn == W_TILE
    return pl.pallas_call(
        functools.partial(_inproj_kernel, rows16=rows16),
        grid=(m // tm, NP // tn),
        in_specs=[pl.BlockSpec((tm, d), lambda i, j: (i, 0)),
                  pl.BlockSpec((1, d), lambda i, j: (0, 0)),
                  pl.BlockSpec((1, 1, d), lambda i, j: (i // per, 0, 0)),
                  pl.BlockSpec((1, 1, d), lambda i, j: (i // per, 0, 0)),
                  pl.BlockSpec((None, d, tn), lambda i, j: (layer, 0, jnp.minimum(j, n_al - 1))),
                  pl.BlockSpec((None, d, tn), lambda i, j: (layer, 0, 0))],
        out_specs=[pl.BlockSpec((tm, tn), lambda i, j: (i, j)),
                   pl.BlockSpec((_S5_PAIRS, rows16, _S5_ROW), lambda i, j: (0, i, 0))],
        out_shape=[jax.ShapeDtypeStruct((m, NP), BF16),
                   jax.ShapeDtypeStruct((_S5_PAIRS, m // S5_T, _S5_ROW), BF16)],
        scratch_shapes=[pltpu.VMEM((tm, d), BF16), pltpu.VMEM((BR // LANE, tm, LANE), F32)],
        compiler_params=_cparams(("parallel", "arbitrary")),
        name="inproj",
    )(x2, nw.reshape(1, d), sc, sh, w_all, w_tail)


def _outproj_kernel(x_ref, y0_ref, y1_ref, y2_ref, y3_ref, g0_ref, g1_ref, g2_ref, g3_ref, w_ref, gt_ref, fw_ref,
                    o_ref, *, final):
    acc = None
    for k, (y_ref, g_ref) in enumerate(((y0_ref, g0_ref), (y1_ref, g1_ref), (y2_ref, g2_ref), (y3_ref, g3_ref))):
        gated = (y_ref[...] * _silu(g_ref[...].astype(F32))).astype(BF16)
        part = jnp.dot(gated, w_ref[k * BR:(k + 1) * BR, :], preferred_element_type=F32)
        acc = part if acc is None else acc + part
    x = x_ref[...] + gt_ref[0] * acc
    if final:
        x = x * lax.rsqrt(jnp.mean(x * x, axis=-1, keepdims=True) + EPS) * fw_ref[...]
    o_ref[...] = x


def _outproj(x2, ys, p, wo_all, layer, gt, fw, rows_per_mod, final, tm=256):
    m, d = x2.shape
    tm = min(tm, rows_per_mod)
    per = rows_per_mod // tm
    yspec = pl.BlockSpec((tm, BR), lambda i: (i, 0))

    def gate(col_unit):
        blk = col_unit * LANE // BR
        return pl.BlockSpec((tm, BR), lambda i: (i, blk))

    return pl.pallas_call(
        functools.partial(_outproj_kernel, final=final),
        grid=(m // tm,),
        in_specs=[pl.BlockSpec((tm, d), lambda i: (i, 0)), yspec, yspec, yspec, yspec,
                  gate(COL_S5_G), gate(COL_HY_G), gate(COL_RET_G), gate(COL_GLA_G),
                  pl.BlockSpec((None, 4 * BR, d), lambda i: (layer, 0, 0)),
                  pl.BlockSpec((1, 1, d), lambda i: (i // per, 0, 0)),
                  pl.BlockSpec((1, d), lambda i: (0, 0))],
        out_specs=pl.BlockSpec((tm, d), lambda i: (i, 0)),
        out_shape=jax.ShapeDtypeStruct((m, d), F32),
        compiler_params=_cparams(("parallel",)),
        name="outproj",
    )(x2, *ys, p, p, p, p, wo_all, gt, fw.reshape(1, d))


def _s5_tables(a_re, a_im, log_dt, b_re, b_im, c_re, c_im):
    hp = lax.Precision.HIGHEST
    t_len, g_n, p_n, s_n = S5_T, S5_GROUPS, S5_STATE, S5_GSIZE
    a_re, a_im = a_re.astype(F32), a_im.astype(F32)
    dt = jnp.exp(log_dt.astype(F32))[:, :, None]
    lam_re, lam_im = a_re * dt, a_im * dt

    tau = jnp.arange(t_len + 1, dtype=F32)[None, None, :, None]
    mag = jnp.exp(lam_re[:, :, None, :] * tau)
    pr, pi = mag * jnp.cos(lam_im[:, :, None, :] * tau), mag * jnp.sin(lam_im[:, :, None, :] * tau)
    ab_re, ab_im = pr[:, :, 1], pi[:, :, 1]
    den = a_re * a_re + a_im * a_im
    nr = ab_re - 1.0
    co_re = (nr * a_re + ab_im * a_im) / den
    co_im = (ab_im * a_re - nr * a_im) / den
    b_re, b_im = b_re.astype(F32), b_im.astype(F32)
    bco_re = co_re[..., None] * b_re - co_im[..., None] * b_im
    bco_im = co_re[..., None] * b_im + co_im[..., None] * b_re
    c_re, c_im = c_re.astype(F32), c_im.astype(F32)

    ca_re = c_re[:, :, None] * pr[:, :, :, None, :] - c_im[:, :, None] * pi[:, :, :, None, :]
    ca_im = c_re[:, :, None] * pi[:, :, :, None, :] + c_im[:, :, None] * pr[:, :, :, None, :]
    ca_cat = jnp.concatenate([ca_re[:, :, :t_len], -ca_im[:, :, :t_len]], axis=-1)
    bco_cat = jnp.concatenate([bco_re, bco_im], axis=-2)
    kk = jnp.einsum('dgmp,dgpj->dgmj', ca_cat.reshape(2, g_n, t_len * s_n, 2 * p_n), bco_cat, precision=hp)
    kk = kk.reshape(2, g_n, t_len, s_n, s_n)
    kf, kb = kk[0], kk[1]
    kfull = jnp.concatenate([kb[:, :0:-1], (kf[:, 0] + kb[:, 0])[:, None], kf[:, 1:]], axis=1)
    kcat = kfull.transpose(0, 3, 1, 2).reshape(g_n, s_n, (2 * t_len - 1) * s_n)
    kcat = jnp.pad(kcat, ((0, 0), (0, 0), (0, _S5_ROW - kcat.shape[-1])))

    we_re = jnp.stack([pr[0, :, :t_len][:, ::-1], pr[1, :, :t_len]], 0)
    we_im = jnp.stack([pi[0, :, :t_len][:, ::-1], pi[1, :, :t_len]], 0)
    bt_re, bt_im = bco_re.transpose(0, 1, 3, 2)[:, :, None], bco_im.transpose(0, 1, 3, 2)[:, :, None]
    w_re = we_re[:, :, :, None, :] * bt_re - we_im[:, :, :, None, :] * bt_im
    w_im = we_re[:, :, :, None, :] * bt_im + we_im[:, :, :, None, :] * bt_re
    wcat = jnp.concatenate([w_re[0], w_im[0], w_re[1], w_im[1]], axis=-1).reshape(g_n, t_len * s_n, 4 * p_n)

    vcat_t = jnp.concatenate([ca_re[0, :, 1:], -ca_im[0, :, 1:], ca_re[1, :, 1:][:, ::-1], -ca_im[1, :, 1:][:, ::-1]],
                             axis=-1).reshape(g_n, t_len * s_n, 4 * p_n)

    pw = jnp.stack([pr[:, :, t_len], pi[:, :, t_len]], 0).reshape(2, 2, g_n // 2, 1, 2 * p_n).transpose(2, 1, 0, 3, 4)
    return _s5_assemble(kcat, wcat, vcat_t) + (pw,)


def _s5_assemble_kernel(kcat_ref, wcat_ref, vcat_ref, e_ref, oh_ref, oht_ref, m_ref, w_ref, v_ref):
    half = S5_T * S5_GSIZE
    oh = oh_ref[0]
    zeros = jnp.zeros((half, half), F32)
    diag = []
    for gl in range(2):
        kc = kcat_ref[gl]
        diag.append(jnp.concatenate(
            [kc[:, (S5_T - 1 - t) * S5_GSIZE:(S5_T - 1 - t) * S5_GSIZE + half] for t in range(S5_T)], axis=0))
    m_old = jnp.concatenate([jnp.concatenate([diag[0], zeros], axis=1),
                             jnp.concatenate([zeros, diag[1]], axis=1)], axis=0).astype(BF16)
    m_rows = jnp.dot(oh, m_old, preferred_element_type=F32).astype(BF16)
    m_ref[0] = _nt(m_rows, oh).astype(BF16)
    w_old = jnp.concatenate([jnp.dot(wcat_ref[gl].astype(BF16), e_ref[gl], preferred_element_type=F32)
                             for gl in range(2)], axis=0).astype(BF16)
    w_ref[0] = jnp.dot(oh, w_old, preferred_element_type=F32).astype(BF16)
    v_old_t = jnp.concatenate([jnp.dot(vcat_ref[gl].astype(BF16), e_ref[gl], preferred_element_type=F32)
                               for gl in range(2)], axis=0).astype(BF16)
    v_ref[0] = _tn(v_old_t, oht_ref[0]).astype(BF16)


def _s5_assemble(kcat, wcat, vcat_t):
    half = S5_T * S5_GSIZE
    perm = np.eye(_S5_ROW, dtype=np.float32)[_s5_lane_perm()]
    onehot = jnp.asarray(perm).astype(BF16)
    onehot_t = jnp.asarray(np.ascontiguousarray(perm.transpose(0, 2, 1))).astype(BF16)
    place = np.zeros((2, half, _S5_ROW), np.float32)
    for gl in range(2):
        for k in range(4):
            for p in range(S5_STATE):
                place[gl, k * S5_STATE + p, (2 * k + gl) * S5_STATE + p] = 1.0
    out = jax.ShapeDtypeStruct((_S5_PAIRS, _S5_ROW, _S5_ROW), BF16)
    big = pl.BlockSpec((1, _S5_ROW, _S5_ROW), lambda g: (g, 0, 0))
    return pl.pallas_call(
        _s5_assemble_kernel,
        grid=(_S5_PAIRS,),
        in_specs=[pl.BlockSpec((2, S5_GSIZE, _S5_ROW), lambda g: (g, 0, 0)),
                  pl.BlockSpec((2, half, half), lambda g: (g, 0, 0)),
                  pl.BlockSpec((2, half, half), lambda g: (g, 0, 0)),
                  pl.BlockSpec((2, half, _S5_ROW), lambda g: (0, 0, 0)),
                  pl.BlockSpec((1, _S5_ROW, _S5_ROW), lambda g: (g % 4, 0, 0)),
                  pl.BlockSpec((1, _S5_ROW, _S5_ROW), lambda g: (g % 4, 0, 0))],
        out_specs=[big, big, big],
        out_shape=[out, out, out],
        compiler_params=_cparams(("parallel",)),
        name="s5_assemble",
    )(kcat, wcat, vcat_t, jnp.asarray(place).astype(BF16), onehot, onehot_t)


def _nt(a, b):
    return lax.dot_general(a, b, (((1,), (1,)), ((), ())), preferred_element_type=F32)


def _tn(a, b):
    return lax.dot_general(a, b, (((0,), (0,)), ((), ())), preferred_element_type=F32)


def _cmul(ar, ai, br, bi):
    return ar * br - ai * bi, ar * bi + ai * br


def _s5_kernel(x_ref, m_ref, w_ref, v_ref, pw_ref, h0_ref, y_ref, fin_ref, s_ref, h_ref, *, nch):
    ln = 2 * S5_STATE
    x = x_ref[0]
    s_all = jnp.dot(x, w_ref[0], preferred_element_type=F32)
    for k in range(4):
        s_ref[k] = s_all[:, k * ln:(k + 1) * ln]
    a_rf, a_if = pw_ref[0, 0, 0], pw_ref[0, 0, 1]
    a_rb, a_ib = pw_ref[0, 1, 0], pw_ref[0, 1, 1]

    nb = 4

    def tile(half, c):
        return pl.ds(half * nch + c, nb, stride=2 * nch)

    def sweep(half_f, half_b, carry):
        def step(s, carry):
            hrf, hif, hrb, hib = carry
            rf, rb = tile(half_f, s), tile(half_b, nch - 1 - s)
            h_ref[0, rf, :] = hrf
            h_ref[1, rf, :] = hif
            h_ref[2, rb, :] = hrb
            h_ref[3, rb, :] = hib
            pr, pi = _cmul(a_rf, a_if, hrf, hif)
            qr, qi = _cmul(a_rb, a_ib, hrb, hib)
            return (pr + s_ref[0, rf, :], pi + s_ref[1, rf, :], qr + s_ref[2, rb, :], qi + s_ref[3, rb, :])
        return lax.fori_loop(0, nch, step, carry, unroll=2)

    lead_f, lead_b = pl.ds(0, nb, stride=2), pl.ds(1, nb, stride=2)
    init = (h0_ref[0, 0, 0, lead_f, :], h0_ref[0, 0, 1, lead_f, :],
            h0_ref[0, 1, 0, lead_b, :], h0_ref[0, 1, 1, lead_b, :])
    hrf, hif, hrb, hib = sweep(1, 0, sweep(0, 1, init))
    fin_ref[...] = jnp.zeros(fin_ref.shape, F32)
    fin_ref[0, 0, 0, lead_b, :] = hrf
    fin_ref[0, 0, 1, lead_b, :] = hif
    fin_ref[0, 1, 0, lead_f, :] = hrb
    fin_ref[0, 1, 1, lead_f, :] = hib

    h_all = jnp.concatenate([h_ref[k] for k in range(4)], axis=1).astype(BF16)
    y_ref[0] = (jnp.dot(x, m_ref[0], preferred_element_type=F32)
                + jnp.dot(h_all, v_ref[0], preferred_element_type=F32))


def _s5_core(xs, tables, h0):
    mpair, wpair, vpair, pw = tables
    gp_n, rows, wdt = xs.shape
    nch = rows // 8
    return pl.pallas_call(
        functools.partial(_s5_kernel, nch=nch),
        grid=(gp_n,),
        in_specs=[pl.BlockSpec((1, rows, wdt), lambda g: (g, 0, 0)),
                  pl.BlockSpec((1, wdt, wdt), lambda g: (g, 0, 0)),
                  pl.BlockSpec((1, wdt, wdt), lambda g: (g, 0, 0)),
                  pl.BlockSpec((1, wdt, wdt), lambda g: (g, 0, 0)),
                  pl.BlockSpec((1,) + pw.shape[1:], lambda g: (g, 0, 0, 0, 0)),
                  pl.BlockSpec((1, 2, 2, 8, 2 * S5_STATE), lambda g: (g, 0, 0, 0, 0))],
        out_specs=[pl.BlockSpec((1, rows, wdt), lambda g: (g, 0, 0)),
                   pl.BlockSpec((1, 2, 2, 8, 2 * S5_STATE), lambda g: (g, 0, 0, 0, 0))],
        out_shape=[jax.ShapeDtypeStruct((gp_n, rows, wdt), F32),
                   jax.ShapeDtypeStruct((gp_n, 2, 2, 8, 2 * S5_STATE), F32)],
        scratch_shapes=[pltpu.VMEM((4, rows, 2 * S5_STATE), F32), pltpu.VMEM((4, rows, 2 * S5_STATE), F32)],
        compiler_params=_cparams(("parallel",)),
        name="s5_core",
    )(xs, mpair, wpair, vpair, pw, h0)


def _s5_next_h0(fin):
    g, d, r, _, n = fin.shape
    sw = fin.reshape(g, d, r, 4, 2, n)[:, :, :, :, ::-1, :]
    keep = jnp.array([[1.0, 0.0], [0.0, 1.0]], F32)[None, :, None, None, :, None]
    return (sw * keep).reshape(fin.shape)


def _s5_glu_kernel(y_ref, u_ref, d_ref, w_ref, b_ref, o_ref, nat_ref, *, rows16):
    _s5_from_chunk_rows(y_ref, nat_ref, rows16)
    y = jnp.concatenate([nat_ref[q] for q in range(BR // LANE)], axis=1) + u_ref[...].astype(F32) * d_ref[...]
    g = jax.nn.gelu(y)
    z = jnp.dot(g.astype(BF16), w_ref[...], preferred_element_type=F32) + b_ref[...]
    o_ref[...] = g * jax.nn.sigmoid(z)


def _s5_glu(ys, p, d_skip, glu_w, glu_b, tm=512):
    m = p.shape[0]
    tm = min(tm, m)
    rows16 = tm // S5_T
    return pl.pallas_call(
        functools.partial(_s5_glu_kernel, rows16=rows16),
        grid=(m // tm,),
        in_specs=[pl.BlockSpec((_S5_PAIRS, rows16, _S5_ROW), lambda i: (0, i, 0)),
                  pl.BlockSpec((tm, BR), lambda i: (i, COL_S5_U * LANE // BR)),
                  pl.BlockSpec((1, BR), lambda i: (0, 0)),
                  pl.BlockSpec((BR, BR), lambda i: (0, 0)),
                  pl.BlockSpec((1, BR), lambda i: (0, 0))],
        out_specs=pl.BlockSpec((tm, BR), lambda i: (i, 0)),
        out_shape=jax.ShapeDtypeStruct((m, BR), F32),
        scratch_shapes=[pltpu.VMEM((BR // LANE, tm, LANE), F32)],
        compiler_params=_cparams(("parallel",)),
        name="s5_glu",
    )(ys, p, d_skip.reshape(1, BR), glu_w.astype(BF16), glu_b.reshape(1, BR))


@functools.lru_cache(maxsize=None)
def _fft_tables_np(n_total, n2_len):
    n1_len = n_total // n2_len
    n1h = n1_len // 2
    k1 = np.arange(n1_len, dtype=np.int64)
    n2 = np.arange(n2_len, dtype=np.int64)
    m = (k1[None, :, None] * k1[None, None, :] * n2_len + n2[:, None, None] * k1[None, :, None]) % n_total
    ang = m.astype(np.float64) * (2.0 * math.pi / n_total)
    c, s = np.cos(ang), np.sin(ang)
    ch, sh = c[:, :, :n1h], s[:, :, :n1h]
    gf = np.concatenate([np.concatenate([ch, sh], 2), np.concatenate([-sh, ch], 2)], 1)
    gk = np.concatenate([ch, -sh], 1)
    cht, sht = ch.transpose(0, 2, 1), sh.transpose(0, 2, 1)
    gi = np.concatenate([np.concatenate([cht, -sht], 2), np.concatenate([sht, cht], 2)], 1)
    a2 = ((n2[:, None] * n2[None, :]) % n2_len).astype(np.float64) * (2.0 * math.pi / n2_len)
    c2, s2 = np.cos(a2), np.sin(a2)
    f2 = np.concatenate([np.concatenate([c2, s2], 1), np.concatenate([-s2, c2], 1)], 0)
    f2c = np.concatenate([np.concatenate([c2, -s2], 1), np.concatenate([s2, c2], 1)], 0)
    return tuple(np.asarray(t, np.float32) for t in (gf, gk, gi, f2, f2c))


def _fft_tables(n_total, n2_len):
    return tuple(jnp.asarray(t).astype(BF16) for t in _fft_tables_np(n_total, n2_len))


_GRP = 8


def _tile_rows(x3):
    t = jnp.swapaxes(x3, 0, 1)
    return [t[r] for r in range(_GRP)]


def _hy_kf_kernel(fwd_ref, bwd_ref, nrm_ref, gk_ref, f2_ref, o_ref, mid_ref, *, n1_len, n2_len):
    n1h = n1_len // 2
    ta = 2 * n1_len // _GRP
    row = lax.broadcasted_iota(jnp.int32, (n1h, LANE), 0)

    def stage1(m, _):
        cols = pl.ds(pl.multiple_of(m * _GRP, _GRP), _GRP)
        fs, bs = _tile_rows(fwd_ref[:, cols, :]), _tile_rows(bwd_ref[:, cols, :])
        bs[0] = jnp.where((row == 0) & (m == 0), 0.0, bs[0])
        for r in range(_GRP):
            n2 = m * _GRP + r
            a = jnp.dot(gk_ref[n2], jnp.concatenate([fs[r], bs[r]], axis=1).astype(BF16),
                        preferred_element_type=F32)
            mid_ref[0, pl.ds(n2 * ta, ta)] = a[:, :LANE].reshape(ta, _GRP, LANE)
            mid_ref[1, pl.ds(n2 * ta, ta)] = a[:, LANE:].reshape(ta, _GRP, LANE)
        return 0
    lax.fori_loop(0, n2_len // _GRP, stage1, 0)
    scale = nrm_ref[...] * (1.0 / (n1_len * n2_len))

    def stage2(mk, _):
        parts = [[_tile_rows(mid_ref[j, pl.ds(off + mk, n2_len, stride=ta)]) for off in (0, n1_len // _GRP)]
                 for j in range(2)]
        for r in range(_GRP):
            rhs = jnp.concatenate([jnp.concatenate([parts[j][0][r], parts[j][1][r]], axis=0) for j in range(2)],
                                  axis=1).astype(BF16)
            xs = jnp.dot(f2_ref[...], rhs, preferred_element_type=F32)
            f, b = xs[:, :LANE], xs[:, LANE:]
            o_ref[0, mk * _GRP + r] = (jnp.concatenate([f[:n2_len] + b[:n2_len], f[n2_len:] - b[n2_len:]], axis=0)
                                       * scale).astype(BF16)
        return 0
    lax.fori_loop(0, n1_len // _GRP, stage2, 0)


def _hy_kf(h2d, nrm, gk, f2, n2_len):
    l, cols = h2d.shape
    n1_len = 2 * l // n2_len
    n1h = n1_len // 2
    tiles = cols // (2 * LANE)
    h2d = h2d.reshape(n1h, n2_len, cols)
    return pl.pallas_call(
        functools.partial(_hy_kf_kernel, n1_len=n1_len, n2_len=n2_len),
        grid=(tiles,),
        in_specs=[pl.BlockSpec((n1h, n2_len, LANE), lambda t: (0, 0, t)),
                  pl.BlockSpec((n1h, n2_len, LANE), lambda t: (0, 0, tiles + t)),
                  pl.BlockSpec((1, LANE), lambda t: (0, t)),
                  pl.BlockSpec(gk.shape, lambda t: (0, 0, 0)),
                  pl.BlockSpec(f2.shape, lambda t: (0, 0))],
        out_specs=pl.BlockSpec((1, n1_len, 2 * n2_len, LANE), lambda t: (t, 0, 0, 0)),
        out_shape=jax.ShapeDtypeStruct((tiles, n1_len, 2 * n2_len, LANE), BF16),
        scratch_shapes=[pltpu.VMEM((2, 4 * l // _GRP, _GRP, LANE), F32)],
        compiler_params=_cparams(("parallel",)),
        name="hy_kf",
    )(h2d, h2d, nrm, gk, f2)


def _hy_conv_kernel(zin_ref, gin_ref, cwz_ref, cbz_ref, cwg_ref, cbg_ref, kf_ref, bias_ref, gf_ref, gi_ref,
                    f2_ref, f2c_ref, o_ref, mid_ref, spec_ref, g_ref, *zs, n1_len, n2_len, order, z_raw):
    n1h = n1_len // 2
    l = n1h * n2_len
    ta = 2 * n1_len // _GRP
    tb = 2 * n2_len // _GRP

    def shortconv(x4, w_ref, b_ref):
        x = x4.astype(F32).reshape(l, LANE)
        row = lax.broadcasted_iota(jnp.int32, (l, LANE), 0)
        prev = jnp.where(row == 0, 0.0, pltpu.roll(x, 1, axis=0))
        nxt = jnp.where(row == l - 1, 0.0, pltpu.roll(x, l - 1, axis=0))
        y = w_ref[0:1, :] * prev + w_ref[1:2, :] * x + w_ref[2:3, :] * nxt + b_ref[...]
        return y.reshape(n1h, n2_len, LANE)

    for j in range(2):
        g_ref[j] = shortconv(gin_ref[j], cwg_ref, cbg_ref)
        if z_raw:
            zs[0][j] = shortconv(zin_ref[j], cwz_ref, cbz_ref)
    z_ref = zs[0] if z_raw else zin_ref

    def stage1(m, _):
        zr = _tile_rows(z_ref[0, :, pl.ds(pl.multiple_of(m * _GRP, _GRP), _GRP), :])
        zi = _tile_rows(z_ref[1, :, pl.ds(pl.multiple_of(m * _GRP, _GRP), _GRP), :])
        for r in range(_GRP):
            n2 = m * _GRP + r
            rhs = jnp.concatenate([zr[r], zi[r]], axis=0).astype(BF16)
            a = jnp.dot(gf_ref[n2], rhs, preferred_element_type=F32)
            mid_ref[pl.ds(n2 * ta, ta)] = a.reshape(ta, _GRP, LANE)
        return 0
    lax.fori_loop(0, n2_len // _GRP, stage1, 0)

    def stage2(mk, _):
        xr = _tile_rows(mid_ref[pl.ds(mk, n2_len, stride=ta)])
        xi = _tile_rows(mid_ref[pl.ds(n1_len // _GRP + mk, n2_len, stride=ta)])
        for rp in range(_GRP // 2):
            cols = [jnp.concatenate([xr[2 * rp + j], xi[2 * rp + j]], axis=0) for j in range(2)]
            xs = jnp.dot(f2_ref[...], jnp.concatenate(cols, axis=1).astype(BF16), preferred_element_type=F32)
            for j in range(2):
                k1 = mk * _GRP + 2 * rp + j
                x = xs[:, j * LANE:(j + 1) * LANE]
                kf = kf_ref[0, k1].astype(F32)
                xre, xim = x[:n2_len], x[n2_len:]
                kr, ki = kf[:n2_len], kf[n2_len:]
                spec_ref[k1] = jnp.concatenate([xre * kr - xim * ki, xre * ki + xim * kr], axis=0).astype(BF16)
        return 0
    lax.fori_loop(0, n1_len // _GRP, stage2, 0)

    def stage2i(kp, _):
        r = jnp.concatenate([spec_ref[2 * kp], spec_ref[2 * kp + 1]], axis=1)
        b = jnp.dot(f2c_ref[...], r, preferred_element_type=F32)
        mid_ref[pl.ds(2 * kp * tb, tb)] = b[:, :LANE].reshape(tb, _GRP, LANE)
        mid_ref[pl.ds((2 * kp + 1) * tb, tb)] = b[:, LANE:].reshape(tb, _GRP, LANE)
        return 0
    lax.fori_loop(0, n1_len // 2, stage2i, 0, unroll=_FFT_UNROLL)

    def stage1i(m, _):
        br = _tile_rows(mid_ref[pl.ds(m, n1_len, stride=tb)])
        bi = _tile_rows(mid_ref[pl.ds(n2_len // _GRP + m, n1_len, stride=tb)])
        outs = []
        for r in range(_GRP):
            rhs = jnp.concatenate([br[r], bi[r]], axis=0).astype(BF16)
            outs.append(jnp.dot(gi_ref[m * _GRP + r], rhs, preferred_element_type=F32))
        cols = pl.ds(pl.multiple_of(m * _GRP, _GRP), _GRP)
        o_ref[0, :, cols, :] = jnp.swapaxes(jnp.stack([o[:n1h] for o in outs], axis=0), 0, 1)
        o_ref[1, :, cols, :] = jnp.swapaxes(jnp.stack([o[n1h:] for o in outs], axis=0), 0, 1)
        return 0
    lax.fori_loop(0, n2_len // _GRP, stage1i, 0)

    bias = bias_ref[order:order + 1, :]
    for j in range(2):
        o_ref[j] = g_ref[j] * (o_ref[j] + bias * z_ref[j])


def _hy_conv(z, zcol, z_unit, g, gcol, g_unit, conv_w, conv_b, kf, bias, tabs, n2_len, order):
    gf, _, gi, f2, f2c = tabs
    bsz, l, _ = z.shape
    n1_len = 2 * l // n2_len
    tiles = BR // LANE
    const2 = lambda *_: (0, 0)
    const3 = lambda *_: (0, 0, 0)
    one = pl.Buffered(1)
    n1h = n1_len // 2
    z_raw = z_unit is not None
    zu = z_unit if z_raw else 0
    z = z.reshape(bsz, n1h, n2_len, z.shape[-1])
    g = g.reshape(bsz, n1h, n2_len, g.shape[-1])
    seq = pltpu.VMEM((2, n1h, n2_len, LANE), F32)
    out = pl.pallas_call(
        functools.partial(_hy_conv_kernel, n1_len=n1_len, n2_len=n2_len, order=order, z_raw=z_raw),
        grid=(tiles, bsz // 2),
        in_specs=[pl.BlockSpec((2, n1h, n2_len, LANE), lambda t, b: (b, 0, 0, zcol + t)),
                  pl.BlockSpec((2, n1h, n2_len, LANE), lambda t, b: (b, 0, 0, gcol + t)),
                  pl.BlockSpec((3, LANE), lambda t, b: (0, zu + t)),
                  pl.BlockSpec((1, LANE), lambda t, b: (0, zu + t)),
                  pl.BlockSpec((3, LANE), lambda t, b: (0, g_unit + t)),
                  pl.BlockSpec((1, LANE), lambda t, b: (0, g_unit + t)),
                  pl.BlockSpec((1, n1_len, 2 * n2_len, LANE), lambda t, b: (order * tiles + t, 0, 0, 0),
                               pipeline_mode=one),
                  pl.BlockSpec((HY_ORDER, LANE), lambda t, b: (0, t)),
                  pl.BlockSpec(gf.shape, const3, pipeline_mode=one),
                  pl.BlockSpec(gi.shape, const3, pipeline_mode=one),
                  pl.BlockSpec(f2.shape, const2, pipeline_mode=one),
                  pl.BlockSpec(f2c.shape, const2, pipeline_mode=one)],
        out_specs=pl.BlockSpec((2, n1h, n2_len, LANE), lambda t, b: (b, 0, 0, t)),
        out_shape=jax.ShapeDtypeStruct((bsz, n1h, n2_len, BR), F32),
        scratch_shapes=[pltpu.VMEM((4 * l // _GRP, _GRP, LANE), F32),
                        pltpu.VMEM((n1_len, 2 * n2_len, LANE), BF16), seq] + ([seq] if z_raw else []),
        compiler_params=_cparams(("parallel", "arbitrary")),
        name="hy_conv%d" % order,
    )(z, g, conv_w, conv_b.reshape(1, -1), conv_w, conv_b.reshape(1, -1), kf, bias, gf, gi, f2, f2c)
    return out.reshape(bsz, l, BR)


def _hyena_filters(l, w1, b1, f1, w2, b2, f2, w3, b3, f3, w4):
    hp = lax.Precision.HIGHEST
    t = jnp.linspace(0.0, 1.0, l, dtype=F32)[:, None]
    wpos = 2.0 * math.pi * jnp.arange(l, dtype=F32)[:, None] / l
    f = jnp.linspace(1e-4, HY_BANDS - 1, HY_BANDS, dtype=F32)[None, :]
    z = jnp.concatenate([t, jnp.cos(f * wpos), -jnp.sin(f * wpos)], axis=-1)
    h = jnp.sin(f1 * (jnp.dot(z, w1, precision=hp) + b1))
    h = jnp.sin(f2 * (jnp.dot(h, w2, precision=hp) + b2))
    h = jnp.sin(f3 * (jnp.dot(h, w3, precision=hp) + b3))
    deltas = jnp.linspace(math.log(HY_TARGET) / HY_SLOW_DECAY,
                          math.log(HY_TARGET) / HY_FAST_DECAY, BR, dtype=F32)
    window = jnp.exp(-t * jnp.tile(jnp.abs(deltas), 2 * HY_ORDER)[None, :])
    h = jnp.dot(h, w4, precision=hp) * window
    half = HY_ORDER * BR
    row0 = (jnp.arange(l) == 0)[:, None]
    ss = (jnp.sum(h[:, :half] * h[:, :half], axis=0)
          + jnp.sum(jnp.where(row0, 0.0, h[:, half:] * h[:, half:]), axis=0))
    return h, lax.rsqrt(ss + EPS).reshape(1, half)


def _hyena(p3, conv_w, conv_b, filters, bias, n2_len):
    bsz, l, _ = p3.shape
    tabs = _fft_tables(2 * l, n2_len)
    h2d, nrm = filters
    kf = _hy_kf(h2d, nrm, tabs[1], tabs[3], n2_len)
    tiles = BR // LANE
    cw, cb = conv_w.astype(F32), conv_b.astype(F32)
    z1 = _hy_conv(p3, COL_HY_X, 0, p3, COL_HY_X + tiles, tiles, cw, cb, kf, bias, tabs, n2_len, 0)
    return _hy_conv(z1, 0, None, p3, COL_HY_X + 2 * tiles, 2 * tiles, cw, cb, kf, bias, tabs, n2_len, 1)


def _ret_kernel(q_ref, k_ref, v_ref, cos_ref, sin_ref, lg_ref, nw_ref, s0_ref, o_ref, sfin_ref,
                vb_ref, qf_ref, qb_ref, kf_ref, kb_ref, *, nchunks, c):
    lg = lg_ref[0]
    jrow = lax.broadcasted_iota(jnp.int32, (c, c), 0)
    icol = lax.broadcasted_iota(jnp.int32, (c, c), 1)
    jf = lax.broadcasted_iota(jnp.int32, (c, RET_DK), 0).astype(F32)
    dsym = jnp.exp(jnp.abs(jrow - icol).astype(F32) * lg[:, 0:1]) * jnp.where(jrow == icol, 2.0, 1.0)
    dq_f = jnp.exp((jf + 1.0) * lg)
    dq_b = jnp.exp((c - jf) * lg)
    dk_f = jnp.exp((c - 1.0 - jf) * lg)
    dk_b = jnp.exp(jf * lg)
    dchunk = jnp.exp(c * lg)

    def rope(x, r0):
        return x * cos_ref[pl.ds(r0, c), :] + pltpu.roll(x, RET_DK // 2, axis=1) * sin_ref[pl.ds(r0, c), :]

    def intra_step(n, _):
        r0 = pl.multiple_of(n * c, c)
        rows = pl.ds(r0, c)
        q = rope(q_ref[0, rows, :].astype(F32), r0)
        k = rope(k_ref[0, rows, :].astype(F32), r0) * (RET_DK ** -0.5)
        v = v_ref[0, rows, :].astype(BF16)
        att = _nt(q.astype(BF16), k.astype(BF16)) * dsym
        o_ref[0, rows, :] = jnp.dot(att.astype(BF16), v, preferred_element_type=F32)
        vb_ref[rows, :] = v
        qf_ref[rows, :] = (q * dq_f).astype(BF16)
        qb_ref[rows, :] = (q * dq_b).astype(BF16)
        kf_ref[rows, :] = (k * dk_f).astype(BF16)
        kb_ref[rows, :] = (k * dk_b).astype(BF16)
        return 0

    lax.fori_loop(0, nchunks, intra_step, 0, unroll=_ATT_UNROLL)

    def sweep(lo, hi, carry, finish):
        def step(i, carry):
            s_f, s_b = carry
            rf = pl.ds(pl.multiple_of(i * c, c), c)
            rb = pl.ds(pl.multiple_of((nchunks - 1 - i) * c, c), c)
            o_f = o_ref[0, rf, :] + jnp.dot(qf_ref[rf, :], s_f.astype(BF16), preferred_element_type=F32)
            o_b = o_ref[0, rb, :] + jnp.dot(qb_ref[rb, :], s_b.astype(BF16), preferred_element_type=F32)
            if finish:
                o_f = o_f * lax.rsqrt(jnp.mean(o_f * o_f, axis=-1, keepdims=True) + EPS) * nw_ref[...]
                o_b = o_b * lax.rsqrt(jnp.mean(o_b * o_b, axis=-1, keepdims=True) + EPS) * nw_ref[...]
            o_ref[0, rf, :] = o_f
            o_ref[0, rb, :] = o_b
            return (dchunk * s_f + _tn(kf_ref[rf, :], vb_ref[rf, :]),
                    dchunk * s_b + _tn(kb_ref[rb, :], vb_ref[rb, :]))
        return lax.fori_loop(lo, hi, step, carry, unroll=min(_ATT_UNROLL, hi - lo))

    carry = sweep(0, nchunks // 2, (s0_ref[0, 0, 0], s0_ref[0, 0, 1]), False)
    s_f, s_b = sweep(nchunks // 2, nchunks, carry, True)
    sfin_ref[0, 0, 0] = s_f
    sfin_ref[0, 0, 1] = s_b


def _retention(p3, cos2, sin2, norm_w, s0):
    bsz, l, _ = p3.shape
    lgs = jnp.log(1.0 - 2.0 ** (-5.0 - jnp.arange(RET_HEADS, dtype=F32)))
    lgs = jnp.broadcast_to(lgs[:, None, None], (RET_HEADS, 1, LANE))
    tab = pl.BlockSpec((l, LANE), lambda b, h: (0, 0))
    st = pl.BlockSpec((1, 1, 2, RET_DK, LANE), lambda b, h: (b, h, 0, 0, 0))
    chunk = RET_CHUNK if l % (2 * RET_CHUNK) == 0 else ATT_CHUNK
    return pl.pallas_call(
        functools.partial(_ret_kernel, nchunks=l // chunk, c=chunk),
        grid=(bsz, RET_HEADS),
        in_specs=[pl.BlockSpec((1, l, LANE), lambda b, h: (b, 0, COL_RET_Q + h)),
                  pl.BlockSpec((1, l, LANE), lambda b, h: (b, 0, COL_RET_K + h)),
                  pl.BlockSpec((1, l, LANE), lambda b, h: (b, 0, COL_RET_V + h)),
                  tab, tab,
                  pl.BlockSpec((1, 1, LANE), lambda b, h: (h, 0, 0)),
                  pl.BlockSpec((1, LANE), lambda b, h: (0, h)),
                  st],
        out_specs=[pl.BlockSpec((1, l, LANE), lambda b, h: (b, 0, h)), st],
        out_shape=[jax.ShapeDtypeStruct((bsz, l, BR), F32),
                   jax.ShapeDtypeStruct((bsz, RET_HEADS, 2, RET_DK, LANE), F32)],
        scratch_shapes=[pltpu.VMEM((l, LANE), BF16)] * 5,
        compiler_params=_cparams(("parallel", "parallel")),
        name="retention",
    )(p3, p3, p3, cos2, sin2, lgs, norm_w.reshape(1, BR), s0)


def _rope_tables(ang):
    cos, sin = jnp.cos(ang), jnp.sin(ang)
    return jnp.concatenate([cos, cos], -1), jnp.concatenate([-sin, sin], -1)


def _latent_angles(rows):
    half = RET_DK // 4
    inv = ROPE_BASE ** (-jnp.arange(half, dtype=F32) / half)
    r = jnp.repeat(jnp.arange(rows, dtype=F32), GRID_W)
    cl = jnp.tile(jnp.arange(GRID_W, dtype=F32), rows)
    return jnp.concatenate([r[:, None] * inv, cl[:, None] * inv], axis=-1)


def _ctx_angles(n_ctx):
    n = RET_DK // 2
    inv = ROPE_BASE ** (-jnp.arange(n, dtype=F32) / n)
    return jnp.arange(n_ctx, dtype=F32)[:, None] * inv


def _gla_kernel(q_ref, k_ref, v_ref, lr_ref, gw_ref, gb_ref, nw_ref, s0_ref, o_ref, sfin_ref,
                vb_ref, qf_ref, qb_ref, kf_ref, kb_ref, df_ref, db_ref, *, nchunks):
    c = ATT_CHUNK
    jrow = lax.broadcasted_iota(jnp.int32, (c, c), 0)
    icol = lax.broadcasted_iota(jnp.int32, (c, c), 1)
    tri_f = jnp.where(icol <= jrow, 1.0, 0.0).astype(BF16)
    tri_b = jnp.where(icol >= jrow, 1.0, 0.0).astype(BF16)
    lane = lax.broadcasted_iota(jnp.int32, (1, LANE), 1)
    head0 = lane < GLA_DK

    gw_both = jnp.concatenate([gw_ref[0], gw_ref[1]], axis=1)
    gb_both = jnp.concatenate([gb_ref[0], gb_ref[1]], axis=1)

    def decays(r0):
        x = jnp.dot(lr_ref[0, pl.ds(r0, c), :].astype(BF16), gw_both, preferred_element_type=F32) + gb_both
        la = (jnp.minimum(x, 0.0) - jnp.log(1.0 + jnp.exp(-jnp.abs(x)))) * (1.0 / GLA_TAU)
        hi = la.astype(BF16)
        lo = (la - hi.astype(F32)).astype(BF16)
        out = []
        for d, tri in enumerate((tri_f, tri_b)):
            cols = slice(d * LANE, (d + 1) * LANE)
            cs = jnp.dot(tri, jnp.concatenate([hi[:, cols], lo[:, cols]], axis=1), preferred_element_type=F32)
            out.append(cs[:, :LANE] + cs[:, LANE:])
        return out

    def intra_step(n, _):
        r0 = pl.multiple_of(n * c, c)
        rows = pl.ds(r0, c)
        q = q_ref[0, rows, :].astype(F32) * (GLA_DK ** -0.5)
        k = k_ref[0, rows, :].astype(F32)
        bf, bb = decays(r0)
        ref_f = bf[c // 2 - 1:c // 2, :]
        ref_b = bb[c // 2:c // 2 + 1, :]
        qf = q * jnp.exp(bf - ref_f)
        kf = (k * jnp.exp(ref_f - bf)).astype(BF16)
        qb = q * jnp.exp(bb - ref_b)
        kb = (k * jnp.exp(ref_b - bb)).astype(BF16)
        for h in range(2):
            hm = head0 if h == 0 else jnp.logical_not(head0)
            v = v_ref[0, rows, h * GLA_DV:(h + 1) * GLA_DV].astype(BF16)
            att = (jnp.where(icol <= jrow, _nt(jnp.where(hm, qf, 0.0).astype(BF16), kf), 0.0)
                   + jnp.where(icol >= jrow, _nt(jnp.where(hm, qb, 0.0).astype(BF16), kb), 0.0))
            o_ref[0, rows, h * GLA_DV:(h + 1) * GLA_DV] = jnp.dot(att.astype(BF16), v, preferred_element_type=F32)
            vb_ref[rows, h * GLA_DV:(h + 1) * GLA_DV] = v
        bf_last, bb_last = bf[c - 1:c, :], bb[0:1, :]
        qf_ref[rows, :] = (q * jnp.exp(bf)).astype(BF16)
        qb_ref[rows, :] = (q * jnp.exp(bb)).astype(BF16)
        kf_ref[rows, :] = (k * jnp.exp(bf_last - bf)).astype(BF16)
        kb_ref[rows, :] = (k * jnp.exp(bb_last - bb)).astype(BF16)
        d0 = pl.multiple_of(n * 8, 8)
        df_ref[pl.ds(d0, 8), :] = jnp.broadcast_to(jnp.exp(bf_last), (8, LANE))
        db_ref[pl.ds(d0, 8), :] = jnp.broadcast_to(jnp.exp(bb_last), (8, LANE))
        return 0

    lax.fori_loop(0, nchunks, intra_step, 0, unroll=_ATT_UNROLL)

    def sweep(lo, hi, carry, finish):
        def step(i, carry):
            sts = list(carry)
            new = []
            for d in range(2):
                n = i if d == 0 else nchunks - 1 - i
                rows = pl.ds(pl.multiple_of(n * c, c), c)
                qs = (qf_ref if d == 0 else qb_ref)[rows, :]
                kd = (kf_ref if d == 0 else kb_ref)[rows, :]
                dec = (df_ref if d == 0 else db_ref)[pl.ds(pl.multiple_of(n * 8, 8), 1), :]
                for h in range(2):
                    hm = head0 if h == 0 else jnp.logical_not(head0)
                    cols = slice(h * GLA_DV, (h + 1) * GLA_DV)
                    st = sts[2 * d + h]
                    o = o_ref[0, rows, cols] + _nt(qs, st.astype(BF16))
                    if finish:
                        o = o * lax.rsqrt(jnp.mean(o * o, axis=-1, keepdims=True) + EPS) * nw_ref[:, cols]
                    o_ref[0, rows, cols] = o
                    new.append(jnp.where(hm, dec * st + _tn(vb_ref[rows, cols], kd), 0.0))
            return tuple(new)
        return lax.fori_loop(lo, hi, step, carry, unroll=min(_ATT_UNROLL, hi - lo))

    carry = (s0_ref[0, 0, 0, 0], s0_ref[0, 0, 0, 1], s0_ref[0, 0, 1, 0], s0_ref[0, 0, 1, 1])
    carry = sweep(0, nchunks // 2, carry, False)
    s_fin = sweep(nchunks // 2, nchunks, carry, True)
    sfin_ref[0, 0, 0, 0] = s_fin[0]
    sfin_ref[0, 0, 0, 1] = s_fin[1]
    sfin_ref[0, 0, 1, 0] = s_fin[2]
    sfin_ref[0, 0, 1, 1] = s_fin[3]


def _gla(p3, gate_w, gate_b, norm_w, s0):
    bsz, l, _ = p3.shape
    pairs = GLA_HEADS // 2
    gw = jnp.zeros((2, LANE, GLA_HEADS * GLA_DK), F32)
    for d in range(2):
        gw = gw.at[d, d * GLA_LR:(d + 1) * GLA_LR, :].set(gate_w[d].astype(F32))
    st = pl.BlockSpec((1, 1, 2, 2, GLA_DV, LANE), lambda b, h: (b, h, 0, 0, 0, 0))
    return pl.pallas_call(
        functools.partial(_gla_kernel, nchunks=l // ATT_CHUNK),
        grid=(bsz, pairs),
        in_specs=[pl.BlockSpec((1, l, LANE), lambda b, h: (b, 0, COL_GLA_Q + h)),
                  pl.BlockSpec((1, l, LANE), lambda b, h: (b, 0, COL_GLA_K + h)),
                  pl.BlockSpec((1, l, 2 * GLA_DV), lambda b, h: (b, 0, COL_GLA_V // 2 + h)),
                  pl.BlockSpec((1, l, LANE), lambda b, h: (b, 0, COL_GLA_LR)),
                  pl.BlockSpec((2, LANE, LANE), lambda b, h: (0, 0, h)),
                  pl.BlockSpec((2, 1, LANE), lambda b, h: (0, 0, h)),
                  pl.BlockSpec((1, 2 * GLA_DV), lambda b, h: (0, h)),
                  st],
        out_specs=[pl.BlockSpec((1, l, 2 * GLA_DV), lambda b, h: (b, 0, h)), st],
        out_shape=[jax.ShapeDtypeStruct((bsz, l, BR), F32),
                   jax.ShapeDtypeStruct((bsz, pairs, 2, 2, GLA_DV, LANE), F32)],
        scratch_shapes=([pltpu.VMEM((l, 2 * GLA_DV), BF16)] + [pltpu.VMEM((l, LANE), BF16)] * 4
                        + [pltpu.VMEM((8 * l // ATT_CHUNK, LANE), F32)] * 2),
        compiler_params=_cparams(("parallel", "parallel")),
        name="gla",
    )(p3, p3, p3, p3, gw.astype(BF16), gate_b.astype(F32).reshape(2, 1, GLA_HEADS * GLA_DK),
      norm_w.reshape(1, BR), s0)


def _mixers(p2, xs, bsz, prm, i, s5_tabs, hy_filters, rope, s5_h0, ret_s0, gla_s0, with_hyena):
    m = p2.shape[0]
    l = m // bsz
    p3 = p2.reshape(bsz, l, NP)
    y, s5_fin = _s5_core(xs, s5_tabs, s5_h0)
    s5 = _s5_glu(y, p2, prm['s5_d'][i], prm['s5_glu_w'][i], prm['s5_glu_b'][i])
    hy = None
    if with_hyena:
        hy = _hyena(p3, prm['hy_conv_w'][i], prm['hy_conv_b'][i], hy_filters, prm['hy_bias'][i].astype(F32),
                    64 if l >= 512 else 16).reshape(m, BR)
    ret, ret_fin = _retention(p3, rope[0], rope[1], prm['ret_norm_w'][i], ret_s0)
    gla, gla_fin = _gla(p3, prm['gla_gate_w'][i], prm['gla_gate_b'][i], prm['gla_norm_w'][i], gla_s0)
    return (s5, hy, ret.reshape(m, BR), gla.reshape(m, BR)), (s5_fin, ret_fin, gla_fin)


def kernel(x, c, ctx, c_ctx, norm_w, ada_w, ada_b, w_in, w_out, s5_a_re, s5_a_im, s5_log_dt, s5_b_re, s5_b_im, s5_c_re, s5_c_im, s5_d, s5_glu_w, s5_glu_b, hy_conv_w, hy_conv_b, hy_w1, hy_b1, hy_f1, hy_w2, hy_b2, hy_f2, hy_w3, hy_b3, hy_f3, hy_w4, hy_bias, ret_norm_w, gla_gate_w, gla_gate_b, gla_norm_w, final_norm_w):
    prm = dict(s5_a_re=s5_a_re, s5_a_im=s5_a_im, s5_log_dt=s5_log_dt, s5_b_re=s5_b_re, s5_b_im=s5_b_im,
               s5_c_re=s5_c_re, s5_c_im=s5_c_im, s5_d=s5_d, s5_glu_w=s5_glu_w, s5_glu_b=s5_glu_b,
               hy_conv_w=hy_conv_w, hy_conv_b=hy_conv_b, hy_w1=hy_w1, hy_b1=hy_b1, hy_f1=hy_f1,
               hy_w2=hy_w2, hy_b2=hy_b2, hy_f2=hy_f2, hy_w3=hy_w3, hy_b3=hy_b3, hy_f3=hy_f3,
               hy_w4=hy_w4, hy_bias=hy_bias, ret_norm_w=ret_norm_w, gla_gate_w=gla_gate_w,
               gla_gate_b=gla_gate_b, gla_norm_w=gla_norm_w)
    bsz, l, d = x.shape
    lc = ctx.shape[1]
    depth = w_in.shape[0]
    cc = jnp.concatenate([c, c_ctx[None, :], jnp.zeros((8 - bsz - 1, d), F32)], axis=0)
    mod = _ada(cc, ada_w, ada_b)
    rope_l = _rope_tables(_latent_angles(l // GRID_W))
    rope_c = _rope_tables(_ctx_angles(lc))
    s5_zero = jnp.zeros((S5_GROUPS // 2, 2, 2, 8, 2 * S5_STATE), F32)
    ret_zero = jnp.zeros((bsz, RET_HEADS, 2, RET_DK, LANE), F32)
    gla_zero = jnp.zeros((bsz, GLA_HEADS // 2, 2, 2, GLA_DV, LANE), F32)

    w_all = _wprep(w_in)
    wo_all = w_out.astype(BF16)
    filt = (hy_w1, hy_b1, hy_f1, hy_w2, hy_b2, hy_f2, hy_w3, hy_b3, hy_f3, hy_w4)

    x_l = x.reshape(bsz * l, d)
    x_c = ctx.reshape(bsz * lc, d)
    for i in range(depth):
        last = i == depth - 1
        sh, sc, gt = mod[i, :, :d], mod[i, :, d:2 * d], mod[i, :, 2 * d:]
        p_c, xs_c = _inproj(x_c, norm_w[i], sc[bsz:bsz + 1, None, :], sh[bsz:bsz + 1, None, :], w_all, i,
                            bsz * lc)
        p_l, xs_l = _inproj(x_l, norm_w[i], sc[:bsz, None, :], sh[:bsz, None, :], w_all, i, l)
        s5_tabs = _s5_tables(s5_a_re[i], s5_a_im[i], s5_log_dt[i], s5_b_re[i], s5_b_im[i], s5_c_re[i],
                             s5_c_im[i])
        filt_i = [w[i].astype(F32) for w in filt]
        ys_c, (s5_fin, ret_fin, gla_fin) = _mixers(
            p_c, xs_c, bsz, prm, i, s5_tabs, None if last else _hyena_filters(lc, *filt_i), rope_c, s5_zero,
            ret_zero, gla_zero, with_hyena=not last)
        ys_l, _ = _mixers(p_l, xs_l, bsz, prm, i, s5_tabs, _hyena_filters(l, *filt_i), rope_l,
                          _s5_next_h0(s5_fin), ret_fin, gla_fin, with_hyena=True)
        x_l = _outproj(x_l, ys_l, p_l, wo_all, i, gt[:bsz, None, :], final_norm_w, l, final=last)
        if not last:
            x_c = _outproj(x_c, ys_c, p_c, wo_all, i, gt[bsz:bsz + 1, None, :], final_norm_w, bsz * lc,
                           final=False)
    return x_l.reshape(bsz, l, d)
```
